```python
import jax, jax.numpy as jnp
from jax import lax
import numpy as np

D_MODEL = 2048
BATCH = 4
SEQ = 2048
DEPTH = 1

DIL_CONFIGS = ((128, 1), (512, 4), (2048, 16))
N_DIL_GROUPS = len(DIL_CONFIGS)
HEADS_PER_GROUP = 4
HEAD_DIM_A = 128
ATT_BLOCK = 128
ROPE_THETA = 10000.0
A_GROUP_W = HEADS_PER_GROUP * HEAD_DIM_A
A_QKV_W = N_DIL_GROUPS * A_GROUP_W
M_HEADS = 4
M_HEAD_DIM = 256
M_W = M_HEADS * M_HEAD_DIM
M_CHUNK = 128
CONV_K = 4
OFF_QA = 0
OFF_KA = OFF_QA + A_QKV_W
OFF_VA = OFF_KA + A_QKV_W
OFF_QKM = OFF_VA + A_QKV_W
OFF_VM = OFF_QKM + 2 * M_W
OFF_OM = OFF_VM + M_W
OFF_IF = OFF_OM + M_W
N_IN = OFF_IF + 2 * M_HEADS
N_EXPERT_GROUPS = 4
EXPERTS_PER_GROUP = 8
TOP_K_IN_GROUP = 2
D_FF_EXPERT = 1024
DEEPNORM_ALPHA = (2 * DEPTH) ** 0.25
DEEPNORM_BETA = (8 * DEPTH) ** -0.25
LN_EPS = 1e-5

kernel_name = "hybrid_dilated_attn_mlstm_hmoe_block"


def _normalize(x):
    xf = x.astype(jnp.float32)
    mu = jnp.mean(xf, axis=-1, keepdims=True)
    var = jnp.mean(jnp.square(xf - mu), axis=-1, keepdims=True)
    return (xf - mu) * lax.rsqrt(var + LN_EPS)


def layer_norm(x, g, b):
    return (_normalize(x) * g + b).astype(x.dtype)


def rope(x):
    S, E = x.shape[1], x.shape[-1]
    inv = ROPE_THETA ** (-jnp.arange(0, E, 2, dtype=jnp.float32) / E)
    ang = jnp.arange(S, dtype=jnp.float32)[:, None] * inv[None, :]
    cos = jnp.cos(ang)[None, :, None, :]
    sin = jnp.sin(ang)[None, :, None, :]
    xf = x.astype(jnp.float32)
    x1, x2 = xf[..., : E // 2], xf[..., E // 2:]
    return jnp.concatenate([x1 * cos - x2 * sin, x2 * cos + x1 * sin], axis=-1).astype(x.dtype)


def dilated_attention_group(q, k, v, window, dilation):
    Bn, S, H, E = q.shape
    L = S // dilation
    nb = -(-L // ATT_BLOCK)
    Lp = nb * ATT_BLOCK
    steps = window // dilation

    def to_sub(a):
        a = a.reshape(Bn, L, dilation, H, E).transpose(0, 2, 3, 1, 4)
        a = jnp.pad(a, ((0, 0), (0, 0), (0, 0), (0, Lp - L), (0, 0)))
        return a.reshape(Bn, dilation, H, nb, ATT_BLOCK, E)

    def with_prev(a):
        prev = jnp.pad(a, ((0, 0), (0, 0), (0, 0), (1, 0), (0, 0), (0, 0)))[:, :, :, :-1]
        return jnp.concatenate([prev, a], axis=4)

    qs = to_sub(q)
    kw = with_prev(to_sub(k))
    vw = with_prev(to_sub(v))
    scores = jnp.einsum('brhnqe,brhnke->brhnqk', qs, kw).astype(jnp.float32) * (E ** -0.5)
    blk = jnp.arange(nb)[:, None, None]
    qi = blk * ATT_BLOCK + jnp.arange(ATT_BLOCK)[None, :, None]
    ki = (blk - 1) * ATT_BLOCK + jnp.arange(2 * ATT_BLOCK)[None, None, :]
    dist = qi - ki
    valid = (dist >= 0) & (dist <= steps) & (ki >= 0)
    scores = jnp.where(valid, scores, -jnp.inf)
    mx = jnp.max(scores, axis=-1, keepdims=True)
    p = jnp.exp(scores - mx)
    den = jnp.sum(p, axis=-1)
    o = jnp.einsum('brhnqk,brhnke->brhnqe', p, vw.astype(jnp.float32)) / den[..., None]
    lse = mx[..., 0] + jnp.log(den)
    o = o.reshape(Bn, dilation, H, Lp, E)[:, :, :, :L].transpose(0, 3, 1, 2, 4).reshape(Bn, S, H, E)
    lse = lse.reshape(Bn, dilation, H, Lp)[..., :L].transpose(0, 3, 1, 2).reshape(Bn, S, H)
    return o, lse


def causal_conv(x, w, b):
    C = x.shape[-1]
    y = lax.conv_general_dilated(x, w[:, None, :], (1,), [(CONV_K - 1, 0)],
                                 dimension_numbers=('NWC', 'WIO', 'NWC'),
                                 feature_group_count=C)
    return y + b


def mlstm_chunkwise(q, k, v, i_pre, f_pre):
    Bn, H, S, Dh = q.shape
    L = M_CHUNK
    nc = S // L
    k = k * (Dh ** -0.5)
    log_f = jax.nn.log_sigmoid(f_pre)

    def to_chunks(a):
        return jnp.moveaxis(a.reshape(a.shape[:2] + (nc, L) + a.shape[3:]), 2, 0)

    causal = jnp.tril(jnp.ones((L, L), dtype=bool))

    def step(carry, inp):
        C, n, m = carry
        qb, kb, vb, ib, fb = inp
        b = jnp.cumsum(fb, axis=-1)
        log_d = jnp.where(causal, b[..., :, None] - b[..., None, :] + ib[..., None, :], -jnp.inf)
        log_inter = b + m[..., None]
        m_t = jnp.maximum(jnp.max(log_d, axis=-1), log_inter)
        s = jnp.einsum('bhtd,bhsd->bhts', qb, kb) * jnp.exp(log_d - m_t[..., None])
        inter = jnp.exp(log_inter - m_t)
        num = jnp.einsum('bhts,bhsd->bhtd', s, vb) + inter[..., None] * jnp.einsum('bhvk,bhtk->bhtv', C, qb)
        den = jnp.sum(s, axis=-1) + inter * jnp.einsum('bhk,bhtk->bht', n, qb)
        h = num / jnp.maximum(jnp.abs(den), jnp.exp(-m_t))[..., None]
        m_new = m_t[..., -1]
        w_state = jnp.exp(b[..., -1:] - b + ib - m_new[..., None])
        decay = jnp.exp(b[..., -1] + m - m_new)
        C_new = decay[..., None, None] * C + jnp.einsum('bhs,bhsv,bhsk->bhvk', w_state, vb, kb)
        n_new = decay[..., None] * n + jnp.einsum('bhs,bhsk->bhk', w_state, kb)
        return (C_new, n_new, m_new), h

    init = (jnp.zeros((Bn, H, Dh, Dh), jnp.float32),
            jnp.zeros((Bn, H, Dh), jnp.float32),
            jnp.zeros((Bn, H), jnp.float32))
    xs = (to_chunks(q), to_chunks(k), to_chunks(v), to_chunks(i_pre), to_chunks(log_f))
    _, hs = lax.scan(step, init, xs)
    return jnp.moveaxis(hs, 0, 2).reshape(Bn, H, S, Dh)


def token_mixer(u, w_in, b_mgate, conv_w, conv_b, m_norm_g, w_proj_a, w_proj_m, w_gate, b_gate, w_out):
    Bn, S, D = u.shape
    proj = u @ w_in
    n_a_heads = N_DIL_GROUPS * HEADS_PER_GROUP
    qa = rope(proj[..., OFF_QA:OFF_KA].reshape(Bn, S, n_a_heads, HEAD_DIM_A))
    ka = rope(proj[..., OFF_KA:OFF_VA].reshape(Bn, S, n_a_heads, HEAD_DIM_A))
    va = proj[..., OFF_VA:OFF_QKM].reshape(Bn, S, n_a_heads, HEAD_DIM_A)
    outs, lses = [], []
    for g, (window, dilation) in enumerate(DIL_CONFIGS):
        hs = slice(g * HEADS_PER_GROUP, (g + 1) * HEADS_PER_GROUP)
        o, lse = dilated_attention_group(qa[:, :, hs], ka[:, :, hs], va[:, :, hs], window, dilation)
        outs.append(o)
        lses.append(lse)
    mix_w = jax.nn.softmax(jnp.stack(lses, axis=0), axis=0)[..., None]
    y_a = jnp.sum(mix_w * jnp.stack(outs, axis=0), axis=0).reshape(Bn, S, A_GROUP_W).astype(u.dtype)

    qk = jax.nn.silu(causal_conv(proj[..., OFF_QKM:OFF_VM], conv_w, conv_b))
    gates = (proj[..., OFF_IF:N_IN] + b_mgate).astype(jnp.float32)

    def to_bhsd(a):
        return a.reshape(Bn, S, M_HEADS, M_HEAD_DIM).transpose(0, 2, 1, 3).astype(jnp.float32)

    h = mlstm_chunkwise(to_bhsd(qk[..., :M_W]), to_bhsd(qk[..., M_W:]),
                        to_bhsd(proj[..., OFF_VM:OFF_OM]),
                        gates[..., :M_HEADS].transpose(0, 2, 1),
                        gates[..., M_HEADS:].transpose(0, 2, 1))
    o_gate = jax.nn.sigmoid(proj[..., OFF_OM:OFF_IF].astype(jnp.float32)).reshape(Bn, S, M_HEADS, M_HEAD_DIM)
    h = o_gate * h.transpose(0, 2, 1, 3)
    h = _normalize(h) * m_norm_g.reshape(M_HEADS, M_HEAD_DIM)
    y_m = h.reshape(Bn, S, M_W).astype(u.dtype)

    g_br = jax.nn.sigmoid((u @ w_gate + b_gate).astype(jnp.float32)).reshape(Bn, S, 2, D)
    merged = g_br[..., 0, :] * (y_a @ w_proj_a) + g_br[..., 1, :] * (y_m @ w_proj_m)
    return merged.astype(u.dtype) @ w_out


def hier_moe(u, w_rg, b_rg, w_re, b_re, w_eg, w_eu, w_ed):
    Bn, S, D = u.shape
    G, E = N_EXPERT_GROUPS, EXPERTS_PER_GROUP
    t = u.reshape(-1, D)
    g_prob = jax.nn.softmax((t @ w_rg + b_rg).astype(jnp.float32), axis=-1)
    g_top = jnp.argmax(g_prob, axis=-1)
    g_w = jnp.max(g_prob, axis=-1)
    e_logits = (t @ w_re + b_re).astype(jnp.float32).reshape(-1, G, E)
    e_sel = jnp.take_along_axis(e_logits, g_top[:, None, None], axis=1)[:, 0]
    top_v, top_i = lax.top_k(jax.nn.softmax(e_sel, axis=-1), TOP_K_IN_GROUP)
    top_v = top_v / jnp.sum(top_v, axis=-1, keepdims=True)
    e_w = jnp.sum(jax.nn.one_hot(top_i, E, dtype=jnp.float32) * top_v[..., None], axis=1)
    combine = (g_w[:, None] * jax.nn.one_hot(g_top, G, dtype=jnp.float32))[:, :, None] * e_w[:, None, :]
    y = jnp.zeros(t.shape, jnp.float32)
    for gi in range(G):
        hg = jax.nn.silu(jnp.einsum('td,edf->tef', t, w_eg[gi])) * jnp.einsum('td,edf->tef', t, w_eu[gi])
        y = y + jnp.einsum('tef,efd->td', hg * combine[:, gi, :, None].astype(hg.dtype), w_ed[gi])
    return y.reshape(Bn, S, D).astype(u.dtype)


def setup_inputs(seed: int = 0) -> dict:
    key = jax.random.key(seed)
    ks = jax.random.split(key, 28)
    D, L = D_MODEL, DEPTH
    G, E, F = N_EXPERT_GROUPS, EXPERTS_PER_GROUP, D_FF_EXPERT

    def nrm(k, shape, std):
        return jax.random.normal(k, shape, jnp.float32) * std

    b_mgate = jnp.concatenate([
        nrm(ks[4], (L, M_HEADS), 0.1),
        3.0 + 3.0 * jax.random.uniform(ks[5], (L, M_HEADS), jnp.float32)], axis=-1)
    return {
        "x": nrm(ks[0], (BATCH, SEQ, D), 1.0),
        "c": nrm(ks[1], (BATCH, D), 1.0),
        "w_ada": nrm(ks[2], (L, D, 6 * D), 0.5 * D ** -0.5),
        "b_ada": nrm(ks[3], (L, 6 * D), 0.01),
        "w_in": nrm(ks[6], (L, D, N_IN), D ** -0.5),
        "b_mgate": b_mgate,
        "conv_w": nrm(ks[7], (L, CONV_K, 2 * M_W), CONV_K ** -0.5),
        "conv_b": nrm(ks[8], (L, 2 * M_W), 0.01),
        "m_norm_g": 1.0 + nrm(ks[9], (L, M_W), 0.02),
        "w_proj_a": nrm(ks[10], (L, A_GROUP_W, D), DEEPNORM_BETA * A_GROUP_W ** -0.5),
        "w_proj_m": nrm(ks[11], (L, M_W, D), DEEPNORM_BETA * M_W ** -0.5),
        "w_gate": nrm(ks[12], (L, D, 2 * D), D ** -0.5),
        "b_gate": nrm(ks[13], (L, 2 * D), 0.01),
        "w_out": nrm(ks[14], (L, D, D), DEEPNORM_BETA * D ** -0.5),
        "ln1_g": 1.0 + nrm(ks[15], (L, D), 0.02),
        "ln1_b": nrm(ks[16], (L, D), 0.01),
        "w_rg": nrm(ks[17], (L, D, G), D ** -0.5),
        "b_rg": nrm(ks[18], (L, G), 0.01),
        "w_re": nrm(ks[19], (L, D, G * E), D ** -0.5),
        "b_re": nrm(ks[20], (L, G * E), 0.01),
        "w_eg": nrm(ks[21], (L, G, E, D, F), D ** -0.5),
        "w_eu": nrm(ks[22], (L, G, E, D, F), D ** -0.5),
        "w_ed": nrm(ks[23], (L, G, E, F, D), DEEPNORM_BETA * F ** -0.5),
        "ln2_g": 1.0 + nrm(ks[24], (L, D), 0.02),
        "ln2_b": nrm(ks[25], (L, D), 0.01),
    }


def reference(x, c, w_ada, b_ada, w_in, b_mgate, conv_w, conv_b, m_norm_g, w_proj_a, w_proj_m,
              w_gate, b_gate, w_out, ln1_g, ln1_b, w_rg, b_rg, w_re, b_re, w_eg, w_eu, w_ed,
              ln2_g, ln2_b):
    for l in range(DEPTH):
        mod = (jax.nn.silu(c) @ w_ada[l] + b_ada[l]).reshape(c.shape[0], 6, D_MODEL)[:, :, None, :]
        shift1, scale1, gate1, shift2, scale2, gate2 = [mod[:, j] for j in range(6)]
        u = (_normalize(x) * (1.0 + scale1) + shift1).astype(x.dtype)
        mix = token_mixer(u, w_in[l], b_mgate[l], conv_w[l], conv_b[l], m_norm_g[l], w_proj_a[l],
                          w_proj_m[l], w_gate[l], b_gate[l], w_out[l])
        x = layer_norm(DEEPNORM_ALPHA * x + gate1 * mix, ln1_g[l], ln1_b[l])
        u = (_normalize(x) * (1.0 + scale2) + shift2).astype(x.dtype)
        ffn = hier_moe(u, w_rg[l], b_rg[l], w_re[l], b_re[l], w_eg[l], w_eu[l], w_ed[l])
        x = layer_norm(DEEPNORM_ALPHA * x + gate2 * ffn, ln2_g[l], ln2_b[l])
    return x
```

```python
import functools

import jax
import jax.numpy as jnp
from jax import lax
from jax.experimental import pallas as pl
from jax.experimental.pallas import tpu as pltpu

F32 = jnp.float32
BF16 = jnp.bfloat16

D_MODEL = 2048
BATCH = 4
SEQ = 2048
TOKENS = BATCH * SEQ
DIL_CONFIGS = ((128, 1), (512, 4), (2048, 16))
N_DIL_GROUPS = 3
HEADS_PER_GROUP = 4
HEAD_DIM_A = 128
ATT_BLOCK = 128
ROPE_THETA = 10000.0
A_GROUP_W = HEADS_PER_GROUP * HEAD_DIM_A
A_QKV_W = N_DIL_GROUPS * A_GROUP_W
M_HEADS = 4
M_HEAD_DIM = 256
M_W = M_HEADS * M_HEAD_DIM
M_CHUNK = 128
CONV_K = 4
N_IN_MAIN = 3 * A_QKV_W + 4 * M_W
N_EXPERT_GROUPS = 4
EXPERTS_PER_GROUP = 8
N_EXPERTS = N_EXPERT_GROUPS * EXPERTS_PER_GROUP
D_FF_EXPERT = 1024
DEEPNORM_ALPHA = 2.0 ** 0.25
LN_EPS = 1e-5

LANES = 128
SUBLANES = 8
VMEM_LIMIT_BYTES = 56 * 1024 * 1024

PROJ_TN = 512
PROJ_TM = 1024
P_TILES_IN = N_IN_MAIN // PROJ_TN
P_ATT_TILES = 3 * A_QKV_W // PROJ_TN
P_M_TILES = P_TILES_IN - P_ATT_TILES
P_WIDTH = N_IN_MAIN
P_ATT_OFF = P_M_TILES * PROJ_TN
P_QM_BLK, P_KM_BLK, P_VM_BLK, P_OM_BLK = 0, 1, 2, 3
MERGE_TM = 256
MOE_TM = 256
MOE_ROWS = 2 * TOKENS + N_EXPERTS * MOE_TM
MOE_TILES = MOE_ROWS // MOE_TM
MOE_TF = 512
MOE_TD = 1024
FINAL_TB = 256


def _cparams(sem, vmem=VMEM_LIMIT_BYTES):
    return pltpu.CompilerParams(dimension_semantics=sem, vmem_limit_bytes=vmem)


def _normalize(x):
    mu = jnp.mean(x, axis=-1, keepdims=True)
    xc = x - mu
    var = jnp.mean(xc * xc, axis=-1, keepdims=True)
    return xc * lax.rsqrt(var + LN_EPS)


def _silu(x):
    return x * jax.nn.sigmoid(x)


def _log_sigmoid(x):
    return jnp.minimum(x, 0.0) - jnp.log(1.0 + jnp.exp(-jnp.abs(x)))


ADA_TN = 1024
ADA_ROWS = 16


def _ada_kernel(c_ref, w_ref, b_ref, o_ref):
    sc = _silu(c_ref[...]).astype(BF16)
    o_ref[...] = jnp.dot(sc, w_ref[...].astype(BF16), preferred_element_type=F32) + b_ref[...]


def _ada(c_pad, w_ada, b_ada):
    n = w_ada.shape[1]
    return pl.pallas_call(
        _ada_kernel,
        out_shape=jax.ShapeDtypeStruct((ADA_ROWS, n), F32),
        grid=(n // ADA_TN,),
        in_specs=[
            pl.BlockSpec((ADA_ROWS, D_MODEL), lambda j: (0, 0)),
            pl.BlockSpec((D_MODEL, ADA_TN), lambda j: (0, j)),
            pl.BlockSpec((1, ADA_TN), lambda j: (0, j)),
        ],
        out_specs=pl.BlockSpec((ADA_ROWS, ADA_TN), lambda j: (0, j)),
        compiler_params=_cparams(("arbitrary",)),
        name="ada_mod",
    )(c_pad, w_ada, b_ada)


LN_CHUNK = 256


def _inproj_kernel(x_ref, mod_ref, w_ref, wif_ref, p_ref, g_ref, u_ref):
    n = pl.program_id(1)

    @pl.when(n == 0)
    def _():
        shift = mod_ref[0:1, :]
        scale = 1.0 + mod_ref[1:2, :]
        wif = wif_ref[...].astype(BF16)

        def body(ci, carry):
            r = pl.multiple_of(ci * LN_CHUNK, LN_CHUNK)
            u = (_normalize(x_ref[pl.ds(r, LN_CHUNK), :]) * scale + shift).astype(BF16)
            u_ref[pl.ds(r, LN_CHUNK), :] = u
            g_ref[pl.ds(r, LN_CHUNK), :] = jnp.dot(u, wif, preferred_element_type=F32)
            return carry

        lax.fori_loop(0, PROJ_TM // LN_CHUNK, body, 0)

    p_ref[...] = jnp.dot(u_ref[...], w_ref[...].astype(BF16), preferred_element_type=F32)


def _inproj(x2, mod, w_in, w_if):
    tiles_per_batch = SEQ // PROJ_TM
    return pl.pallas_call(
        _inproj_kernel,
        out_shape=(
            jax.ShapeDtypeStruct((TOKENS, P_WIDTH), F32),
            jax.ShapeDtypeStruct((TOKENS, LANES), F32),
            jax.ShapeDtypeStruct((TOKENS, D_MODEL), BF16),
        ),
        grid=(TOKENS // PROJ_TM, P_TILES_IN),
        in_specs=[
            pl.BlockSpec((PROJ_TM, D_MODEL), lambda m, n: (m, 0)),
            pl.BlockSpec((None, 6, D_MODEL), lambda m, n: (m // tiles_per_batch, 0, 0)),
            pl.BlockSpec((D_MODEL, PROJ_TN), lambda m, n: (0, n)),
            pl.BlockSpec((D_MODEL, LANES), lambda m, n: (0, 0)),
        ],
        out_specs=(
            pl.BlockSpec((PROJ_TM, PROJ_TN),
                         lambda m, n: (m, jnp.where(n < P_ATT_TILES, n + P_M_TILES, n - P_ATT_TILES))),
            pl.BlockSpec((PROJ_TM, LANES), lambda m, n: (m, 0)),
            pl.BlockSpec((PROJ_TM, D_MODEL), lambda m, n: (m, 0)),
        ),
        compiler_params=_cparams(("arbitrary", "arbitrary")),
        name="in_proj",
    )(x2, mod, w_in, w_if)


def _gateproj_kernel(u_ref, w_ref, b_ref, o_ref):
    acc = jnp.dot(u_ref[...], w_ref[...].astype(BF16), preferred_element_type=F32)
    o_ref[...] = jax.nn.sigmoid(acc + b_ref[...]).astype(BF16)


def _gateproj(u, w_gate, b_gate):
    n = w_gate.shape[1]
    return pl.pallas_call(
        _gateproj_kernel,
        out_shape=jax.ShapeDtypeStruct((TOKENS, n), BF16),
        grid=(TOKENS // PROJ_TM, n // PROJ_TN),
        in_specs=[
            pl.BlockSpec((PROJ_TM, D_MODEL), lambda m, j: (m, 0)),
            pl.BlockSpec((D_MODEL, PROJ_TN), lambda m, j: (0, j)),
            pl.BlockSpec((1, PROJ_TN), lambda m, j: (0, j)),
        ],
        out_specs=pl.BlockSpec((PROJ_TM, PROJ_TN), lambda m, j: (m, j)),
        compiler_params=_cparams(("arbitrary", "arbitrary")),
        name="gate_proj",
    )(u, w_gate, b_gate)


def _attn_kernel(q_ref, k_ref, v_ref, cos_ref, sin_ref, o_ref, l_ref, qr_sc, kr_sc, vb_sc, *, nb, hp):
    cos = cos_ref[...]
    sin = sin_ref[...]
    scale = HEAD_DIM_A ** -0.5
    blk = ATT_BLOCK
    qi2 = lax.broadcasted_iota(jnp.int32, (blk, 2 * blk), 0)
    kc2 = lax.broadcasted_iota(jnp.int32, (blk, 2 * blk), 1)
    mask_prev_cur = (kc2 >= qi2) & (kc2 <= qi2 + blk)
    qi1 = lax.broadcasted_iota(jnp.int32, (blk, blk), 0)
    kc1 = lax.broadcasted_iota(jnp.int32, (blk, blk), 1)
    mask_cur = kc1 <= qi1
    half = HEAD_DIM_A // 2
    for h in range(hp):
        cs = slice(h * HEAD_DIM_A, (h + 1) * HEAD_DIM_A)
        q = q_ref[:, cs]
        k = k_ref[:, cs]
        qr_sc[...] = (q * cos + pltpu.roll(q, half, 1) * sin).astype(BF16)
        kr_sc[...] = (k * cos + pltpu.roll(k, half, 1) * sin).astype(BF16)
        vb_sc[...] = v_ref[:, cs].astype(BF16)
        for j in range(nb):
            rows = slice(j * blk, (j + 1) * blk)
            qb = qr_sc[rows, :]
            if j == 0:
                kw = kr_sc[0:blk, :]
                vw = vb_sc[0:blk, :]
                mask = mask_cur
            else:
                kw = kr_sc[(j - 1) * blk:(j + 1) * blk, :]
                vw = vb_sc[(j - 1) * blk:(j + 1) * blk, :]
                mask = mask_prev_cur
            s = lax.dot_general(qb, kw, (((1,), (1,)), ((), ())), preferred_element_type=F32) * scale
            s = jnp.where(mask, s, -jnp.inf)
            mx = jnp.max(s, axis=-1, keepdims=True)
            p = jnp.exp(s - mx)
            den = jnp.sum(p, axis=-1, keepdims=True)
            o = jnp.dot(p.astype(BF16), vw, preferred_element_type=F32) / den
            o_ref[rows, cs] = o
            l_ref[rows, cs] = jnp.broadcast_to(mx + jnp.log(den), (blk, HEAD_DIM_A))


def _attention_group(p3, cos_t, sin_t, g, hp):
    window, d = DIL_CONFIGS[g]
    assert window // d == ATT_BLOCK
    L = SEQ // d
    nb = L // ATT_BLOCK
    bw = hp * HEAD_DIM_A
    pv = p3.reshape(BATCH, L, d * P_WIDTH)
    cv = cos_t.reshape(L, d * HEAD_DIM_A)
    sv = sin_t.reshape(L, d * HEAD_DIM_A)
    row_blocks = P_WIDTH // bw
    qoff = (P_ATT_OFF + g * A_GROUP_W) // bw
    koff = (P_ATT_OFF + A_QKV_W + g * A_GROUP_W) // bw
    voff = (P_ATT_OFF + 2 * A_QKV_W + g * A_GROUP_W) // bw
    nhb = HEADS_PER_GROUP // hp
    kern = functools.partial(_attn_kernel, nb=nb, hp=hp)
    o, lse = pl.pallas_call(
        kern,
        out_shape=(
            jax.ShapeDtypeStruct((BATCH, L, d * A_GROUP_W), F32),
            jax.ShapeDtypeStruct((BATCH, L, d * A_GROUP_W), F32),
        ),
        grid=(BATCH, d, nhb),
        in_specs=[
            pl.BlockSpec((None, L, bw), lambda b, r, hb: (b, 0, r * row_blocks + qoff + hb)),
            pl.BlockSpec((None, L, bw), lambda b, r, hb: (b, 0, r * row_blocks + koff + hb)),
            pl.BlockSpec((None, L, bw), lambda b, r, hb: (b, 0, r * row_blocks + voff + hb)),
            pl.BlockSpec((L, HEAD_DIM_A), lambda b, r, hb: (0, r)),
            pl.BlockSpec((L, HEAD_DIM_A), lambda b, r, hb: (0, r)),
        ],
        out_specs=(
            pl.BlockSpec((None, L, bw), lambda b, r, hb: (b, 0, r * nhb + hb)),
            pl.BlockSpec((None, L, bw), lambda b, r, hb: (b, 0, r * nhb + hb)),
        ),
        scratch_shapes=[pltpu.VMEM((L, HEAD_DIM_A), BF16)] * 3,
        compiler_params=_cparams(("arbitrary", "arbitrary", "arbitrary")),
        name=f"dil_attn_g{g}",
    )(pv, pv, pv, cv, sv)
    return o.reshape(TOKENS, A_GROUP_W), lse.reshape(TOKENS, A_GROUP_W)


CONV_HALO = SUBLANES


def _mlstm_kernel(q_ref, k_ref, v_ref, og_ref, gc_ref, gr_ref, bc_ref, br_ref, cw_ref, cb_ref, ng_ref,
                  y_ref, xq_sc, xk_sc, ct_sc, n_sc, m_sc):
    c = pl.program_id(1)
    L = M_CHUNK
    lo = CONV_HALO

    @pl.when(c == 0)
    def _():
        xq_sc[0:lo, :] = jnp.zeros((lo, M_W), F32)
        xk_sc[0:lo, :] = jnp.zeros((lo, M_W), F32)
        ct_sc[...] = jnp.zeros_like(ct_sc)
        n_sc[...] = jnp.zeros_like(n_sc)
        m_sc[...] = jnp.zeros_like(m_sc)

    xq_sc[lo:lo + L, :] = q_ref[...]
    xk_sc[lo:lo + L, :] = k_ref[...]

    def conv(x_sc, w, b):
        acc = x_sc[lo:lo + L, :] * w[CONV_K - 1:CONV_K, :] + b
        for j in range(CONV_K - 1):
            off = lo - (CONV_K - 1) + j
            acc = acc + x_sc[off:off + L, :] * w[j:j + 1, :]
        return acc

    cw = cw_ref[...]
    cb = cb_ref[...]
    qc = _silu(conv(xq_sc, cw[:, :M_W], cb[:, :M_W]))
    kc = _silu(conv(xk_sc, cw[:, M_W:], cb[:, M_W:])) * (M_HEAD_DIM ** -0.5)
    xq_sc[0:lo, :] = xq_sc[L:L + lo, :]
    xk_sc[0:lo, :] = xk_sc[L:L + lo, :]

    row = lax.broadcasted_iota(jnp.int32, (L, L), 0)
    col = lax.broadcasted_iota(jnp.int32, (L, L), 1)
    causal = row >= col
    for h in range(M_HEADS):
        hs = slice(h * M_HEAD_DIM, (h + 1) * M_HEAD_DIM)
        q = qc[:, hs]
        k = kc[:, hs]
        i_col = gc_ref[:, h:h + 1] + bc_ref[:, h:h + 1]
        lf_col = _log_sigmoid(gc_ref[:, M_HEADS + h:M_HEADS + h + 1] + bc_ref[:, M_HEADS + h:M_HEADS + h + 1])
        i_row = gr_ref[h:h + 1, :] + br_ref[h:h + 1, :]
        lf_row = _log_sigmoid(gr_ref[M_HEADS + h:M_HEADS + h + 1, :] + br_ref[M_HEADS + h:M_HEADS + h + 1, :])
        b_col = jnp.sum(jnp.where(causal, lf_row, 0.0), axis=1, keepdims=True)
        b_row = jnp.sum(jnp.where(row <= col, lf_col, 0.0), axis=0, keepdims=True)
        log_d = jnp.where(causal, b_col - b_row + i_row, -jnp.inf)
        m_prev = m_sc[h, 0:1, 0:1]
        log_inter = b_col + m_prev
        m_t = jnp.maximum(jnp.max(log_d, axis=1, keepdims=True), log_inter)
        qb = q.astype(BF16)
        kb = k.astype(BF16)
        vb = v_ref[:, hs].astype(BF16)
        s = lax.dot_general(qb, kb, (((1,), (1,)), ((), ())), preferred_element_type=F32) * jnp.exp(log_d - m_t)
        inter = jnp.exp(log_inter - m_t)
        ct = ct_sc[h]
        n_row = n_sc[h, 0:1, :]
        num = (jnp.dot(s.astype(BF16), vb, preferred_element_type=F32)
               + inter * jnp.dot(qb, ct.astype(BF16), preferred_element_type=F32))
        den = jnp.sum(s, axis=1, keepdims=True) + inter * jnp.sum(q * n_row, axis=1, keepdims=True)
        hh = num / jnp.maximum(jnp.abs(den), jnp.exp(-m_t))
        m_new = m_t[L - 1:L, :]
        b_last = b_col[L - 1:L, :]
        w_col = jnp.exp(b_last - b_col + i_col - m_new)
        decay = jnp.exp(b_last + m_prev - m_new)
        kw = k * w_col
        ct_sc[h] = decay * ct + lax.dot_general(kw.astype(BF16), vb, (((0,), (0,)), ((), ())),
                                                preferred_element_type=F32)
        n_sc[h] = jnp.broadcast_to(decay * n_row + jnp.sum(kw, axis=0, keepdims=True), (SUBLANES, M_HEAD_DIM))
        m_sc[h] = jnp.broadcast_to(m_new, (SUBLANES, LANES))
        z = jax.nn.sigmoid(og_ref[:, hs]) * hh
        y_ref[:, hs] = (_normalize(z) * ng_ref[:, hs]).astype(BF16)


def _mlstm(p, gcol, grow, bcol, brow, conv_w, conv_b, norm_g):
    nc = SEQ // M_CHUNK
    slab = lambda blk: pl.BlockSpec((M_CHUNK, M_W), lambda b, c: (b * nc + c, blk))
    return pl.pallas_call(
        _mlstm_kernel,
        out_shape=jax.ShapeDtypeStruct((TOKENS, M_W), BF16),
        grid=(BATCH, nc),
        in_specs=[
            slab(P_QM_BLK), slab(P_KM_BLK), slab(P_VM_BLK), slab(P_OM_BLK),
            pl.BlockSpec((None, M_CHUNK, 2 * M_HEADS), lambda b, c: (b, c, 0)),
            pl.BlockSpec((None, 2 * M_HEADS, M_CHUNK), lambda b, c: (b, 0, c)),
            pl.BlockSpec((1, 2 * M_HEADS), lambda b, c: (0, 0)),
            pl.BlockSpec((2 * M_HEADS, 1), lambda b, c: (0, 0)),
            pl.BlockSpec((CONV_K, 2 * M_W), lambda b, c: (0, 0)),
            pl.BlockSpec((1, 2 * M_W), lambda b, c: (0, 0)),
            pl.BlockSpec((1, M_W), lambda b, c: (0, 0)),
        ],
        out_specs=pl.BlockSpec((M_CHUNK, M_W), lambda b, c: (b * nc + c, 0)),
        scratch_shapes=[
            pltpu.VMEM((M_CHUNK + CONV_HALO, M_W), F32),
            pltpu.VMEM((M_CHUNK + CONV_HALO, M_W), F32),
            pltpu.VMEM((M_HEADS, M_HEAD_DIM, M_HEAD_DIM), F32),
            pltpu.VMEM((M_HEADS, SUBLANES, M_HEAD_DIM), F32),
            pltpu.VMEM((M_HEADS, SUBLANES, LANES), F32),
        ],
        compiler_params=_cparams(("arbitrary", "arbitrary")),
        name="mlstm",
    )(p, p, p, p, gcol, grow, bcol, brow, conv_w, conv_b, norm_g)


RT_E1, RT_E2, RT_W1, RT_W2 = 0, 1, 2, 3


def _route(logits):
    lane = lax.broadcasted_iota(jnp.int32, logits.shape, 1).astype(F32)
    big = float(LANES)
    is_g = lane < N_EXPERT_GROUPS
    gl = jnp.where(is_g, logits, -jnp.inf)
    gexp = jnp.exp(gl - jnp.max(gl, axis=1, keepdims=True))
    gprob = gexp / jnp.sum(gexp, axis=1, keepdims=True)
    g_w = jnp.max(gprob, axis=1, keepdims=True)
    g_top = jnp.min(jnp.where(is_g & (gprob == g_w), lane, big), axis=1, keepdims=True)
    lo = N_EXPERT_GROUPS + EXPERTS_PER_GROUP * g_top
    in_grp = (lane >= lo) & (lane < lo + EXPERTS_PER_GROUP)
    el = jnp.where(in_grp, logits, -jnp.inf)
    eexp = jnp.exp(el - jnp.max(el, axis=1, keepdims=True))
    eprob = eexp / jnp.sum(eexp, axis=1, keepdims=True)
    v1 = jnp.max(eprob, axis=1, keepdims=True)
    i1 = jnp.min(jnp.where(in_grp & (eprob == v1), lane, big), axis=1, keepdims=True)
    rest = jnp.where(in_grp & (lane != i1), eprob, -1.0)
    v2 = jnp.max(rest, axis=1, keepdims=True)
    i2 = jnp.min(jnp.where(rest == v2, lane, big), axis=1, keepdims=True)
    tot = v1 + v2
    w1 = g_w * (v1 / tot)
    w2 = g_w * (v2 / tot)
    e1 = i1 - N_EXPERT_GROUPS
    e2 = i2 - N_EXPERT_GROUPS
    rec = jnp.where(lane == RT_E1, e1, jnp.where(lane == RT_E2, e2, jnp.where(lane == RT_W1, w1, w2)))
    return jnp.where(lane <= RT_W2, rec, 0.0)


def _merge_kernel(o1_ref, o2_ref, o3_ref, l1_ref, l2_ref, l3_ref, ym_ref, g_ref, x_ref, mod_ref,
                  wpa_ref, wpm_ref, wout_ref, lng_ref, lnb_ref, wr_ref, br_ref,
                  x1_ref, u2_ref, rt_ref):
    l1 = l1_ref[...]
    l2 = l2_ref[...]
    l3 = l3_ref[...]
    mx = jnp.maximum(jnp.maximum(l1, l2), l3)
    e1 = jnp.exp(l1 - mx)
    e2 = jnp.exp(l2 - mx)
    e3 = jnp.exp(l3 - mx)
    den = e1 + e2 + e3
    ya = ((e1 / den) * o1_ref[...] + (e2 / den) * o2_ref[...] + (e3 / den) * o3_ref[...]).astype(BF16)
    pa = jnp.dot(ya, wpa_ref[...], preferred_element_type=F32)
    pm = jnp.dot(ym_ref[...], wpm_ref[...], preferred_element_type=F32)
    merged = g_ref[:, :D_MODEL].astype(F32) * pa + g_ref[:, D_MODEL:].astype(F32) * pm
    mix = jnp.dot(merged.astype(BF16), wout_ref[...], preferred_element_type=F32)
    z = DEEPNORM_ALPHA * x_ref[...] + mod_ref[2:3, :] * mix
    x1 = _normalize(z) * lng_ref[...] + lnb_ref[...]
    x1_ref[...] = x1
    u2 = _normalize(x1) * (1.0 + mod_ref[4:5, :]) + mod_ref[3:4, :]
    u2_ref[...] = u2
    logits = jnp.dot(u2.astype(BF16), wr_ref[...], preferred_element_type=F32) + br_ref[...]
    rt_ref[...] = _route(logits)


def _merge(o_list, l_list, ym, g, x2, mod, wpa, wpm, wout, lng, lnb, wr, br):
    tm = MERGE_TM
    tiles_per_batch = SEQ // tm
    rowblk = lambda w: pl.BlockSpec((tm, w), lambda m: (m, 0))
    const = lambda shape: pl.BlockSpec(shape, lambda m: (0,) * len(shape), pipeline_mode=pl.Buffered(1))
    return pl.pallas_call(
        _merge_kernel,
        out_shape=(
            jax.ShapeDtypeStruct((TOKENS, D_MODEL), F32),
            jax.ShapeDtypeStruct((TOKENS, D_MODEL), F32),
            jax.ShapeDtypeStruct((TOKENS, LANES), F32),
        ),
        grid=(TOKENS // tm,),
        in_specs=[
            rowblk(A_GROUP_W), rowblk(A_GROUP_W), rowblk(A_GROUP_W),
            rowblk(A_GROUP_W), rowblk(A_GROUP_W), rowblk(A_GROUP_W),
            rowblk(M_W), rowblk(2 * D_MODEL), rowblk(D_MODEL),
            pl.BlockSpec((None, 6, D_MODEL), lambda m: (m // tiles_per_batch, 0, 0)),
            const((A_GROUP_W, D_MODEL)), const((M_W, D_MODEL)), const((D_MODEL, D_MODEL)),
            const((1, D_MODEL)), const((1, D_MODEL)),
            const((D_MODEL, LANES)), const((1, LANES)),
        ],
        out_specs=(rowblk(D_MODEL), rowblk(D_MODEL), rowblk(LANES)),
        compiler_params=_cparams(("arbitrary",)),
        name="merge_ln1_route",
    )(*o_list, *l_list, ym, g, x2, mod, wpa, wpm, wout, lng, lnb, wr, br)


def _gather_kernel(idx_ref, nu_ref, src_ref, out_ref, buf, sem):
    i = pl.program_id(0)
    tg = buf.shape[0]

    @pl.when(i < nu_ref[0])
    def _():
        base = i * tg

        def issue(r, carry):
            row = idx_ref[base + r]
            pltpu.make_async_copy(src_ref.at[pl.ds(row, 1)], buf.at[pl.ds(r, 1)], sem).start()
            return carry

        lax.fori_loop(0, tg, issue, 0)
        pltpu.make_async_copy(src_ref.at[pl.ds(0, tg)], buf, sem).wait()
        out_ref[...] = buf[...].astype(out_ref.dtype)

    @pl.when(i >= nu_ref[0])
    def _():
        out_ref[...] = jnp.zeros_like(out_ref)


def _gather_rows(row_token, n_used, src):
    tg = MOE_TM
    return pl.pallas_call(
        _gather_kernel,
        out_shape=jax.ShapeDtypeStruct((MOE_ROWS, D_MODEL), BF16),
        grid_spec=pltpu.PrefetchScalarGridSpec(
            num_scalar_prefetch=2,
            grid=(MOE_TILES,),
            in_specs=[pl.BlockSpec(memory_space=pl.ANY)],
            out_specs=pl.BlockSpec((tg, D_MODEL), lambda i, idx, nu: (i, 0)),
            scratch_shapes=[pltpu.VMEM((tg, D_MODEL), F32), pltpu.SemaphoreType.DMA],
        ),
        compiler_params=_cparams(("arbitrary",)),
        name="moe_gather",
    )(row_token, n_used, src)


def _expert_changed(te_ref, i):
    return (i == 0) | (te_ref[i] != te_ref[jnp.maximum(i - 1, 0)])


def _moe_up_kernel(te_ref, nu_ref, x_ref, wg_ref, wu_ref, h_ref, wgb, wub):
    i = pl.program_id(1)

    @pl.when(i < nu_ref[0])
    def _():
        @pl.when(_expert_changed(te_ref, i))
        def _():
            wgb[...] = wg_ref[...].astype(BF16)
            wub[...] = wu_ref[...].astype(BF16)

        x = x_ref[...]
        a = jnp.dot(x, wgb[...], preferred_element_type=F32)
        b = jnp.dot(x, wub[...], preferred_element_type=F32)
        h_ref[...] = (_silu(a) * b).astype(BF16)

    @pl.when(i >= nu_ref[0])
    def _():
        h_ref[...] = jnp.zeros_like(h_ref)


def _moe_up(te, n_used, xs, w_eg, w_eu):
    tm = MOE_TM
    row = lambda f, i, te, nu: (jnp.minimum(i, nu[0] - 1), 0)
    wmap = lambda f, i, te, nu: (te[i], 0, f)
    return pl.pallas_call(
        _moe_up_kernel,
        out_shape=jax.ShapeDtypeStruct((MOE_ROWS, D_FF_EXPERT), BF16),
        grid_spec=pltpu.PrefetchScalarGridSpec(
            num_scalar_prefetch=2,
            grid=(D_FF_EXPERT // MOE_TF, MOE_TILES),
            in_specs=[
                pl.BlockSpec((tm, D_MODEL), row),
                pl.BlockSpec((None, D_MODEL, MOE_TF), wmap),
                pl.BlockSpec((None, D_MODEL, MOE_TF), wmap),
            ],
            out_specs=pl.BlockSpec((tm, MOE_TF), lambda f, i, te, nu: (i, f)),
            scratch_shapes=[pltpu.VMEM((D_MODEL, MOE_TF), BF16)] * 2,
        ),
        compiler_params=_cparams(("arbitrary", "arbitrary")),
        name="moe_up",
    )(te, n_used, xs, w_eg, w_eu)


def _moe_down_kernel(te_ref, nu_ref, h_ref, wd_ref, rw_ref, o_ref, wdb):
    i = pl.program_id(1)

    @pl.when(i < nu_ref[0])
    def _():
        @pl.when(_expert_changed(te_ref, i))
        def _():
            wdb[...] = wd_ref[...].astype(BF16)

        o_ref[...] = jnp.dot(h_ref[...], wdb[...], preferred_element_type=F32) * rw_ref[...]

    @pl.when(i >= nu_ref[0])
    def _():
        o_ref[...] = jnp.zeros_like(o_ref)


def _moe_down(te, n_used, hs, w_ed, row_w):
    tm = MOE_TM
    row = lambda d, i, te, nu: (jnp.minimum(i, nu[0] - 1), 0)
    return pl.pallas_call(
        _moe_down_kernel,
        out_shape=jax.ShapeDtypeStruct((MOE_ROWS, D_MODEL), F32),
        grid_spec=pltpu.PrefetchScalarGridSpec(
            num_scalar_prefetch=2,
            grid=(D_MODEL // MOE_TD, MOE_TILES),
            in_specs=[
                pl.BlockSpec((tm, D_FF_EXPERT), row),
                pl.BlockSpec((None, D_FF_EXPERT, MOE_TD), lambda d, i, te, nu: (te[i], 0, d)),
                pl.BlockSpec((tm, 1), row),
            ],
            out_specs=pl.BlockSpec((tm, MOE_TD), lambda d, i, te, nu: (i, d)),
            scratch_shapes=[pltpu.VMEM((D_FF_EXPERT, MOE_TD), BF16)],
        ),
        compiler_params=_cparams(("arbitrary", "arbitrary")),
        name="moe_down",
    )(te, n_used, hs, w_ed, row_w)


def _final_kernel(pos_ref, o_ref, x1_ref, mod_ref, lng_ref, lnb_ref, y_ref, buf, sem):
    i = pl.program_id(0)
    tb = x1_ref.shape[0]
    base = i * (2 * tb)

    def issue(r, carry):
        row = pos_ref[base + r]
        pltpu.make_async_copy(o_ref.at[pl.ds(row, 1)], buf.at[pl.ds(r, 1)], sem).start()
        return carry

    lax.fori_loop(0, 2 * tb, issue, 0)
    pltpu.make_async_copy(o_ref.at[pl.ds(0, 2 * tb)], buf, sem).wait()
    ffn = buf[0:tb, :] + buf[tb:2 * tb, :]
    z = DEEPNORM_ALPHA * x1_ref[...] + mod_ref[5:6, :] * ffn
    y_ref[...] = _normalize(z) * lng_ref[...] + lnb_ref[...]


def _final(pos_tiles, moe_out, x1, mod, lng, lnb):
    tb = FINAL_TB
    tiles_per_batch = SEQ // tb
    return pl.pallas_call(
        _final_kernel,
        out_shape=jax.ShapeDtypeStruct((TOKENS, D_MODEL), F32),
        grid_spec=pltpu.PrefetchScalarGridSpec(
            num_scalar_prefetch=1,
            grid=(TOKENS // tb,),
            in_specs=[
                pl.BlockSpec(memory_space=pl.ANY),
                pl.BlockSpec((tb, D_MODEL), lambda i, pos: (i, 0)),
                pl.BlockSpec((None, 6, D_MODEL), lambda i, pos: (i // tiles_per_batch, 0, 0)),
                pl.BlockSpec((1, D_MODEL), lambda i, pos: (0, 0)),
                pl.BlockSpec((1, D_MODEL), lambda i, pos: (0, 0)),
            ],
            out_specs=pl.BlockSpec((tb, D_MODEL), lambda i, pos: (i, 0)),
            scratch_shapes=[pltpu.VMEM((2 * tb, D_MODEL), F32), pltpu.SemaphoreType.DMA],
        ),
        compiler_params=_cparams(("arbitrary",)),
        name="combine_ln2",
    )(pos_tiles, moe_out, x1, mod, lng, lnb)


def _routing_tables(e_ids, e_w):
    tm = MOE_TM
    e_flat = e_ids.reshape(-1)
    w_flat = e_w.reshape(-1)
    onehot = (e_flat[:, None] == jnp.arange(N_EXPERTS, dtype=jnp.int32)[None, :]).astype(jnp.int32)
    csum = jnp.cumsum(onehot, axis=0)
    rank = jnp.sum((csum - onehot) * onehot, axis=1)
    counts = csum[-1]
    padded = ((counts + tm - 1) // tm) * tm
    pend = jnp.cumsum(padded)
    pstart = pend - padded
    dest = pstart[e_flat] + rank
    tok = jnp.arange(2 * TOKENS, dtype=jnp.int32) // 2
    row_token = jnp.zeros((MOE_ROWS,), jnp.int32).at[dest].set(tok)
    row_w = jnp.zeros((MOE_ROWS,), F32).at[dest].set(w_flat)
    n_used = (pend[-1] // tm).astype(jnp.int32)
    tile_start = jnp.arange(MOE_TILES, dtype=jnp.int32) * tm
    te = jnp.sum((pend[None, :] <= tile_start[:, None]).astype(jnp.int32), axis=1)
    te = jnp.minimum(te, N_EXPERTS - 1)
    te_last = te[jnp.maximum(n_used - 1, 0)]
    te = jnp.where(jnp.arange(MOE_TILES) < n_used, te, te_last)
    pos = dest.reshape(TOKENS, 2)
    return row_token, row_w.reshape(MOE_ROWS, 1), te, n_used.reshape(1), pos


def _rope_tables():
    inv = ROPE_THETA ** (-jnp.arange(0, HEAD_DIM_A, 2, dtype=F32) / HEAD_DIM_A)
    ang = jnp.arange(SEQ, dtype=F32)[:, None] * inv[None, :]
    cos = jnp.cos(ang)
    sin = jnp.sin(ang)
    return jnp.concatenate([cos, cos], axis=-1), jnp.concatenate([-sin, sin], axis=-1)


def kernel(x, c, w_ada, b_ada, w_in, b_mgate, conv_w, conv_b, m_norm_g, w_proj_a, w_proj_m, w_gate, b_gate,
           w_out, ln1_g, ln1_b, w_rg, b_rg, w_re, b_re, w_eg, w_eu, w_ed, ln2_g, ln2_b):
    assert x.shape == (BATCH, SEQ, D_MODEL) and w_ada.shape[0] == 1
    l = 0
    x2 = x.reshape(TOKENS, D_MODEL)

    c_pad = jnp.zeros((ADA_ROWS, D_MODEL), F32).at[:BATCH].set(c)
    mod = _ada(c_pad, w_ada[l], b_ada[l][None, :])[:BATCH].reshape(BATCH, 6, D_MODEL)

    w_if = jnp.zeros((D_MODEL, LANES), F32).at[:, :2 * M_HEADS].set(w_in[l][:, N_IN_MAIN:])
    p, gates, u = _inproj(x2, mod, w_in[l], w_if)
    g = _gateproj(u, w_gate[l], b_gate[l][None, :])

    cos_t, sin_t = _rope_tables()
    p3 = p.reshape(BATCH, SEQ, P_WIDTH)
    o_list, l_list = [], []
    for gi, hp in enumerate((1, 2, 4)):
        o, lse = _attention_group(p3, cos_t, sin_t, gi, hp)
        o_list.append(o)
        l_list.append(lse)

    gcol = gates[:, :2 * M_HEADS].reshape(BATCH, SEQ, 2 * M_HEADS)
    grow = jnp.transpose(gcol, (0, 2, 1))
    ym = _mlstm(p, gcol, grow, b_mgate[l][None, :], b_mgate[l][:, None], conv_w[l], conv_b[l][None, :],
                m_norm_g[l][None, :])

    wr = (jnp.zeros((D_MODEL, LANES), F32)
          .at[:, :N_EXPERT_GROUPS].set(w_rg[l])
          .at[:, N_EXPERT_GROUPS:N_EXPERT_GROUPS + N_EXPERTS].set(w_re[l])).astype(BF16)
    br = (jnp.zeros((1, LANES), F32)
          .at[0, :N_EXPERT_GROUPS].set(b_rg[l])
          .at[0, N_EXPERT_GROUPS:N_EXPERT_GROUPS + N_EXPERTS].set(b_re[l]))
    x1, u2, rt = _merge(o_list, l_list, ym, g, x2, mod,
                        w_proj_a[l].astype(BF16), w_proj_m[l].astype(BF16), w_out[l].astype(BF16),
                        ln1_g[l][None, :], ln1_b[l][None, :], wr, br)

    e_ids = rt[:, RT_E1:RT_E2 + 1].astype(jnp.int32)
    e_w = rt[:, RT_W1:RT_W2 + 1]
    row_token, row_w, te, n_used, pos = _routing_tables(e_ids, e_w)

    xs = _gather_rows(row_token, n_used, u2)
    hs = _moe_up(te, n_used, xs,
                 w_eg[l].reshape(N_EXPERTS, D_MODEL, D_FF_EXPERT), w_eu[l].reshape(N_EXPERTS, D_MODEL, D_FF_EXPERT))
    mo = _moe_down(te, n_used, hs, w_ed[l].reshape(N_EXPERTS, D_FF_EXPERT, D_MODEL), row_w)

    nt = TOKENS // FINAL_TB
    pos_tiles = jnp.transpose(pos.reshape(nt, FINAL_TB, 2), (0, 2, 1)).reshape(-1)
    y = _final(pos_tiles, mo, x1, mod, ln2_g[l][None, :], ln2_b[l][None, :])
    return y.reshape(BATCH, SEQ, D_MODEL)
```

```python
import functools

import jax
import jax.numpy as jnp
from jax import lax
from jax.experimental import pallas as pl
from jax.experimental.pallas import tpu as pltpu

F32 = jnp.float32
BF16 = jnp.bfloat16

D_MODEL = 2048
BATCH = 4
SEQ = 2048
TOKENS = BATCH * SEQ
DIL_CONFIGS = ((128, 1), (512, 4), (2048, 16))
N_DIL_GROUPS = 3
HEADS_PER_GROUP = 4
HEAD_DIM_A = 128
ATT_BLOCK = 128
ROPE_THETA = 10000.0
A_GROUP_W = HEADS_PER_GROUP * HEAD_DIM_A
A_QKV_W = N_DIL_GROUPS * A_GROUP_W
M_HEADS = 4
M_HEAD_DIM = 256
M_W = M_HEADS * M_HEAD_DIM
M_CHUNK = 128
CONV_K = 4
N_IN_MAIN = 3 * A_QKV_W + 4 * M_W
N_EXPERT_GROUPS = 4
EXPERTS_PER_GROUP = 8
N_EXPERTS = N_EXPERT_GROUPS * EXPERTS_PER_GROUP
D_FF_EXPERT = 1024
DEEPNORM_ALPHA = 2.0 ** 0.25
LN_EPS = 1e-5

LANES = 128
SUBLANES = 8
VMEM_LIMIT_BYTES = 56 * 1024 * 1024

PROJ_TN = 512
PROJ_TM = 1024
P_TILES_IN = N_IN_MAIN // PROJ_TN
P_ATT_TILES = 3 * A_QKV_W // PROJ_TN
P_M_TILES = P_TILES_IN - P_ATT_TILES
P_WIDTH = N_IN_MAIN
P_ATT_OFF = P_M_TILES * PROJ_TN
P_QM_BLK, P_KM_BLK, P_VM_BLK, P_OM_BLK = 0, 1, 2, 3
MERGE_TM = 256
MOE_TM = 256
MOE_ROWS = 2 * TOKENS + N_EXPERTS * MOE_TM
MOE_TILES = MOE_ROWS // MOE_TM
MOE_TF = 512
FINAL_TB = 256
ROW_CHUNKS = D_MODEL // LANES


def _cparams(sem, vmem=VMEM_LIMIT_BYTES):
    return pltpu.CompilerParams(dimension_semantics=sem, vmem_limit_bytes=vmem)


def _normalize(x):
    mu = jnp.mean(x, axis=-1, keepdims=True)
    xc = x - mu
    var = jnp.mean(xc * xc, axis=-1, keepdims=True)
    return xc * lax.rsqrt(var + LN_EPS)


def _silu(x):
    return x * jax.nn.sigmoid(x)


def _log_sigmoid(x):
    return jnp.minimum(x, 0.0) - jnp.log(1.0 + jnp.exp(-jnp.abs(x)))


ADA_TN = 1024
ADA_ROWS = 16


def _ada_kernel(c_ref, w_ref, b_ref, o_ref):
    sc = _silu(c_ref[...]).astype(BF16)
    o_ref[...] = jnp.dot(sc, w_ref[...].astype(BF16), preferred_element_type=F32) + b_ref[...]


def _ada(c_pad, w_ada, b_ada):
    n = w_ada.shape[1]
    return pl.pallas_call(
        _ada_kernel,
        out_shape=jax.ShapeDtypeStruct((ADA_ROWS, n), F32),
        grid=(n // ADA_TN,),
        in_specs=[
            pl.BlockSpec((ADA_ROWS, D_MODEL), lambda j: (0, 0)),
            pl.BlockSpec((D_MODEL, ADA_TN), lambda j: (0, j)),
            pl.BlockSpec((1, ADA_TN), lambda j: (0, j)),
        ],
        out_specs=pl.BlockSpec((ADA_ROWS, ADA_TN), lambda j: (0, j)),
        compiler_params=_cparams(("arbitrary",)),
        name="ada_mod",
    )(c_pad, w_ada, b_ada)


LN_CHUNK = 256


def _inproj_kernel(x_ref, mod_ref, w_ref, wif_ref, p_ref, g_ref, u_ref):
    n = pl.program_id(1)

    @pl.when(n == 0)
    def _():
        shift = mod_ref[0:1, :]
        scale = 1.0 + mod_ref[1:2, :]
        wif = wif_ref[...].astype(BF16)

        def body(ci, carry):
            r = pl.multiple_of(ci * LN_CHUNK, LN_CHUNK)
            u = (_normalize(x_ref[pl.ds(r, LN_CHUNK), :]) * scale + shift).astype(BF16)
            u_ref[pl.ds(r, LN_CHUNK), :] = u
            g_ref[pl.ds(r, LN_CHUNK), :] = jnp.dot(u, wif, preferred_element_type=F32)
            return carry

        lax.fori_loop(0, PROJ_TM // LN_CHUNK, body, 0)

    p_ref[...] = jnp.dot(u_ref[...], w_ref[...].astype(BF16), preferred_element_type=F32)


def _inproj(x2, mod, w_in, w_if):
    tiles_per_batch = SEQ // PROJ_TM
    return pl.pallas_call(
        _inproj_kernel,
        out_shape=(
            jax.ShapeDtypeStruct((TOKENS, P_WIDTH), F32),
            jax.ShapeDtypeStruct((TOKENS, LANES), F32),
            jax.ShapeDtypeStruct((TOKENS, D_MODEL), BF16),
        ),
        grid=(TOKENS // PROJ_TM, P_TILES_IN),
        in_specs=[
            pl.BlockSpec((PROJ_TM, D_MODEL), lambda m, n: (m, 0)),
            pl.BlockSpec((None, 6, D_MODEL), lambda m, n: (m // tiles_per_batch, 0, 0)),
            pl.BlockSpec((D_MODEL, PROJ_TN), lambda m, n: (0, n)),
            pl.BlockSpec((D_MODEL, LANES), lambda m, n: (0, 0)),
        ],
        out_specs=(
            pl.BlockSpec((PROJ_TM, PROJ_TN),
                         lambda m, n: (m, jnp.where(n < P_ATT_TILES, n + P_M_TILES, n - P_ATT_TILES))),
            pl.BlockSpec((PROJ_TM, LANES), lambda m, n: (m, 0)),
            pl.BlockSpec((PROJ_TM, D_MODEL), lambda m, n: (m, 0)),
        ),
        compiler_params=_cparams(("arbitrary", "arbitrary")),
        name="in_proj",
    )(x2, mod, w_in, w_if)


def _gateproj_kernel(u_ref, w_ref, b_ref, o_ref):
    acc = jnp.dot(u_ref[...], w_ref[...].astype(BF16), preferred_element_type=F32)
    o_ref[...] = jax.nn.sigmoid(acc + b_ref[...]).astype(BF16)


def _gateproj(u, w_gate, b_gate):
    n = w_gate.shape[1]
    return pl.pallas_call(
        _gateproj_kernel,
        out_shape=jax.ShapeDtypeStruct((TOKENS, n), BF16),
        grid=(TOKENS // PROJ_TM, n // PROJ_TN),
        in_specs=[
            pl.BlockSpec((PROJ_TM, D_MODEL), lambda m, j: (m, 0)),
            pl.BlockSpec((D_MODEL, PROJ_TN), lambda m, j: (0, j)),
            pl.BlockSpec((1, PROJ_TN), lambda m, j: (0, j)),
        ],
        out_specs=pl.BlockSpec((PROJ_TM, PROJ_TN), lambda m, j: (m, j)),
        compiler_params=_cparams(("arbitrary", "arbitrary")),
        name="gate_proj",
    )(u, w_gate, b_gate)


def _rows(start, size, stride):
    return pl.ds(start, size) if stride == 1 else pl.ds(start, size, stride=stride)


def _attn_kernel(q0, k0, v0, q1, k1, v1, q2, k2, v2, cos_ref, sin_ref, y_ref,
                 qr_sc, kr_sc, o0, o1, o2, l0, l1, l2):
    cos = cos_ref[...]
    sin = sin_ref[...]
    scale = HEAD_DIM_A ** -0.5
    blk = ATT_BLOCK
    qi2 = lax.broadcasted_iota(jnp.int32, (blk, 2 * blk), 0)
    kc2 = lax.broadcasted_iota(jnp.int32, (blk, 2 * blk), 1)
    mask_prev_cur = (kc2 >= qi2) & (kc2 <= qi2 + blk)
    qi1 = lax.broadcasted_iota(jnp.int32, (blk, blk), 0)
    kc1 = lax.broadcasted_iota(jnp.int32, (blk, blk), 1)
    mask_cur = kc1 <= qi1
    half = HEAD_DIM_A // 2
    groups = ((q0, k0, v0, o0, l0), (q1, k1, v1, o1, l1), (q2, k2, v2, o2, l2))
    for (window, d), (q_ref, k_ref, v_ref, o_sc, l_sc) in zip(DIL_CONFIGS, groups):
        nb = SEQ // d // blk
        q = q_ref[...]
        k = k_ref[...]
        qr_sc[...] = q * cos + pltpu.roll(q, half, 1) * sin
        kr_sc[...] = k * cos + pltpu.roll(k, half, 1) * sin
        for r in range(d):
            for j in range(nb):
                start = r + j * blk * d
                cur = _rows(start, blk, d)
                qb = qr_sc[cur, :].astype(BF16)
                if j == 0:
                    keys = cur
                    mask = mask_cur
                else:
                    keys = _rows(start - blk * d, 2 * blk, d)
                    mask = mask_prev_cur
                kw = kr_sc[keys, :].astype(BF16)
                vw = v_ref[keys, :].astype(BF16)
                s = lax.dot_general(qb, kw, (((1,), (1,)), ((), ())), preferred_element_type=F32) * scale
                s = jnp.where(mask, s, -jnp.inf)
                mx = jnp.max(s, axis=-1, keepdims=True)
                p = jnp.exp(s - mx)
                den = jnp.sum(p, axis=-1, keepdims=True)
                o_sc[cur, :] = jnp.dot(p.astype(BF16), vw, preferred_element_type=F32) / den
                l_sc[cur, :] = jnp.broadcast_to(mx + jnp.log(den), (blk, HEAD_DIM_A))
    la = l0[...]
    lb = l1[...]
    lc = l2[...]
    mx = jnp.maximum(jnp.maximum(la, lb), lc)
    ea = jnp.exp(la - mx)
    eb = jnp.exp(lb - mx)
    ec = jnp.exp(lc - mx)
    den = ea + eb + ec
    y_ref[...] = ((ea / den) * o0[...] + (eb / den) * o1[...] + (ec / den) * o2[...]).astype(BF16)


def _attention(p, cos_t, sin_t):
    for window, d in DIL_CONFIGS:
        assert window // d == ATT_BLOCK and SEQ % (d * ATT_BLOCK) == 0
    col0 = P_ATT_OFF // HEAD_DIM_A

    def slab(part, g):
        off = col0 + (part * A_QKV_W + g * A_GROUP_W) // HEAD_DIM_A
        return pl.BlockSpec((SEQ, HEAD_DIM_A), lambda b, h: (b, off + h))

    in_specs = [slab(part, g) for g in range(N_DIL_GROUPS) for part in range(3)]
    table = pl.BlockSpec((SEQ, HEAD_DIM_A), lambda b, h: (0, 0))
    return pl.pallas_call(
        _attn_kernel,
        out_shape=jax.ShapeDtypeStruct((TOKENS, A_GROUP_W), BF16),
        grid=(BATCH, HEADS_PER_GROUP),
        in_specs=in_specs + [table, table],
        out_specs=pl.BlockSpec((SEQ, HEAD_DIM_A), lambda b, h: (b, h)),
        scratch_shapes=[pltpu.VMEM((SEQ, HEAD_DIM_A), F32)] * 8,
        compiler_params=_cparams(("arbitrary", "arbitrary")),
        name="dil_attn",
    )(*([p] * 9), cos_t, sin_t)


CONV_HALO = SUBLANES


def _mlstm_kernel(q_ref, k_ref, v_ref, og_ref, gc_ref, gr_ref, bc_ref, br_ref, cw_ref, cb_ref, ng_ref,
                  y_ref, xq_sc, xk_sc, ct_sc, n_sc, m_sc):
    c = pl.program_id(1)
    L = M_CHUNK
    lo = CONV_HALO

    @pl.when(c == 0)
    def _():
        xq_sc[0:lo, :] = jnp.zeros((lo, M_W), F32)
        xk_sc[0:lo, :] = jnp.zeros((lo, M_W), F32)
        ct_sc[...] = jnp.zeros_like(ct_sc)
        n_sc[...] = jnp.zeros_like(n_sc)
        m_sc[...] = jnp.zeros_like(m_sc)

    xq_sc[lo:lo + L, :] = q_ref[...]
    xk_sc[lo:lo + L, :] = k_ref[...]

    def conv(x_sc, w, b):
        acc = x_sc[lo:lo + L, :] * w[CONV_K - 1:CONV_K, :] + b
        for j in range(CONV_K - 1):
            off = lo - (CONV_K - 1) + j
            acc = acc + x_sc[off:off + L, :] * w[j:j + 1, :]
        return acc

    cw = cw_ref[...]
    cb = cb_ref[...]
    qc = _silu(conv(xq_sc, cw[:, :M_W], cb[:, :M_W]))
    kc = _silu(conv(xk_sc, cw[:, M_W:], cb[:, M_W:])) * (M_HEAD_DIM ** -0.5)
    xq_sc[0:lo, :] = xq_sc[L:L + lo, :]
    xk_sc[0:lo, :] = xk_sc[L:L + lo, :]

    row = lax.broadcasted_iota(jnp.int32, (L, L), 0)
    col = lax.broadcasted_iota(jnp.int32, (L, L), 1)
    causal = row >= col
    for h in range(M_HEADS):
        hs = slice(h * M_HEAD_DIM, (h + 1) * M_HEAD_DIM)
        q = qc[:, hs]
        k = kc[:, hs]
        i_col = gc_ref[:, h:h + 1] + bc_ref[:, h:h + 1]
        lf_col = _log_sigmoid(gc_ref[:, M_HEADS + h:M_HEADS + h + 1] + bc_ref[:, M_HEADS + h:M_HEADS + h + 1])
        i_row = gr_ref[h:h + 1, :] + br_ref[h:h + 1, :]
        lf_row = _log_sigmoid(gr_ref[M_HEADS + h:M_HEADS + h + 1, :] + br_ref[M_HEADS + h:M_HEADS + h + 1, :])
        b_col = jnp.sum(jnp.where(causal, lf_row, 0.0), axis=1, keepdims=True)
        b_row = jnp.sum(jnp.where(row <= col, lf_col, 0.0), axis=0, keepdims=True)
        log_d = jnp.where(causal, b_col - b_row + i_row, -jnp.inf)
        m_prev = m_sc[h, 0:1, 0:1]
        log_inter = b_col + m_prev
        m_t = jnp.maximum(jnp.max(log_d, axis=1, keepdims=True), log_inter)
        qb = q.astype(BF16)
        kb = k.astype(BF16)
        vb = v_ref[:, hs].astype(BF16)
        s = lax.dot_general(qb, kb, (((1,), (1,)), ((), ())), preferred_element_type=F32) * jnp.exp(log_d - m_t)
        inter = jnp.exp(log_inter - m_t)
        ct = ct_sc[h]
        n_row = n_sc[h, 0:1, :]
        num = (jnp.dot(s.astype(BF16), vb, preferred_element_type=F32)
               + inter * jnp.dot(qb, ct.astype(BF16), preferred_element_type=F32))
        den = jnp.sum(s, axis=1, keepdims=True) + inter * jnp.sum(q * n_row, axis=1, keepdims=True)
        hh = num / jnp.maximum(jnp.abs(den), jnp.exp(-m_t))
        m_new = m_t[L - 1:L, :]
        b_last = b_col[L - 1:L, :]
        w_col = jnp.exp(b_last - b_col + i_col - m_new)
        decay = jnp.exp(b_last + m_prev - m_new)
        kw = k * w_col
        ct_sc[h] = decay * ct + lax.dot_general(kw.astype(BF16), vb, (((0,), (0,)), ((), ())),
                                                preferred_element_type=F32)
        n_sc[h] = jnp.broadcast_to(decay * n_row + jnp.sum(kw, axis=0, keepdims=True), (SUBLANES, M_HEAD_DIM))
        m_sc[h] = jnp.broadcast_to(m_new, (SUBLANES, LANES))
        z = jax.nn.sigmoid(og_ref[:, hs]) * hh
        y_ref[:, hs] = (_normalize(z) * ng_ref[:, hs]).astype(BF16)


def _mlstm(p, gcol, grow, bcol, brow, conv_w, conv_b, norm_g):
    nc = SEQ // M_CHUNK
    slab = lambda blk: pl.BlockSpec((M_CHUNK, M_W), lambda b, c: (b * nc + c, blk))
    return pl.pallas_call(
        _mlstm_kernel,
        out_shape=jax.ShapeDtypeStruct((TOKENS, M_W), BF16),
        grid=(BATCH, nc),
        in_specs=[
            slab(P_QM_BLK), slab(P_KM_BLK), slab(P_VM_BLK), slab(P_OM_BLK),
            pl.BlockSpec((None, M_CHUNK, 2 * M_HEADS), lambda b, c: (b, c, 0)),
            pl.BlockSpec((None, 2 * M_HEADS, M_CHUNK), lambda b, c: (b, 0, c)),
            pl.BlockSpec((1, 2 * M_HEADS), lambda b, c: (0, 0)),
            pl.BlockSpec((2 * M_HEADS, 1), lambda b, c: (0, 0)),
            pl.BlockSpec((CONV_K, 2 * M_W), lambda b, c: (0, 0)),
            pl.BlockSpec((1, 2 * M_W), lambda b, c: (0, 0)),
            pl.BlockSpec((1, M_W), lambda b, c: (0, 0)),
        ],
        out_specs=pl.BlockSpec((M_CHUNK, M_W), lambda b, c: (b * nc + c, 0)),
        scratch_shapes=[
            pltpu.VMEM((M_CHUNK + CONV_HALO, M_W), F32),
            pltpu.VMEM((M_CHUNK + CONV_HALO, M_W), F32),
            pltpu.VMEM((M_HEADS, M_HEAD_DIM, M_HEAD_DIM), F32),
            pltpu.VMEM((M_HEADS, SUBLANES, M_HEAD_DIM), F32),
            pltpu.VMEM((M_HEADS, SUBLANES, LANES), F32),
        ],
        compiler_params=_cparams(("arbitrary", "arbitrary")),
        name="mlstm",
    )(p, p, p, p, gcol, grow, bcol, brow, conv_w, conv_b, norm_g)


RT_E1, RT_E2, RT_W1, RT_W2 = 0, 1, 2, 3


def _route(logits):
    lane = lax.broadcasted_iota(jnp.int32, logits.shape, 1).astype(F32)
    big = float(LANES)
    is_g = lane < N_EXPERT_GROUPS
    gl = jnp.where(is_g, logits, -jnp.inf)
    gexp = jnp.exp(gl - jnp.max(gl, axis=1, keepdims=True))
    gprob = gexp / jnp.sum(gexp, axis=1, keepdims=True)
    g_w = jnp.max(gprob, axis=1, keepdims=True)
    g_top = jnp.min(jnp.where(is_g & (gprob == g_w), lane, big), axis=1, keepdims=True)
    lo = N_EXPERT_GROUPS + EXPERTS_PER_GROUP * g_top
    in_grp = (lane >= lo) & (lane < lo + EXPERTS_PER_GROUP)
    el = jnp.where(in_grp, logits, -jnp.inf)
    eexp = jnp.exp(el - jnp.max(el, axis=1, keepdims=True))
    eprob = eexp / jnp.sum(eexp, axis=1, keepdims=True)
    v1 = jnp.max(eprob, axis=1, keepdims=True)
    i1 = jnp.min(jnp.where(in_grp & (eprob == v1), lane, big), axis=1, keepdims=True)
    rest = jnp.where(in_grp & (lane != i1), eprob, -1.0)
    v2 = jnp.max(rest, axis=1, keepdims=True)
    i2 = jnp.min(jnp.where(rest == v2, lane, big), axis=1, keepdims=True)
    tot = v1 + v2
    w1 = g_w * (v1 / tot)
    w2 = g_w * (v2 / tot)
    e1 = i1 - N_EXPERT_GROUPS
    e2 = i2 - N_EXPERT_GROUPS
    rec = jnp.where(lane == RT_E1, e1, jnp.where(lane == RT_E2, e2, jnp.where(lane == RT_W1, w1, w2)))
    return jnp.where(lane <= RT_W2, rec, 0.0)


def _merge_kernel(ya_ref, ym_ref, g_ref, x_ref, mod_ref,
                  wpa_ref, wpm_ref, wout_ref, lng_ref, lnb_ref, wr_ref, br_ref,
                  x1_ref, u2_ref, rt_ref):
    tm = x_ref.shape[0]
    pa = jnp.dot(ya_ref[...], wpa_ref[...], preferred_element_type=F32)
    pm = jnp.dot(ym_ref[...], wpm_ref[...], preferred_element_type=F32)
    merged = g_ref[:, :D_MODEL].astype(F32) * pa + g_ref[:, D_MODEL:].astype(F32) * pm
    mix = jnp.dot(merged.astype(BF16), wout_ref[...], preferred_element_type=F32)
    z = DEEPNORM_ALPHA * x_ref[...] + mod_ref[2:3, :] * mix
    x1 = _normalize(z) * lng_ref[...] + lnb_ref[...]
    x1_ref[...] = x1
    u2 = _normalize(x1) * (1.0 + mod_ref[4:5, :]) + mod_ref[3:4, :]
    for kc in range(ROW_CHUNKS):
        u2_ref[pl.ds(kc, tm, stride=ROW_CHUNKS), :] = u2[:, kc * LANES:(kc + 1) * LANES]
    logits = jnp.dot(u2.astype(BF16), wr_ref[...], preferred_element_type=F32) + br_ref[...]
    rt_ref[...] = _route(logits)


def _merge(ya, ym, g, x2, mod, wpa, wpm, wout, lng, lnb, wr, br):
    tm = MERGE_TM
    tiles_per_batch = SEQ // tm
    rowblk = lambda w: pl.BlockSpec((tm, w), lambda m: (m, 0))
    const = lambda shape: pl.BlockSpec(shape, lambda m: (0,) * len(shape), pipeline_mode=pl.Buffered(1))
    return pl.pallas_call(
        _merge_kernel,
        out_shape=(
            jax.ShapeDtypeStruct((TOKENS, D_MODEL), F32),
            jax.ShapeDtypeStruct((TOKENS * ROW_CHUNKS, LANES), F32),
            jax.ShapeDtypeStruct((TOKENS, LANES), F32),
        ),
        grid=(TOKENS // tm,),
        in_specs=[
            rowblk(A_GROUP_W), rowblk(M_W), rowblk(2 * D_MODEL), rowblk(D_MODEL),
            pl.BlockSpec((None, 6, D_MODEL), lambda m: (m // tiles_per_batch, 0, 0)),
            const((A_GROUP_W, D_MODEL)), const((M_W, D_MODEL)), const((D_MODEL, D_MODEL)),
            const((1, D_MODEL)), const((1, D_MODEL)),
            const((D_MODEL, LANES)), const((1, LANES)),
        ],
        out_specs=(rowblk(D_MODEL), pl.BlockSpec((tm * ROW_CHUNKS, LANES), lambda m: (m, 0)), rowblk(LANES)),
        compiler_params=_cparams(("arbitrary",)),
        name="merge_ln1_route",
    )(ya, ym, g, x2, mod, wpa, wpm, wout, lng, lnb, wr, br)


def _row_copy(src_ref, row, buf, slot, sem):
    src = src_ref.at[pl.ds(pl.multiple_of(row * ROW_CHUNKS, ROW_CHUNKS), ROW_CHUNKS)]
    dst = buf.at[pl.ds(pl.multiple_of(slot * ROW_CHUNKS, ROW_CHUNKS), ROW_CHUNKS)]
    return pltpu.make_async_copy(src, dst, sem)


def _gather_kernel(idx_ref, nu_ref, src_ref, out_ref, buf, sem):
    i = pl.program_id(0)
    tg = out_ref.shape[0]

    @pl.when(i < nu_ref[0])
    def _():
        base = i * tg

        def issue(r, carry):
            _row_copy(src_ref, idx_ref[base + r], buf, r, sem).start()
            return carry

        lax.fori_loop(0, tg, issue, 0)
        pltpu.make_async_copy(src_ref.at[pl.ds(0, tg * ROW_CHUNKS)], buf, sem).wait()
        for kc in range(ROW_CHUNKS):
            out_ref[:, kc * LANES:(kc + 1) * LANES] = buf[pl.ds(kc, tg, stride=ROW_CHUNKS), :].astype(out_ref.dtype)

    @pl.when(i >= nu_ref[0])
    def _():
        out_ref[...] = jnp.zeros_like(out_ref)


def _gather_rows(row_token, n_used, src):
    tg = MOE_TM
    return pl.pallas_call(
        _gather_kernel,
        out_shape=jax.ShapeDtypeStruct((MOE_ROWS, D_MODEL), BF16),
        grid_spec=pltpu.PrefetchScalarGridSpec(
            num_scalar_prefetch=2,
            grid=(MOE_TILES,),
            in_specs=[pl.BlockSpec(memory_space=pl.ANY)],
            out_specs=pl.BlockSpec((tg, D_MODEL), lambda i, idx, nu: (i, 0)),
            scratch_shapes=[pltpu.VMEM((tg * ROW_CHUNKS, LANES), F32), pltpu.SemaphoreType.DMA],
        ),
        compiler_params=_cparams(("arbitrary",)),
        name="moe_gather",
    )(row_token, n_used, src)


def _expert_changed(te_ref, i):
    return (i == 0) | (te_ref[i] != te_ref[jnp.maximum(i - 1, 0)])


def _moe_up_kernel(te_ref, nu_ref, x_ref, wg_ref, wu_ref, h_ref, wgb, wub):
    i = pl.program_id(1)

    @pl.when(i < nu_ref[0])
    def _():
        @pl.when(_expert_changed(te_ref, i))
        def _():
            wgb[...] = wg_ref[...].astype(BF16)
            wub[...] = wu_ref[...].astype(BF16)

        x = x_ref[...]
        a = jnp.dot(x, wgb[...], preferred_element_type=F32)
        b = jnp.dot(x, wub[...], preferred_element_type=F32)
        h_ref[...] = (_silu(a) * b).astype(BF16)

    @pl.when(i >= nu_ref[0])
    def _():
        h_ref[...] = jnp.zeros_like(h_ref)


def _moe_up(te, n_used, xs, w_eg, w_eu):
    tm = MOE_TM
    row = lambda f, i, te, nu: (jnp.minimum(i, nu[0] - 1), 0)
    wmap = lambda f, i, te, nu: (te[i], 0, f)
    return pl.pallas_call(
        _moe_up_kernel,
        out_shape=jax.ShapeDtypeStruct((MOE_ROWS, D_FF_EXPERT), BF16),
        grid_spec=pltpu.PrefetchScalarGridSpec(
            num_scalar_prefetch=2,
            grid=(D_FF_EXPERT // MOE_TF, MOE_TILES),
            in_specs=[
                pl.BlockSpec((tm, D_MODEL), row),
                pl.BlockSpec((None, D_MODEL, MOE_TF), wmap),
                pl.BlockSpec((None, D_MODEL, MOE_TF), wmap),
            ],
            out_specs=pl.BlockSpec((tm, MOE_TF), lambda f, i, te, nu: (i, f)),
            scratch_shapes=[pltpu.VMEM((D_MODEL, MOE_TF), BF16)] * 2,
        ),
        compiler_params=_cparams(("arbitrary", "arbitrary")),
        name="moe_up",
    )(te, n_used, xs, w_eg, w_eu)


def _moe_down_kernel(te_ref, nu_ref, h_ref, wd_ref, o_ref, wdb):
    i = pl.program_id(0)
    tm = h_ref.shape[0]

    @pl.when(i < nu_ref[0])
    def _():
        @pl.when(_expert_changed(te_ref, i))
        def _():
            wdb[...] = wd_ref[...].astype(BF16)

        o = jnp.dot(h_ref[...], wdb[...], preferred_element_type=F32)
        for kc in range(ROW_CHUNKS):
            o_ref[pl.ds(kc, tm, stride=ROW_CHUNKS), :] = o[:, kc * LANES:(kc + 1) * LANES]

    @pl.when(i >= nu_ref[0])
    def _():
        o_ref[...] = jnp.zeros_like(o_ref)


def _moe_down(te, n_used, hs, w_ed):
    tm = MOE_TM
    row = lambda i, te, nu: (jnp.minimum(i, nu[0] - 1), 0)
    return pl.pallas_call(
        _moe_down_kernel,
        out_shape=jax.ShapeDtypeStruct((MOE_ROWS * ROW_CHUNKS, LANES), F32),
        grid_spec=pltpu.PrefetchScalarGridSpec(
            num_scalar_prefetch=2,
            grid=(MOE_TILES,),
            in_specs=[
                pl.BlockSpec((tm, D_FF_EXPERT), row),
                pl.BlockSpec((None, D_FF_EXPERT, D_MODEL), lambda i, te, nu: (te[i], 0, 0)),
            ],
            out_specs=pl.BlockSpec((tm * ROW_CHUNKS, LANES), lambda i, te, nu: (i, 0)),
            scratch_shapes=[pltpu.VMEM((D_FF_EXPERT, D_MODEL), BF16)],
        ),
        compiler_params=_cparams(("arbitrary",)),
        name="moe_down",
    )(te, n_used, hs, w_ed)


def _final_kernel(pos_ref, o_ref, x1_ref, rt_ref, mod_ref, lng_ref, lnb_ref, y_ref, buf, sem):
    i = pl.program_id(0)
    tb = x1_ref.shape[0]
    base = i * (2 * tb)

    def issue(r, carry):
        _row_copy(o_ref, pos_ref[base + r], buf, r, sem).start()
        return carry

    lax.fori_loop(0, 2 * tb, issue, 0)
    pltpu.make_async_copy(o_ref.at[pl.ds(0, 2 * tb * ROW_CHUNKS)], buf, sem).wait()
    w1 = rt_ref[:, RT_W1:RT_W1 + 1]
    w2 = rt_ref[:, RT_W2:RT_W2 + 1]
    ffn = jnp.concatenate(
        [w1 * buf[pl.ds(kc, tb, stride=ROW_CHUNKS), :]
         + w2 * buf[pl.ds(tb * ROW_CHUNKS + kc, tb, stride=ROW_CHUNKS), :]
         for kc in range(ROW_CHUNKS)], axis=1)
    z = DEEPNORM_ALPHA * x1_ref[...] + mod_ref[5:6, :] * ffn
    y_ref[...] = _normalize(z) * lng_ref[...] + lnb_ref[...]


def _final(pos_tiles, moe_out, x1, rt, mod, lng, lnb):
    tb = FINAL_TB
    tiles_per_batch = SEQ // tb
    return pl.pallas_call(
        _final_kernel,
        out_shape=jax.ShapeDtypeStruct((TOKENS, D_MODEL), F32),
        grid_spec=pltpu.PrefetchScalarGridSpec(
            num_scalar_prefetch=1,
            grid=(TOKENS // tb,),
            in_specs=[
                pl.BlockSpec(memory_space=pl.ANY),
                pl.BlockSpec((tb, D_MODEL), lambda i, pos: (i, 0)),
                pl.BlockSpec((tb, LANES), lambda i, pos: (i, 0)),
                pl.BlockSpec((None, 6, D_MODEL), lambda i, pos: (i // tiles_per_batch, 0, 0)),
                pl.BlockSpec((1, D_MODEL), lambda i, pos: (0, 0)),
                pl.BlockSpec((1, D_MODEL), lambda i, pos: (0, 0)),
            ],
            out_specs=pl.BlockSpec((tb, D_MODEL), lambda i, pos: (i, 0)),
            scratch_shapes=[pltpu.VMEM((2 * tb * ROW_CHUNKS, LANES), F32), pltpu.SemaphoreType.DMA],
        ),
        compiler_params=_cparams(("arbitrary",)),
        name="combine_ln2",
    )(pos_tiles, moe_out, x1, rt, mod, lng, lnb)


def _routing_tables(e_ids):
    tm = MOE_TM
    e_flat = e_ids.reshape(-1)
    onehot = (e_flat[:, None] == jnp.arange(N_EXPERTS, dtype=jnp.int32)[None, :]).astype(jnp.int32)
    csum = jnp.cumsum(onehot, axis=0)
    rank = jnp.sum((csum - onehot) * onehot, axis=1)
    counts = csum[-1]
    padded = ((counts + tm - 1) // tm) * tm
    pend = jnp.cumsum(padded)
    pstart = pend - padded
    dest = pstart[e_flat] + rank
    tok = jnp.arange(2 * TOKENS, dtype=jnp.int32) // 2
    row_token = jnp.zeros((MOE_ROWS,), jnp.int32).at[dest].set(tok)
    n_used = (pend[-1] // tm).astype(jnp.int32)
    tile_start = jnp.arange(MOE_TILES, dtype=jnp.int32) * tm
    te = jnp.sum((pend[None, :] <= tile_start[:, None]).astype(jnp.int32), axis=1)
    te = jnp.minimum(te, N_EXPERTS - 1)
    te_last = te[jnp.maximum(n_used - 1, 0)]
    te = jnp.where(jnp.arange(MOE_TILES) < n_used, te, te_last)
    pos = dest.reshape(TOKENS, 2)
    return row_token, te, n_used.reshape(1), pos


def _rope_tables():
    inv = ROPE_THETA ** (-jnp.arange(0, HEAD_DIM_A, 2, dtype=F32) / HEAD_DIM_A)
    ang = jnp.arange(SEQ, dtype=F32)[:, None] * inv[None, :]
    cos = jnp.cos(ang)
    sin = jnp.sin(ang)
    return jnp.concatenate([cos, cos], axis=-1), jnp.concatenate([-sin, sin], axis=-1)


def kernel(x, c, w_ada, b_ada, w_in, b_mgate, conv_w, conv_b, m_norm_g, w_proj_a, w_proj_m, w_gate, b_gate,
           w_out, ln1_g, ln1_b, w_rg, b_rg, w_re, b_re, w_eg, w_eu, w_ed, ln2_g, ln2_b):
    assert x.shape == (BATCH, SEQ, D_MODEL) and w_ada.shape[0] == 1
    l = 0
    x2 = x.reshape(TOKENS, D_MODEL)

    c_pad = jnp.zeros((ADA_ROWS, D_MODEL), F32).at[:BATCH].set(c)
    mod = _ada(c_pad, w_ada[l], b_ada[l][None, :])[:BATCH].reshape(BATCH, 6, D_MODEL)

    w_if = jnp.zeros((D_MODEL, LANES), F32).at[:, :2 * M_HEADS].set(w_in[l][:, N_IN_MAIN:])
    p, gates, u = _inproj(x2, mod, w_in[l], w_if)
    g = _gateproj(u, w_gate[l], b_gate[l][None, :])

    cos_t, sin_t = _rope_tables()
    ya = _attention(p, cos_t, sin_t)

    gcol = gates[:, :2 * M_HEADS].reshape(BATCH, SEQ, 2 * M_HEADS)
    grow = jnp.transpose(gcol, (0, 2, 1))
    ym = _mlstm(p, gcol, grow, b_mgate[l][None, :], b_mgate[l][:, None], conv_w[l], conv_b[l][None, :],
                m_norm_g[l][None, :])

    wr = (jnp.zeros((D_MODEL, LANES), F32)
          .at[:, :N_EXPERT_GROUPS].set(w_rg[l])
          .at[:, N_EXPERT_GROUPS:N_EXPERT_GROUPS + N_EXPERTS].set(w_re[l])).astype(BF16)
    br = (jnp.zeros((1, LANES), F32)
          .at[0, :N_EXPERT_GROUPS].set(b_rg[l])
          .at[0, N_EXPERT_GROUPS:N_EXPERT_GROUPS + N_EXPERTS].set(b_re[l]))
    x1, u2, rt = _merge(ya, ym, g, x2, mod,
                        w_proj_a[l].astype(BF16), w_proj_m[l].astype(BF16), w_out[l].astype(BF16),
                        ln1_g[l][None, :], ln1_b[l][None, :], wr, br)

    e_ids = rt[:, RT_E1:RT_E2 + 1].astype(jnp.int32)
    row_token, te, n_used, pos = _routing_tables(e_ids)

    xs = _gather_rows(row_token, n_used, u2)
    hs = _moe_up(te, n_used, xs,
                 w_eg[l].reshape(N_EXPERTS, D_MODEL, D_FF_EXPERT), w_eu[l].reshape(N_EXPERTS, D_MODEL, D_FF_EXPERT))
    mo = _moe_down(te, n_used, hs, w_ed[l].reshape(N_EXPERTS, D_FF_EXPERT, D_MODEL))

    nt = TOKENS // FINAL_TB
    pos_tiles = jnp.transpose(pos.reshape(nt, FINAL_TB, 2), (0, 2, 1)).reshape(-1)
    y = _final(pos_tiles, mo, x1, rt, mod, ln2_g[l][None, :], ln2_b[l][None, :])
    return y.reshape(BATCH, SEQ, D_MODEL)
```

```python
import functools

import jax
import jax.numpy as jnp
from jax import lax
from jax.experimental import pallas as pl
from jax.experimental.pallas import tpu as pltpu

F32 = jnp.float32
BF16 = jnp.bfloat16

D_MODEL = 2048
BATCH = 4
SEQ = 2048
TOKENS = BATCH * SEQ
DIL_CONFIGS = ((128, 1), (512, 4), (2048, 16))
N_DIL_GROUPS = 3
HEADS_PER_GROUP = 4
HEAD_DIM_A = 128
ATT_BLOCK = 128
ROPE_THETA = 10000.0
A_GROUP_W = HEADS_PER_GROUP * HEAD_DIM_A
A_QKV_W = N_DIL_GROUPS * A_GROUP_W
M_HEADS = 4
M_HEAD_DIM = 256
M_W = M_HEADS * M_HEAD_DIM
M_CHUNK = 128
CONV_K = 4
N_IN_MAIN = 3 * A_QKV_W + 4 * M_W
N_EXPERT_GROUPS = 4
EXPERTS_PER_GROUP = 8
N_EXPERTS = N_EXPERT_GROUPS * EXPERTS_PER_GROUP
D_FF_EXPERT = 1024
DEEPNORM_ALPHA = 2.0 ** 0.25
LN_EPS = 1e-5

LANES = 128
SUBLANES = 8
VMEM_LIMIT_BYTES = 56 * 1024 * 1024

PROJ_TN = 512
PROJ_TM = 1024
P_TILES_IN = N_IN_MAIN // PROJ_TN
P_ATT_TILES = 3 * A_QKV_W // PROJ_TN
P_M_TILES = P_TILES_IN - P_ATT_TILES
P_WIDTH = N_IN_MAIN
P_ATT_OFF = P_M_TILES * PROJ_TN
P_QM_BLK, P_KM_BLK, P_VM_BLK, P_OM_BLK = 0, 1, 2, 3
MERGE_TM = 256
MOE_TM = 256
MOE_ROWS = 2 * TOKENS + N_EXPERTS * MOE_TM
MOE_TILES = MOE_ROWS // MOE_TM
FINAL_TB = 256


def _cparams(sem, vmem=VMEM_LIMIT_BYTES):
    return pltpu.CompilerParams(dimension_semantics=sem, vmem_limit_bytes=vmem)


def _normalize(x):
    mu = jnp.mean(x, axis=-1, keepdims=True)
    xc = x - mu
    var = jnp.mean(xc * xc, axis=-1, keepdims=True)
    return xc * lax.rsqrt(var + LN_EPS)


def _silu(x):
    return x * jax.nn.sigmoid(x)


def _log_sigmoid(x):
    return jnp.minimum(x, 0.0) - jnp.log(1.0 + jnp.exp(-jnp.abs(x)))


ADA_TN = 1024
ADA_ROWS = 16


def _ada_kernel(c_ref, w_ref, b_ref, o_ref):
    sc = _silu(c_ref[...]).astype(BF16)
    o_ref[...] = jnp.dot(sc, w_ref[...].astype(BF16), preferred_element_type=F32) + b_ref[...]


def _ada(c_pad, w_ada, b_ada):
    n = w_ada.shape[1]
    return pl.pallas_call(
        _ada_kernel,
        out_shape=jax.ShapeDtypeStruct((ADA_ROWS, n), F32),
        grid=(n // ADA_TN,),
        in_specs=[
            pl.BlockSpec((ADA_ROWS, D_MODEL), lambda j: (0, 0)),
            pl.BlockSpec((D_MODEL, ADA_TN), lambda j: (0, j)),
            pl.BlockSpec((1, ADA_TN), lambda j: (0, j)),
        ],
        out_specs=pl.BlockSpec((ADA_ROWS, ADA_TN), lambda j: (0, j)),
        compiler_params=_cparams(("arbitrary",)),
        name="ada_mod",
    )(c_pad, w_ada, b_ada)


LN_CHUNK = 256


def _inproj_kernel(x_ref, mod_ref, w_ref, wif_ref, p_ref, g_ref, u_ref):
    n = pl.program_id(1)

    @pl.when(n == 0)
    def _():
        shift = mod_ref[0:1, :]
        scale = 1.0 + mod_ref[1:2, :]
        wif = wif_ref[...].astype(BF16)

        def body(ci, carry):
            r = pl.multiple_of(ci * LN_CHUNK, LN_CHUNK)
            u = (_normalize(x_ref[pl.ds(r, LN_CHUNK), :]) * scale + shift).astype(BF16)
            u_ref[pl.ds(r, LN_CHUNK), :] = u
            g_ref[pl.ds(r, LN_CHUNK), :] = jnp.dot(u, wif, preferred_element_type=F32)
            return carry

        lax.fori_loop(0, PROJ_TM // LN_CHUNK, body, 0)

    p_ref[...] = jnp.dot(u_ref[...], w_ref[...].astype(BF16), preferred_element_type=F32)


def _inproj(x2, mod, w_in, w_if):
    tiles_per_batch = SEQ // PROJ_TM
    return pl.pallas_call(
        _inproj_kernel,
        out_shape=(
            jax.ShapeDtypeStruct((TOKENS, P_WIDTH), F32),
            jax.ShapeDtypeStruct((TOKENS, LANES), F32),
            jax.ShapeDtypeStruct((TOKENS, D_MODEL), BF16),
        ),
        grid=(TOKENS // PROJ_TM, P_TILES_IN),
        in_specs=[
            pl.BlockSpec((PROJ_TM, D_MODEL), lambda m, n: (m, 0)),
            pl.BlockSpec((None, 6, D_MODEL), lambda m, n: (m // tiles_per_batch, 0, 0)),
            pl.BlockSpec((D_MODEL, PROJ_TN), lambda m, n: (0, n)),
            pl.BlockSpec((D_MODEL, LANES), lambda m, n: (0, 0)),
        ],
        out_specs=(
            pl.BlockSpec((PROJ_TM, PROJ_TN),
                         lambda m, n: (m, jnp.where(n < P_ATT_TILES, n + P_M_TILES, n - P_ATT_TILES))),
            pl.BlockSpec((PROJ_TM, LANES), lambda m, n: (m, 0)),
            pl.BlockSpec((PROJ_TM, D_MODEL), lambda m, n: (m, 0)),
        ),
        compiler_params=_cparams(("arbitrary", "arbitrary")),
        name="in_proj",
    )(x2, mod, w_in, w_if)


def _gateproj_kernel(u_ref, w_ref, b_ref, o_ref):
    acc = jnp.dot(u_ref[...], w_ref[...].astype(BF16), preferred_element_type=F32)
    o_ref[...] = jax.nn.sigmoid(acc + b_ref[...]).astype(BF16)


def _gateproj(u, w_gate, b_gate):
    n = w_gate.shape[1]
    return pl.pallas_call(
        _gateproj_kernel,
        out_shape=jax.ShapeDtypeStruct((TOKENS, n), BF16),
        grid=(TOKENS // PROJ_TM, n // PROJ_TN),
        in_specs=[
            pl.BlockSpec((PROJ_TM, D_MODEL), lambda m, j: (m, 0)),
            pl.BlockSpec((D_MODEL, PROJ_TN), lambda m, j: (0, j)),
            pl.BlockSpec((1, PROJ_TN), lambda m, j: (0, j)),
        ],
        out_specs=pl.BlockSpec((PROJ_TM, PROJ_TN), lambda m, j: (m, j)),
        compiler_params=_cparams(("arbitrary", "arbitrary")),
        name="gate_proj",
    )(u, w_gate, b_gate)


def _rows(start, size, stride):
    return pl.ds(start, size) if stride == 1 else pl.ds(start, size, stride=stride)


def _attn_kernel(q0, k0, v0, q1, k1, v1, q2, k2, v2, cos_ref, sin_ref, y_ref,
                 qr_sc, kr_sc, o0, o1, o2, l0, l1, l2):
    cos = cos_ref[...]
    sin = sin_ref[...]
    scale = HEAD_DIM_A ** -0.5
    blk = ATT_BLOCK
    qi2 = lax.broadcasted_iota(jnp.int32, (blk, 2 * blk), 0)
    kc2 = lax.broadcasted_iota(jnp.int32, (blk, 2 * blk), 1)
    mask_prev_cur = (kc2 >= qi2) & (kc2 <= qi2 + blk)
    qi1 = lax.broadcasted_iota(jnp.int32, (blk, blk), 0)
    kc1 = lax.broadcasted_iota(jnp.int32, (blk, blk), 1)
    mask_cur = kc1 <= qi1
    half = HEAD_DIM_A // 2
    groups = ((q0, k0, v0, o0, l0), (q1, k1, v1, o1, l1), (q2, k2, v2, o2, l2))
    for (window, d), (q_ref, k_ref, v_ref, o_sc, l_sc) in zip(DIL_CONFIGS, groups):
        nb = SEQ // d // blk
        q = q_ref[...]
        k = k_ref[...]
        qr_sc[...] = q * cos + pltpu.roll(q, half, 1) * sin
        kr_sc[...] = k * cos + pltpu.roll(k, half, 1) * sin
        for r in range(d):
            for j in range(nb):
                start = r + j * blk * d
                cur = _rows(start, blk, d)
                qb = qr_sc[cur, :].astype(BF16)
                if j == 0:
                    keys = cur
                    mask = mask_cur
                else:
                    keys = _rows(start - blk * d, 2 * blk, d)
                    mask = mask_prev_cur
                kw = kr_sc[keys, :].astype(BF16)
                vw = v_ref[keys, :].astype(BF16)
                s = lax.dot_general(qb, kw, (((1,), (1,)), ((), ())), preferred_element_type=F32) * scale
                s = jnp.where(mask, s, -jnp.inf)
                mx = jnp.max(s, axis=-1, keepdims=True)
                p = jnp.exp(s - mx)
                den = jnp.sum(p, axis=-1, keepdims=True)
                o_sc[cur, :] = jnp.dot(p.astype(BF16), vw, preferred_element_type=F32) / den
                l_sc[cur, :] = jnp.broadcast_to(mx + jnp.log(den), (blk, HEAD_DIM_A))
    la = l0[...]
    lb = l1[...]
    lc = l2[...]
    mx = jnp.maximum(jnp.maximum(la, lb), lc)
    ea = jnp.exp(la - mx)
    eb = jnp.exp(lb - mx)
    ec = jnp.exp(lc - mx)
    den = ea + eb + ec
    y_ref[...] = ((ea / den) * o0[...] + (eb / den) * o1[...] + (ec / den) * o2[...]).astype(BF16)


def _attention(p, cos_t, sin_t):
    for window, d in DIL_CONFIGS:
        assert window // d == ATT_BLOCK and SEQ % (d * ATT_BLOCK) == 0
    col0 = P_ATT_OFF // HEAD_DIM_A

    def slab(part, g):
        off = col0 + (part * A_QKV_W + g * A_GROUP_W) // HEAD_DIM_A
        return pl.BlockSpec((SEQ, HEAD_DIM_A), lambda b, h: (b, off + h))

    in_specs = [slab(part, g) for g in range(N_DIL_GROUPS) for part in range(3)]
    table = pl.BlockSpec((SEQ, HEAD_DIM_A), lambda b, h: (0, 0))
    return pl.pallas_call(
        _attn_kernel,
        out_shape=jax.ShapeDtypeStruct((TOKENS, A_GROUP_W), BF16),
        grid=(BATCH, HEADS_PER_GROUP),
        in_specs=in_specs + [table, table],
        out_specs=pl.BlockSpec((SEQ, HEAD_DIM_A), lambda b, h: (b, h)),
        scratch_shapes=[pltpu.VMEM((SEQ, HEAD_DIM_A), F32)] * 8,
        compiler_params=_cparams(("arbitrary", "arbitrary")),
        name="dil_attn",
    )(*([p] * 9), cos_t, sin_t)


CONV_HALO = SUBLANES


def _mlstm_kernel(q_ref, k_ref, v_ref, og_ref, gc_ref, gr_ref, bc_ref, br_ref, cw_ref, cb_ref, ng_ref,
                  y_ref, xq_sc, xk_sc, ct_sc, n_sc, m_sc):
    c = pl.program_id(1)
    L = M_CHUNK
    lo = CONV_HALO

    @pl.when(c == 0)
    def _():
        xq_sc[0:lo, :] = jnp.zeros((lo, M_W), F32)
        xk_sc[0:lo, :] = jnp.zeros((lo, M_W), F32)
        ct_sc[...] = jnp.zeros_like(ct_sc)
        n_sc[...] = jnp.zeros_like(n_sc)
        m_sc[...] = jnp.zeros_like(m_sc)

    xq_sc[lo:lo + L, :] = q_ref[...]
    xk_sc[lo:lo + L, :] = k_ref[...]

    def conv(x_sc, w, b):
        acc = x_sc[lo:lo + L, :] * w[CONV_K - 1:CONV_K, :] + b
        for j in range(CONV_K - 1):
            off = lo - (CONV_K - 1) + j
            acc = acc + x_sc[off:off + L, :] * w[j:j + 1, :]
        return acc

    cw = cw_ref[...]
    cb = cb_ref[...]
    qc = _silu(conv(xq_sc, cw[:, :M_W], cb[:, :M_W]))
    kc = _silu(conv(xk_sc, cw[:, M_W:], cb[:, M_W:])) * (M_HEAD_DIM ** -0.5)
    xq_sc[0:lo, :] = xq_sc[L:L + lo, :]
    xk_sc[0:lo, :] = xk_sc[L:L + lo, :]

    row = lax.broadcasted_iota(jnp.int32, (L, L), 0)
    col = lax.broadcasted_iota(jnp.int32, (L, L), 1)
    causal = row >= col
    for h in range(M_HEADS):
        hs = slice(h * M_HEAD_DIM, (h + 1) * M_HEAD_DIM)
        q = qc[:, hs]
        k = kc[:, hs]
        i_col = gc_ref[:, h:h + 1] + bc_ref[:, h:h + 1]
        lf_col = _log_sigmoid(gc_ref[:, M_HEADS + h:M_HEADS + h + 1] + bc_ref[:, M_HEADS + h:M_HEADS + h + 1])
        i_row = gr_ref[h:h + 1, :] + br_ref[h:h + 1, :]
        lf_row = _log_sigmoid(gr_ref[M_HEADS + h:M_HEADS + h + 1, :] + br_ref[M_HEADS + h:M_HEADS + h + 1, :])
        b_col = jnp.sum(jnp.where(causal, lf_row, 0.0), axis=1, keepdims=True)
        b_row = jnp.sum(jnp.where(row <= col, lf_col, 0.0), axis=0, keepdims=True)
        log_d = jnp.where(causal, b_col - b_row + i_row, -jnp.inf)
        m_prev = m_sc[h, 0:1, 0:1]
        log_inter = b_col + m_prev
        m_t = jnp.maximum(jnp.max(log_d, axis=1, keepdims=True), log_inter)
        qb = q.astype(BF16)
        kb = k.astype(BF16)
        vb = v_ref[:, hs].astype(BF16)
        s = lax.dot_general(qb, kb, (((1,), (1,)), ((), ())), preferred_element_type=F32) * jnp.exp(log_d - m_t)
        inter = jnp.exp(log_inter - m_t)
        ct = ct_sc[h]
        n_row = n_sc[h, 0:1, :]
        num = (jnp.dot(s.astype(BF16), vb, preferred_element_type=F32)
               + inter * jnp.dot(qb, ct.astype(BF16), preferred_element_type=F32))
        den = jnp.sum(s, axis=1, keepdims=True) + inter * jnp.sum(q * n_row, axis=1, keepdims=True)
        hh = num / jnp.maximum(jnp.abs(den), jnp.exp(-m_t))
        m_new = m_t[L - 1:L, :]
        b_last = b_col[L - 1:L, :]
        w_col = jnp.exp(b_last - b_col + i_col - m_new)
        decay = jnp.exp(b_last + m_prev - m_new)
        kw = k * w_col
        ct_sc[h] = decay * ct + lax.dot_general(kw.astype(BF16), vb, (((0,), (0,)), ((), ())),
                                                preferred_element_type=F32)
        n_sc[h] = jnp.broadcast_to(decay * n_row + jnp.sum(kw, axis=0, keepdims=True), (SUBLANES, M_HEAD_DIM))
        m_sc[h] = jnp.broadcast_to(m_new, (SUBLANES, LANES))
        z = jax.nn.sigmoid(og_ref[:, hs]) * hh
        y_ref[:, hs] = (_normalize(z) * ng_ref[:, hs]).astype(BF16)


def _mlstm(p, gcol, grow, bcol, brow, conv_w, conv_b, norm_g):
    nc = SEQ // M_CHUNK
    slab = lambda blk: pl.BlockSpec((M_CHUNK, M_W), lambda b, c: (b * nc + c, blk))
    return pl.pallas_call(
        _mlstm_kernel,
        out_shape=jax.ShapeDtypeStruct((TOKENS, M_W), BF16),
        grid=(BATCH, nc),
        in_specs=[
            slab(P_QM_BLK), slab(P_KM_BLK), slab(P_VM_BLK), slab(P_OM_BLK),
            pl.BlockSpec((None, M_CHUNK, 2 * M_HEADS), lambda b, c: (b, c, 0)),
            pl.BlockSpec((None, 2 * M_HEADS, M_CHUNK), lambda b, c: (b, 0, c)),
            pl.BlockSpec((1, 2 * M_HEADS), lambda b, c: (0, 0)),
            pl.BlockSpec((2 * M_HEADS, 1), lambda b, c: (0, 0)),
            pl.BlockSpec((CONV_K, 2 * M_W), lambda b, c: (0, 0)),
            pl.BlockSpec((1, 2 * M_W), lambda b, c: (0, 0)),
            pl.BlockSpec((1, M_W), lambda b, c: (0, 0)),
        ],
        out_specs=pl.BlockSpec((M_CHUNK, M_W), lambda b, c: (b * nc + c, 0)),
        scratch_shapes=[
            pltpu.VMEM((M_CHUNK + CONV_HALO, M_W), F32),
            pltpu.VMEM((M_CHUNK + CONV_HALO, M_W), F32),
            pltpu.VMEM((M_HEADS, M_HEAD_DIM, M_HEAD_DIM), F32),
            pltpu.VMEM((M_HEADS, SUBLANES, M_HEAD_DIM), F32),
            pltpu.VMEM((M_HEADS, SUBLANES, LANES), F32),
        ],
        compiler_params=_cparams(("arbitrary", "arbitrary")),
        name="mlstm",
    )(p, p, p, p, gcol, grow, bcol, brow, conv_w, conv_b, norm_g)


RT_E1, RT_E2, RT_W1, RT_W2 = 0, 1, 2, 3


def _route(logits):
    lane = lax.broadcasted_iota(jnp.int32, logits.shape, 1).astype(F32)
    big = float(LANES)
    is_g = lane < N_EXPERT_GROUPS
    gl = jnp.where(is_g, logits, -jnp.inf)
    gexp = jnp.exp(gl - jnp.max(gl, axis=1, keepdims=True))
    gprob = gexp / jnp.sum(gexp, axis=1, keepdims=True)
    g_w = jnp.max(gprob, axis=1, keepdims=True)
    g_top = jnp.min(jnp.where(is_g & (gprob == g_w), lane, big), axis=1, keepdims=True)
    lo = N_EXPERT_GROUPS + EXPERTS_PER_GROUP * g_top
    in_grp = (lane >= lo) & (lane < lo + EXPERTS_PER_GROUP)
    el = jnp.where(in_grp, logits, -jnp.inf)
    eexp = jnp.exp(el - jnp.max(el, axis=1, keepdims=True))
    eprob = eexp / jnp.sum(eexp, axis=1, keepdims=True)
    v1 = jnp.max(eprob, axis=1, keepdims=True)
    i1 = jnp.min(jnp.where(in_grp & (eprob == v1), lane, big), axis=1, keepdims=True)
    rest = jnp.where(in_grp & (lane != i1), eprob, -1.0)
    v2 = jnp.max(rest, axis=1, keepdims=True)
    i2 = jnp.min(jnp.where(rest == v2, lane, big), axis=1, keepdims=True)
    tot = v1 + v2
    w1 = g_w * (v1 / tot)
    w2 = g_w * (v2 / tot)
    e1 = i1 - N_EXPERT_GROUPS
    e2 = i2 - N_EXPERT_GROUPS
    rec = jnp.where(lane == RT_E1, e1, jnp.where(lane == RT_E2, e2, jnp.where(lane == RT_W1, w1, w2)))
    return jnp.where(lane <= RT_W2, rec, 0.0)


def _merge_kernel(ya_ref, ym_ref, g_ref, x_ref, mod_ref,
                  wpa_ref, wpm_ref, wout_ref, lng_ref, lnb_ref, wr_ref, br_ref,
                  x1_ref, u2_ref, rt_ref):
    pa = jnp.dot(ya_ref[...], wpa_ref[...], preferred_element_type=F32)
    pm = jnp.dot(ym_ref[...], wpm_ref[...], preferred_element_type=F32)
    merged = g_ref[:, :D_MODEL].astype(F32) * pa + g_ref[:, D_MODEL:].astype(F32) * pm
    mix = jnp.dot(merged.astype(BF16), wout_ref[...], preferred_element_type=F32)
    z = DEEPNORM_ALPHA * x_ref[...] + mod_ref[2:3, :] * mix
    x1 = _normalize(z) * lng_ref[...] + lnb_ref[...]
    x1_ref[...] = x1
    u2 = _normalize(x1) * (1.0 + mod_ref[4:5, :]) + mod_ref[3:4, :]
    u2_ref[...] = u2
    logits = jnp.dot(u2.astype(BF16), wr_ref[...], preferred_element_type=F32) + br_ref[...]
    rt_ref[...] = _route(logits)


def _merge(ya, ym, g, x2, mod, wpa, wpm, wout, lng, lnb, wr, br):
    tm = MERGE_TM
    tiles_per_batch = SEQ // tm
    rowblk = lambda w: pl.BlockSpec((tm, w), lambda m: (m, 0))
    const = lambda shape: pl.BlockSpec(shape, lambda m: (0,) * len(shape), pipeline_mode=pl.Buffered(1))
    return pl.pallas_call(
        _merge_kernel,
        out_shape=(
            jax.ShapeDtypeStruct((TOKENS, D_MODEL), F32),
            jax.ShapeDtypeStruct((TOKENS, D_MODEL), F32),
            jax.ShapeDtypeStruct((TOKENS, LANES), F32),
        ),
        grid=(TOKENS // tm,),
        in_specs=[
            rowblk(A_GROUP_W), rowblk(M_W), rowblk(2 * D_MODEL), rowblk(D_MODEL),
            pl.BlockSpec((None, 6, D_MODEL), lambda m: (m // tiles_per_batch, 0, 0)),
            const((A_GROUP_W, D_MODEL)), const((M_W, D_MODEL)), const((D_MODEL, D_MODEL)),
            const((1, D_MODEL)), const((1, D_MODEL)),
            const((D_MODEL, LANES)), const((1, LANES)),
        ],
        out_specs=(rowblk(D_MODEL), rowblk(D_MODEL), rowblk(LANES)),
        compiler_params=_cparams(("arbitrary",)),
        name="merge_ln1_route",
    )(ya, ym, g, x2, mod, wpa, wpm, wout, lng, lnb, wr, br)


GATHER_UNROLL = 8
CAST_ROWS = 128


def _issue_rows(src_ref, idx_ref, base, buf, slot, sem, nrows):
    def body(blk, carry):
        for j in range(GATHER_UNROLL):
            r = blk * GATHER_UNROLL + j
            pltpu.make_async_copy(src_ref.at[pl.ds(idx_ref[base + r], 1)],
                                  buf.at[slot, pl.ds(r, 1)], sem.at[slot]).start()
        return carry

    lax.fori_loop(0, nrows // GATHER_UNROLL, body, 0)


def _wait_rows(src_ref, buf, slot, sem, nrows):
    pltpu.make_async_copy(src_ref.at[pl.ds(0, nrows)], buf.at[slot], sem.at[slot]).wait()


def _expert_changed(te_ref, i):
    return (i == 0) | (te_ref[i] != te_ref[jnp.maximum(i - 1, 0)])


def _moe_kernel(te_ref, tn_ref, nu_ref, tok_ref, u_ref, wg_ref, wu_ref, wd_ref, o_ref,
                stg, stu, std, wgb, wub, wdb, xbuf, wsem, xsem):
    i = pl.program_id(0)
    nu = nu_ref[0]
    tm = o_ref.shape[0]

    def weight_copies(e):
        return (pltpu.make_async_copy(wg_ref.at[e], stg, wsem.at[0]),
                pltpu.make_async_copy(wu_ref.at[e], stu, wsem.at[1]),
                pltpu.make_async_copy(wd_ref.at[e], std, wsem.at[2]))

    @pl.when(i == 0)
    def _():
        for cp in weight_copies(te_ref[0]):
            cp.start()
        _issue_rows(u_ref, tok_ref, 0, xbuf, 0, xsem, tm)

    @pl.when(i < nu)
    def _():
        slot = i % 2

        @pl.when(_expert_changed(te_ref, i))
        def _():
            for cp in weight_copies(te_ref[i]):
                cp.wait()
            for src, dst in ((stg, wgb), (stu, wub), (std, wdb)):
                def cast_rows(ci, carry, src=src, dst=dst):
                    r = pl.multiple_of(ci * CAST_ROWS, CAST_ROWS)
                    dst[pl.ds(r, CAST_ROWS), :] = src[pl.ds(r, CAST_ROWS), :].astype(BF16)
                    return carry

                lax.fori_loop(0, src.shape[0] // CAST_ROWS, cast_rows, 0)

            @pl.when(tn_ref[i] >= 0)
            def _():
                for cp in weight_copies(tn_ref[i]):
                    cp.start()

        @pl.when(i + 1 < nu)
        def _():
            _issue_rows(u_ref, tok_ref, (i + 1) * tm, xbuf, 1 - slot, xsem, tm)

        _wait_rows(u_ref, xbuf, slot, xsem, tm)
        x = xbuf[slot].astype(BF16)
        a = jnp.dot(x, wgb[...], preferred_element_type=F32)
        b = jnp.dot(x, wub[...], preferred_element_type=F32)
        h = (_silu(a) * b).astype(BF16)
        o_ref[...] = jnp.dot(h, wdb[...], preferred_element_type=F32)

    @pl.when(i >= nu)
    def _():
        o_ref[...] = jnp.zeros_like(o_ref)


def _moe(te, te_next, n_used, row_token, u2, w_eg, w_eu, w_ed):
    tm = MOE_TM
    return pl.pallas_call(
        _moe_kernel,
        out_shape=jax.ShapeDtypeStruct((MOE_ROWS, D_MODEL), F32),
        grid_spec=pltpu.PrefetchScalarGridSpec(
            num_scalar_prefetch=4,
            grid=(MOE_TILES,),
            in_specs=[pl.BlockSpec(memory_space=pl.ANY)] * 4,
            out_specs=pl.BlockSpec((tm, D_MODEL), lambda i, te, tn, nu, tok: (i, 0)),
            scratch_shapes=[
                pltpu.VMEM((D_MODEL, D_FF_EXPERT), F32),
                pltpu.VMEM((D_MODEL, D_FF_EXPERT), F32),
                pltpu.VMEM((D_FF_EXPERT, D_MODEL), F32),
                pltpu.VMEM((D_MODEL, D_FF_EXPERT), BF16),
                pltpu.VMEM((D_MODEL, D_FF_EXPERT), BF16),
                pltpu.VMEM((D_FF_EXPERT, D_MODEL), BF16),
                pltpu.VMEM((2, tm, D_MODEL), F32),
                pltpu.SemaphoreType.DMA((3,)),
                pltpu.SemaphoreType.DMA((2,)),
            ],
        ),
        compiler_params=_cparams(("arbitrary",)),
        name="moe_experts",
    )(te, te_next, n_used, row_token, u2, w_eg, w_eu, w_ed)


def _final_kernel(pos_ref, o_ref, x1_ref, rt_ref, mod_ref, lng_ref, lnb_ref, y_ref, buf, sem):
    i = pl.program_id(0)
    tb = x1_ref.shape[0]
    slot = i % 2

    @pl.when(i == 0)
    def _():
        _issue_rows(o_ref, pos_ref, 0, buf, 0, sem, 2 * tb)

    @pl.when(i + 1 < pl.num_programs(0))
    def _():
        _issue_rows(o_ref, pos_ref, (i + 1) * (2 * tb), buf, 1 - slot, sem, 2 * tb)

    _wait_rows(o_ref, buf, slot, sem, 2 * tb)
    ffn = (rt_ref[:, RT_W1:RT_W1 + 1] * buf[slot, 0:tb, :]
           + rt_ref[:, RT_W2:RT_W2 + 1] * buf[slot, tb:2 * tb, :])
    z = DEEPNORM_ALPHA * x1_ref[...] + mod_ref[5:6, :] * ffn
    y_ref[...] = _normalize(z) * lng_ref[...] + lnb_ref[...]


def _final(pos_tiles, moe_out, x1, rt, mod, lng, lnb):
    tb = FINAL_TB
    tiles_per_batch = SEQ // tb
    return pl.pallas_call(
        _final_kernel,
        out_shape=jax.ShapeDtypeStruct((TOKENS, D_MODEL), F32),
        grid_spec=pltpu.PrefetchScalarGridSpec(
            num_scalar_prefetch=1,
            grid=(TOKENS // tb,),
            in_specs=[
                pl.BlockSpec(memory_space=pl.ANY),
                pl.BlockSpec((tb, D_MODEL), lambda i, pos: (i, 0)),
                pl.BlockSpec((tb, LANES), lambda i, pos: (i, 0)),
                pl.BlockSpec((None, 6, D_MODEL), lambda i, pos: (i // tiles_per_batch, 0, 0)),
                pl.BlockSpec((1, D_MODEL), lambda i, pos: (0, 0)),
                pl.BlockSpec((1, D_MODEL), lambda i, pos: (0, 0)),
            ],
            out_specs=pl.BlockSpec((tb, D_MODEL), lambda i, pos: (i, 0)),
            scratch_shapes=[pltpu.VMEM((2, 2 * tb, D_MODEL), F32), pltpu.SemaphoreType.DMA((2,))],
        ),
        compiler_params=_cparams(("arbitrary",)),
        name="combine_ln2",
    )(pos_tiles, moe_out, x1, rt, mod, lng, lnb)


def _routing_tables(e_ids):
    tm = MOE_TM
    e_flat = e_ids.reshape(-1)
    onehot = (e_flat[:, None] == jnp.arange(N_EXPERTS, dtype=jnp.int32)[None, :]).astype(jnp.int32)
    csum = jnp.cumsum(onehot, axis=0)
    rank = jnp.sum((csum - onehot) * onehot, axis=1)
    counts = csum[-1]
    padded = ((counts + tm - 1) // tm) * tm
    pend = jnp.cumsum(padded)
    pstart = pend - padded
    dest = pstart[e_flat] + rank
    tok = jnp.arange(2 * TOKENS, dtype=jnp.int32) // 2
    row_token = jnp.zeros((MOE_ROWS,), jnp.int32).at[dest].set(tok)
    n_used = (pend[-1] // tm).astype(jnp.int32)
    tile_start = jnp.arange(MOE_TILES, dtype=jnp.int32) * tm
    te = jnp.sum((pend[None, :] <= tile_start[:, None]).astype(jnp.int32), axis=1)
    te = jnp.minimum(te, N_EXPERTS - 1)
    te_last = te[jnp.maximum(n_used - 1, 0)]
    te = jnp.where(jnp.arange(MOE_TILES) < n_used, te, te_last)
    next_run = pend[te] // tm
    te_next = jnp.where(next_run < n_used, te[jnp.minimum(next_run, MOE_TILES - 1)], -1).astype(jnp.int32)
    pos = dest.reshape(TOKENS, 2)
    return row_token, te, te_next, n_used.reshape(1), pos


def _rope_tables():
    inv = ROPE_THETA ** (-jnp.arange(0, HEAD_DIM_A, 2, dtype=F32) / HEAD_DIM_A)
    ang = jnp.arange(SEQ, dtype=F32)[:, None] * inv[None, :]
    cos = jnp.cos(ang)
    sin = jnp.sin(ang)
    return jnp.concatenate([cos, cos], axis=-1), jnp.concatenate([-sin, sin], axis=-1)


def kernel(x, c, w_ada, b_ada, w_in, b_mgate, conv_w, conv_b, m_norm_g, w_proj_a, w_proj_m, w_gate, b_gate,
           w_out, ln1_g, ln1_b, w_rg, b_rg, w_re, b_re, w_eg, w_eu, w_ed, ln2_g, ln2_b):
    assert x.shape == (BATCH, SEQ, D_MODEL) and w_ada.shape[0] == 1
    l = 0
    x2 = x.reshape(TOKENS, D_MODEL)

    c_pad = jnp.zeros((ADA_ROWS, D_MODEL), F32).at[:BATCH].set(c)
    mod = _ada(c_pad, w_ada[l], b_ada[l][None, :])[:BATCH].reshape(BATCH, 6, D_MODEL)

    w_if = jnp.zeros((D_MODEL, LANES), F32).at[:, :2 * M_HEADS].set(w_in[l][:, N_IN_MAIN:])
    p, gates, u = _inproj(x2, mod, w_in[l], w_if)
    g = _gateproj(u, w_gate[l], b_gate[l][None, :])

    cos_t, sin_t = _rope_tables()
    ya = _attention(p, cos_t, sin_t)

    gcol = gates[:, :2 * M_HEADS].reshape(BATCH, SEQ, 2 * M_HEADS)
    grow = jnp.transpose(gcol, (0, 2, 1))
    ym = _mlstm(p, gcol, grow, b_mgate[l][None, :], b_mgate[l][:, None], conv_w[l], conv_b[l][None, :],
                m_norm_g[l][None, :])

    wr = (jnp.zeros((D_MODEL, LANES), F32)
          .at[:, :N_EXPERT_GROUPS].set(w_rg[l])
          .at[:, N_EXPERT_GROUPS:N_EXPERT_GROUPS + N_EXPERTS].set(w_re[l])).astype(BF16)
    br = (jnp.zeros((1, LANES), F32)
          .at[0, :N_EXPERT_GROUPS].set(b_rg[l])
          .at[0, N_EXPERT_GROUPS:N_EXPERT_GROUPS + N_EXPERTS].set(b_re[l]))
    x1, u2, rt = _merge(ya, ym, g, x2, mod,
                        w_proj_a[l].astype(BF16), w_proj_m[l].astype(BF16), w_out[l].astype(BF16),
                        ln1_g[l][None, :], ln1_b[l][None, :], wr, br)

    e_ids = rt[:, RT_E1:RT_E2 + 1].astype(jnp.int32)
    row_token, te, te_next, n_used, pos = _routing_tables(e_ids)

    mo = _moe(te, te_next, n_used, row_token, u2,
              w_eg[l].reshape(N_EXPERTS, D_MODEL, D_FF_EXPERT), w_eu[l].reshape(N_EXPERTS, D_MODEL, D_FF_EXPERT),
              w_ed[l].reshape(N_EXPERTS, D_FF_EXPERT, D_MODEL))

    nt = TOKENS // FINAL_TB
    pos_tiles = jnp.transpose(pos.reshape(nt, FINAL_TB, 2), (0, 2, 1)).reshape(-1)
    y = _final(pos_tiles, mo, x1, rt, mod, ln2_g[l][None, :], ln2_b[l][None, :])
    return y.reshape(BATCH, SEQ, D_MODEL)
```

```python
import functools

import jax
import jax.numpy as jnp
from jax import lax
from jax.experimental import pallas as pl
from jax.experimental.pallas import tpu as pltpu

F32 = jnp.float32
BF16 = jnp.bfloat16

D_MODEL = 2048
BATCH = 4
SEQ = 2048
TOKENS = BATCH * SEQ
DIL_CONFIGS = ((128, 1), (512, 4), (2048, 16))
N_DIL_GROUPS = 3
HEADS_PER_GROUP = 4
HEAD_DIM_A = 128
ATT_BLOCK = 128
ROPE_THETA = 10000.0
A_GROUP_W = HEADS_PER_GROUP * HEAD_DIM_A
A_QKV_W = N_DIL_GROUPS * A_GROUP_W
M_HEADS = 4
M_HEAD_DIM = 256
M_W = M_HEADS * M_HEAD_DIM
M_CHUNK = 128
CONV_K = 4
N_IN_MAIN = 3 * A_QKV_W + 4 * M_W
N_EXPERT_GROUPS = 4
EXPERTS_PER_GROUP = 8
N_EXPERTS = N_EXPERT_GROUPS * EXPERTS_PER_GROUP
D_FF_EXPERT = 1024
DEEPNORM_ALPHA = 2.0 ** 0.25
LN_EPS = 1e-5

LANES = 128
SUBLANES = 8
VMEM_LIMIT_BYTES = 56 * 1024 * 1024

PROJ_TN = 512
PROJ_TM = 1024
P_TILES_IN = N_IN_MAIN // PROJ_TN
P_ATT_TILES = 3 * A_QKV_W // PROJ_TN
P_M_TILES = P_TILES_IN - P_ATT_TILES
P_WIDTH = N_IN_MAIN
P_ATT_OFF = P_M_TILES * PROJ_TN
P_QM_BLK, P_KM_BLK, P_VM_BLK, P_OM_BLK = 0, 1, 2, 3
MERGE_TM = 256
MOE_TM = 256
MOE_ROWS = 2 * TOKENS + N_EXPERTS * MOE_TM
MOE_TILES = MOE_ROWS // MOE_TM
FINAL_TB = 256


def _cparams(sem, vmem=VMEM_LIMIT_BYTES):
    return pltpu.CompilerParams(dimension_semantics=sem, vmem_limit_bytes=vmem)


def _normalize(x):
    mu = jnp.mean(x, axis=-1, keepdims=True)
    xc = x - mu
    var = jnp.mean(xc * xc, axis=-1, keepdims=True)
    return xc * lax.rsqrt(var + LN_EPS)


def _silu(x):
    return x * jax.nn.sigmoid(x)


def _log_sigmoid(x):
    return jnp.minimum(x, 0.0) - jnp.log(1.0 + jnp.exp(-jnp.abs(x)))


_CONTRACT_LAST = (((1,), (1,)), ((), ()))
LOG2E = 1.4426950408889634


ADA_TN = 1024
ADA_ROWS = 16


def _ada_kernel(c_ref, w_ref, b_ref, o_ref):
    sc = _silu(c_ref[...]).astype(BF16)
    o_ref[...] = jnp.dot(sc, w_ref[...].astype(BF16), preferred_element_type=F32) + b_ref[...]


def _ada(c_pad, w_ada, b_ada):
    n = w_ada.shape[1]
    return pl.pallas_call(
        _ada_kernel,
        out_shape=jax.ShapeDtypeStruct((ADA_ROWS, n), F32),
        grid=(n // ADA_TN,),
        in_specs=[
            pl.BlockSpec((ADA_ROWS, D_MODEL), lambda j: (0, 0)),
            pl.BlockSpec((D_MODEL, ADA_TN), lambda j: (0, j)),
            pl.BlockSpec((1, ADA_TN), lambda j: (0, j)),
        ],
        out_specs=pl.BlockSpec((ADA_ROWS, ADA_TN), lambda j: (0, j)),
        compiler_params=_cparams(("arbitrary",)),
        name="ada_mod",
    )(c_pad, w_ada, b_ada)


LN_CHUNK = 256


def _inproj_kernel(x_ref, mod_ref, w_ref, wif_ref, p_ref, g_ref, u_ref):
    n = pl.program_id(1)

    @pl.when(n == 0)
    def _():
        shift = mod_ref[0:1, :]
        scale = 1.0 + mod_ref[1:2, :]
        wif = wif_ref[...].astype(BF16)

        def body(ci, carry):
            r = pl.multiple_of(ci * LN_CHUNK, LN_CHUNK)
            u = (_normalize(x_ref[pl.ds(r, LN_CHUNK), :]) * scale + shift).astype(BF16)
            u_ref[pl.ds(r, LN_CHUNK), :] = u
            g_ref[pl.ds(r, LN_CHUNK), :] = lax.dot_general(u, wif, _CONTRACT_LAST, preferred_element_type=F32)
            return carry

        lax.fori_loop(0, PROJ_TM // LN_CHUNK, body, 0)

    p_ref[...] = lax.dot_general(u_ref[...], w_ref[...].astype(BF16), _CONTRACT_LAST,
                                 preferred_element_type=F32)


def _inproj(x2, mod, w_in_t, w_if_t):
    tiles_per_batch = SEQ // PROJ_TM
    return pl.pallas_call(
        _inproj_kernel,
        out_shape=(
            jax.ShapeDtypeStruct((TOKENS, P_WIDTH), F32),
            jax.ShapeDtypeStruct((TOKENS, LANES), F32),
            jax.ShapeDtypeStruct((TOKENS, D_MODEL), BF16),
        ),
        grid=(TOKENS // PROJ_TM, P_TILES_IN),
        in_specs=[
            pl.BlockSpec((PROJ_TM, D_MODEL), lambda m, n: (m, 0)),
            pl.BlockSpec((None, 6, D_MODEL), lambda m, n: (m // tiles_per_batch, 0, 0)),
            pl.BlockSpec((PROJ_TN, D_MODEL), lambda m, n: (n, 0)),
            pl.BlockSpec((LANES, D_MODEL), lambda m, n: (0, 0)),
        ],
        out_specs=(
            pl.BlockSpec((PROJ_TM, PROJ_TN),
                         lambda m, n: (m, jnp.where(n < P_ATT_TILES, n + P_M_TILES, n - P_ATT_TILES))),
            pl.BlockSpec((PROJ_TM, LANES), lambda m, n: (m, 0)),
            pl.BlockSpec((PROJ_TM, D_MODEL), lambda m, n: (m, 0)),
        ),
        compiler_params=_cparams(("arbitrary", "arbitrary")),
        name="in_proj",
    )(x2, mod, w_in_t, w_if_t)


def _gateproj_kernel(u_ref, w_ref, b_ref, o_ref):
    acc = jnp.dot(u_ref[...], w_ref[...].astype(BF16), preferred_element_type=F32)
    o_ref[...] = jax.nn.sigmoid(acc + b_ref[...]).astype(BF16)


def _gateproj(u, w_gate, b_gate):
    n = w_gate.shape[1]
    return pl.pallas_call(
        _gateproj_kernel,
        out_shape=jax.ShapeDtypeStruct((TOKENS, n), BF16),
        grid=(TOKENS // PROJ_TM, n // PROJ_TN),
        in_specs=[
            pl.BlockSpec((PROJ_TM, D_MODEL), lambda m, j: (m, 0)),
            pl.BlockSpec((D_MODEL, PROJ_TN), lambda m, j: (0, j)),
            pl.BlockSpec((1, PROJ_TN), lambda m, j: (0, j)),
        ],
        out_specs=pl.BlockSpec((PROJ_TM, PROJ_TN), lambda m, j: (m, j)),
        compiler_params=_cparams(("arbitrary", "arbitrary")),
        name="gate_proj",
    )(u, w_gate, b_gate)


def _rows(start, size, stride):
    return pl.ds(start, size) if stride == 1 else pl.ds(start, size, stride=stride)


def _attn_kernel(q0, k0, v0, q1, k1, v1, q2, k2, v2, cos_ref, sin_ref, y_ref,
                 qr_sc, kr_sc, o0, o1, o2, l0, l1, l2):
    cos = cos_ref[...]
    sin = sin_ref[...]
    scale = HEAD_DIM_A ** -0.5
    blk = ATT_BLOCK
    qi2 = lax.broadcasted_iota(jnp.int32, (blk, 2 * blk), 0)
    kc2 = lax.broadcasted_iota(jnp.int32, (blk, 2 * blk), 1)
    mask_prev_cur = (kc2 >= qi2) & (kc2 <= qi2 + blk)
    qi1 = lax.broadcasted_iota(jnp.int32, (blk, blk), 0)
    kc1 = lax.broadcasted_iota(jnp.int32, (blk, blk), 1)
    mask_cur = kc1 <= qi1
    half = HEAD_DIM_A // 2
    groups = ((q0, k0, v0, o0, l0), (q1, k1, v1, o1, l1), (q2, k2, v2, o2, l2))
    for (window, d), (q_ref, k_ref, v_ref, o_sc, l_sc) in zip(DIL_CONFIGS, groups):
        nb = SEQ // d // blk
        q = q_ref[...]
        k = k_ref[...]
        qr_sc[...] = q * cos + pltpu.roll(q, half, 1) * sin
        kr_sc[...] = k * cos + pltpu.roll(k, half, 1) * sin
        for r in range(d):
            for j in range(nb):
                start = r + j * blk * d
                cur = _rows(start, blk, d)
                qb = qr_sc[cur, :].astype(BF16)
                if j == 0:
                    keys = cur
                    mask = mask_cur
                else:
                    keys = _rows(start - blk * d, 2 * blk, d)
                    mask = mask_prev_cur
                kw = kr_sc[keys, :].astype(BF16)
                vb = v_ref[keys, :].astype(BF16)
                vw = jnp.concatenate([vb, jnp.ones_like(vb)], axis=1)
                s = lax.dot_general(qb, kw, _CONTRACT_LAST, preferred_element_type=F32)
                s = jnp.where(mask, s, -jnp.inf)
                mx = jnp.max(s, axis=-1, keepdims=True)
                p = jnp.exp2((s - mx) * (scale * LOG2E))
                pv = jnp.dot(p.astype(BF16), vw, preferred_element_type=F32)
                den = pv[:, HEAD_DIM_A:]
                o_sc[cur, :] = pv[:, :HEAD_DIM_A] / den
                l_sc[cur, :] = mx * scale + jnp.log(den)
    la = l0[...]
    lb = l1[...]
    lc = l2[...]
    mx = jnp.maximum(jnp.maximum(la, lb), lc)
    ea = jnp.exp(la - mx)
    eb = jnp.exp(lb - mx)
    ec = jnp.exp(lc - mx)
    den = ea + eb + ec
    y_ref[...] = ((ea / den) * o0[...] + (eb / den) * o1[...] + (ec / den) * o2[...]).astype(BF16)


def _attention(p, cos_t, sin_t):
    for window, d in DIL_CONFIGS:
        assert window // d == ATT_BLOCK and SEQ % (d * ATT_BLOCK) == 0
    col0 = P_ATT_OFF // HEAD_DIM_A

    def slab(part, g):
        off = col0 + (part * A_QKV_W + g * A_GROUP_W) // HEAD_DIM_A
        return pl.BlockSpec((SEQ, HEAD_DIM_A), lambda b, h: (b, off + h))

    in_specs = [slab(part, g) for g in range(N_DIL_GROUPS) for part in range(3)]
    table = pl.BlockSpec((SEQ, HEAD_DIM_A), lambda b, h: (0, 0))
    return pl.pallas_call(
        _attn_kernel,
        out_shape=jax.ShapeDtypeStruct((TOKENS, A_GROUP_W), BF16),
        grid=(BATCH, HEADS_PER_GROUP),
        in_specs=in_specs + [table, table],
        out_specs=pl.BlockSpec((SEQ, HEAD_DIM_A), lambda b, h: (b, h)),
        scratch_shapes=[pltpu.VMEM((SEQ, HEAD_DIM_A), F32)] * 8,
        compiler_params=_cparams(("arbitrary", "arbitrary")),
        name="dil_attn",
    )(*([p] * 9), cos_t, sin_t)


CONV_HALO = SUBLANES


def _mlstm_kernel(q_ref, k_ref, v_ref, og_ref, gc_ref, gr_ref, bc_ref, br_ref, cw_ref, cb_ref, ng_ref,
                  y_ref, xq_sc, xk_sc, ct_sc, n_sc, m_sc):
    c = pl.program_id(1)
    L = M_CHUNK
    lo = CONV_HALO

    @pl.when(c == 0)
    def _():
        xq_sc[0:lo, :] = jnp.zeros((lo, M_W), F32)
        xk_sc[0:lo, :] = jnp.zeros((lo, M_W), F32)
        ct_sc[...] = jnp.zeros_like(ct_sc)
        n_sc[...] = jnp.zeros_like(n_sc)
        m_sc[...] = jnp.zeros_like(m_sc)

    xq_sc[lo:lo + L, :] = q_ref[...]
    xk_sc[lo:lo + L, :] = k_ref[...]

    def conv(x_sc, w, b):
        acc = x_sc[lo:lo + L, :] * w[CONV_K - 1:CONV_K, :] + b
        for j in range(CONV_K - 1):
            off = lo - (CONV_K - 1) + j
            acc = acc + x_sc[off:off + L, :] * w[j:j + 1, :]
        return acc

    cw = cw_ref[...]
    cb = cb_ref[...]
    qc = _silu(conv(xq_sc, cw[:, :M_W], cb[:, :M_W]))
    kc = _silu(conv(xk_sc, cw[:, M_W:], cb[:, M_W:])) * (M_HEAD_DIM ** -0.5)
    xq_sc[0:lo, :] = xq_sc[L:L + lo, :]
    xk_sc[0:lo, :] = xk_sc[L:L + lo, :]

    row = lax.broadcasted_iota(jnp.int32, (L, L), 0)
    col = lax.broadcasted_iota(jnp.int32, (L, L), 1)
    causal = row >= col
    for h in range(M_HEADS):
        hs = slice(h * M_HEAD_DIM, (h + 1) * M_HEAD_DIM)
        q = qc[:, hs]
        k = kc[:, hs]
        i_col = gc_ref[:, h:h + 1] + bc_ref[:, h:h + 1]
        lf_col = _log_sigmoid(gc_ref[:, M_HEADS + h:M_HEADS + h + 1] + bc_ref[:, M_HEADS + h:M_HEADS + h + 1])
        i_row = gr_ref[h:h + 1, :] + br_ref[h:h + 1, :]
        lf_row = _log_sigmoid(gr_ref[M_HEADS + h:M_HEADS + h + 1, :] + br_ref[M_HEADS + h:M_HEADS + h + 1, :])
        b_col = jnp.sum(jnp.where(causal, lf_row, 0.0), axis=1, keepdims=True)
        b_row = jnp.sum(jnp.where(row <= col, lf_col, 0.0), axis=0, keepdims=True)
        log_d = jnp.where(causal, b_col - b_row + i_row, -jnp.inf)
        m_prev = m_sc[h, 0:1, 0:1]
        log_inter = b_col + m_prev
        m_t = jnp.maximum(jnp.max(log_d, axis=1, keepdims=True), log_inter)
        qb = q.astype(BF16)
        kb = k.astype(BF16)
        vb = v_ref[:, hs].astype(BF16)
        s = lax.dot_general(qb, kb, (((1,), (1,)), ((), ())), preferred_element_type=F32) * jnp.exp(log_d - m_t)
        inter = jnp.exp(log_inter - m_t)
        ct = ct_sc[h]
        n_row = n_sc[h, 0:1, :]
        num = (jnp.dot(s.astype(BF16), vb, preferred_element_type=F32)
               + inter * jnp.dot(qb, ct.astype(BF16), preferred_element_type=F32))
        den = jnp.sum(s, axis=1, keepdims=True) + inter * jnp.sum(q * n_row, axis=1, keepdims=True)
        hh = num / jnp.maximum(jnp.abs(den), jnp.exp(-m_t))
        m_new = m_t[L - 1:L, :]
        b_last = b_col[L - 1:L, :]
        w_col = jnp.exp(b_last - b_col + i_col - m_new)
        decay = jnp.exp(b_last + m_prev - m_new)
        kw = k * w_col
        ct_sc[h] = decay * ct + lax.dot_general(kw.astype(BF16), vb, (((0,), (0,)), ((), ())),
                                                preferred_element_type=F32)
        n_sc[h] = jnp.broadcast_to(decay * n_row + jnp.sum(kw, axis=0, keepdims=True), (SUBLANES, M_HEAD_DIM))
        m_sc[h] = jnp.broadcast_to(m_new, (SUBLANES, LANES))
        z = jax.nn.sigmoid(og_ref[:, hs]) * hh
        y_ref[:, hs] = (_normalize(z) * ng_ref[:, hs]).astype(BF16)


def _mlstm(p, gcol, grow, bcol, brow, conv_w, conv_b, norm_g):
    nc = SEQ // M_CHUNK
    slab = lambda blk: pl.BlockSpec((M_CHUNK, M_W), lambda b, c: (b * nc + c, blk))
    return pl.pallas_call(
        _mlstm_kernel,
        out_shape=jax.ShapeDtypeStruct((TOKENS, M_W), BF16),
        grid=(BATCH, nc),
        in_specs=[
            slab(P_QM_BLK), slab(P_KM_BLK), slab(P_VM_BLK), slab(P_OM_BLK),
            pl.BlockSpec((None, M_CHUNK, 2 * M_HEADS), lambda b, c: (b, c, 0)),
            pl.BlockSpec((None, 2 * M_HEADS, M_CHUNK), lambda b, c: (b, 0, c)),
            pl.BlockSpec((1, 2 * M_HEADS), lambda b, c: (0, 0)),
            pl.BlockSpec((2 * M_HEADS, 1), lambda b, c: (0, 0)),
            pl.BlockSpec((CONV_K, 2 * M_W), lambda b, c: (0, 0)),
            pl.BlockSpec((1, 2 * M_W), lambda b, c: (0, 0)),
            pl.BlockSpec((1, M_W), lambda b, c: (0, 0)),
        ],
        out_specs=pl.BlockSpec((M_CHUNK, M_W), lambda b, c: (b * nc + c, 0)),
        scratch_shapes=[
            pltpu.VMEM((M_CHUNK + CONV_HALO, M_W), F32),
            pltpu.VMEM((M_CHUNK + CONV_HALO, M_W), F32),
            pltpu.VMEM((M_HEADS, M_HEAD_DIM, M_HEAD_DIM), F32),
            pltpu.VMEM((M_HEADS, SUBLANES, M_HEAD_DIM), F32),
            pltpu.VMEM((M_HEADS, SUBLANES, LANES), F32),
        ],
        compiler_params=_cparams(("arbitrary", "arbitrary")),
        name="mlstm",
    )(p, p, p, p, gcol, grow, bcol, brow, conv_w, conv_b, norm_g)


RT_E1, RT_E2, RT_W1, RT_W2 = 0, 1, 2, 3


def _route(logits):
    lane = lax.broadcasted_iota(jnp.int32, logits.shape, 1).astype(F32)
    big = float(LANES)
    is_g = lane < N_EXPERT_GROUPS
    gl = jnp.where(is_g, logits, -jnp.inf)
    gexp = jnp.exp(gl - jnp.max(gl, axis=1, keepdims=True))
    gprob = gexp / jnp.sum(gexp, axis=1, keepdims=True)
    g_w = jnp.max(gprob, axis=1, keepdims=True)
    g_top = jnp.min(jnp.where(is_g & (gprob == g_w), lane, big), axis=1, keepdims=True)
    lo = N_EXPERT_GROUPS + EXPERTS_PER_GROUP * g_top
    in_grp = (lane >= lo) & (lane < lo + EXPERTS_PER_GROUP)
    el = jnp.where(in_grp, logits, -jnp.inf)
    eexp = jnp.exp(el - jnp.max(el, axis=1, keepdims=True))
    eprob = eexp / jnp.sum(eexp, axis=1, keepdims=True)
    v1 = jnp.max(eprob, axis=1, keepdims=True)
    i1 = jnp.min(jnp.where(in_grp & (eprob == v1), lane, big), axis=1, keepdims=True)
    rest = jnp.where(in_grp & (lane != i1), eprob, -1.0)
    v2 = jnp.max(rest, axis=1, keepdims=True)
    i2 = jnp.min(jnp.where(rest == v2, lane, big), axis=1, keepdims=True)
    tot = v1 + v2
    w1 = g_w * (v1 / tot)
    w2 = g_w * (v2 / tot)
    e1 = i1 - N_EXPERT_GROUPS
    e2 = i2 - N_EXPERT_GROUPS
    rec = jnp.where(lane == RT_E1, e1, jnp.where(lane == RT_E2, e2, jnp.where(lane == RT_W1, w1, w2)))
    return jnp.where(lane <= RT_W2, rec, 0.0)


def _merge_kernel(ya_ref, ym_ref, g_ref, x_ref, mod_ref,
                  wpa_ref, wpm_ref, wout_ref, lng_ref, lnb_ref, wr_ref, br_ref,
                  x1_ref, u2_ref, rt_ref):
    pa = jnp.dot(ya_ref[...], wpa_ref[...], preferred_element_type=F32)
    pm = jnp.dot(ym_ref[...], wpm_ref[...], preferred_element_type=F32)
    merged = g_ref[:, :D_MODEL].astype(F32) * pa + g_ref[:, D_MODEL:].astype(F32) * pm
    mix = jnp.dot(merged.astype(BF16), wout_ref[...], preferred_element_type=F32)
    z = DEEPNORM_ALPHA * x_ref[...] + mod_ref[2:3, :] * mix
    x1 = _normalize(z) * lng_ref[...] + lnb_ref[...]
    x1_ref[...] = x1
    u2 = _normalize(x1) * (1.0 + mod_ref[4:5, :]) + mod_ref[3:4, :]
    u2_ref[...] = u2
    logits = jnp.dot(u2.astype(BF16), wr_ref[...], preferred_element_type=F32) + br_ref[...]
    rt_ref[...] = _route(logits)


def _merge(ya, ym, g, x2, mod, wpa, wpm, wout, lng, lnb, wr, br):
    tm = MERGE_TM
    tiles_per_batch = SEQ // tm
    rowblk = lambda w: pl.BlockSpec((tm, w), lambda m: (m, 0))
    const = lambda shape: pl.BlockSpec(shape, lambda m: (0,) * len(shape), pipeline_mode=pl.Buffered(1))
    return pl.pallas_call(
        _merge_kernel,
        out_shape=(
            jax.ShapeDtypeStruct((TOKENS, D_MODEL), F32),
            jax.ShapeDtypeStruct((TOKENS, D_MODEL), F32),
            jax.ShapeDtypeStruct((TOKENS, LANES), F32),
        ),
        grid=(TOKENS // tm,),
        in_specs=[
            rowblk(A_GROUP_W), rowblk(M_W), rowblk(2 * D_MODEL), rowblk(D_MODEL),
            pl.BlockSpec((None, 6, D_MODEL), lambda m: (m // tiles_per_batch, 0, 0)),
            const((A_GROUP_W, D_MODEL)), const((M_W, D_MODEL)), const((D_MODEL, D_MODEL)),
            const((1, D_MODEL)), const((1, D_MODEL)),
            const((D_MODEL, LANES)), const((1, LANES)),
        ],
        out_specs=(rowblk(D_MODEL), rowblk(D_MODEL), rowblk(LANES)),
        compiler_params=_cparams(("arbitrary",)),
        name="merge_ln1_route",
    )(ya, ym, g, x2, mod, wpa, wpm, wout, lng, lnb, wr, br)


GATHER_UNROLL = 8
CAST_ROWS = 128


def _issue_rows(src_ref, idx_ref, base, buf, slot, sem, nrows):
    def body(blk, carry):
        for j in range(GATHER_UNROLL):
            r = blk * GATHER_UNROLL + j
            pltpu.make_async_copy(src_ref.at[pl.ds(idx_ref[base + r], 1)],
                                  buf.at[slot, pl.ds(r, 1)], sem.at[slot]).start()
        return carry

    lax.fori_loop(0, nrows // GATHER_UNROLL, body, 0)


def _wait_rows(src_ref, buf, slot, sem, nrows):
    pltpu.make_async_copy(src_ref.at[pl.ds(0, nrows)], buf.at[slot], sem.at[slot]).wait()


def _expert_changed(te_ref, i):
    return (i == 0) | (te_ref[i] != te_ref[jnp.maximum(i - 1, 0)])


def _moe_kernel(te_ref, tn_ref, nu_ref, tok_ref, u_ref, wg_ref, wu_ref, wd_ref, o_ref,
                stg, stu, std, wgb, wub, wdb, xbuf, wsem, xsem):
    i = pl.program_id(0)
    nu = nu_ref[0]
    tm = o_ref.shape[0]

    def weight_copies(e):
        return (pltpu.make_async_copy(wg_ref.at[e], stg, wsem.at[0]),
                pltpu.make_async_copy(wu_ref.at[e], stu, wsem.at[1]),
                pltpu.make_async_copy(wd_ref.at[e], std, wsem.at[2]))

    @pl.when(i == 0)
    def _():
        _issue_rows(u_ref, tok_ref, 0, xbuf, 0, xsem, tm)
        for cp in weight_copies(te_ref[0]):
            cp.start(priority=1)

    @pl.when(i < nu)
    def _():
        slot = i % 2

        @pl.when(i + 1 < nu)
        def _():
            _issue_rows(u_ref, tok_ref, (i + 1) * tm, xbuf, 1 - slot, xsem, tm)

        @pl.when(_expert_changed(te_ref, i))
        def _():
            for cp in weight_copies(te_ref[i]):
                cp.wait()
            for src, dst in ((stg, wgb), (stu, wub), (std, wdb)):
                def cast_rows(ci, carry, src=src, dst=dst):
                    r = pl.multiple_of(ci * CAST_ROWS, CAST_ROWS)
                    dst[pl.ds(r, CAST_ROWS), :] = src[pl.ds(r, CAST_ROWS), :].astype(BF16)
                    return carry

                lax.fori_loop(0, src.shape[0] // CAST_ROWS, cast_rows, 0)

            @pl.when(tn_ref[i] >= 0)
            def _():
                for cp in weight_copies(tn_ref[i]):
                    cp.start(priority=1)

        _wait_rows(u_ref, xbuf, slot, xsem, tm)
        x = xbuf[slot].astype(BF16)
        a = jnp.dot(x, wgb[...], preferred_element_type=F32)
        b = jnp.dot(x, wub[...], preferred_element_type=F32)
        h = (_silu(a) * b).astype(BF16)
        o_ref[...] = jnp.dot(h, wdb[...], preferred_element_type=F32)

    @pl.when(i >= nu)
    def _():
        o_ref[...] = jnp.zeros_like(o_ref)


def _moe(te, te_next, n_used, row_token, u2, w_eg, w_eu, w_ed):
    tm = MOE_TM
    return pl.pallas_call(
        _moe_kernel,
        out_shape=jax.ShapeDtypeStruct((MOE_ROWS, D_MODEL), F32),
        grid_spec=pltpu.PrefetchScalarGridSpec(
            num_scalar_prefetch=4,
            grid=(MOE_TILES,),
            in_specs=[pl.BlockSpec(memory_space=pl.ANY)] * 4,
            out_specs=pl.BlockSpec((tm, D_MODEL), lambda i, te, tn, nu, tok: (i, 0)),
            scratch_shapes=[
                pltpu.VMEM((D_MODEL, D_FF_EXPERT), F32),
                pltpu.VMEM((D_MODEL, D_FF_EXPERT), F32),
                pltpu.VMEM((D_FF_EXPERT, D_MODEL), F32),
                pltpu.VMEM((D_MODEL, D_FF_EXPERT), BF16),
                pltpu.VMEM((D_MODEL, D_FF_EXPERT), BF16),
                pltpu.VMEM((D_FF_EXPERT, D_MODEL), BF16),
                pltpu.VMEM((2, tm, D_MODEL), F32),
                pltpu.SemaphoreType.DMA((3,)),
                pltpu.SemaphoreType.DMA((2,)),
            ],
        ),
        compiler_params=_cparams(("arbitrary",)),
        name="moe_experts",
    )(te, te_next, n_used, row_token, u2, w_eg, w_eu, w_ed)


def _final_kernel(pos_ref, o_ref, x1_ref, rt_ref, mod_ref, lng_ref, lnb_ref, y_ref, buf, sem):
    i = pl.program_id(0)
    tb = x1_ref.shape[0]
    slot = i % 2

    @pl.when(i == 0)
    def _():
        _issue_rows(o_ref, pos_ref, 0, buf, 0, sem, 2 * tb)

    @pl.when(i + 1 < pl.num_programs(0))
    def _():
        _issue_rows(o_ref, pos_ref, (i + 1) * (2 * tb), buf, 1 - slot, sem, 2 * tb)

    _wait_rows(o_ref, buf, slot, sem, 2 * tb)
    ffn = (rt_ref[:, RT_W1:RT_W1 + 1] * buf[slot, 0:tb, :]
           + rt_ref[:, RT_W2:RT_W2 + 1] * buf[slot, tb:2 * tb, :])
    z = DEEPNORM_ALPHA * x1_ref[...] + mod_ref[5:6, :] * ffn
    y_ref[...] = _normalize(z) * lng_ref[...] + lnb_ref[...]


def _final(pos_tiles, moe_out, x1, rt, mod, lng, lnb):
    tb = FINAL_TB
    tiles_per_batch = SEQ // tb
    return pl.pallas_call(
        _final_kernel,
        out_shape=jax.ShapeDtypeStruct((TOKENS, D_MODEL), F32),
        grid_spec=pltpu.PrefetchScalarGridSpec(
            num_scalar_prefetch=1,
            grid=(TOKENS // tb,),
            in_specs=[
                pl.BlockSpec(memory_space=pl.ANY),
                pl.BlockSpec((tb, D_MODEL), lambda i, pos: (i, 0)),
                pl.BlockSpec((tb, LANES), lambda i, pos: (i, 0)),
                pl.BlockSpec((None, 6, D_MODEL), lambda i, pos: (i // tiles_per_batch, 0, 0)),
                pl.BlockSpec((1, D_MODEL), lambda i, pos: (0, 0)),
                pl.BlockSpec((1, D_MODEL), lambda i, pos: (0, 0)),
            ],
            out_specs=pl.BlockSpec((tb, D_MODEL), lambda i, pos: (i, 0)),
            scratch_shapes=[pltpu.VMEM((2, 2 * tb, D_MODEL), F32), pltpu.SemaphoreType.DMA((2,))],
        ),
        compiler_params=_cparams(("arbitrary",)),
        name="combine_ln2",
    )(pos_tiles, moe_out, x1, rt, mod, lng, lnb)


def _routing_tables(e_ids):
    tm = MOE_TM
    e_flat = e_ids.reshape(-1)
    onehot = (e_flat[:, None] == jnp.arange(N_EXPERTS, dtype=jnp.int32)[None, :]).astype(jnp.int32)
    csum = jnp.cumsum(onehot, axis=0)
    rank = jnp.sum((csum - onehot) * onehot, axis=1)
    counts = csum[-1]
    padded = ((counts + tm - 1) // tm) * tm
    pend = jnp.cumsum(padded)
    pstart = pend - padded
    dest = pstart[e_flat] + rank
    tok = jnp.arange(2 * TOKENS, dtype=jnp.int32) // 2
    row_token = jnp.zeros((MOE_ROWS,), jnp.int32).at[dest].set(tok)
    n_used = (pend[-1] // tm).astype(jnp.int32)
    tile_start = jnp.arange(MOE_TILES, dtype=jnp.int32) * tm
    te = jnp.sum((pend[None, :] <= tile_start[:, None]).astype(jnp.int32), axis=1)
    te = jnp.minimum(te, N_EXPERTS - 1)
    te_last = te[jnp.maximum(n_used - 1, 0)]
    te = jnp.where(jnp.arange(MOE_TILES) < n_used, te, te_last)
    next_run = pend[te] // tm
    te_next = jnp.where(next_run < n_used, te[jnp.minimum(next_run, MOE_TILES - 1)], -1).astype(jnp.int32)
    pos = dest.reshape(TOKENS, 2)
    return row_token, te, te_next, n_used.reshape(1), pos


def _rope_tables():
    inv = ROPE_THETA ** (-jnp.arange(0, HEAD_DIM_A, 2, dtype=F32) / HEAD_DIM_A)
    ang = jnp.arange(SEQ, dtype=F32)[:, None] * inv[None, :]
    cos = jnp.cos(ang)
    sin = jnp.sin(ang)
    return jnp.concatenate([cos, cos], axis=-1), jnp.concatenate([-sin, sin], axis=-1)


def kernel(x, c, w_ada, b_ada, w_in, b_mgate, conv_w, conv_b, m_norm_g, w_proj_a, w_proj_m, w_gate, b_gate,
           w_out, ln1_g, ln1_b, w_rg, b_rg, w_re, b_re, w_eg, w_eu, w_ed, ln2_g, ln2_b):
    assert x.shape == (BATCH, SEQ, D_MODEL) and w_ada.shape[0] == 1
    l = 0
    x2 = x.reshape(TOKENS, D_MODEL)

    c_pad = jnp.zeros((ADA_ROWS, D_MODEL), F32).at[:BATCH].set(c)
    mod = _ada(c_pad, w_ada[l], b_ada[l][None, :])[:BATCH].reshape(BATCH, 6, D_MODEL)

    w_in_t = jnp.swapaxes(w_in[l], 0, 1)
    w_if_t = jnp.zeros((LANES, D_MODEL), F32).at[:2 * M_HEADS].set(w_in_t[N_IN_MAIN:])
    p, gates, u = _inproj(x2, mod, w_in_t, w_if_t)
    g = _gateproj(u, w_gate[l], b_gate[l][None, :])

    cos_t, sin_t = _rope_tables()
    ya = _attention(p, cos_t, sin_t)

    gcol = gates[:, :2 * M_HEADS].reshape(BATCH, SEQ, 2 * M_HEADS)
    grow = jnp.transpose(gcol, (0, 2, 1))
    ym = _mlstm(p, gcol, grow, b_mgate[l][None, :], b_mgate[l][:, None], conv_w[l], conv_b[l][None, :],
                m_norm_g[l][None, :])

    wr = (jnp.zeros((D_MODEL, LANES), F32)
          .at[:, :N_EXPERT_GROUPS].set(w_rg[l])
          .at[:, N_EXPERT_GROUPS:N_EXPERT_GROUPS + N_EXPERTS].set(w_re[l])).astype(BF16)
    br = (jnp.zeros((1, LANES), F32)
          .at[0, :N_EXPERT_GROUPS].set(b_rg[l])
          .at[0, N_EXPERT_GROUPS:N_EXPERT_GROUPS + N_EXPERTS].set(b_re[l]))
    x1, u2, rt = _merge(ya, ym, g, x2, mod,
                        w_proj_a[l].astype(BF16), w_proj_m[l].astype(BF16), w_out[l].astype(BF16),
                        ln1_g[l][None, :], ln1_b[l][None, :], wr, br)

    e_ids = rt[:, RT_E1:RT_E2 + 1].astype(jnp.int32)
    row_token, te, te_next, n_used, pos = _routing_tables(e_ids)

    mo = _moe(te, te_next, n_used, row_token, u2,
              w_eg[l].reshape(N_EXPERTS, D_MODEL, D_FF_EXPERT), w_eu[l].reshape(N_EXPERTS, D_MODEL, D_FF_EXPERT),
              w_ed[l].reshape(N_EXPERTS, D_FF_EXPERT, D_MODEL))

    nt = TOKENS // FINAL_TB
    pos_tiles = jnp.transpose(pos.reshape(nt, FINAL_TB, 2), (0, 2, 1)).reshape(-1)
    y = _final(pos_tiles, mo, x1, rt, mod, ln2_g[l][None, :], ln2_b[l][None, :])
    return y.reshape(BATCH, SEQ, D_MODEL)
```

```python
import functools

import jax
import jax.numpy as jnp
from jax import lax
from jax.experimental import pallas as pl
from jax.experimental.pallas import tpu as pltpu

F32 = jnp.float32
BF16 = jnp.bfloat16

D_MODEL = 2048
BATCH = 4
SEQ = 2048
TOKENS = BATCH * SEQ
DIL_CONFIGS = ((128, 1), (512, 4), (2048, 16))
N_DIL_GROUPS = 3
HEADS_PER_GROUP = 4
HEAD_DIM_A = 128
ATT_BLOCK = 128
ROPE_THETA = 10000.0
A_GROUP_W = HEADS_PER_GROUP * HEAD_DIM_A
A_QKV_W = N_DIL_GROUPS * A_GROUP_W
M_HEADS = 4
M_HEAD_DIM = 256
M_W = M_HEADS * M_HEAD_DIM
M_CHUNK = 128
CONV_K = 4
N_IN_MAIN = 3 * A_QKV_W + 4 * M_W
N_EXPERT_GROUPS = 4
EXPERTS_PER_GROUP = 8
N_EXPERTS = N_EXPERT_GROUPS * EXPERTS_PER_GROUP
D_FF_EXPERT = 1024
DEEPNORM_ALPHA = 2.0 ** 0.25
LN_EPS = 1e-5

LANES = 128
SUBLANES = 8
VMEM_LIMIT_BYTES = 56 * 1024 * 1024

PROJ_TN = 512
GATE_TN = 1024
PROJ_TM = 1024
P_TILES_IN = N_IN_MAIN // PROJ_TN
P_ATT_TILES = 3 * A_QKV_W // PROJ_TN
P_M_TILES = P_TILES_IN - P_ATT_TILES
P_WIDTH = N_IN_MAIN
P_ATT_OFF = P_M_TILES * PROJ_TN
P_QM_BLK, P_KM_BLK, P_VM_BLK, P_OM_BLK = 0, 1, 2, 3
MERGE_TM = 256
MOE_TM = 256
MOE_ROWS = 2 * TOKENS + N_EXPERTS * MOE_TM
MOE_TILES = MOE_ROWS // MOE_TM
MOE_ROW_SLOTS = 3
FINAL_TB = 256


def _cparams(sem, vmem=VMEM_LIMIT_BYTES):
    return pltpu.CompilerParams(dimension_semantics=sem, vmem_limit_bytes=vmem)


def _normalize(x):
    mu = jnp.mean(x, axis=-1, keepdims=True)
    xc = x - mu
    var = jnp.mean(xc * xc, axis=-1, keepdims=True)
    return xc * lax.rsqrt(var + LN_EPS)


def _silu(x):
    return x * jax.nn.sigmoid(x)


def _log_sigmoid(x):
    return jnp.minimum(x, 0.0) - jnp.log(1.0 + jnp.exp(-jnp.abs(x)))


_CONTRACT_LAST = (((1,), (1,)), ((), ()))
LOG2E = 1.4426950408889634


ADA_TN = 1024
ADA_ROWS = 16


def _ada_kernel(c_ref, w_ref, b_ref, o_ref):
    sc = _silu(c_ref[...]).astype(BF16)
    o_ref[...] = jnp.dot(sc, w_ref[...].astype(BF16), preferred_element_type=F32) + b_ref[...]


def _ada(c_pad, w_ada, b_ada):
    n = w_ada.shape[1]
    return pl.pallas_call(
        _ada_kernel,
        out_shape=jax.ShapeDtypeStruct((ADA_ROWS, n), F32),
        grid=(n // ADA_TN,),
        in_specs=[
            pl.BlockSpec((ADA_ROWS, D_MODEL), lambda j: (0, 0)),
            pl.BlockSpec((D_MODEL, ADA_TN), lambda j: (0, j)),
            pl.BlockSpec((1, ADA_TN), lambda j: (0, j)),
        ],
        out_specs=pl.BlockSpec((ADA_ROWS, ADA_TN), lambda j: (0, j)),
        compiler_params=_cparams(("arbitrary",)),
        name="ada_mod",
    )(c_pad, w_ada, b_ada)


LN_CHUNK = 256


def _inproj_kernel(x_ref, mod_ref, w_ref, wif_ref, p_ref, g_ref, u_ref):
    n = pl.program_id(1)

    @pl.when(n == 0)
    def _():
        shift = mod_ref[0:1, :]
        scale = 1.0 + mod_ref[1:2, :]
        wif = wif_ref[...].astype(BF16)

        def body(ci, carry):
            r = pl.multiple_of(ci * LN_CHUNK, LN_CHUNK)
            u = (_normalize(x_ref[pl.ds(r, LN_CHUNK), :]) * scale + shift).astype(BF16)
            u_ref[pl.ds(r, LN_CHUNK), :] = u
            g_ref[pl.ds(r, LN_CHUNK), :] = lax.dot_general(u, wif, _CONTRACT_LAST, preferred_element_type=F32)
            return carry

        lax.fori_loop(0, PROJ_TM // LN_CHUNK, body, 0)

    p_ref[...] = lax.dot_general(u_ref[...], w_ref[...].astype(BF16), _CONTRACT_LAST,
                                 preferred_element_type=F32)


def _inproj(x2, mod, w_in_t, w_if_t):
    tiles_per_batch = SEQ // PROJ_TM
    return pl.pallas_call(
        _inproj_kernel,
        out_shape=(
            jax.ShapeDtypeStruct((TOKENS, P_WIDTH), F32),
            jax.ShapeDtypeStruct((TOKENS, LANES), F32),
            jax.ShapeDtypeStruct((TOKENS, D_MODEL), BF16),
        ),
        grid=(TOKENS // PROJ_TM, P_TILES_IN),
        in_specs=[
            pl.BlockSpec((PROJ_TM, D_MODEL), lambda m, n: (m, 0)),
            pl.BlockSpec((None, 6, D_MODEL), lambda m, n: (m // tiles_per_batch, 0, 0)),
            pl.BlockSpec((PROJ_TN, D_MODEL), lambda m, n: (n, 0)),
            pl.BlockSpec((LANES, D_MODEL), lambda m, n: (0, 0)),
        ],
        out_specs=(
            pl.BlockSpec((PROJ_TM, PROJ_TN),
                         lambda m, n: (m, jnp.where(n < P_ATT_TILES, n + P_M_TILES, n - P_ATT_TILES))),
            pl.BlockSpec((PROJ_TM, LANES), lambda m, n: (m, 0)),
            pl.BlockSpec((PROJ_TM, D_MODEL), lambda m, n: (m, 0)),
        ),
        compiler_params=_cparams(("arbitrary", "arbitrary")),
        name="in_proj",
    )(x2, mod, w_in_t, w_if_t)


def _gateproj_kernel(u_ref, w_ref, b_ref, o_ref):
    acc = jnp.dot(u_ref[...], w_ref[...].astype(BF16), preferred_element_type=F32)
    o_ref[...] = jax.nn.sigmoid(acc + b_ref[...]).astype(BF16)


def _gateproj(u, w_gate, b_gate):
    n = w_gate.shape[1]
    return pl.pallas_call(
        _gateproj_kernel,
        out_shape=jax.ShapeDtypeStruct((TOKENS, n), BF16),
        grid=(TOKENS // PROJ_TM, n // GATE_TN),
        in_specs=[
            pl.BlockSpec((PROJ_TM, D_MODEL), lambda m, j: (m, 0)),
            pl.BlockSpec((D_MODEL, GATE_TN), lambda m, j: (0, j)),
            pl.BlockSpec((1, GATE_TN), lambda m, j: (0, j)),
        ],
        out_specs=pl.BlockSpec((PROJ_TM, GATE_TN), lambda m, j: (m, j)),
        compiler_params=_cparams(("arbitrary", "arbitrary")),
        name="gate_proj",
    )(u, w_gate, b_gate)


def _rows(start, size, stride):
    return pl.ds(start, size) if stride == 1 else pl.ds(start, size, stride=stride)


def _attn_kernel(q0, k0, v0, q1, k1, v1, q2, k2, v2, cos_ref, sin_ref, y_ref,
                 qr_sc, kr_sc, o0, o1, o2, l0, l1, l2):
    cos = cos_ref[...]
    sin = sin_ref[...]
    scale = HEAD_DIM_A ** -0.5
    blk = ATT_BLOCK
    qi2 = lax.broadcasted_iota(jnp.int32, (blk, 2 * blk), 0)
    kc2 = lax.broadcasted_iota(jnp.int32, (blk, 2 * blk), 1)
    mask_prev_cur = (kc2 >= qi2) & (kc2 <= qi2 + blk)
    qi1 = lax.broadcasted_iota(jnp.int32, (blk, blk), 0)
    kc1 = lax.broadcasted_iota(jnp.int32, (blk, blk), 1)
    mask_cur = kc1 <= qi1
    half = HEAD_DIM_A // 2
    groups = ((q0, k0, v0, o0, l0), (q1, k1, v1, o1, l1), (q2, k2, v2, o2, l2))
    for (window, d), (q_ref, k_ref, v_ref, o_sc, l_sc) in zip(DIL_CONFIGS, groups):
        nb = SEQ // d // blk
        q = q_ref[...]
        k = k_ref[...]
        qr_sc[...] = q * cos + pltpu.roll(q, half, 1) * sin
        kr_sc[...] = k * cos + pltpu.roll(k, half, 1) * sin
        for r in range(d):
            for j in range(nb):
                start = r + j * blk * d
                cur = _rows(start, blk, d)
                qb = qr_sc[cur, :].astype(BF16)
                if j == 0:
                    keys = cur
                    mask = mask_cur
                else:
                    keys = _rows(start - blk * d, 2 * blk, d)
                    mask = mask_prev_cur
                kw = kr_sc[keys, :].astype(BF16)
                vb = v_ref[keys, :].astype(BF16)
                vw = jnp.concatenate([vb, jnp.ones_like(vb)], axis=1)
                s = lax.dot_general(qb, kw, _CONTRACT_LAST, preferred_element_type=F32)
                s = jnp.where(mask, s, -jnp.inf)
                mx = jnp.max(s, axis=-1, keepdims=True)
                p = jnp.exp2((s - mx) * (scale * LOG2E))
                pv = jnp.dot(p.astype(BF16), vw, preferred_element_type=F32)
                den = pv[:, HEAD_DIM_A:]
                o_sc[cur, :] = pv[:, :HEAD_DIM_A] / den
                l_sc[cur, :] = mx * scale + jnp.log(den)
    la = l0[...]
    lb = l1[...]
    lc = l2[...]
    mx = jnp.maximum(jnp.maximum(la, lb), lc)
    ea = jnp.exp(la - mx)
    eb = jnp.exp(lb - mx)
    ec = jnp.exp(lc - mx)
    den = ea + eb + ec
    y_ref[...] = ((ea / den) * o0[...] + (eb / den) * o1[...] + (ec / den) * o2[...]).astype(BF16)


def _attention(p, cos_t, sin_t):
    for window, d in DIL_CONFIGS:
        assert window // d == ATT_BLOCK and SEQ % (d * ATT_BLOCK) == 0
    col0 = P_ATT_OFF // HEAD_DIM_A

    def slab(part, g):
        off = col0 + (part * A_QKV_W + g * A_GROUP_W) // HEAD_DIM_A
        return pl.BlockSpec((SEQ, HEAD_DIM_A), lambda b, h: (b, off + h))

    in_specs = [slab(part, g) for g in range(N_DIL_GROUPS) for part in range(3)]
    table = pl.BlockSpec((SEQ, HEAD_DIM_A), lambda b, h: (0, 0))
    return pl.pallas_call(
        _attn_kernel,
        out_shape=jax.ShapeDtypeStruct((TOKENS, A_GROUP_W), BF16),
        grid=(BATCH, HEADS_PER_GROUP),
        in_specs=in_specs + [table, table],
        out_specs=pl.BlockSpec((SEQ, HEAD_DIM_A), lambda b, h: (b, h)),
        scratch_shapes=[pltpu.VMEM((SEQ, HEAD_DIM_A), F32)] * 8,
        compiler_params=_cparams(("arbitrary", "arbitrary")),
        name="dil_attn",
    )(*([p] * 9), cos_t, sin_t)


CONV_HALO = SUBLANES


def _mlstm_kernel(q_ref, k_ref, v_ref, og_ref, gc_ref, gr_ref, bc_ref, br_ref, cw_ref, cb_ref, ng_ref,
                  y_ref, xq_sc, xk_sc, ct_sc, n_sc, m_sc):
    c = pl.program_id(1)
    L = M_CHUNK
    lo = CONV_HALO

    @pl.when(c == 0)
    def _():
        xq_sc[0:lo, :] = jnp.zeros((lo, M_W), F32)
        xk_sc[0:lo, :] = jnp.zeros((lo, M_W), F32)
        ct_sc[...] = jnp.zeros_like(ct_sc)
        n_sc[...] = jnp.zeros_like(n_sc)
        m_sc[...] = jnp.zeros_like(m_sc)

    xq_sc[lo:lo + L, :] = q_ref[...]
    xk_sc[lo:lo + L, :] = k_ref[...]

    def conv(x_sc, w, b):
        acc = x_sc[lo:lo + L, :] * w[CONV_K - 1:CONV_K, :] + b
        for j in range(CONV_K - 1):
            off = lo - (CONV_K - 1) + j
            acc = acc + x_sc[off:off + L, :] * w[j:j + 1, :]
        return acc

    cw = cw_ref[...]
    cb = cb_ref[...]
    qc = _silu(conv(xq_sc, cw[:, :M_W], cb[:, :M_W]))
    kc = _silu(conv(xk_sc, cw[:, M_W:], cb[:, M_W:])) * (M_HEAD_DIM ** -0.5)
    xq_sc[0:lo, :] = xq_sc[L:L + lo, :]
    xk_sc[0:lo, :] = xk_sc[L:L + lo, :]

    row = lax.broadcasted_iota(jnp.int32, (L, L), 0)
    col = lax.broadcasted_iota(jnp.int32, (L, L), 1)
    causal = row >= col
    for h in range(M_HEADS):
        hs = slice(h * M_HEAD_DIM, (h + 1) * M_HEAD_DIM)
        q = qc[:, hs]
        k = kc[:, hs]
        i_col = gc_ref[:, h:h + 1] + bc_ref[:, h:h + 1]
        lf_col = _log_sigmoid(gc_ref[:, M_HEADS + h:M_HEADS + h + 1] + bc_ref[:, M_HEADS + h:M_HEADS + h + 1])
        i_row = gr_ref[h:h + 1, :] + br_ref[h:h + 1, :]
        lf_row = _log_sigmoid(gr_ref[M_HEADS + h:M_HEADS + h + 1, :] + br_ref[M_HEADS + h:M_HEADS + h + 1, :])
        b_col = jnp.sum(jnp.where(causal, lf_row, 0.0), axis=1, keepdims=True)
        b_row = jnp.sum(jnp.where(row <= col, lf_col, 0.0), axis=0, keepdims=True)
        log_d = jnp.where(causal, b_col - b_row + i_row, -jnp.inf)
        m_prev = m_sc[h, 0:1, 0:1]
        log_inter = b_col + m_prev
        m_t = jnp.maximum(jnp.max(log_d, axis=1, keepdims=True), log_inter)
        qb = q.astype(BF16)
        kb = k.astype(BF16)
        vb = v_ref[:, hs].astype(BF16)
        s = lax.dot_general(qb, kb, (((1,), (1,)), ((), ())), preferred_element_type=F32) * jnp.exp(log_d - m_t)
        inter = jnp.exp(log_inter - m_t)
        ct = ct_sc[h]
        n_row = n_sc[h, 0:1, :]
        num = (jnp.dot(s.astype(BF16), vb, preferred_element_type=F32)
               + inter * jnp.dot(qb, ct.astype(BF16), preferred_element_type=F32))
        den = jnp.sum(s, axis=1, keepdims=True) + inter * jnp.sum(q * n_row, axis=1, keepdims=True)
        hh = num / jnp.maximum(jnp.abs(den), jnp.exp(-m_t))
        m_new = m_t[L - 1:L, :]
        b_last = b_col[L - 1:L, :]
        w_col = jnp.exp(b_last - b_col + i_col - m_new)
        decay = jnp.exp(b_last + m_prev - m_new)
        kw = k * w_col
        ct_sc[h] = decay * ct + lax.dot_general(kw.astype(BF16), vb, (((0,), (0,)), ((), ())),
                                                preferred_element_type=F32)
        n_sc[h] = jnp.broadcast_to(decay * n_row + jnp.sum(kw, axis=0, keepdims=True), (SUBLANES, M_HEAD_DIM))
        m_sc[h] = jnp.broadcast_to(m_new, (SUBLANES, LANES))
        z = jax.nn.sigmoid(og_ref[:, hs]) * hh
        y_ref[:, hs] = (_normalize(z) * ng_ref[:, hs]).astype(BF16)


def _mlstm(p, gcol, grow, bcol, brow, conv_w, conv_b, norm_g):
    nc = SEQ // M_CHUNK
    slab = lambda blk: pl.BlockSpec((M_CHUNK, M_W), lambda b, c: (b * nc + c, blk))
    return pl.pallas_call(
        _mlstm_kernel,
        out_shape=jax.ShapeDtypeStruct((TOKENS, M_W), BF16),
        grid=(BATCH, nc),
        in_specs=[
            slab(P_QM_BLK), slab(P_KM_BLK), slab(P_VM_BLK), slab(P_OM_BLK),
            pl.BlockSpec((None, M_CHUNK, 2 * M_HEADS), lambda b, c: (b, c, 0)),
            pl.BlockSpec((None, 2 * M_HEADS, M_CHUNK), lambda b, c: (b, 0, c)),
            pl.BlockSpec((1, 2 * M_HEADS), lambda b, c: (0, 0)),
            pl.BlockSpec((2 * M_HEADS, 1), lambda b, c: (0, 0)),
            pl.BlockSpec((CONV_K, 2 * M_W), lambda b, c: (0, 0)),
            pl.BlockSpec((1, 2 * M_W), lambda b, c: (0, 0)),
            pl.BlockSpec((1, M_W), lambda b, c: (0, 0)),
        ],
        out_specs=pl.BlockSpec((M_CHUNK, M_W), lambda b, c: (b * nc + c, 0)),
        scratch_shapes=[
            pltpu.VMEM((M_CHUNK + CONV_HALO, M_W), F32),
            pltpu.VMEM((M_CHUNK + CONV_HALO, M_W), F32),
            pltpu.VMEM((M_HEADS, M_HEAD_DIM, M_HEAD_DIM), F32),
            pltpu.VMEM((M_HEADS, SUBLANES, M_HEAD_DIM), F32),
            pltpu.VMEM((M_HEADS, SUBLANES, LANES), F32),
        ],
        compiler_params=_cparams(("arbitrary", "arbitrary")),
        name="mlstm",
    )(p, p, p, p, gcol, grow, bcol, brow, conv_w, conv_b, norm_g)


RT_E1, RT_E2, RT_W1, RT_W2 = 0, 1, 2, 3


def _route(logits):
    lane = lax.broadcasted_iota(jnp.int32, logits.shape, 1).astype(F32)
    big = float(LANES)
    is_g = lane < N_EXPERT_GROUPS
    gl = jnp.where(is_g, logits, -jnp.inf)
    gexp = jnp.exp(gl - jnp.max(gl, axis=1, keepdims=True))
    gprob = gexp / jnp.sum(gexp, axis=1, keepdims=True)
    g_w = jnp.max(gprob, axis=1, keepdims=True)
    g_top = jnp.min(jnp.where(is_g & (gprob == g_w), lane, big), axis=1, keepdims=True)
    lo = N_EXPERT_GROUPS + EXPERTS_PER_GROUP * g_top
    in_grp = (lane >= lo) & (lane < lo + EXPERTS_PER_GROUP)
    el = jnp.where(in_grp, logits, -jnp.inf)
    eexp = jnp.exp(el - jnp.max(el, axis=1, keepdims=True))
    eprob = eexp / jnp.sum(eexp, axis=1, keepdims=True)
    v1 = jnp.max(eprob, axis=1, keepdims=True)
    i1 = jnp.min(jnp.where(in_grp & (eprob == v1), lane, big), axis=1, keepdims=True)
    rest = jnp.where(in_grp & (lane != i1), eprob, -1.0)
    v2 = jnp.max(rest, axis=1, keepdims=True)
    i2 = jnp.min(jnp.where(rest == v2, lane, big), axis=1, keepdims=True)
    tot = v1 + v2
    w1 = g_w * (v1 / tot)
    w2 = g_w * (v2 / tot)
    e1 = i1 - N_EXPERT_GROUPS
    e2 = i2 - N_EXPERT_GROUPS
    rec = jnp.where(lane == RT_E1, e1, jnp.where(lane == RT_E2, e2, jnp.where(lane == RT_W1, w1, w2)))
    return jnp.where(lane <= RT_W2, rec, 0.0)


def _merge_kernel(ya_ref, ym_ref, g_ref, x_ref, mod_ref,
                  wpa_ref, wpm_ref, wout_ref, lng_ref, lnb_ref, wr_ref, br_ref,
                  x1_ref, u2_ref, rt_ref):
    pa = jnp.dot(ya_ref[...], wpa_ref[...], preferred_element_type=F32)
    pm = jnp.dot(ym_ref[...], wpm_ref[...], preferred_element_type=F32)
    merged = g_ref[:, :D_MODEL].astype(F32) * pa + g_ref[:, D_MODEL:].astype(F32) * pm
    mix = jnp.dot(merged.astype(BF16), wout_ref[...], preferred_element_type=F32)
    z = DEEPNORM_ALPHA * x_ref[...] + mod_ref[2:3, :] * mix
    x1 = _normalize(z) * lng_ref[...] + lnb_ref[...]
    x1_ref[...] = x1
    u2 = _normalize(x1) * (1.0 + mod_ref[4:5, :]) + mod_ref[3:4, :]
    u2_ref[...] = u2
    logits = jnp.dot(u2.astype(BF16), wr_ref[...], preferred_element_type=F32) + br_ref[...]
    rt_ref[...] = _route(logits)


def _merge(ya, ym, g, x2, mod, wpa, wpm, wout, lng, lnb, wr, br):
    tm = MERGE_TM
    tiles_per_batch = SEQ // tm
    rowblk = lambda w: pl.BlockSpec((tm, w), lambda m: (m, 0))
    const = lambda shape: pl.BlockSpec(shape, lambda m: (0,) * len(shape), pipeline_mode=pl.Buffered(1))
    return pl.pallas_call(
        _merge_kernel,
        out_shape=(
            jax.ShapeDtypeStruct((TOKENS, D_MODEL), F32),
            jax.ShapeDtypeStruct((TOKENS, D_MODEL), F32),
            jax.ShapeDtypeStruct((TOKENS, LANES), F32),
        ),
        grid=(TOKENS // tm,),
        in_specs=[
            rowblk(A_GROUP_W), rowblk(M_W), rowblk(2 * D_MODEL), rowblk(D_MODEL),
            pl.BlockSpec((None, 6, D_MODEL), lambda m: (m // tiles_per_batch, 0, 0)),
            const((A_GROUP_W, D_MODEL)), const((M_W, D_MODEL)), const((D_MODEL, D_MODEL)),
            const((1, D_MODEL)), const((1, D_MODEL)),
            const((D_MODEL, LANES)), const((1, LANES)),
        ],
        out_specs=(rowblk(D_MODEL), rowblk(D_MODEL), rowblk(LANES)),
        compiler_params=_cparams(("arbitrary",)),
        name="merge_ln1_route",
    )(ya, ym, g, x2, mod, wpa, wpm, wout, lng, lnb, wr, br)


CAST_ROWS = 128


def _issue_rows(src_ref, idx_ref, base, buf, slot, sem, nrows):
    def body(blk, carry):
        for j in range(SUBLANES):
            row = idx_ref[base + blk * SUBLANES + j]
            src = src_ref.at[lax.shift_right_logical(row, 3), pl.ds(row & (SUBLANES - 1), 1)]
            pltpu.make_async_copy(src, buf.at[slot, blk, pl.ds(j, 1)], sem.at[slot]).start()
        return carry

    lax.fori_loop(0, nrows // SUBLANES, body, 0)


def _wait_rows(src_ref, buf, slot, sem, nrows):
    pltpu.make_async_copy(src_ref.at[pl.ds(0, nrows // SUBLANES)], buf.at[slot], sem.at[slot]).wait()


def _expert_changed(te_ref, i):
    return (i == 0) | (te_ref[i] != te_ref[jnp.maximum(i - 1, 0)])


def _moe_kernel(te_ref, tn_ref, nu_ref, tok_ref, u_ref, wg_ref, wu_ref, wd_ref, o_ref,
                stg, stu, std, wgb, wub, wdb, xbuf, wsem, xsem):
    i = pl.program_id(0)
    nu = nu_ref[0]
    tm = o_ref.shape[0]

    def weight_copies(e):
        return (pltpu.make_async_copy(wg_ref.at[e], stg, wsem.at[0]),
                pltpu.make_async_copy(wu_ref.at[e], stu, wsem.at[1]),
                pltpu.make_async_copy(wd_ref.at[e], std, wsem.at[2]))

    @pl.when(i == 0)
    def _():
        _issue_rows(u_ref, tok_ref, 0, xbuf, 0, xsem, tm)

        @pl.when(nu > 1)
        def _():
            _issue_rows(u_ref, tok_ref, tm, xbuf, 1, xsem, tm)

        for cp in weight_copies(te_ref[0]):
            cp.start(priority=1)

    @pl.when(i < nu)
    def _():
        slot = i % MOE_ROW_SLOTS

        @pl.when(i + 2 < nu)
        def _():
            _issue_rows(u_ref, tok_ref, (i + 2) * tm, xbuf, (i + 2) % MOE_ROW_SLOTS, xsem, tm)

        @pl.when(_expert_changed(te_ref, i))
        def _():
            for cp in weight_copies(te_ref[i]):
                cp.wait()
            for src, dst in ((stg, wgb), (stu, wub), (std, wdb)):
                def cast_rows(ci, carry, src=src, dst=dst):
                    r = pl.multiple_of(ci * CAST_ROWS, CAST_ROWS)
                    dst[pl.ds(r, CAST_ROWS), :] = src[pl.ds(r, CAST_ROWS), :].astype(BF16)
                    return carry

                lax.fori_loop(0, src.shape[0] // CAST_ROWS, cast_rows, 0)

            @pl.when(tn_ref[i] >= 0)
            def _():
                for cp in weight_copies(tn_ref[i]):
                    cp.start(priority=1)

        _wait_rows(u_ref, xbuf, slot, xsem, tm)
        x = xbuf[slot].reshape(tm, D_MODEL).astype(BF16)
        a = jnp.dot(x, wgb[...], preferred_element_type=F32)
        b = jnp.dot(x, wub[...], preferred_element_type=F32)
        h = (_silu(a) * b).astype(BF16)
        o_ref[...] = jnp.dot(h, wdb[...], preferred_element_type=F32)

    @pl.when(i >= nu)
    def _():
        o_ref[...] = jnp.zeros_like(o_ref)


def _moe(te, te_next, n_used, row_token, u2, w_eg, w_eu, w_ed):
    tm = MOE_TM
    return pl.pallas_call(
        _moe_kernel,
        out_shape=jax.ShapeDtypeStruct((MOE_ROWS, D_MODEL), F32),
        grid_spec=pltpu.PrefetchScalarGridSpec(
            num_scalar_prefetch=4,
            grid=(MOE_TILES,),
            in_specs=[pl.BlockSpec(memory_space=pl.ANY)] * 4,
            out_specs=pl.BlockSpec((tm, D_MODEL), lambda i, te, tn, nu, tok: (i, 0)),
            scratch_shapes=[
                pltpu.VMEM((D_MODEL, D_FF_EXPERT), F32),
                pltpu.VMEM((D_MODEL, D_FF_EXPERT), F32),
                pltpu.VMEM((D_FF_EXPERT, D_MODEL), F32),
                pltpu.VMEM((D_MODEL, D_FF_EXPERT), BF16),
                pltpu.VMEM((D_MODEL, D_FF_EXPERT), BF16),
                pltpu.VMEM((D_FF_EXPERT, D_MODEL), BF16),
                pltpu.VMEM((MOE_ROW_SLOTS, tm // SUBLANES, SUBLANES, D_MODEL), F32),
                pltpu.SemaphoreType.DMA((3,)),
                pltpu.SemaphoreType.DMA((MOE_ROW_SLOTS,)),
            ],
        ),
        compiler_params=_cparams(("arbitrary",)),
        name="moe_experts",
    )(te, te_next, n_used, row_token, u2.reshape(TOKENS // SUBLANES, SUBLANES, D_MODEL), w_eg, w_eu, w_ed)


def _final_kernel(pos_ref, o_ref, x1_ref, rt_ref, mod_ref, lng_ref, lnb_ref, y_ref, buf, sem):
    i = pl.program_id(0)
    tb = x1_ref.shape[0]
    slot = i % 2

    @pl.when(i == 0)
    def _():
        _issue_rows(o_ref, pos_ref, 0, buf, 0, sem, 2 * tb)

    @pl.when(i + 1 < pl.num_programs(0))
    def _():
        _issue_rows(o_ref, pos_ref, (i + 1) * (2 * tb), buf, 1 - slot, sem, 2 * tb)

    _wait_rows(o_ref, buf, slot, sem, 2 * tb)
    rows = buf[slot].reshape(2 * tb, D_MODEL)
    ffn = rt_ref[:, RT_W1:RT_W1 + 1] * rows[0:tb, :] + rt_ref[:, RT_W2:RT_W2 + 1] * rows[tb:2 * tb, :]
    z = DEEPNORM_ALPHA * x1_ref[...] + mod_ref[5:6, :] * ffn
    y_ref[...] = _normalize(z) * lng_ref[...] + lnb_ref[...]


def _final(pos_tiles, moe_out, x1, rt, mod, lng, lnb):
    tb = FINAL_TB
    tiles_per_batch = SEQ // tb
    return pl.pallas_call(
        _final_kernel,
        out_shape=jax.ShapeDtypeStruct((TOKENS, D_MODEL), F32),
        grid_spec=pltpu.PrefetchScalarGridSpec(
            num_scalar_prefetch=1,
            grid=(TOKENS // tb,),
            in_specs=[
                pl.BlockSpec(memory_space=pl.ANY),
                pl.BlockSpec((tb, D_MODEL), lambda i, pos: (i, 0)),
                pl.BlockSpec((tb, LANES), lambda i, pos: (i, 0)),
                pl.BlockSpec((None, 6, D_MODEL), lambda i, pos: (i // tiles_per_batch, 0, 0)),
                pl.BlockSpec((1, D_MODEL), lambda i, pos: (0, 0)),
                pl.BlockSpec((1, D_MODEL), lambda i, pos: (0, 0)),
            ],
            out_specs=pl.BlockSpec((tb, D_MODEL), lambda i, pos: (i, 0)),
            scratch_shapes=[pltpu.VMEM((2, 2 * tb // SUBLANES, SUBLANES, D_MODEL), F32),
                            pltpu.SemaphoreType.DMA((2,))],
        ),
        compiler_params=_cparams(("arbitrary",)),
        name="combine_ln2",
    )(pos_tiles, moe_out.reshape(MOE_ROWS // SUBLANES, SUBLANES, D_MODEL), x1, rt, mod, lng, lnb)


def _routing_tables(e_ids):
    tm = MOE_TM
    e_flat = e_ids.reshape(-1)
    onehot = (e_flat[:, None] == jnp.arange(N_EXPERTS, dtype=jnp.int32)[None, :]).astype(jnp.int32)
    csum = jnp.cumsum(onehot, axis=0)
    rank = jnp.sum((csum - onehot) * onehot, axis=1)
    counts = csum[-1]
    padded = ((counts + tm - 1) // tm) * tm
    pend = jnp.cumsum(padded)
    pstart = pend - padded
    dest = pstart[e_flat] + rank
    tok = jnp.arange(2 * TOKENS, dtype=jnp.int32) // 2
    row_token = jnp.zeros((MOE_ROWS,), jnp.int32).at[dest].set(tok)
    n_used = (pend[-1] // tm).astype(jnp.int32)
    tile_start = jnp.arange(MOE_TILES, dtype=jnp.int32) * tm
    te = jnp.sum((pend[None, :] <= tile_start[:, None]).astype(jnp.int32), axis=1)
    te = jnp.minimum(te, N_EXPERTS - 1)
    te_last = te[jnp.maximum(n_used - 1, 0)]
    te = jnp.where(jnp.arange(MOE_TILES) < n_used, te, te_last)
    next_run = pend[te] // tm
    te_next = jnp.where(next_run < n_used, te[jnp.minimum(next_run, MOE_TILES - 1)], -1).astype(jnp.int32)
    pos = dest.reshape(TOKENS, 2)
    return row_token, te, te_next, n_used.reshape(1), pos


def _rope_tables():
    inv = ROPE_THETA ** (-jnp.arange(0, HEAD_DIM_A, 2, dtype=F32) / HEAD_DIM_A)
    ang = jnp.arange(SEQ, dtype=F32)[:, None] * inv[None, :]
    cos = jnp.cos(ang)
    sin = jnp.sin(ang)
    return jnp.concatenate([cos, cos], axis=-1), jnp.concatenate([-sin, sin], axis=-1)


def kernel(x, c, w_ada, b_ada, w_in, b_mgate, conv_w, conv_b, m_norm_g, w_proj_a, w_proj_m, w_gate, b_gate,
           w_out, ln1_g, ln1_b, w_rg, b_rg, w_re, b_re, w_eg, w_eu, w_ed, ln2_g, ln2_b):
    assert x.shape == (BATCH, SEQ, D_MODEL) and w_ada.shape[0] == 1
    l = 0
    x2 = x.reshape(TOKENS, D_MODEL)

    c_pad = jnp.zeros((ADA_ROWS, D_MODEL), F32).at[:BATCH].set(c)
    mod = _ada(c_pad, w_ada[l], b_ada[l][None, :])[:BATCH].reshape(BATCH, 6, D_MODEL)

    w_in_t = jnp.swapaxes(w_in[l], 0, 1)
    w_if_t = jnp.zeros((LANES, D_MODEL), F32).at[:2 * M_HEADS].set(w_in_t[N_IN_MAIN:])
    p, gates, u = _inproj(x2, mod, w_in_t, w_if_t)
    g = _gateproj(u, w_gate[l], b_gate[l][None, :])

    cos_t, sin_t = _rope_tables()
    ya = _attention(p, cos_t, sin_t)

    gcol = gates[:, :2 * M_HEADS].reshape(BATCH, SEQ, 2 * M_HEADS)
    grow = jnp.transpose(gcol, (0, 2, 1))
    ym = _mlstm(p, gcol, grow, b_mgate[l][None, :], b_mgate[l][:, None], conv_w[l], conv_b[l][None, :],
                m_norm_g[l][None, :])

    wr = (jnp.zeros((D_MODEL, LANES), F32)
          .at[:, :N_EXPERT_GROUPS].set(w_rg[l])
          .at[:, N_EXPERT_GROUPS:N_EXPERT_GROUPS + N_EXPERTS].set(w_re[l])).astype(BF16)
    br = (jnp.zeros((1, LANES), F32)
          .at[0, :N_EXPERT_GROUPS].set(b_rg[l])
          .at[0, N_EXPERT_GROUPS:N_EXPERT_GROUPS + N_EXPERTS].set(b_re[l]))
    x1, u2, rt = _merge(ya, ym, g, x2, mod,
                        w_proj_a[l].astype(BF16), w_proj_m[l].astype(BF16), w_out[l].astype(BF16),
                        ln1_g[l][None, :], ln1_b[l][None, :], wr, br)

    e_ids = rt[:, RT_E1:RT_E2 + 1].astype(jnp.int32)
    row_token, te, te_next, n_used, pos = _routing_tables(e_ids)

    mo = _moe(te, te_next, n_used, row_token, u2,
              w_eg[l].reshape(N_EXPERTS, D_MODEL, D_FF_EXPERT), w_eu[l].reshape(N_EXPERTS, D_MODEL, D_FF_EXPERT),
              w_ed[l].reshape(N_EXPERTS, D_FF_EXPERT, D_MODEL))

    nt = TOKENS // FINAL_TB
    pos_tiles = jnp.transpose(pos.reshape(nt, FINAL_TB, 2), (0, 2, 1)).reshape(-1)
    y = _final(pos_tiles, mo, x1, rt, mod, ln2_g[l][None, :], ln2_b[l][None, :])
    return y.reshape(BATCH, SEQ, D_MODEL)
```

```python
import functools

import jax
import jax.numpy as jnp
from jax import lax
from jax.experimental import pallas as pl
from jax.experimental.pallas import tpu as pltpu

F32 = jnp.float32
BF16 = jnp.bfloat16

D_MODEL = 2048
BATCH = 4
SEQ = 2048
TOKENS = BATCH * SEQ
DIL_CONFIGS = ((128, 1), (512, 4), (2048, 16))
N_DIL_GROUPS = 3
HEADS_PER_GROUP = 4
HEAD_DIM_A = 128
ATT_BLOCK = 128
ROPE_THETA = 10000.0
A_GROUP_W = HEADS_PER_GROUP * HEAD_DIM_A
A_QKV_W = N_DIL_GROUPS * A_GROUP_W
M_HEADS = 4
M_HEAD_DIM = 256
M_W = M_HEADS * M_HEAD_DIM
M_CHUNK = 128
CONV_K = 4
N_IN_MAIN = 3 * A_QKV_W + 4 * M_W
N_EXPERT_GROUPS = 4
EXPERTS_PER_GROUP = 8
N_EXPERTS = N_EXPERT_GROUPS * EXPERTS_PER_GROUP
D_FF_EXPERT = 1024
DEEPNORM_ALPHA = 2.0 ** 0.25
LN_EPS = 1e-5

LANES = 128
SUBLANES = 8
VMEM_LIMIT_BYTES = 56 * 1024 * 1024

PROJ_TN = 512
GATE_TN = 1024
PROJ_TM = 1024
P_TILES_IN = N_IN_MAIN // PROJ_TN
P_ATT_TILES = 3 * A_QKV_W // PROJ_TN
P_M_TILES = P_TILES_IN - P_ATT_TILES
P_WIDTH = N_IN_MAIN
P_ATT_OFF = P_M_TILES * PROJ_TN
P_QM_BLK, P_KM_BLK, P_VM_BLK, P_OM_BLK = 0, 1, 2, 3
MERGE_TM = 256
MOE_TM = 256
MOE_ROWS = 2 * TOKENS + N_EXPERTS * MOE_TM
MOE_TILES = MOE_ROWS // MOE_TM
MOE_ROW_SLOTS = 3
FINAL_TB = 256


def _cparams(sem, vmem=VMEM_LIMIT_BYTES):
    return pltpu.CompilerParams(dimension_semantics=sem, vmem_limit_bytes=vmem)


def _normalize(x):
    mu = jnp.mean(x, axis=-1, keepdims=True)
    xc = x - mu
    var = jnp.mean(xc * xc, axis=-1, keepdims=True)
    return xc * lax.rsqrt(var + LN_EPS)


def _silu(x):
    return x * jax.nn.sigmoid(x)


def _log_sigmoid(x):
    return jnp.minimum(x, 0.0) - jnp.log(1.0 + jnp.exp(-jnp.abs(x)))


_CONTRACT_LAST = (((1,), (1,)), ((), ()))
LOG2E = 1.4426950408889634


ADA_TN = 1024
ADA_ROWS = 16


def _ada_kernel(c_ref, w_ref, b_ref, o_ref):
    sc = _silu(c_ref[...]).astype(BF16)
    o_ref[...] = jnp.dot(sc, w_ref[...].astype(BF16), preferred_element_type=F32) + b_ref[...]


def _ada(c_pad, w_ada, b_ada):
    n = w_ada.shape[1]
    return pl.pallas_call(
        _ada_kernel,
        out_shape=jax.ShapeDtypeStruct((ADA_ROWS, n), F32),
        grid=(n // ADA_TN,),
        in_specs=[
            pl.BlockSpec((ADA_ROWS, D_MODEL), lambda j: (0, 0)),
            pl.BlockSpec((D_MODEL, ADA_TN), lambda j: (0, j)),
            pl.BlockSpec((1, ADA_TN), lambda j: (0, j)),
        ],
        out_specs=pl.BlockSpec((ADA_ROWS, ADA_TN), lambda j: (0, j)),
        compiler_params=_cparams(("arbitrary",)),
        name="ada_mod",
    )(c_pad, w_ada, b_ada)


LN_CHUNK = 256


def _inproj_kernel(x_ref, mod_ref, w_ref, wif_ref, p_ref, g_ref, u_ref):
    n = pl.program_id(1)

    @pl.when(n == 0)
    def _():
        shift = mod_ref[0:1, :]
        scale = 1.0 + mod_ref[1:2, :]
        wif = wif_ref[...].astype(BF16)
        w = w_ref[...].astype(BF16)
        for ci in range(PROJ_TM // LN_CHUNK):
            rows = slice(ci * LN_CHUNK, (ci + 1) * LN_CHUNK)
            u = (_normalize(x_ref[rows, :]) * scale + shift).astype(BF16)
            u_ref[rows, :] = u
            g_ref[rows, :] = lax.dot_general(u, wif, _CONTRACT_LAST, preferred_element_type=F32)
            p_ref[rows, :] = lax.dot_general(u, w, _CONTRACT_LAST, preferred_element_type=F32)

    @pl.when(n > 0)
    def _():
        p_ref[...] = lax.dot_general(u_ref[...], w_ref[...].astype(BF16), _CONTRACT_LAST,
                                     preferred_element_type=F32)


def _inproj(x2, mod, w_in_t, w_if_t):
    tiles_per_batch = SEQ // PROJ_TM
    return pl.pallas_call(
        _inproj_kernel,
        out_shape=(
            jax.ShapeDtypeStruct((TOKENS, P_WIDTH), F32),
            jax.ShapeDtypeStruct((TOKENS, LANES), F32),
            jax.ShapeDtypeStruct((TOKENS, D_MODEL), BF16),
        ),
        grid=(TOKENS // PROJ_TM, P_TILES_IN),
        in_specs=[
            pl.BlockSpec((PROJ_TM, D_MODEL), lambda m, n: (m, 0)),
            pl.BlockSpec((None, 6, D_MODEL), lambda m, n: (m // tiles_per_batch, 0, 0)),
            pl.BlockSpec((PROJ_TN, D_MODEL), lambda m, n: (n, 0)),
            pl.BlockSpec((LANES, D_MODEL), lambda m, n: (0, 0)),
        ],
        out_specs=(
            pl.BlockSpec((PROJ_TM, PROJ_TN),
                         lambda m, n: (m, jnp.where(n < P_ATT_TILES, n + P_M_TILES, n - P_ATT_TILES))),
            pl.BlockSpec((PROJ_TM, LANES), lambda m, n: (m, 0)),
            pl.BlockSpec((PROJ_TM, D_MODEL), lambda m, n: (m, 0)),
        ),
        compiler_params=_cparams(("arbitrary", "arbitrary")),
        name="in_proj",
    )(x2, mod, w_in_t, w_if_t)


def _gateproj_kernel(u_ref, w_ref, b_ref, o_ref):
    acc = jnp.dot(u_ref[...], w_ref[...].astype(BF16), preferred_element_type=F32)
    o_ref[...] = jax.nn.sigmoid(acc + b_ref[...]).astype(BF16)


def _gateproj(u, w_gate, b_gate):
    n = w_gate.shape[1]
    return pl.pallas_call(
        _gateproj_kernel,
        out_shape=jax.ShapeDtypeStruct((TOKENS, n), BF16),
        grid=(TOKENS // PROJ_TM, n // GATE_TN),
        in_specs=[
            pl.BlockSpec((PROJ_TM, D_MODEL), lambda m, j: (m, 0)),
            pl.BlockSpec((D_MODEL, GATE_TN), lambda m, j: (0, j)),
            pl.BlockSpec((1, GATE_TN), lambda m, j: (0, j)),
        ],
        out_specs=pl.BlockSpec((PROJ_TM, GATE_TN), lambda m, j: (m, j)),
        compiler_params=_cparams(("arbitrary", "arbitrary")),
        name="gate_proj",
    )(u, w_gate, b_gate)


def _rows(start, size, stride):
    return pl.ds(start, size) if stride == 1 else pl.ds(start, size, stride=stride)


def _attn_kernel(q0, k0, v0, q1, k1, v1, q2, k2, v2, cos_ref, sin_ref, y_ref,
                 qr_sc, kr_sc, o0, o1, o2, l0, l1, l2):
    cos = cos_ref[...]
    sin = sin_ref[...]
    scale = HEAD_DIM_A ** -0.5
    blk = ATT_BLOCK
    qi2 = lax.broadcasted_iota(jnp.int32, (blk, 2 * blk), 0)
    kc2 = lax.broadcasted_iota(jnp.int32, (blk, 2 * blk), 1)
    mask_prev_cur = (kc2 >= qi2) & (kc2 <= qi2 + blk)
    qi1 = lax.broadcasted_iota(jnp.int32, (blk, blk), 0)
    kc1 = lax.broadcasted_iota(jnp.int32, (blk, blk), 1)
    mask_cur = kc1 <= qi1
    half = HEAD_DIM_A // 2
    groups = ((q0, k0, v0, o0, l0), (q1, k1, v1, o1, l1), (q2, k2, v2, o2, l2))
    for (window, d), (q_ref, k_ref, v_ref, o_sc, l_sc) in zip(DIL_CONFIGS, groups):
        nb = SEQ // d // blk
        q = q_ref[...]
        k = k_ref[...]
        qr_sc[...] = q * cos + pltpu.roll(q, half, 1) * sin
        kr_sc[...] = k * cos + pltpu.roll(k, half, 1) * sin
        for r in range(d):
            for j in range(nb):
                start = r + j * blk * d
                cur = _rows(start, blk, d)
                qb = qr_sc[cur, :].astype(BF16)
                if j == 0:
                    keys = cur
                    mask = mask_cur
                else:
                    keys = _rows(start - blk * d, 2 * blk, d)
                    mask = mask_prev_cur
                kw = kr_sc[keys, :].astype(BF16)
                vb = v_ref[keys, :].astype(BF16)
                vw = jnp.concatenate([vb, jnp.ones_like(vb)], axis=1)
                s = lax.dot_general(qb, kw, _CONTRACT_LAST, preferred_element_type=F32)
                s = jnp.where(mask, s, -jnp.inf)
                mx = jnp.max(s, axis=-1, keepdims=True)
                p = jnp.exp2((s - mx) * (scale * LOG2E))
                pv = jnp.dot(p.astype(BF16), vw, preferred_element_type=F32)
                den = pv[:, HEAD_DIM_A:]
                o_sc[cur, :] = pv[:, :HEAD_DIM_A] / den
                l_sc[cur, :] = mx * scale + jnp.log(den)
    la = l0[...]
    lb = l1[...]
    lc = l2[...]
    mx = jnp.maximum(jnp.maximum(la, lb), lc)
    ea = jnp.exp(la - mx)
    eb = jnp.exp(lb - mx)
    ec = jnp.exp(lc - mx)
    den = ea + eb + ec
    y_ref[...] = ((ea / den) * o0[...] + (eb / den) * o1[...] + (ec / den) * o2[...]).astype(BF16)


def _attention(p, cos_t, sin_t):
    for window, d in DIL_CONFIGS:
        assert window // d == ATT_BLOCK and SEQ % (d * ATT_BLOCK) == 0
    col0 = P_ATT_OFF // HEAD_DIM_A

    def slab(part, g):
        off = col0 + (part * A_QKV_W + g * A_GROUP_W) // HEAD_DIM_A
        return pl.BlockSpec((SEQ, HEAD_DIM_A), lambda b, h: (b, off + h))

    in_specs = [slab(part, g) for g in range(N_DIL_GROUPS) for part in range(3)]
    table = pl.BlockSpec((SEQ, HEAD_DIM_A), lambda b, h: (0, 0))
    return pl.pallas_call(
        _attn_kernel,
        out_shape=jax.ShapeDtypeStruct((TOKENS, A_GROUP_W), BF16),
        grid=(BATCH, HEADS_PER_GROUP),
        in_specs=in_specs + [table, table],
        out_specs=pl.BlockSpec((SEQ, HEAD_DIM_A), lambda b, h: (b, h)),
        scratch_shapes=[pltpu.VMEM((SEQ, HEAD_DIM_A), F32)] * 8,
        compiler_params=_cparams(("arbitrary", "arbitrary")),
        name="dil_attn",
    )(*([p] * 9), cos_t, sin_t)


CONV_HALO = SUBLANES


def _mlstm_kernel(q_ref, k_ref, v_ref, og_ref, gc_ref, gr_ref, bc_ref, br_ref, cw_ref, cb_ref, ng_ref,
                  y_ref, xq_sc, xk_sc, ct_sc, n_sc, m_sc):
    c = pl.program_id(1)
    L = M_CHUNK
    lo = CONV_HALO

    @pl.when(c == 0)
    def _():
        xq_sc[0:lo, :] = jnp.zeros((lo, M_W), F32)
        xk_sc[0:lo, :] = jnp.zeros((lo, M_W), F32)
        ct_sc[...] = jnp.zeros_like(ct_sc)
        n_sc[...] = jnp.zeros_like(n_sc)
        m_sc[...] = jnp.zeros_like(m_sc)

    xq_sc[lo:lo + L, :] = q_ref[...]
    xk_sc[lo:lo + L, :] = k_ref[...]

    def conv(x_sc, w, b):
        acc = x_sc[lo:lo + L, :] * w[CONV_K - 1:CONV_K, :] + b
        for j in range(CONV_K - 1):
            off = lo - (CONV_K - 1) + j
            acc = acc + x_sc[off:off + L, :] * w[j:j + 1, :]
        return acc

    cw = cw_ref[...]
    cb = cb_ref[...]
    qc = _silu(conv(xq_sc, cw[:, :M_W], cb[:, :M_W]))
    kc = _silu(conv(xk_sc, cw[:, M_W:], cb[:, M_W:])) * (M_HEAD_DIM ** -0.5)
    xq_sc[0:lo, :] = xq_sc[L:L + lo, :]
    xk_sc[0:lo, :] = xk_sc[L:L + lo, :]

    row = lax.broadcasted_iota(jnp.int32, (L, L), 0)
    col = lax.broadcasted_iota(jnp.int32, (L, L), 1)
    causal = row >= col
    for h in range(M_HEADS):
        hs = slice(h * M_HEAD_DIM, (h + 1) * M_HEAD_DIM)
        q = qc[:, hs]
        k = kc[:, hs]
        i_col = gc_ref[:, h:h + 1] + bc_ref[:, h:h + 1]
        lf_col = _log_sigmoid(gc_ref[:, M_HEADS + h:M_HEADS + h + 1] + bc_ref[:, M_HEADS + h:M_HEADS + h + 1])
        i_row = gr_ref[h:h + 1, :] + br_ref[h:h + 1, :]
        lf_row = _log_sigmoid(gr_ref[M_HEADS + h:M_HEADS + h + 1, :] + br_ref[M_HEADS + h:M_HEADS + h + 1, :])
        b_col = jnp.sum(jnp.where(causal, lf_row, 0.0), axis=1, keepdims=True)
        b_row = jnp.sum(jnp.where(row <= col, lf_col, 0.0), axis=0, keepdims=True)
        log_d = jnp.where(causal, b_col - b_row + i_row, -jnp.inf)
        m_prev = m_sc[h, 0:1, 0:1]
        log_inter = b_col + m_prev
        m_t = jnp.maximum(jnp.max(log_d, axis=1, keepdims=True), log_inter)
        qb = q.astype(BF16)
        kb = k.astype(BF16)
        vb = v_ref[:, hs].astype(BF16)
        s = lax.dot_general(qb, kb, (((1,), (1,)), ((), ())), preferred_element_type=F32) * jnp.exp(log_d - m_t)
        inter = jnp.exp(log_inter - m_t)
        ct = ct_sc[h]
        n_row = n_sc[h, 0:1, :]
        num = (jnp.dot(s.astype(BF16), vb, preferred_element_type=F32)
               + inter * jnp.dot(qb, ct.astype(BF16), preferred_element_type=F32))
        den = jnp.sum(s, axis=1, keepdims=True) + inter * jnp.sum(q * n_row, axis=1, keepdims=True)
        hh = num / jnp.maximum(jnp.abs(den), jnp.exp(-m_t))
        m_new = m_t[L - 1:L, :]
        b_last = b_col[L - 1:L, :]
        w_col = jnp.exp(b_last - b_col + i_col - m_new)
        decay = jnp.exp(b_last + m_prev - m_new)
        kw = k * w_col
        ct_sc[h] = decay * ct + lax.dot_general(kw.astype(BF16), vb, (((0,), (0,)), ((), ())),
                                                preferred_element_type=F32)
        n_sc[h] = jnp.broadcast_to(decay * n_row + jnp.sum(kw, axis=0, keepdims=True), (SUBLANES, M_HEAD_DIM))
        m_sc[h] = jnp.broadcast_to(m_new, (SUBLANES, LANES))
        z = jax.nn.sigmoid(og_ref[:, hs]) * hh
        y_ref[:, hs] = (_normalize(z) * ng_ref[:, hs]).astype(BF16)


def _mlstm(p, gcol, grow, bcol, brow, conv_w, conv_b, norm_g):
    nc = SEQ // M_CHUNK
    slab = lambda blk: pl.BlockSpec((M_CHUNK, M_W), lambda b, c: (b * nc + c, blk))
    return pl.pallas_call(
        _mlstm_kernel,
        out_shape=jax.ShapeDtypeStruct((TOKENS, M_W), BF16),
        grid=(BATCH, nc),
        in_specs=[
            slab(P_QM_BLK), slab(P_KM_BLK), slab(P_VM_BLK), slab(P_OM_BLK),
            pl.BlockSpec((None, M_CHUNK, 2 * M_HEADS), lambda b, c: (b, c, 0)),
            pl.BlockSpec((None, 2 * M_HEADS, M_CHUNK), lambda b, c: (b, 0, c)),
            pl.BlockSpec((1, 2 * M_HEADS), lambda b, c: (0, 0)),
            pl.BlockSpec((2 * M_HEADS, 1), lambda b, c: (0, 0)),
            pl.BlockSpec((CONV_K, 2 * M_W), lambda b, c: (0, 0)),
            pl.BlockSpec((1, 2 * M_W), lambda b, c: (0, 0)),
            pl.BlockSpec((1, M_W), lambda b, c: (0, 0)),
        ],
        out_specs=pl.BlockSpec((M_CHUNK, M_W), lambda b, c: (b * nc + c, 0)),
        scratch_shapes=[
            pltpu.VMEM((M_CHUNK + CONV_HALO, M_W), F32),
            pltpu.VMEM((M_CHUNK + CONV_HALO, M_W), F32),
            pltpu.VMEM((M_HEADS, M_HEAD_DIM, M_HEAD_DIM), F32),
            pltpu.VMEM((M_HEADS, SUBLANES, M_HEAD_DIM), F32),
            pltpu.VMEM((M_HEADS, SUBLANES, LANES), F32),
        ],
        compiler_params=_cparams(("arbitrary", "arbitrary")),
        name="mlstm",
    )(p, p, p, p, gcol, grow, bcol, brow, conv_w, conv_b, norm_g)


RT_E1, RT_E2, RT_W1, RT_W2 = 0, 1, 2, 3


def _route(logits):
    lane = lax.broadcasted_iota(jnp.int32, logits.shape, 1).astype(F32)
    big = float(LANES)
    is_g = lane < N_EXPERT_GROUPS
    gl = jnp.where(is_g, logits, -jnp.inf)
    gexp = jnp.exp(gl - jnp.max(gl, axis=1, keepdims=True))
    gprob = gexp / jnp.sum(gexp, axis=1, keepdims=True)
    g_w = jnp.max(gprob, axis=1, keepdims=True)
    g_top = jnp.min(jnp.where(is_g & (gprob == g_w), lane, big), axis=1, keepdims=True)
    lo = N_EXPERT_GROUPS + EXPERTS_PER_GROUP * g_top
    in_grp = (lane >= lo) & (lane < lo + EXPERTS_PER_GROUP)
    el = jnp.where(in_grp, logits, -jnp.inf)
    eexp = jnp.exp(el - jnp.max(el, axis=1, keepdims=True))
    eprob = eexp / jnp.sum(eexp, axis=1, keepdims=True)
    v1 = jnp.max(eprob, axis=1, keepdims=True)
    i1 = jnp.min(jnp.where(in_grp & (eprob == v1), lane, big), axis=1, keepdims=True)
    rest = jnp.where(in_grp & (lane != i1), eprob, -1.0)
    v2 = jnp.max(rest, axis=1, keepdims=True)
    i2 = jnp.min(jnp.where(rest == v2, lane, big), axis=1, keepdims=True)
    tot = v1 + v2
    w1 = g_w * (v1 / tot)
    w2 = g_w * (v2 / tot)
    e1 = i1 - N_EXPERT_GROUPS
    e2 = i2 - N_EXPERT_GROUPS
    rec = jnp.where(lane == RT_E1, e1, jnp.where(lane == RT_E2, e2, jnp.where(lane == RT_W1, w1, w2)))
    return jnp.where(lane <= RT_W2, rec, 0.0)


def _merge_kernel(ya_ref, ym_ref, g_ref, x_ref, mod_ref,
                  wpa_ref, wpm_ref, wout_ref, lng_ref, lnb_ref, wr_ref, br_ref,
                  x1_ref, u2_ref, rt_ref):
    pa = jnp.dot(ya_ref[...], wpa_ref[...], preferred_element_type=F32)
    pm = jnp.dot(ym_ref[...], wpm_ref[...], preferred_element_type=F32)
    merged = g_ref[:, :D_MODEL].astype(F32) * pa + g_ref[:, D_MODEL:].astype(F32) * pm
    mix = jnp.dot(merged.astype(BF16), wout_ref[...], preferred_element_type=F32)
    z = DEEPNORM_ALPHA * x_ref[...] + mod_ref[2:3, :] * mix
    x1 = _normalize(z) * lng_ref[...] + lnb_ref[...]
    x1_ref[...] = x1
    u2 = _normalize(x1) * (1.0 + mod_ref[4:5, :]) + mod_ref[3:4, :]
    u2_ref[...] = u2
    logits = jnp.dot(u2.astype(BF16), wr_ref[...], preferred_element_type=F32) + br_ref[...]
    rt_ref[...] = _route(logits)


def _merge(ya, ym, g, x2, mod, wpa, wpm, wout, lng, lnb, wr, br):
    tm = MERGE_TM
    tiles_per_batch = SEQ // tm
    rowblk = lambda w: pl.BlockSpec((tm, w), lambda m: (m, 0))
    const = lambda shape: pl.BlockSpec(shape, lambda m: (0,) * len(shape), pipeline_mode=pl.Buffered(1))
    return pl.pallas_call(
        _merge_kernel,
        out_shape=(
            jax.ShapeDtypeStruct((TOKENS, D_MODEL), F32),
            jax.ShapeDtypeStruct((TOKENS, D_MODEL), F32),
            jax.ShapeDtypeStruct((TOKENS, LANES), F32),
        ),
        grid=(TOKENS // tm,),
        in_specs=[
            rowblk(A_GROUP_W), rowblk(M_W), rowblk(2 * D_MODEL), rowblk(D_MODEL),
            pl.BlockSpec((None, 6, D_MODEL), lambda m: (m // tiles_per_batch, 0, 0)),
            const((A_GROUP_W, D_MODEL)), const((M_W, D_MODEL)), const((D_MODEL, D_MODEL)),
            const((1, D_MODEL)), const((1, D_MODEL)),
            const((D_MODEL, LANES)), const((1, LANES)),
        ],
        out_specs=(rowblk(D_MODEL), rowblk(D_MODEL), rowblk(LANES)),
        compiler_params=_cparams(("arbitrary",)),
        name="merge_ln1_route",
    )(ya, ym, g, x2, mod, wpa, wpm, wout, lng, lnb, wr, br)


CAST_ROWS = 128


def _issue_rows(src_ref, idx_ref, base, buf, slot, sem, nrows):
    def body(blk, carry):
        for j in range(SUBLANES):
            row = idx_ref[base + blk * SUBLANES + j]
            src = src_ref.at[lax.shift_right_logical(row, 3), pl.ds(row & (SUBLANES - 1), 1)]
            pltpu.make_async_copy(src, buf.at[slot, blk, pl.ds(j, 1)], sem.at[slot]).start()
        return carry

    lax.fori_loop(0, nrows // SUBLANES, body, 0)


def _wait_rows(src_ref, buf, slot, sem, nrows):
    groups = nrows // SUBLANES
    pltpu.make_async_copy(src_ref.at[pl.ds(0, groups)], buf.at[slot, pl.ds(0, groups)], sem.at[slot]).wait()


def _expert_changed(te_ref, i):
    return (i == 0) | (te_ref[i] != te_ref[jnp.maximum(i - 1, 0)])


def _moe_kernel(te_ref, tn_ref, tv_ref, nu_ref, tok_ref, u_ref, wg_ref, wu_ref, wd_ref, o_ref,
                stg, stu, std, wgb, wub, wdb, xbuf, wsem, xsem):
    i = pl.program_id(0)
    nu = nu_ref[0]
    tm = o_ref.shape[0]
    half = tm // 2

    def weight_copies(e):
        return (pltpu.make_async_copy(wg_ref.at[e], stg, wsem.at[0]),
                pltpu.make_async_copy(wu_ref.at[e], stu, wsem.at[1]),
                pltpu.make_async_copy(wd_ref.at[e], std, wsem.at[2]))

    def for_tile_rows(t, fn):
        @pl.when(tv_ref[t] <= half)
        def _():
            fn(half)

        @pl.when(tv_ref[t] > half)
        def _():
            fn(tm)

    def issue_tile(t):
        for_tile_rows(t, lambda nrows: _issue_rows(u_ref, tok_ref, t * tm, xbuf, t % MOE_ROW_SLOTS, xsem, nrows))

    @pl.when(i == 0)
    def _():
        issue_tile(0)

        @pl.when(nu > 1)
        def _():
            issue_tile(1)

        for cp in weight_copies(te_ref[0]):
            cp.start(priority=1)

    @pl.when(i < nu)
    def _():
        slot = i % MOE_ROW_SLOTS

        @pl.when(i + 2 < nu)
        def _():
            issue_tile(i + 2)

        @pl.when(_expert_changed(te_ref, i))
        def _():
            for cp in weight_copies(te_ref[i]):
                cp.wait()
            for src, dst in ((stg, wgb), (stu, wub), (std, wdb)):
                def cast_rows(ci, carry, src=src, dst=dst):
                    r = pl.multiple_of(ci * CAST_ROWS, CAST_ROWS)
                    dst[pl.ds(r, CAST_ROWS), :] = src[pl.ds(r, CAST_ROWS), :].astype(BF16)
                    return carry

                lax.fori_loop(0, src.shape[0] // CAST_ROWS, cast_rows, 0)

            @pl.when(tn_ref[i] >= 0)
            def _():
                for cp in weight_copies(tn_ref[i]):
                    cp.start(priority=1)

        def compute(nrows):
            _wait_rows(u_ref, xbuf, slot, xsem, nrows)
            x = xbuf[slot, 0:nrows // SUBLANES].reshape(nrows, D_MODEL).astype(BF16)
            a = jnp.dot(x, wgb[...], preferred_element_type=F32)
            b = jnp.dot(x, wub[...], preferred_element_type=F32)
            h = (_silu(a) * b).astype(BF16)
            o_ref[0:nrows, :] = jnp.dot(h, wdb[...], preferred_element_type=F32)
            if nrows < tm:
                o_ref[nrows:tm, :] = jnp.zeros((tm - nrows, D_MODEL), F32)

        for_tile_rows(i, compute)

    @pl.when(i >= nu)
    def _():
        o_ref[...] = jnp.zeros_like(o_ref)


def _moe(te, te_next, tile_rows, n_used, row_token, u2, w_eg, w_eu, w_ed):
    tm = MOE_TM
    return pl.pallas_call(
        _moe_kernel,
        out_shape=jax.ShapeDtypeStruct((MOE_ROWS, D_MODEL), F32),
        grid_spec=pltpu.PrefetchScalarGridSpec(
            num_scalar_prefetch=5,
            grid=(MOE_TILES,),
            in_specs=[pl.BlockSpec(memory_space=pl.ANY)] * 4,
            out_specs=pl.BlockSpec((tm, D_MODEL), lambda i, te, tn, tv, nu, tok: (i, 0)),
            scratch_shapes=[
                pltpu.VMEM((D_MODEL, D_FF_EXPERT), F32),
                pltpu.VMEM((D_MODEL, D_FF_EXPERT), F32),
                pltpu.VMEM((D_FF_EXPERT, D_MODEL), F32),
                pltpu.VMEM((D_MODEL, D_FF_EXPERT), BF16),
                pltpu.VMEM((D_MODEL, D_FF_EXPERT), BF16),
                pltpu.VMEM((D_FF_EXPERT, D_MODEL), BF16),
                pltpu.VMEM((MOE_ROW_SLOTS, tm // SUBLANES, SUBLANES, D_MODEL), F32),
                pltpu.SemaphoreType.DMA((3,)),
                pltpu.SemaphoreType.DMA((MOE_ROW_SLOTS,)),
            ],
        ),
        compiler_params=_cparams(("arbitrary",)),
        name="moe_experts",
    )(te, te_next, tile_rows, n_used, row_token, u2.reshape(TOKENS // SUBLANES, SUBLANES, D_MODEL),
      w_eg, w_eu, w_ed)


def _final_kernel(pos_ref, o_ref, x1_ref, rt_ref, mod_ref, lng_ref, lnb_ref, y_ref, buf, sem):
    i = pl.program_id(0)
    tb = x1_ref.shape[0]
    slot = i % 2

    @pl.when(i == 0)
    def _():
        _issue_rows(o_ref, pos_ref, 0, buf, 0, sem, 2 * tb)

    @pl.when(i + 1 < pl.num_programs(0))
    def _():
        _issue_rows(o_ref, pos_ref, (i + 1) * (2 * tb), buf, 1 - slot, sem, 2 * tb)

    _wait_rows(o_ref, buf, slot, sem, 2 * tb)
    rows = buf[slot].reshape(2 * tb, D_MODEL)
    ffn = rt_ref[:, RT_W1:RT_W1 + 1] * rows[0:tb, :] + rt_ref[:, RT_W2:RT_W2 + 1] * rows[tb:2 * tb, :]
    z = DEEPNORM_ALPHA * x1_ref[...] + mod_ref[5:6, :] * ffn
    y_ref[...] = _normalize(z) * lng_ref[...] + lnb_ref[...]


def _final(pos_tiles, moe_out, x1, rt, mod, lng, lnb):
    tb = FINAL_TB
    tiles_per_batch = SEQ // tb
    return pl.pallas_call(
        _final_kernel,
        out_shape=jax.ShapeDtypeStruct((TOKENS, D_MODEL), F32),
        grid_spec=pltpu.PrefetchScalarGridSpec(
            num_scalar_prefetch=1,
            grid=(TOKENS // tb,),
            in_specs=[
                pl.BlockSpec(memory_space=pl.ANY),
                pl.BlockSpec((tb, D_MODEL), lambda i, pos: (i, 0)),
                pl.BlockSpec((tb, LANES), lambda i, pos: (i, 0)),
                pl.BlockSpec((None, 6, D_MODEL), lambda i, pos: (i // tiles_per_batch, 0, 0)),
                pl.BlockSpec((1, D_MODEL), lambda i, pos: (0, 0)),
                pl.BlockSpec((1, D_MODEL), lambda i, pos: (0, 0)),
            ],
            out_specs=pl.BlockSpec((tb, D_MODEL), lambda i, pos: (i, 0)),
            scratch_shapes=[pltpu.VMEM((2, 2 * tb // SUBLANES, SUBLANES, D_MODEL), F32),
                            pltpu.SemaphoreType.DMA((2,))],
        ),
        compiler_params=_cparams(("arbitrary",)),
        name="combine_ln2",
    )(pos_tiles, moe_out.reshape(MOE_ROWS // SUBLANES, SUBLANES, D_MODEL), x1, rt, mod, lng, lnb)


def _routing_tables(e_ids):
    tm = MOE_TM
    e_flat = e_ids.reshape(-1)
    onehot = (e_flat[:, None] == jnp.arange(N_EXPERTS, dtype=jnp.int32)[None, :]).astype(jnp.int32)
    csum = jnp.cumsum(onehot, axis=0)
    rank = jnp.sum((csum - onehot) * onehot, axis=1)
    counts = csum[-1]
    padded = ((counts + tm - 1) // tm) * tm
    pend = jnp.cumsum(padded)
    pstart = pend - padded
    dest = pstart[e_flat] + rank
    tok = jnp.arange(2 * TOKENS, dtype=jnp.int32) // 2
    row_token = jnp.zeros((MOE_ROWS,), jnp.int32).at[dest].set(tok)
    n_used = (pend[-1] // tm).astype(jnp.int32)
    tile_start = jnp.arange(MOE_TILES, dtype=jnp.int32) * tm
    te = jnp.sum((pend[None, :] <= tile_start[:, None]).astype(jnp.int32), axis=1)
    te = jnp.minimum(te, N_EXPERTS - 1)
    te_last = te[jnp.maximum(n_used - 1, 0)]
    te = jnp.where(jnp.arange(MOE_TILES) < n_used, te, te_last)
    next_run = pend[te] // tm
    te_next = jnp.where(next_run < n_used, te[jnp.minimum(next_run, MOE_TILES - 1)], -1).astype(jnp.int32)
    tile_rows = jnp.clip(pstart[te] + counts[te] - tile_start, 0, tm).astype(jnp.int32)
    pos = dest.reshape(TOKENS, 2)
    return row_token, te, te_next, tile_rows, n_used.reshape(1), pos


def _rope_tables():
    inv = ROPE_THETA ** (-jnp.arange(0, HEAD_DIM_A, 2, dtype=F32) / HEAD_DIM_A)
    ang = jnp.arange(SEQ, dtype=F32)[:, None] * inv[None, :]
    cos = jnp.cos(ang)
    sin = jnp.sin(ang)
    return jnp.concatenate([cos, cos], axis=-1), jnp.concatenate([-sin, sin], axis=-1)


def kernel(x, c, w_ada, b_ada, w_in, b_mgate, conv_w, conv_b, m_norm_g, w_proj_a, w_proj_m, w_gate, b_gate,
           w_out, ln1_g, ln1_b, w_rg, b_rg, w_re, b_re, w_eg, w_eu, w_ed, ln2_g, ln2_b):
    assert x.shape == (BATCH, SEQ, D_MODEL) and w_ada.shape[0] == 1
    l = 0
    x2 = x.reshape(TOKENS, D_MODEL)

    c_pad = jnp.zeros((ADA_ROWS, D_MODEL), F32).at[:BATCH].set(c)
    mod = _ada(c_pad, w_ada[l], b_ada[l][None, :])[:BATCH].reshape(BATCH, 6, D_MODEL)

    w_in_t = jnp.swapaxes(w_in[l], 0, 1)
    w_if_t = jnp.zeros((LANES, D_MODEL), F32).at[:2 * M_HEADS].set(w_in_t[N_IN_MAIN:])
    p, gates, u = _inproj(x2, mod, w_in_t, w_if_t)
    g = _gateproj(u, w_gate[l], b_gate[l][None, :])

    cos_t, sin_t = _rope_tables()
    ya = _attention(p, cos_t, sin_t)

    gcol = gates[:, :2 * M_HEADS].reshape(BATCH, SEQ, 2 * M_HEADS)
    grow = jnp.transpose(gcol, (0, 2, 1))
    ym = _mlstm(p, gcol, grow, b_mgate[l][None, :], b_mgate[l][:, None], conv_w[l], conv_b[l][None, :],
                m_norm_g[l][None, :])

    wr = (jnp.zeros((D_MODEL, LANES), F32)
          .at[:, :N_EXPERT_GROUPS].set(w_rg[l])
          .at[:, N_EXPERT_GROUPS:N_EXPERT_GROUPS + N_EXPERTS].set(w_re[l])).astype(BF16)
    br = (jnp.zeros((1, LANES), F32)
          .at[0, :N_EXPERT_GROUPS].set(b_rg[l])
          .at[0, N_EXPERT_GROUPS:N_EXPERT_GROUPS + N_EXPERTS].set(b_re[l]))
    x1, u2, rt = _merge(ya, ym, g, x2, mod,
                        w_proj_a[l].astype(BF16), w_proj_m[l].astype(BF16), w_out[l].astype(BF16),
                        ln1_g[l][None, :], ln1_b[l][None, :], wr, br)

    e_ids = rt[:, RT_E1:RT_E2 + 1].astype(jnp.int32)
    row_token, te, te_next, tile_rows, n_used, pos = _routing_tables(e_ids)

    mo = _moe(te, te_next, tile_rows, n_used, row_token, u2,
              w_eg[l].reshape(N_EXPERTS, D_MODEL, D_FF_EXPERT), w_eu[l].reshape(N_EXPERTS, D_MODEL, D_FF_EXPERT),
              w_ed[l].reshape(N_EXPERTS, D_FF_EXPERT, D_MODEL))

    nt = TOKENS // FINAL_TB
    pos_tiles = jnp.transpose(pos.reshape(nt, FINAL_TB, 2), (0, 2, 1)).reshape(-1)
    y = _final(pos_tiles, mo, x1, rt, mod, ln2_g[l][None, :], ln2_b[l][None, :])
    return y.reshape(BATCH, SEQ, D_MODEL)
```

```python
import functools

import jax
import jax.numpy as jnp
from jax import lax
from jax.experimental import pallas as pl
from jax.experimental.pallas import tpu as pltpu

F32 = jnp.float32
BF16 = jnp.bfloat16

D_MODEL = 2048
BATCH = 4
SEQ = 2048
TOKENS = BATCH * SEQ
DIL_CONFIGS = ((128, 1), (512, 4), (2048, 16))
N_DIL_GROUPS = 3
HEADS_PER_GROUP = 4
HEAD_DIM_A = 128
ATT_BLOCK = 128
ROPE_THETA = 10000.0
A_GROUP_W = HEADS_PER_GROUP * HEAD_DIM_A
A_QKV_W = N_DIL_GROUPS * A_GROUP_W
M_HEADS = 4
M_HEAD_DIM = 256
M_W = M_HEADS * M_HEAD_DIM
M_CHUNK = 128
CONV_K = 4
N_IN_MAIN = 3 * A_QKV_W + 4 * M_W
N_EXPERT_GROUPS = 4
EXPERTS_PER_GROUP = 8
N_EXPERTS = N_EXPERT_GROUPS * EXPERTS_PER_GROUP
D_FF_EXPERT = 1024
DEEPNORM_ALPHA = 2.0 ** 0.25
LN_EPS = 1e-5

LANES = 128
SUBLANES = 8
VMEM_LIMIT_BYTES = 56 * 1024 * 1024

PROJ_TN = 512
GATE_TN = 1024
PROJ_TM = 1024
P_TILES_IN = N_IN_MAIN // PROJ_TN
P_ATT_TILES = 3 * A_QKV_W // PROJ_TN
P_M_TILES = P_TILES_IN - P_ATT_TILES
P_WIDTH = N_IN_MAIN
P_ATT_OFF = P_M_TILES * PROJ_TN
P_QM_BLK, P_KM_BLK, P_VM_BLK, P_OM_BLK = 0, 1, 2, 3
MERGE_TM = 256
MERGE_SUB = 256
MOE_TM = 256
MOE_ROWS = 2 * TOKENS + N_EXPERTS * MOE_TM
MOE_TILES = MOE_ROWS // MOE_TM
MOE_ROW_SLOTS = 3
FINAL_TB = 256


def _cparams(sem, vmem=VMEM_LIMIT_BYTES):
    return pltpu.CompilerParams(dimension_semantics=sem, vmem_limit_bytes=vmem)


def _normalize(x):
    mu = jnp.mean(x, axis=-1, keepdims=True)
    xc = x - mu
    var = jnp.mean(xc * xc, axis=-1, keepdims=True)
    return xc * lax.rsqrt(var + LN_EPS)


def _silu(x):
    return x * jax.nn.sigmoid(x)


def _log_sigmoid(x):
    return jnp.minimum(x, 0.0) - jnp.log(1.0 + jnp.exp(-jnp.abs(x)))


_CONTRACT_LAST = (((1,), (1,)), ((), ()))
LOG2E = 1.4426950408889634


ADA_TN = 1024
ADA_ROWS = 16


def _ada_kernel(c_ref, w_ref, b_ref, o_ref):
    sc = _silu(c_ref[...]).astype(BF16)
    o_ref[...] = jnp.dot(sc, w_ref[...].astype(BF16), preferred_element_type=F32) + b_ref[...]


def _ada(c_pad, w_ada, b_ada):
    n = w_ada.shape[1]
    return pl.pallas_call(
        _ada_kernel,
        out_shape=jax.ShapeDtypeStruct((ADA_ROWS, n), F32),
        grid=(n // ADA_TN,),
        in_specs=[
            pl.BlockSpec((ADA_ROWS, D_MODEL), lambda j: (0, 0)),
            pl.BlockSpec((D_MODEL, ADA_TN), lambda j: (0, j)),
            pl.BlockSpec((1, ADA_TN), lambda j: (0, j)),
        ],
        out_specs=pl.BlockSpec((ADA_ROWS, ADA_TN), lambda j: (0, j)),
        compiler_params=_cparams(("arbitrary",)),
        name="ada_mod",
    )(c_pad, w_ada, b_ada)


LN_CHUNK = 256


def _inproj_kernel(x_ref, mod_ref, w_ref, wif_ref, p_ref, g_ref, u_ref):
    n = pl.program_id(1)

    @pl.when(n == 0)
    def _():
        shift = mod_ref[0:1, :]
        scale = 1.0 + mod_ref[1:2, :]
        wif = wif_ref[...].astype(BF16)
        w = w_ref[...].astype(BF16)
        for ci in range(PROJ_TM // LN_CHUNK):
            rows = slice(ci * LN_CHUNK, (ci + 1) * LN_CHUNK)
            u = (_normalize(x_ref[rows, :]) * scale + shift).astype(BF16)
            u_ref[rows, :] = u
            g_ref[rows, :] = lax.dot_general(u, wif, _CONTRACT_LAST, preferred_element_type=F32)
            p_ref[rows, :] = lax.dot_general(u, w, _CONTRACT_LAST, preferred_element_type=F32)

    @pl.when(n > 0)
    def _():
        p_ref[...] = lax.dot_general(u_ref[...], w_ref[...].astype(BF16), _CONTRACT_LAST,
                                     preferred_element_type=F32)


def _inproj(x2, mod, w_in_t, w_if_t):
    tiles_per_batch = SEQ // PROJ_TM
    return pl.pallas_call(
        _inproj_kernel,
        out_shape=(
            jax.ShapeDtypeStruct((TOKENS, P_WIDTH), F32),
            jax.ShapeDtypeStruct((TOKENS, LANES), F32),
            jax.ShapeDtypeStruct((TOKENS, D_MODEL), BF16),
        ),
        grid=(TOKENS // PROJ_TM, P_TILES_IN),
        in_specs=[
            pl.BlockSpec((PROJ_TM, D_MODEL), lambda m, n: (m, 0)),
            pl.BlockSpec((None, 6, D_MODEL), lambda m, n: (m // tiles_per_batch, 0, 0)),
            pl.BlockSpec((PROJ_TN, D_MODEL), lambda m, n: (n, 0)),
            pl.BlockSpec((LANES, D_MODEL), lambda m, n: (0, 0)),
        ],
        out_specs=(
            pl.BlockSpec((PROJ_TM, PROJ_TN),
                         lambda m, n: (m, jnp.where(n < P_ATT_TILES, n + P_M_TILES, n - P_ATT_TILES))),
            pl.BlockSpec((PROJ_TM, LANES), lambda m, n: (m, 0)),
            pl.BlockSpec((PROJ_TM, D_MODEL), lambda m, n: (m, 0)),
        ),
        compiler_params=_cparams(("arbitrary", "arbitrary")),
        name="in_proj",
    )(x2, mod, w_in_t, w_if_t)


def _gateproj_kernel(u_ref, w_ref, b_ref, o_ref):
    acc = jnp.dot(u_ref[...], w_ref[...].astype(BF16), preferred_element_type=F32)
    o_ref[...] = jax.nn.sigmoid(acc + b_ref[...]).astype(BF16)


def _gateproj(u, w_gate, b_gate):
    n = w_gate.shape[1]
    return pl.pallas_call(
        _gateproj_kernel,
        out_shape=jax.ShapeDtypeStruct((TOKENS, n), BF16),
        grid=(TOKENS // PROJ_TM, n // GATE_TN),
        in_specs=[
            pl.BlockSpec((PROJ_TM, D_MODEL), lambda m, j: (m, 0)),
            pl.BlockSpec((D_MODEL, GATE_TN), lambda m, j: (0, j)),
            pl.BlockSpec((1, GATE_TN), lambda m, j: (0, j)),
        ],
        out_specs=pl.BlockSpec((PROJ_TM, GATE_TN), lambda m, j: (m, j)),
        compiler_params=_cparams(("arbitrary", "arbitrary")),
        name="gate_proj",
    )(u, w_gate, b_gate)


def _rows(start, size, stride):
    return pl.ds(start, size) if stride == 1 else pl.ds(start, size, stride=stride)


def _attn_kernel(q0, k0, v0, q1, k1, v1, q2, k2, v2, cos_ref, sin_ref, y_ref,
                 qr_sc, kr_sc, o0, o1, o2, l0, l1, l2):
    cos = cos_ref[...]
    sin = sin_ref[...]
    scale = HEAD_DIM_A ** -0.5
    blk = ATT_BLOCK
    qi2 = lax.broadcasted_iota(jnp.int32, (blk, 2 * blk), 0)
    kc2 = lax.broadcasted_iota(jnp.int32, (blk, 2 * blk), 1)
    mask_prev_cur = (kc2 >= qi2) & (kc2 <= qi2 + blk)
    qi1 = lax.broadcasted_iota(jnp.int32, (blk, blk), 0)
    kc1 = lax.broadcasted_iota(jnp.int32, (blk, blk), 1)
    mask_cur = kc1 <= qi1
    half = HEAD_DIM_A // 2
    groups = ((q0, k0, v0, o0, l0), (q1, k1, v1, o1, l1), (q2, k2, v2, o2, l2))
    for (window, d), (q_ref, k_ref, v_ref, o_sc, l_sc) in zip(DIL_CONFIGS, groups):
        nb = SEQ // d // blk
        q = q_ref[...]
        k = k_ref[...]
        qr_sc[...] = q * cos + pltpu.roll(q, half, 1) * sin
        kr_sc[...] = k * cos + pltpu.roll(k, half, 1) * sin
        for r in range(d):
            for j in range(nb):
                start = r + j * blk * d
                cur = _rows(start, blk, d)
                qb = qr_sc[cur, :].astype(BF16)
                if j == 0:
                    keys = cur
                    mask = mask_cur
                else:
                    keys = _rows(start - blk * d, 2 * blk, d)
                    mask = mask_prev_cur
                kw = kr_sc[keys, :].astype(BF16)
                vb = v_ref[keys, :].astype(BF16)
                vw = jnp.concatenate([vb, jnp.ones_like(vb)], axis=1)
                s = lax.dot_general(qb, kw, _CONTRACT_LAST, preferred_element_type=F32)
                s = jnp.where(mask, s, -jnp.inf)
                mx = jnp.max(s, axis=-1, keepdims=True)
                p = jnp.exp2((s - mx) * (scale * LOG2E))
                pv = jnp.dot(p.astype(BF16), vw, preferred_element_type=F32)
                den = pv[:, HEAD_DIM_A:]
                o_sc[cur, :] = pv[:, :HEAD_DIM_A] / den
                l_sc[cur, :] = mx * scale + jnp.log(den)
    la = l0[...]
    lb = l1[...]
    lc = l2[...]
    mx = jnp.maximum(jnp.maximum(la, lb), lc)
    ea = jnp.exp(la - mx)
    eb = jnp.exp(lb - mx)
    ec = jnp.exp(lc - mx)
    den = ea + eb + ec
    y_ref[...] = ((ea / den) * o0[...] + (eb / den) * o1[...] + (ec / den) * o2[...]).astype(BF16)


def _attention(p, cos_t, sin_t):
    for window, d in DIL_CONFIGS:
        assert window // d == ATT_BLOCK and SEQ % (d * ATT_BLOCK) == 0
    col0 = P_ATT_OFF // HEAD_DIM_A

    def slab(part, g):
        off = col0 + (part * A_QKV_W + g * A_GROUP_W) // HEAD_DIM_A
        return pl.BlockSpec((SEQ, HEAD_DIM_A), lambda b, h: (b, off + h))

    in_specs = [slab(part, g) for g in range(N_DIL_GROUPS) for part in range(3)]
    table = pl.BlockSpec((SEQ, HEAD_DIM_A), lambda b, h: (0, 0))
    return pl.pallas_call(
        _attn_kernel,
        out_shape=jax.ShapeDtypeStruct((TOKENS, A_GROUP_W), BF16),
        grid=(BATCH, HEADS_PER_GROUP),
        in_specs=in_specs + [table, table],
        out_specs=pl.BlockSpec((SEQ, HEAD_DIM_A), lambda b, h: (b, h)),
        scratch_shapes=[pltpu.VMEM((SEQ, HEAD_DIM_A), F32)] * 8,
        compiler_params=_cparams(("arbitrary", "arbitrary")),
        name="dil_attn",
    )(*([p] * 9), cos_t, sin_t)


CONV_HALO = SUBLANES


def _mlstm_kernel(q_ref, k_ref, v_ref, og_ref, gc_ref, gr_ref, bc_ref, br_ref, cw_ref, cb_ref, ng_ref,
                  y_ref, xq_sc, xk_sc, ct_sc, n_sc, m_sc):
    c = pl.program_id(1)
    L = M_CHUNK
    lo = CONV_HALO

    @pl.when(c == 0)
    def _():
        xq_sc[0:lo, :] = jnp.zeros((lo, M_W), F32)
        xk_sc[0:lo, :] = jnp.zeros((lo, M_W), F32)
        ct_sc[...] = jnp.zeros_like(ct_sc)
        n_sc[...] = jnp.zeros_like(n_sc)
        m_sc[...] = jnp.zeros_like(m_sc)

    xq_sc[lo:lo + L, :] = q_ref[...]
    xk_sc[lo:lo + L, :] = k_ref[...]

    def conv(x_sc, w, b):
        acc = x_sc[lo:lo + L, :] * w[CONV_K - 1:CONV_K, :] + b
        for j in range(CONV_K - 1):
            off = lo - (CONV_K - 1) + j
            acc = acc + x_sc[off:off + L, :] * w[j:j + 1, :]
        return acc

    cw = cw_ref[...]
    cb = cb_ref[...]
    qc = _silu(conv(xq_sc, cw[:, :M_W], cb[:, :M_W]))
    kc = _silu(conv(xk_sc, cw[:, M_W:], cb[:, M_W:])) * (M_HEAD_DIM ** -0.5)
    xq_sc[0:lo, :] = xq_sc[L:L + lo, :]
    xk_sc[0:lo, :] = xk_sc[L:L + lo, :]

    row = lax.broadcasted_iota(jnp.int32, (L, L), 0)
    col = lax.broadcasted_iota(jnp.int32, (L, L), 1)
    causal = row >= col
    for h in range(M_HEADS):
        hs = slice(h * M_HEAD_DIM, (h + 1) * M_HEAD_DIM)
        q = qc[:, hs]
        k = kc[:, hs]
        i_col = gc_ref[:, h:h + 1] + bc_ref[:, h:h + 1]
        lf_col = _log_sigmoid(gc_ref[:, M_HEADS + h:M_HEADS + h + 1] + bc_ref[:, M_HEADS + h:M_HEADS + h + 1])
        i_row = gr_ref[h:h + 1, :] + br_ref[h:h + 1, :]
        lf_row = _log_sigmoid(gr_ref[M_HEADS + h:M_HEADS + h + 1, :] + br_ref[M_HEADS + h:M_HEADS + h + 1, :])
        b_col = jnp.sum(jnp.where(causal, lf_row, 0.0), axis=1, keepdims=True)
        b_row = jnp.sum(jnp.where(row <= col, lf_col, 0.0), axis=0, keepdims=True)
        log_d = jnp.where(causal, b_col - b_row + i_row, -jnp.inf)
        m_prev = m_sc[h, 0:1, 0:1]
        log_inter = b_col + m_prev
        m_t = jnp.maximum(jnp.max(log_d, axis=1, keepdims=True), log_inter)
        qb = q.astype(BF16)
        kb = k.astype(BF16)
        vb = v_ref[:, hs].astype(BF16)
        s = lax.dot_general(qb, kb, (((1,), (1,)), ((), ())), preferred_element_type=F32) * jnp.exp(log_d - m_t)
        inter = jnp.exp(log_inter - m_t)
        ct = ct_sc[h]
        n_row = n_sc[h, 0:1, :]
        num = (jnp.dot(s.astype(BF16), vb, preferred_element_type=F32)
               + inter * jnp.dot(qb, ct.astype(BF16), preferred_element_type=F32))
        den = jnp.sum(s, axis=1, keepdims=True) + inter * jnp.sum(q * n_row, axis=1, keepdims=True)
        hh = num / jnp.maximum(jnp.abs(den), jnp.exp(-m_t))
        m_new = m_t[L - 1:L, :]
        b_last = b_col[L - 1:L, :]
        w_col = jnp.exp(b_last - b_col + i_col - m_new)
        decay = jnp.exp(b_last + m_prev - m_new)
        kw = k * w_col
        ct_sc[h] = decay * ct + lax.dot_general(kw.astype(BF16), vb, (((0,), (0,)), ((), ())),
                                                preferred_element_type=F32)
        n_sc[h] = jnp.broadcast_to(decay * n_row + jnp.sum(kw, axis=0, keepdims=True), (SUBLANES, M_HEAD_DIM))
        m_sc[h] = jnp.broadcast_to(m_new, (SUBLANES, LANES))
        z = jax.nn.sigmoid(og_ref[:, hs]) * hh
        y_ref[:, hs] = (_normalize(z) * ng_ref[:, hs]).astype(BF16)


def _mlstm(p, gcol, grow, bcol, brow, conv_w, conv_b, norm_g):
    nc = SEQ // M_CHUNK
    slab = lambda blk: pl.BlockSpec((M_CHUNK, M_W), lambda b, c: (b * nc + c, blk))
    return pl.pallas_call(
        _mlstm_kernel,
        out_shape=jax.ShapeDtypeStruct((TOKENS, M_W), BF16),
        grid=(BATCH, nc),
        in_specs=[
            slab(P_QM_BLK), slab(P_KM_BLK), slab(P_VM_BLK), slab(P_OM_BLK),
            pl.BlockSpec((None, M_CHUNK, 2 * M_HEADS), lambda b, c: (b, c, 0)),
            pl.BlockSpec((None, 2 * M_HEADS, M_CHUNK), lambda b, c: (b, 0, c)),
            pl.BlockSpec((1, 2 * M_HEADS), lambda b, c: (0, 0)),
            pl.BlockSpec((2 * M_HEADS, 1), lambda b, c: (0, 0)),
            pl.BlockSpec((CONV_K, 2 * M_W), lambda b, c: (0, 0)),
            pl.BlockSpec((1, 2 * M_W), lambda b, c: (0, 0)),
            pl.BlockSpec((1, M_W), lambda b, c: (0, 0)),
        ],
        out_specs=pl.BlockSpec((M_CHUNK, M_W), lambda b, c: (b * nc + c, 0)),
        scratch_shapes=[
            pltpu.VMEM((M_CHUNK + CONV_HALO, M_W), F32),
            pltpu.VMEM((M_CHUNK + CONV_HALO, M_W), F32),
            pltpu.VMEM((M_HEADS, M_HEAD_DIM, M_HEAD_DIM), F32),
            pltpu.VMEM((M_HEADS, SUBLANES, M_HEAD_DIM), F32),
            pltpu.VMEM((M_HEADS, SUBLANES, LANES), F32),
        ],
        compiler_params=_cparams(("arbitrary", "arbitrary")),
        name="mlstm",
    )(p, p, p, p, gcol, grow, bcol, brow, conv_w, conv_b, norm_g)


RT_E1, RT_E2, RT_W1, RT_W2 = 0, 1, 2, 3


def _route(logits):
    lane = lax.broadcasted_iota(jnp.int32, logits.shape, 1).astype(F32)
    big = float(LANES)
    is_g = lane < N_EXPERT_GROUPS
    gl = jnp.where(is_g, logits, -jnp.inf)
    gexp = jnp.exp(gl - jnp.max(gl, axis=1, keepdims=True))
    gprob = gexp / jnp.sum(gexp, axis=1, keepdims=True)
    g_w = jnp.max(gprob, axis=1, keepdims=True)
    g_top = jnp.min(jnp.where(is_g & (gprob == g_w), lane, big), axis=1, keepdims=True)
    lo = N_EXPERT_GROUPS + EXPERTS_PER_GROUP * g_top
    in_grp = (lane >= lo) & (lane < lo + EXPERTS_PER_GROUP)
    el = jnp.where(in_grp, logits, -jnp.inf)
    eexp = jnp.exp(el - jnp.max(el, axis=1, keepdims=True))
    eprob = eexp / jnp.sum(eexp, axis=1, keepdims=True)
    v1 = jnp.max(eprob, axis=1, keepdims=True)
    i1 = jnp.min(jnp.where(in_grp & (eprob == v1), lane, big), axis=1, keepdims=True)
    rest = jnp.where(in_grp & (lane != i1), eprob, -1.0)
    v2 = jnp.max(rest, axis=1, keepdims=True)
    i2 = jnp.min(jnp.where(rest == v2, lane, big), axis=1, keepdims=True)
    tot = v1 + v2
    w1 = g_w * (v1 / tot)
    w2 = g_w * (v2 / tot)
    e1 = i1 - N_EXPERT_GROUPS
    e2 = i2 - N_EXPERT_GROUPS
    rec = jnp.where(lane == RT_E1, e1, jnp.where(lane == RT_E2, e2, jnp.where(lane == RT_W1, w1, w2)))
    return jnp.where(lane <= RT_W2, rec, 0.0)


def _merge_kernel(ya_ref, ym_ref, g_ref, x_ref, mod_ref,
                  wpa_ref, wpm_ref, wout_ref, lng_ref, lnb_ref, wr_ref, br_ref,
                  x1_ref, u2_ref, rt_ref):
    for sb in range(MERGE_TM // MERGE_SUB):
        rows = slice(sb * MERGE_SUB, (sb + 1) * MERGE_SUB)
        pa = jnp.dot(ya_ref[rows, :], wpa_ref[...], preferred_element_type=F32)
        pm = jnp.dot(ym_ref[rows, :], wpm_ref[...], preferred_element_type=F32)
        merged = g_ref[rows, :D_MODEL].astype(F32) * pa + g_ref[rows, D_MODEL:].astype(F32) * pm
        mix = jnp.dot(merged.astype(BF16), wout_ref[...], preferred_element_type=F32)
        z = DEEPNORM_ALPHA * x_ref[rows, :] + mod_ref[2:3, :] * mix
        x1 = _normalize(z) * lng_ref[...] + lnb_ref[...]
        x1_ref[rows, :] = x1
        u2 = _normalize(x1) * (1.0 + mod_ref[4:5, :]) + mod_ref[3:4, :]
        u2_ref[rows, :] = u2
        logits = jnp.dot(u2.astype(BF16), wr_ref[...], preferred_element_type=F32) + br_ref[...]
        rt_ref[rows, :] = _route(logits)


def _merge(ya, ym, g, x2, mod, wpa, wpm, wout, lng, lnb, wr, br):
    tm = MERGE_TM
    tiles_per_batch = SEQ // tm
    rowblk = lambda w: pl.BlockSpec((tm, w), lambda m: (m, 0))
    const = lambda shape: pl.BlockSpec(shape, lambda m: (0,) * len(shape), pipeline_mode=pl.Buffered(1))
    return pl.pallas_call(
        _merge_kernel,
        out_shape=(
            jax.ShapeDtypeStruct((TOKENS, D_MODEL), F32),
            jax.ShapeDtypeStruct((TOKENS, D_MODEL), F32),
            jax.ShapeDtypeStruct((TOKENS, LANES), F32),
        ),
        grid=(TOKENS // tm,),
        in_specs=[
            rowblk(A_GROUP_W), rowblk(M_W), rowblk(2 * D_MODEL), rowblk(D_MODEL),
            pl.BlockSpec((None, 6, D_MODEL), lambda m: (m // tiles_per_batch, 0, 0)),
            const((A_GROUP_W, D_MODEL)), const((M_W, D_MODEL)), const((D_MODEL, D_MODEL)),
            const((1, D_MODEL)), const((1, D_MODEL)),
            const((D_MODEL, LANES)), const((1, LANES)),
        ],
        out_specs=(rowblk(D_MODEL), rowblk(D_MODEL), rowblk(LANES)),
        compiler_params=_cparams(("arbitrary",)),
        name="merge_ln1_route",
    )(ya, ym, g, x2, mod, wpa, wpm, wout, lng, lnb, wr, br)


CAST_ROWS = 128


def _issue_rows(src_ref, idx_ref, base, buf, slot, sem, nrows, unrolled=False):
    def body(blk, carry):
        for j in range(SUBLANES):
            row = idx_ref[base + blk * SUBLANES + j]
            src = src_ref.at[lax.shift_right_logical(row, 3), pl.ds(row & (SUBLANES - 1), 1)]
            pltpu.make_async_copy(src, buf.at[slot, blk, pl.ds(j, 1)], sem.at[slot]).start()
        return carry

    if unrolled:
        for blk in range(nrows // SUBLANES):
            body(blk, 0)
    else:
        lax.fori_loop(0, nrows // SUBLANES, body, 0)


def _wait_rows(src_ref, buf, slot, sem, nrows):
    groups = nrows // SUBLANES
    pltpu.make_async_copy(src_ref.at[pl.ds(0, groups)], buf.at[slot, pl.ds(0, groups)], sem.at[slot]).wait()


def _expert_changed(te_ref, i):
    return (i == 0) | (te_ref[i] != te_ref[jnp.maximum(i - 1, 0)])


def _moe_kernel(te_ref, tn_ref, tv_ref, nu_ref, tok_ref, u_ref, wg_ref, wu_ref, wd_ref, o_ref,
                stg, stu, std, wgb, wub, wdb, xbuf, wsem, xsem):
    i = pl.program_id(0)
    nu = nu_ref[0]
    tm = o_ref.shape[0]
    half = tm // 2

    def weight_copies(e):
        return (pltpu.make_async_copy(wg_ref.at[e], stg, wsem.at[0]),
                pltpu.make_async_copy(wu_ref.at[e], stu, wsem.at[1]),
                pltpu.make_async_copy(wd_ref.at[e], std, wsem.at[2]))

    @pl.when(i == 0)
    def _():
        _issue_rows(u_ref, tok_ref, 0, xbuf, 0, xsem, tm)
        _issue_rows(u_ref, tok_ref, jnp.minimum(1, nu - 1) * tm, xbuf, 1, xsem, tm)
        for cp in weight_copies(te_ref[0]):
            cp.start(priority=1)

    @pl.when(i < nu)
    def _():
        slot = i % MOE_ROW_SLOTS

        @pl.when(_expert_changed(te_ref, i))
        def _():
            for cp in weight_copies(te_ref[i]):
                cp.wait()
            for src, dst in ((stg, wgb), (stu, wub), (std, wdb)):
                def cast_rows(ci, carry, src=src, dst=dst):
                    r = pl.multiple_of(ci * CAST_ROWS, CAST_ROWS)
                    dst[pl.ds(r, CAST_ROWS), :] = src[pl.ds(r, CAST_ROWS), :].astype(BF16)
                    return carry

                lax.fori_loop(0, src.shape[0] // CAST_ROWS, cast_rows, 0)

            @pl.when(tn_ref[i] >= 0)
            def _():
                for cp in weight_copies(tn_ref[i]):
                    cp.start(priority=1)

        def compute(nrows):
            _wait_rows(u_ref, xbuf, slot, xsem, tm)
            x = xbuf[slot, 0:nrows // SUBLANES].reshape(nrows, D_MODEL).astype(BF16)
            _issue_rows(u_ref, tok_ref, jnp.minimum(i + 2, nu - 1) * tm, xbuf, (i + 2) % MOE_ROW_SLOTS, xsem, tm,
                        unrolled=True)
            a = jnp.dot(x, wgb[...], preferred_element_type=F32)
            b = jnp.dot(x, wub[...], preferred_element_type=F32)
            h = (_silu(a) * b).astype(BF16)
            o_ref[0:nrows, :] = jnp.dot(h, wdb[...], preferred_element_type=F32)
            if nrows < tm:
                o_ref[nrows:tm, :] = jnp.zeros((tm - nrows, D_MODEL), F32)

        @pl.when(tv_ref[i] <= half)
        def _():
            compute(half)

        @pl.when(tv_ref[i] > half)
        def _():
            compute(tm)

        @pl.when(i == nu - 1)
        def _():
            _wait_rows(u_ref, xbuf, (i + 1) % MOE_ROW_SLOTS, xsem, tm)
            _wait_rows(u_ref, xbuf, (i + 2) % MOE_ROW_SLOTS, xsem, tm)

    @pl.when(i >= nu)
    def _():
        o_ref[...] = jnp.zeros_like(o_ref)


def _moe(te, te_next, tile_rows, n_used, row_token, u2, w_eg, w_eu, w_ed):
    tm = MOE_TM
    return pl.pallas_call(
        _moe_kernel,
        out_shape=jax.ShapeDtypeStruct((MOE_ROWS, D_MODEL), F32),
        grid_spec=pltpu.PrefetchScalarGridSpec(
            num_scalar_prefetch=5,
            grid=(MOE_TILES,),
            in_specs=[pl.BlockSpec(memory_space=pl.ANY)] * 4,
            out_specs=pl.BlockSpec((tm, D_MODEL), lambda i, te, tn, tv, nu, tok: (i, 0)),
            scratch_shapes=[
                pltpu.VMEM((D_MODEL, D_FF_EXPERT), F32),
                pltpu.VMEM((D_MODEL, D_FF_EXPERT), F32),
                pltpu.VMEM((D_FF_EXPERT, D_MODEL), F32),
                pltpu.VMEM((D_MODEL, D_FF_EXPERT), BF16),
                pltpu.VMEM((D_MODEL, D_FF_EXPERT), BF16),
                pltpu.VMEM((D_FF_EXPERT, D_MODEL), BF16),
                pltpu.VMEM((MOE_ROW_SLOTS, tm // SUBLANES, SUBLANES, D_MODEL), F32),
                pltpu.SemaphoreType.DMA((3,)),
                pltpu.SemaphoreType.DMA((MOE_ROW_SLOTS,)),
            ],
        ),
        compiler_params=_cparams(("arbitrary",)),
        name="moe_experts",
    )(te, te_next, tile_rows, n_used, row_token, u2.reshape(TOKENS // SUBLANES, SUBLANES, D_MODEL),
      w_eg, w_eu, w_ed)


def _final_kernel(pos_ref, o_ref, x1_ref, rt_ref, mod_ref, lng_ref, lnb_ref, y_ref, buf, sem):
    i = pl.program_id(0)
    tb = x1_ref.shape[0]
    slot = i % 2

    last = pl.num_programs(0) - 1

    @pl.when(i == 0)
    def _():
        _issue_rows(o_ref, pos_ref, 0, buf, 0, sem, 2 * tb)

    _wait_rows(o_ref, buf, slot, sem, 2 * tb)
    rows = buf[slot].reshape(2 * tb, D_MODEL)
    ffn = rt_ref[:, RT_W1:RT_W1 + 1] * rows[0:tb, :] + rt_ref[:, RT_W2:RT_W2 + 1] * rows[tb:2 * tb, :]
    _issue_rows(o_ref, pos_ref, jnp.minimum(i + 1, last) * (2 * tb), buf, 1 - slot, sem, 2 * tb, unrolled=True)
    z = DEEPNORM_ALPHA * x1_ref[...] + mod_ref[5:6, :] * ffn
    y_ref[...] = _normalize(z) * lng_ref[...] + lnb_ref[...]

    @pl.when(i == last)
    def _():
        _wait_rows(o_ref, buf, 1 - slot, sem, 2 * tb)


def _final(pos_tiles, moe_out, x1, rt, mod, lng, lnb):
    tb = FINAL_TB
    tiles_per_batch = SEQ // tb
    return pl.pallas_call(
        _final_kernel,
        out_shape=jax.ShapeDtypeStruct((TOKENS, D_MODEL), F32),
        grid_spec=pltpu.PrefetchScalarGridSpec(
            num_scalar_prefetch=1,
            grid=(TOKENS // tb,),
            in_specs=[
                pl.BlockSpec(memory_space=pl.ANY),
                pl.BlockSpec((tb, D_MODEL), lambda i, pos: (i, 0)),
                pl.BlockSpec((tb, LANES), lambda i, pos: (i, 0)),
                pl.BlockSpec((None, 6, D_MODEL), lambda i, pos: (i // tiles_per_batch, 0, 0)),
                pl.BlockSpec((1, D_MODEL), lambda i, pos: (0, 0)),
                pl.BlockSpec((1, D_MODEL), lambda i, pos: (0, 0)),
            ],
            out_specs=pl.BlockSpec((tb, D_MODEL), lambda i, pos: (i, 0)),
            scratch_shapes=[pltpu.VMEM((2, 2 * tb // SUBLANES, SUBLANES, D_MODEL), F32),
                            pltpu.SemaphoreType.DMA((2,))],
        ),
        compiler_params=_cparams(("arbitrary",)),
        name="combine_ln2",
    )(pos_tiles, moe_out.reshape(MOE_ROWS // SUBLANES, SUBLANES, D_MODEL), x1, rt, mod, lng, lnb)


def _routing_tables(e_ids):
    tm = MOE_TM
    e_flat = e_ids.reshape(-1)
    onehot = (e_flat[:, None] == jnp.arange(N_EXPERTS, dtype=jnp.int32)[None, :]).astype(jnp.int32)
    csum = jnp.cumsum(onehot, axis=0)
    rank = jnp.sum((csum - onehot) * onehot, axis=1)
    counts = csum[-1]
    padded = ((counts + tm - 1) // tm) * tm
    pend = jnp.cumsum(padded)
    pstart = pend - padded
    dest = pstart[e_flat] + rank
    tok = jnp.arange(2 * TOKENS, dtype=jnp.int32) // 2
    row_token = jnp.zeros((MOE_ROWS,), jnp.int32).at[dest].set(tok)
    n_used = (pend[-1] // tm).astype(jnp.int32)
    tile_start = jnp.arange(MOE_TILES, dtype=jnp.int32) * tm
    te = jnp.sum((pend[None, :] <= tile_start[:, None]).astype(jnp.int32), axis=1)
    te = jnp.minimum(te, N_EXPERTS - 1)
    te_last = te[jnp.maximum(n_used - 1, 0)]
    te = jnp.where(jnp.arange(MOE_TILES) < n_used, te, te_last)
    next_run = pend[te] // tm
    te_next = jnp.where(next_run < n_used, te[jnp.minimum(next_run, MOE_TILES - 1)], -1).astype(jnp.int32)
    tile_rows = jnp.clip(pstart[te] + counts[te] - tile_start, 0, tm).astype(jnp.int32)
    pos = dest.reshape(TOKENS, 2)
    return row_token, te, te_next, tile_rows, n_used.reshape(1), pos


def _rope_tables():
    inv = ROPE_THETA ** (-jnp.arange(0, HEAD_DIM_A, 2, dtype=F32) / HEAD_DIM_A)
    ang = jnp.arange(SEQ, dtype=F32)[:, None] * inv[None, :]
    cos = jnp.cos(ang)
    sin = jnp.sin(ang)
    return jnp.concatenate([cos, cos], axis=-1), jnp.concatenate([-sin, sin], axis=-1)


def kernel(x, c, w_ada, b_ada, w_in, b_mgate, conv_w, conv_b, m_norm_g, w_proj_a, w_proj_m, w_gate, b_gate,
           w_out, ln1_g, ln1_b, w_rg, b_rg, w_re, b_re, w_eg, w_eu, w_ed, ln2_g, ln2_b):
    assert x.shape == (BATCH, SEQ, D_MODEL) and w_ada.shape[0] == 1
    l = 0
    x2 = x.reshape(TOKENS, D_MODEL)

    c_pad = jnp.zeros((ADA_ROWS, D_MODEL), F32).at[:BATCH].set(c)
    mod = _ada(c_pad, w_ada[l], b_ada[l][None, :])[:BATCH].reshape(BATCH, 6, D_MODEL)

    w_in_t = jnp.swapaxes(w_in[l], 0, 1)
    w_if_t = jnp.zeros((LANES, D_MODEL), F32).at[:2 * M_HEADS].set(w_in_t[N_IN_MAIN:])
    p, gates, u = _inproj(x2, mod, w_in_t, w_if_t)
    g = _gateproj(u, w_gate[l], b_gate[l][None, :])

    cos_t, sin_t = _rope_tables()
    ya = _attention(p, cos_t, sin_t)

    gcol = gates[:, :2 * M_HEADS].reshape(BATCH, SEQ, 2 * M_HEADS)
    grow = jnp.transpose(gcol, (0, 2, 1))
    ym = _mlstm(p, gcol, grow, b_mgate[l][None, :], b_mgate[l][:, None], conv_w[l], conv_b[l][None, :],
                m_norm_g[l][None, :])

    wr = (jnp.zeros((D_MODEL, LANES), F32)
          .at[:, :N_EXPERT_GROUPS].set(w_rg[l])
          .at[:, N_EXPERT_GROUPS:N_EXPERT_GROUPS + N_EXPERTS].set(w_re[l])).astype(BF16)
    br = (jnp.zeros((1, LANES), F32)
          .at[0, :N_EXPERT_GROUPS].set(b_rg[l])
          .at[0, N_EXPERT_GROUPS:N_EXPERT_GROUPS + N_EXPERTS].set(b_re[l]))
    x1, u2, rt = _merge(ya, ym, g, x2, mod,
                        w_proj_a[l].astype(BF16), w_proj_m[l].astype(BF16), w_out[l].astype(BF16),
                        ln1_g[l][None, :], ln1_b[l][None, :], wr, br)

    e_ids = rt[:, RT_E1:RT_E2 + 1].astype(jnp.int32)
    row_token, te, te_next, tile_rows, n_used, pos = _routing_tables(e_ids)

    mo = _moe(te, te_next, tile_rows, n_used, row_token, u2,
              w_eg[l].reshape(N_EXPERTS, D_MODEL, D_FF_EXPERT), w_eu[l].reshape(N_EXPERTS, D_MODEL, D_FF_EXPERT),
              w_ed[l].reshape(N_EXPERTS, D_FF_EXPERT, D_MODEL))

    nt = TOKENS // FINAL_TB
    pos_tiles = jnp.transpose(pos.reshape(nt, FINAL_TB, 2), (0, 2, 1)).reshape(-1)
    y = _final(pos_tiles, mo, x1, rt, mod, ln2_g[l][None, :], ln2_b[l][None, :])
    return y.reshape(BATCH, SEQ, D_MODEL)
```

```python
import functools

import jax
import jax.numpy as jnp
from jax import lax
from jax.experimental import pallas as pl
from jax.experimental.pallas import tpu as pltpu

F32 = jnp.float32
BF16 = jnp.bfloat16

D_MODEL = 2048
BATCH = 4
SEQ = 2048
TOKENS = BATCH * SEQ
DIL_CONFIGS = ((128, 1), (512, 4), (2048, 16))
N_DIL_GROUPS = 3
HEADS_PER_GROUP = 4
HEAD_DIM_A = 128
ATT_BLOCK = 128
ROPE_THETA = 10000.0
A_GROUP_W = HEADS_PER_GROUP * HEAD_DIM_A
A_QKV_W = N_DIL_GROUPS * A_GROUP_W
M_HEADS = 4
M_HEAD_DIM = 256
M_W = M_HEADS * M_HEAD_DIM
M_CHUNK = 128
CONV_K = 4
N_IN_MAIN = 3 * A_QKV_W + 4 * M_W
N_EXPERT_GROUPS = 4
EXPERTS_PER_GROUP = 8
N_EXPERTS = N_EXPERT_GROUPS * EXPERTS_PER_GROUP
D_FF_EXPERT = 1024
DEEPNORM_ALPHA = 2.0 ** 0.25
LN_EPS = 1e-5

LANES = 128
SUBLANES = 8
VMEM_LIMIT_BYTES = 56 * 1024 * 1024

PROJ_TN = 512
GATE_TN = 1024
PROJ_TM = 1024
P_TILES_IN = N_IN_MAIN // PROJ_TN
P_ATT_TILES = 3 * A_QKV_W // PROJ_TN
P_M_TILES = P_TILES_IN - P_ATT_TILES
P_WIDTH = N_IN_MAIN
P_ATT_OFF = P_M_TILES * PROJ_TN
P_QM_BLK, P_KM_BLK, P_VM_BLK, P_OM_BLK = 0, 1, 2, 3
MERGE_TM = 256
MERGE_SUB = 256
MOE_TM = 256
MOE_ROWS = 2 * TOKENS + N_EXPERTS * MOE_TM
MOE_TILES = MOE_ROWS // MOE_TM
MOE_ROW_SLOTS = 3
FINAL_TB = 256


def _cparams(sem, vmem=VMEM_LIMIT_BYTES):
    return pltpu.CompilerParams(dimension_semantics=sem, vmem_limit_bytes=vmem)


def _normalize(x):
    mu = jnp.mean(x, axis=-1, keepdims=True)
    xc = x - mu
    var = jnp.mean(xc * xc, axis=-1, keepdims=True)
    return xc * lax.rsqrt(var + LN_EPS)


def _silu(x):
    return x * jax.nn.sigmoid(x)


def _log_sigmoid(x):
    return jnp.minimum(x, 0.0) - jnp.log(1.0 + jnp.exp(-jnp.abs(x)))


_CONTRACT_LAST = (((1,), (1,)), ((), ()))
LOG2E = 1.4426950408889634


ADA_TN = 1024
ADA_ROWS = 16


def _ada_kernel(c_ref, w_ref, b_ref, o_ref):
    sc = _silu(c_ref[...]).astype(BF16)
    o_ref[...] = jnp.dot(sc, w_ref[...].astype(BF16), preferred_element_type=F32) + b_ref[...]


def _ada(c_pad, w_ada, b_ada):
    n = w_ada.shape[1]
    return pl.pallas_call(
        _ada_kernel,
        out_shape=jax.ShapeDtypeStruct((ADA_ROWS, n), F32),
        grid=(n // ADA_TN,),
        in_specs=[
            pl.BlockSpec((ADA_ROWS, D_MODEL), lambda j: (0, 0)),
            pl.BlockSpec((D_MODEL, ADA_TN), lambda j: (0, j)),
            pl.BlockSpec((1, ADA_TN), lambda j: (0, j)),
        ],
        out_specs=pl.BlockSpec((ADA_ROWS, ADA_TN), lambda j: (0, j)),
        compiler_params=_cparams(("arbitrary",)),
        name="ada_mod",
    )(c_pad, w_ada, b_ada)


LN_CHUNK = 256


def _inproj_kernel(x_ref, mod_ref, w_ref, wif_ref, p_ref, g_ref, u_ref):
    n = pl.program_id(1)

    @pl.when(n == 0)
    def _():
        shift = mod_ref[0:1, :]
        scale = 1.0 + mod_ref[1:2, :]
        wif = wif_ref[...].astype(BF16)
        w = w_ref[...].astype(BF16)
        for ci in range(PROJ_TM // LN_CHUNK):
            rows = slice(ci * LN_CHUNK, (ci + 1) * LN_CHUNK)
            u = (_normalize(x_ref[rows, :]) * scale + shift).astype(BF16)
            u_ref[rows, :] = u
            g_ref[rows, :] = lax.dot_general(u, wif, _CONTRACT_LAST, preferred_element_type=F32)
            p_ref[rows, :] = lax.dot_general(u, w, _CONTRACT_LAST, preferred_element_type=F32)

    @pl.when(n > 0)
    def _():
        p_ref[...] = lax.dot_general(u_ref[...], w_ref[...].astype(BF16), _CONTRACT_LAST,
                                     preferred_element_type=F32)


def _inproj(x2, mod, w_in_t, w_if_t):
    tiles_per_batch = SEQ // PROJ_TM
    return pl.pallas_call(
        _inproj_kernel,
        out_shape=(
            jax.ShapeDtypeStruct((TOKENS, P_WIDTH), F32),
            jax.ShapeDtypeStruct((TOKENS, LANES), F32),
            jax.ShapeDtypeStruct((TOKENS, D_MODEL), BF16),
        ),
        grid=(TOKENS // PROJ_TM, P_TILES_IN),
        in_specs=[
            pl.BlockSpec((PROJ_TM, D_MODEL), lambda m, n: (m, 0)),
            pl.BlockSpec((None, 6, D_MODEL), lambda m, n: (m // tiles_per_batch, 0, 0)),
            pl.BlockSpec((PROJ_TN, D_MODEL), lambda m, n: (n, 0)),
            pl.BlockSpec((LANES, D_MODEL), lambda m, n: (0, 0)),
        ],
        out_specs=(
            pl.BlockSpec((PROJ_TM, PROJ_TN),
                         lambda m, n: (m, jnp.where(n < P_ATT_TILES, n + P_M_TILES, n - P_ATT_TILES))),
            pl.BlockSpec((PROJ_TM, LANES), lambda m, n: (m, 0)),
            pl.BlockSpec((PROJ_TM, D_MODEL), lambda m, n: (m, 0)),
        ),
        compiler_params=_cparams(("arbitrary", "arbitrary")),
        name="in_proj",
    )(x2, mod, w_in_t, w_if_t)


def _gateproj_kernel(u_ref, w_ref, b_ref, o_ref):
    acc = jnp.dot(u_ref[...], w_ref[...].astype(BF16), preferred_element_type=F32)
    o_ref[...] = jax.nn.sigmoid(acc + b_ref[...]).astype(BF16)


def _gateproj(u, w_gate, b_gate):
    n = w_gate.shape[1]
    return pl.pallas_call(
        _gateproj_kernel,
        out_shape=jax.ShapeDtypeStruct((TOKENS, n), BF16),
        grid=(TOKENS // PROJ_TM, n // GATE_TN),
        in_specs=[
            pl.BlockSpec((PROJ_TM, D_MODEL), lambda m, j: (m, 0)),
            pl.BlockSpec((D_MODEL, GATE_TN), lambda m, j: (0, j)),
            pl.BlockSpec((1, GATE_TN), lambda m, j: (0, j)),
        ],
        out_specs=pl.BlockSpec((PROJ_TM, GATE_TN), lambda m, j: (m, j)),
        compiler_params=_cparams(("arbitrary", "arbitrary")),
        name="gate_proj",
    )(u, w_gate, b_gate)


def _rows(start, size, stride):
    return pl.ds(start, size) if stride == 1 else pl.ds(start, size, stride=stride)


def _attn_kernel(q0, k0, v0, q1, k1, v1, q2, k2, v2, cos_ref, sin_ref, y_ref,
                 qr_sc, kr_sc, o0, o1, o2, l0, l1, l2):
    cos = cos_ref[...]
    sin = sin_ref[...]
    scale = HEAD_DIM_A ** -0.5
    blk = ATT_BLOCK
    qi2 = lax.broadcasted_iota(jnp.int32, (blk, 2 * blk), 0)
    kc2 = lax.broadcasted_iota(jnp.int32, (blk, 2 * blk), 1)
    mask_prev_cur = (kc2 >= qi2) & (kc2 <= qi2 + blk)
    qi1 = lax.broadcasted_iota(jnp.int32, (blk, blk), 0)
    kc1 = lax.broadcasted_iota(jnp.int32, (blk, blk), 1)
    mask_cur = kc1 <= qi1
    half = HEAD_DIM_A // 2
    groups = ((q0, k0, v0, o0, l0), (q1, k1, v1, o1, l1), (q2, k2, v2, o2, l2))
    for (window, d), (q_ref, k_ref, v_ref, o_sc, l_sc) in zip(DIL_CONFIGS, groups):
        nb = SEQ // d // blk
        q = q_ref[...]
        k = k_ref[...]
        qr_sc[...] = q * cos + pltpu.roll(q, half, 1) * sin
        kr_sc[...] = k * cos + pltpu.roll(k, half, 1) * sin
        for r in range(d):
            for j in range(nb):
                start = r + j * blk * d
                cur = _rows(start, blk, d)
                qb = qr_sc[cur, :].astype(BF16)
                if j == 0:
                    keys = cur
                    mask = mask_cur
                else:
                    keys = _rows(start - blk * d, 2 * blk, d)
                    mask = mask_prev_cur
                kw = kr_sc[keys, :].astype(BF16)
                vb = v_ref[keys, :].astype(BF16)
                vw = jnp.concatenate([vb, jnp.ones_like(vb)], axis=1)
                s = lax.dot_general(qb, kw, _CONTRACT_LAST, preferred_element_type=F32)
                s = jnp.where(mask, s, -jnp.inf)
                mx = jnp.max(s, axis=-1, keepdims=True)
                p = jnp.exp2((s - mx) * (scale * LOG2E))
                pv = jnp.dot(p.astype(BF16), vw, preferred_element_type=F32)
                den = pv[:, HEAD_DIM_A:]
                o_sc[cur, :] = pv[:, :HEAD_DIM_A] / den
                l_sc[cur, :] = mx * scale + jnp.log(den)
    la = l0[...]
    lb = l1[...]
    lc = l2[...]
    mx = jnp.maximum(jnp.maximum(la, lb), lc)
    ea = jnp.exp(la - mx)
    eb = jnp.exp(lb - mx)
    ec = jnp.exp(lc - mx)
    den = ea + eb + ec
    y_ref[...] = ((ea / den) * o0[...] + (eb / den) * o1[...] + (ec / den) * o2[...]).astype(BF16)


def _attention(p, cos_t, sin_t):
    for window, d in DIL_CONFIGS:
        assert window // d == ATT_BLOCK and SEQ % (d * ATT_BLOCK) == 0
    col0 = P_ATT_OFF // HEAD_DIM_A

    def slab(part, g):
        off = col0 + (part * A_QKV_W + g * A_GROUP_W) // HEAD_DIM_A
        return pl.BlockSpec((SEQ, HEAD_DIM_A), lambda b, h: (b, off + h))

    in_specs = [slab(part, g) for g in range(N_DIL_GROUPS) for part in range(3)]
    table = pl.BlockSpec((SEQ, HEAD_DIM_A), lambda b, h: (0, 0))
    return pl.pallas_call(
        _attn_kernel,
        out_shape=jax.ShapeDtypeStruct((TOKENS, A_GROUP_W), BF16),
        grid=(BATCH, HEADS_PER_GROUP),
        in_specs=in_specs + [table, table],
        out_specs=pl.BlockSpec((SEQ, HEAD_DIM_A), lambda b, h: (b, h)),
        scratch_shapes=[pltpu.VMEM((SEQ, HEAD_DIM_A), F32)] * 8,
        compiler_params=_cparams(("arbitrary", "arbitrary")),
        name="dil_attn",
    )(*([p] * 9), cos_t, sin_t)


CONV_HALO = SUBLANES


def _mlstm_kernel(q_ref, k_ref, v_ref, og_ref, gc_ref, gr_ref, bc_ref, br_ref, cw_ref, cb_ref, ng_ref,
                  y_ref, xq_sc, xk_sc, ct_sc, n_sc, m_sc):
    c = pl.program_id(1)
    L = M_CHUNK
    lo = CONV_HALO

    @pl.when(c == 0)
    def _():
        xq_sc[0:lo, :] = jnp.zeros((lo, M_W), F32)
        xk_sc[0:lo, :] = jnp.zeros((lo, M_W), F32)
        ct_sc[...] = jnp.zeros_like(ct_sc)
        n_sc[...] = jnp.zeros_like(n_sc)
        m_sc[...] = jnp.zeros_like(m_sc)

    xq_sc[lo:lo + L, :] = q_ref[...]
    xk_sc[lo:lo + L, :] = k_ref[...]

    def conv(x_sc, w, b):
        acc = x_sc[lo:lo + L, :] * w[CONV_K - 1:CONV_K, :] + b
        for j in range(CONV_K - 1):
            off = lo - (CONV_K - 1) + j
            acc = acc + x_sc[off:off + L, :] * w[j:j + 1, :]
        return acc

    cw = cw_ref[...]
    cb = cb_ref[...]
    qc = _silu(conv(xq_sc, cw[:, :M_W], cb[:, :M_W]))
    kc = _silu(conv(xk_sc, cw[:, M_W:], cb[:, M_W:])) * (M_HEAD_DIM ** -0.5)
    xq_sc[0:lo, :] = xq_sc[L:L + lo, :]
    xk_sc[0:lo, :] = xk_sc[L:L + lo, :]

    row = lax.broadcasted_iota(jnp.int32, (L, L), 0)
    col = lax.broadcasted_iota(jnp.int32, (L, L), 1)
    causal = row >= col
    for h in range(M_HEADS):
        hs = slice(h * M_HEAD_DIM, (h + 1) * M_HEAD_DIM)
        q = qc[:, hs]
        k = kc[:, hs]
        i_col = gc_ref[:, h:h + 1] + bc_ref[:, h:h + 1]
        lf_col = _log_sigmoid(gc_ref[:, M_HEADS + h:M_HEADS + h + 1] + bc_ref[:, M_HEADS + h:M_HEADS + h + 1])
        i_row = gr_ref[h:h + 1, :] + br_ref[h:h + 1, :]
        lf_row = _log_sigmoid(gr_ref[M_HEADS + h:M_HEADS + h + 1, :] + br_ref[M_HEADS + h:M_HEADS + h + 1, :])
        b_col = jnp.sum(jnp.where(causal, lf_row, 0.0), axis=1, keepdims=True)
        b_row = jnp.sum(jnp.where(row <= col, lf_col, 0.0), axis=0, keepdims=True)
        log_d = jnp.where(causal, b_col - b_row + i_row, -jnp.inf)
        m_prev = m_sc[h, 0:1, 0:1]
        log_inter = b_col + m_prev
        m_t = jnp.maximum(jnp.max(log_d, axis=1, keepdims=True), log_inter)
        qb = q.astype(BF16)
        kb = k.astype(BF16)
        vb = v_ref[:, hs].astype(BF16)
        s = lax.dot_general(qb, kb, (((1,), (1,)), ((), ())), preferred_element_type=F32) * jnp.exp(log_d - m_t)
        inter = jnp.exp(log_inter - m_t)
        ct = ct_sc[h]
        n_row = n_sc[h, 0:1, :]
        num = (jnp.dot(s.astype(BF16), vb, preferred_element_type=F32)
               + inter * jnp.dot(qb, ct.astype(BF16), preferred_element_type=F32))
        den = jnp.sum(s, axis=1, keepdims=True) + inter * jnp.sum(q * n_row, axis=1, keepdims=True)
        hh = num / jnp.maximum(jnp.abs(den), jnp.exp(-m_t))
        m_new = m_t[L - 1:L, :]
        b_last = b_col[L - 1:L, :]
        w_col = jnp.exp(b_last - b_col + i_col - m_new)
        decay = jnp.exp(b_last + m_prev - m_new)
        kw = k * w_col
        ct_sc[h] = decay * ct + lax.dot_general(kw.astype(BF16), vb, (((0,), (0,)), ((), ())),
                                                preferred_element_type=F32)
        n_sc[h] = jnp.broadcast_to(decay * n_row + jnp.sum(kw, axis=0, keepdims=True), (SUBLANES, M_HEAD_DIM))
        m_sc[h] = jnp.broadcast_to(m_new, (SUBLANES, LANES))
        z = jax.nn.sigmoid(og_ref[:, hs]) * hh
        y_ref[:, hs] = (_normalize(z) * ng_ref[:, hs]).astype(BF16)


def _mlstm(p, gcol, grow, bcol, brow, conv_w, conv_b, norm_g):
    nc = SEQ // M_CHUNK
    slab = lambda blk: pl.BlockSpec((M_CHUNK, M_W), lambda b, c: (b * nc + c, blk))
    return pl.pallas_call(
        _mlstm_kernel,
        out_shape=jax.ShapeDtypeStruct((TOKENS, M_W), BF16),
        grid=(BATCH, nc),
        in_specs=[
            slab(P_QM_BLK), slab(P_KM_BLK), slab(P_VM_BLK), slab(P_OM_BLK),
            pl.BlockSpec((None, M_CHUNK, 2 * M_HEADS), lambda b, c: (b, c, 0)),
            pl.BlockSpec((None, 2 * M_HEADS, M_CHUNK), lambda b, c: (b, 0, c)),
            pl.BlockSpec((1, 2 * M_HEADS), lambda b, c: (0, 0)),
            pl.BlockSpec((2 * M_HEADS, 1), lambda b, c: (0, 0)),
            pl.BlockSpec((CONV_K, 2 * M_W), lambda b, c: (0, 0)),
            pl.BlockSpec((1, 2 * M_W), lambda b, c: (0, 0)),
            pl.BlockSpec((1, M_W), lambda b, c: (0, 0)),
        ],
        out_specs=pl.BlockSpec((M_CHUNK, M_W), lambda b, c: (b * nc + c, 0)),
        scratch_shapes=[
            pltpu.VMEM((M_CHUNK + CONV_HALO, M_W), F32),
            pltpu.VMEM((M_CHUNK + CONV_HALO, M_W), F32),
            pltpu.VMEM((M_HEADS, M_HEAD_DIM, M_HEAD_DIM), F32),
            pltpu.VMEM((M_HEADS, SUBLANES, M_HEAD_DIM), F32),
            pltpu.VMEM((M_HEADS, SUBLANES, LANES), F32),
        ],
        compiler_params=_cparams(("arbitrary", "arbitrary")),
        name="mlstm",
    )(p, p, p, p, gcol, grow, bcol, brow, conv_w, conv_b, norm_g)


RT_E1, RT_E2, RT_W1, RT_W2 = 0, 1, 2, 3


def _route(logits):
    lane = lax.broadcasted_iota(jnp.int32, logits.shape, 1).astype(F32)
    big = float(LANES)
    is_g = lane < N_EXPERT_GROUPS
    gl = jnp.where(is_g, logits, -jnp.inf)
    gexp = jnp.exp(gl - jnp.max(gl, axis=1, keepdims=True))
    gprob = gexp / jnp.sum(gexp, axis=1, keepdims=True)
    g_w = jnp.max(gprob, axis=1, keepdims=True)
    g_top = jnp.min(jnp.where(is_g & (gprob == g_w), lane, big), axis=1, keepdims=True)
    lo = N_EXPERT_GROUPS + EXPERTS_PER_GROUP * g_top
    in_grp = (lane >= lo) & (lane < lo + EXPERTS_PER_GROUP)
    el = jnp.where(in_grp, logits, -jnp.inf)
    eexp = jnp.exp(el - jnp.max(el, axis=1, keepdims=True))
    eprob = eexp / jnp.sum(eexp, axis=1, keepdims=True)
    v1 = jnp.max(eprob, axis=1, keepdims=True)
    i1 = jnp.min(jnp.where(in_grp & (eprob == v1), lane, big), axis=1, keepdims=True)
    rest = jnp.where(in_grp & (lane != i1), eprob, -1.0)
    v2 = jnp.max(rest, axis=1, keepdims=True)
    i2 = jnp.min(jnp.where(rest == v2, lane, big), axis=1, keepdims=True)
    tot = v1 + v2
    w1 = g_w * (v1 / tot)
    w2 = g_w * (v2 / tot)
    e1 = i1 - N_EXPERT_GROUPS
    e2 = i2 - N_EXPERT_GROUPS
    rec = jnp.where(lane == RT_E1, e1, jnp.where(lane == RT_E2, e2, jnp.where(lane == RT_W1, w1, w2)))
    return jnp.where(lane <= RT_W2, rec, 0.0)


def _merge_kernel(ya_ref, ym_ref, g_ref, x_ref, mod_ref,
                  wpa_ref, wpm_ref, wout_ref, lng_ref, lnb_ref, wr_ref, br_ref,
                  x1_ref, u2_ref, rt_ref):
    for sb in range(MERGE_TM // MERGE_SUB):
        rows = slice(sb * MERGE_SUB, (sb + 1) * MERGE_SUB)
        pa = jnp.dot(ya_ref[rows, :], wpa_ref[...], preferred_element_type=F32)
        pm = jnp.dot(ym_ref[rows, :], wpm_ref[...], preferred_element_type=F32)
        merged = g_ref[rows, :D_MODEL].astype(F32) * pa + g_ref[rows, D_MODEL:].astype(F32) * pm
        mix = jnp.dot(merged.astype(BF16), wout_ref[...], preferred_element_type=F32)
        z = DEEPNORM_ALPHA * x_ref[rows, :] + mod_ref[2:3, :] * mix
        x1 = _normalize(z) * lng_ref[...] + lnb_ref[...]
        x1_ref[rows, :] = x1
        u2 = _normalize(x1) * (1.0 + mod_ref[4:5, :]) + mod_ref[3:4, :]
        u2_ref[rows, :] = u2
        logits = jnp.dot(u2.astype(BF16), wr_ref[...], preferred_element_type=F32) + br_ref[...]
        rt_ref[rows, :] = _route(logits)


def _merge(ya, ym, g, x2, mod, wpa, wpm, wout, lng, lnb, wr, br):
    tm = MERGE_TM
    tiles_per_batch = SEQ // tm
    rowblk = lambda w: pl.BlockSpec((tm, w), lambda m: (m, 0))
    const = lambda shape: pl.BlockSpec(shape, lambda m: (0,) * len(shape), pipeline_mode=pl.Buffered(1))
    return pl.pallas_call(
        _merge_kernel,
        out_shape=(
            jax.ShapeDtypeStruct((TOKENS, D_MODEL), F32),
            jax.ShapeDtypeStruct((TOKENS, D_MODEL), F32),
            jax.ShapeDtypeStruct((TOKENS, LANES), F32),
        ),
        grid=(TOKENS // tm,),
        in_specs=[
            rowblk(A_GROUP_W), rowblk(M_W), rowblk(2 * D_MODEL), rowblk(D_MODEL),
            pl.BlockSpec((None, 6, D_MODEL), lambda m: (m // tiles_per_batch, 0, 0)),
            const((A_GROUP_W, D_MODEL)), const((M_W, D_MODEL)), const((D_MODEL, D_MODEL)),
            const((1, D_MODEL)), const((1, D_MODEL)),
            const((D_MODEL, LANES)), const((1, LANES)),
        ],
        out_specs=(rowblk(D_MODEL), rowblk(D_MODEL), rowblk(LANES)),
        compiler_params=_cparams(("arbitrary",)),
        name="merge_ln1_route",
    )(ya, ym, g, x2, mod, wpa, wpm, wout, lng, lnb, wr, br)


CAST_ROWS = 128


def _issue_rows(src_ref, idx_ref, base, buf, slot, sem, nrows):
    def body(blk, carry):
        for j in range(SUBLANES):
            row = idx_ref[base + blk * SUBLANES + j]
            src = src_ref.at[lax.shift_right_logical(row, 3), pl.ds(row & (SUBLANES - 1), 1)]
            pltpu.make_async_copy(src, buf.at[slot, blk, pl.ds(j, 1)], sem.at[slot]).start(priority=j % 2)
        return carry

    lax.fori_loop(0, nrows // SUBLANES, body, 0)


def _wait_rows(src_ref, buf, slot, sem, nrows):
    groups = nrows // SUBLANES
    pltpu.make_async_copy(src_ref.at[pl.ds(0, groups)], buf.at[slot, pl.ds(0, groups)], sem.at[slot]).wait()


def _expert_changed(te_ref, i):
    return (i == 0) | (te_ref[i] != te_ref[jnp.maximum(i - 1, 0)])


def _moe_kernel(te_ref, tn_ref, tv_ref, nu_ref, tok_ref, u_ref, wg_ref, wu_ref, wd_ref, o_ref,
                stg, stu, std, wgb, wub, wdb, xbuf, wsem, xsem):
    i = pl.program_id(0)
    nu = nu_ref[0]
    tm = o_ref.shape[0]
    half = tm // 2

    def weight_copies(e):
        return (pltpu.make_async_copy(wg_ref.at[e], stg, wsem.at[0]),
                pltpu.make_async_copy(wu_ref.at[e], stu, wsem.at[1]),
                pltpu.make_async_copy(wd_ref.at[e], std, wsem.at[2]))

    def for_tile_rows(t, fn):
        @pl.when(tv_ref[t] <= half)
        def _():
            fn(half)

        @pl.when(tv_ref[t] > half)
        def _():
            fn(tm)

    def issue_tile(t):
        for_tile_rows(t, lambda nrows: _issue_rows(u_ref, tok_ref, t * tm, xbuf, t % MOE_ROW_SLOTS, xsem, nrows))

    @pl.when(i == 0)
    def _():
        issue_tile(0)

        @pl.when(nu > 1)
        def _():
            issue_tile(1)

        for cp in weight_copies(te_ref[0]):
            cp.start(priority=1)

    @pl.when(i < nu)
    def _():
        slot = i % MOE_ROW_SLOTS

        @pl.when(i + 2 < nu)
        def _():
            issue_tile(i + 2)

        @pl.when(_expert_changed(te_ref, i))
        def _():
            for cp in weight_copies(te_ref[i]):
                cp.wait()
            for src, dst in ((stg, wgb), (stu, wub), (std, wdb)):
                def cast_rows(ci, carry, src=src, dst=dst):
                    r = pl.multiple_of(ci * CAST_ROWS, CAST_ROWS)
                    dst[pl.ds(r, CAST_ROWS), :] = src[pl.ds(r, CAST_ROWS), :].astype(BF16)
                    return carry

                lax.fori_loop(0, src.shape[0] // CAST_ROWS, cast_rows, 0)

            @pl.when(tn_ref[i] >= 0)
            def _():
                for cp in weight_copies(tn_ref[i]):
                    cp.start(priority=1)

        def compute(nrows):
            _wait_rows(u_ref, xbuf, slot, xsem, nrows)
            x = xbuf[slot, 0:nrows // SUBLANES].reshape(nrows, D_MODEL).astype(BF16)
            a = jnp.dot(x, wgb[...], preferred_element_type=F32)
            b = jnp.dot(x, wub[...], preferred_element_type=F32)
            h = (_silu(a) * b).astype(BF16)
            o_ref[0:nrows, :] = jnp.dot(h, wdb[...], preferred_element_type=F32)
            if nrows < tm:
                o_ref[nrows:tm, :] = jnp.zeros((tm - nrows, D_MODEL), F32)

        for_tile_rows(i, compute)

    @pl.when(i >= nu)
    def _():
        o_ref[...] = jnp.zeros_like(o_ref)


def _moe(te, te_next, tile_rows, n_used, row_token, u2, w_eg, w_eu, w_ed):
    tm = MOE_TM
    return pl.pallas_call(
        _moe_kernel,
        out_shape=jax.ShapeDtypeStruct((MOE_ROWS, D_MODEL), F32),
        grid_spec=pltpu.PrefetchScalarGridSpec(
            num_scalar_prefetch=5,
            grid=(MOE_TILES,),
            in_specs=[pl.BlockSpec(memory_space=pl.ANY)] * 4,
            out_specs=pl.BlockSpec((tm, D_MODEL), lambda i, te, tn, tv, nu, tok: (i, 0)),
            scratch_shapes=[
                pltpu.VMEM((D_MODEL, D_FF_EXPERT), F32),
                pltpu.VMEM((D_MODEL, D_FF_EXPERT), F32),
                pltpu.VMEM((D_FF_EXPERT, D_MODEL), F32),
                pltpu.VMEM((D_MODEL, D_FF_EXPERT), BF16),
                pltpu.VMEM((D_MODEL, D_FF_EXPERT), BF16),
                pltpu.VMEM((D_FF_EXPERT, D_MODEL), BF16),
                pltpu.VMEM((MOE_ROW_SLOTS, tm // SUBLANES, SUBLANES, D_MODEL), F32),
                pltpu.SemaphoreType.DMA((3,)),
                pltpu.SemaphoreType.DMA((MOE_ROW_SLOTS,)),
            ],
        ),
        compiler_params=_cparams(("arbitrary",)),
        name="moe_experts",
    )(te, te_next, tile_rows, n_used, row_token, u2.reshape(TOKENS // SUBLANES, SUBLANES, D_MODEL),
      w_eg, w_eu, w_ed)


def _final_kernel(pos_ref, o_ref, x1_ref, rt_ref, mod_ref, lng_ref, lnb_ref, y_ref, buf, sem):
    i = pl.program_id(0)
    tb = x1_ref.shape[0]
    slot = i % 2

    @pl.when(i == 0)
    def _():
        _issue_rows(o_ref, pos_ref, 0, buf, 0, sem, 2 * tb)

    @pl.when(i + 1 < pl.num_programs(0))
    def _():
        _issue_rows(o_ref, pos_ref, (i + 1) * (2 * tb), buf, 1 - slot, sem, 2 * tb)

    _wait_rows(o_ref, buf, slot, sem, 2 * tb)
    rows = buf[slot].reshape(2 * tb, D_MODEL)
    ffn = rt_ref[:, RT_W1:RT_W1 + 1] * rows[0:tb, :] + rt_ref[:, RT_W2:RT_W2 + 1] * rows[tb:2 * tb, :]
    z = DEEPNORM_ALPHA * x1_ref[...] + mod_ref[5:6, :] * ffn
    y_ref[...] = _normalize(z) * lng_ref[...] + lnb_ref[...]


def _final(pos_tiles, moe_out, x1, rt, mod, lng, lnb):
    tb = FINAL_TB
    tiles_per_batch = SEQ // tb
    return pl.pallas_call(
        _final_kernel,
        out_shape=jax.ShapeDtypeStruct((TOKENS, D_MODEL), F32),
        grid_spec=pltpu.PrefetchScalarGridSpec(
            num_scalar_prefetch=1,
            grid=(TOKENS // tb,),
            in_specs=[
                pl.BlockSpec(memory_space=pl.ANY),
                pl.BlockSpec((tb, D_MODEL), lambda i, pos: (i, 0)),
                pl.BlockSpec((tb, LANES), lambda i, pos: (i, 0)),
                pl.BlockSpec((None, 6, D_MODEL), lambda i, pos: (i // tiles_per_batch, 0, 0)),
                pl.BlockSpec((1, D_MODEL), lambda i, pos: (0, 0)),
                pl.BlockSpec((1, D_MODEL), lambda i, pos: (0, 0)),
            ],
            out_specs=pl.BlockSpec((tb, D_MODEL), lambda i, pos: (i, 0)),
            scratch_shapes=[pltpu.VMEM((2, 2 * tb // SUBLANES, SUBLANES, D_MODEL), F32),
                            pltpu.SemaphoreType.DMA((2,))],
        ),
        compiler_params=_cparams(("arbitrary",)),
        name="combine_ln2",
    )(pos_tiles, moe_out.reshape(MOE_ROWS // SUBLANES, SUBLANES, D_MODEL), x1, rt, mod, lng, lnb)


def _routing_tables(e_ids):
    tm = MOE_TM
    e_flat = e_ids.reshape(-1)
    onehot = (e_flat[:, None] == jnp.arange(N_EXPERTS, dtype=jnp.int32)[None, :]).astype(jnp.int32)
    csum = jnp.cumsum(onehot, axis=0)
    rank = jnp.sum((csum - onehot) * onehot, axis=1)
    counts = csum[-1]
    padded = ((counts + tm - 1) // tm) * tm
    pend = jnp.cumsum(padded)
    pstart = pend - padded
    dest = pstart[e_flat] + rank
    tok = jnp.arange(2 * TOKENS, dtype=jnp.int32) // 2
    row_token = jnp.zeros((MOE_ROWS,), jnp.int32).at[dest].set(tok)
    n_used = (pend[-1] // tm).astype(jnp.int32)
    tile_start = jnp.arange(MOE_TILES, dtype=jnp.int32) * tm
    te = jnp.sum((pend[None, :] <= tile_start[:, None]).astype(jnp.int32), axis=1)
    te = jnp.minimum(te, N_EXPERTS - 1)
    te_last = te[jnp.maximum(n_used - 1, 0)]
    te = jnp.where(jnp.arange(MOE_TILES) < n_used, te, te_last)
    next_run = pend[te] // tm
    te_next = jnp.where(next_run < n_used, te[jnp.minimum(next_run, MOE_TILES - 1)], -1).astype(jnp.int32)
    tile_rows = jnp.clip(pstart[te] + counts[te] - tile_start, 0, tm).astype(jnp.int32)
    pos = dest.reshape(TOKENS, 2)
    return row_token, te, te_next, tile_rows, n_used.reshape(1), pos


def _rope_tables():
    inv = ROPE_THETA ** (-jnp.arange(0, HEAD_DIM_A, 2, dtype=F32) / HEAD_DIM_A)
    ang = jnp.arange(SEQ, dtype=F32)[:, None] * inv[None, :]
    cos = jnp.cos(ang)
    sin = jnp.sin(ang)
    return jnp.concatenate([cos, cos], axis=-1), jnp.concatenate([-sin, sin], axis=-1)


def kernel(x, c, w_ada, b_ada, w_in, b_mgate, conv_w, conv_b, m_norm_g, w_proj_a, w_proj_m, w_gate, b_gate,
           w_out, ln1_g, ln1_b, w_rg, b_rg, w_re, b_re, w_eg, w_eu, w_ed, ln2_g, ln2_b):
    assert x.shape == (BATCH, SEQ, D_MODEL) and w_ada.shape[0] == 1
    l = 0
    x2 = x.reshape(TOKENS, D_MODEL)

    c_pad = jnp.zeros((ADA_ROWS, D_MODEL), F32).at[:BATCH].set(c)
    mod = _ada(c_pad, w_ada[l], b_ada[l][None, :])[:BATCH].reshape(BATCH, 6, D_MODEL)

    w_in_t = jnp.swapaxes(w_in[l], 0, 1)
    w_if_t = jnp.zeros((LANES, D_MODEL), F32).at[:2 * M_HEADS].set(w_in_t[N_IN_MAIN:])
    p, gates, u = _inproj(x2, mod, w_in_t, w_if_t)
    g = _gateproj(u, w_gate[l], b_gate[l][None, :])

    cos_t, sin_t = _rope_tables()
    ya = _attention(p, cos_t, sin_t)

    gcol = gates[:, :2 * M_HEADS].reshape(BATCH, SEQ, 2 * M_HEADS)
    grow = jnp.transpose(gcol, (0, 2, 1))
    ym = _mlstm(p, gcol, grow, b_mgate[l][None, :], b_mgate[l][:, None], conv_w[l], conv_b[l][None, :],
                m_norm_g[l][None, :])

    wr = (jnp.zeros((D_MODEL, LANES), F32)
          .at[:, :N_EXPERT_GROUPS].set(w_rg[l])
          .at[:, N_EXPERT_GROUPS:N_EXPERT_GROUPS + N_EXPERTS].set(w_re[l])).astype(BF16)
    br = (jnp.zeros((1, LANES), F32)
          .at[0, :N_EXPERT_GROUPS].set(b_rg[l])
          .at[0, N_EXPERT_GROUPS:N_EXPERT_GROUPS + N_EXPERTS].set(b_re[l]))
    x1, u2, rt = _merge(ya, ym, g, x2, mod,
                        w_proj_a[l].astype(BF16), w_proj_m[l].astype(BF16), w_out[l].astype(BF16),
                        ln1_g[l][None, :], ln1_b[l][None, :], wr, br)

    e_ids = rt[:, RT_E1:RT_E2 + 1].astype(jnp.int32)
    row_token, te, te_next, tile_rows, n_used, pos = _routing_tables(e_ids)

    mo = _moe(te, te_next, tile_rows, n_used, row_token, u2,
              w_eg[l].reshape(N_EXPERTS, D_MODEL, D_FF_EXPERT), w_eu[l].reshape(N_EXPERTS, D_MODEL, D_FF_EXPERT),
              w_ed[l].reshape(N_EXPERTS, D_FF_EXPERT, D_MODEL))

    nt = TOKENS // FINAL_TB
    pos_tiles = jnp.transpose(pos.reshape(nt, FINAL_TB, 2), (0, 2, 1)).reshape(-1)
    y = _final(pos_tiles, mo, x1, rt, mod, ln2_g[l][None, :], ln2_b[l][None, :])
    return y.reshape(BATCH, SEQ, D_MODEL)
```

```python
import functools

import jax
import jax.numpy as jnp
from jax import lax
from jax.experimental import pallas as pl
from jax.experimental.pallas import tpu as pltpu

F32 = jnp.float32
BF16 = jnp.bfloat16

D_MODEL = 2048
BATCH = 4
SEQ = 2048
TOKENS = BATCH * SEQ
DIL_CONFIGS = ((128, 1), (512, 4), (2048, 16))
N_DIL_GROUPS = 3
HEADS_PER_GROUP = 4
HEAD_DIM_A = 128
ATT_BLOCK = 128
ROPE_THETA = 10000.0
A_GROUP_W = HEADS_PER_GROUP * HEAD_DIM_A
A_QKV_W = N_DIL_GROUPS * A_GROUP_W
M_HEADS = 4
M_HEAD_DIM = 256
M_W = M_HEADS * M_HEAD_DIM
M_CHUNK = 128
CONV_K = 4
N_IN_MAIN = 3 * A_QKV_W + 4 * M_W
N_EXPERT_GROUPS = 4
EXPERTS_PER_GROUP = 8
N_EXPERTS = N_EXPERT_GROUPS * EXPERTS_PER_GROUP
D_FF_EXPERT = 1024
DEEPNORM_ALPHA = 2.0 ** 0.25
LN_EPS = 1e-5

LANES = 128
SUBLANES = 8
VMEM_LIMIT_BYTES = 56 * 1024 * 1024

PROJ_TN = 512
GATE_TN = 1024
PROJ_TM = 1024
P_TILES_IN = N_IN_MAIN // PROJ_TN
P_ATT_TILES = 3 * A_QKV_W // PROJ_TN
P_M_TILES = P_TILES_IN - P_ATT_TILES
P_WIDTH = N_IN_MAIN
P_ATT_OFF = P_M_TILES * PROJ_TN
P_QM_BLK, P_KM_BLK, P_VM_BLK, P_OM_BLK = 0, 1, 2, 3
MERGE_TM = 256
MERGE_SUB = 256
MOE_TM = 256
MOE_ROWS = 2 * TOKENS + N_EXPERTS * MOE_TM
MOE_TILES = MOE_ROWS // MOE_TM
MOE_ROW_SLOTS = 3
FINAL_TB = 256


def _cparams(sem, vmem=VMEM_LIMIT_BYTES):
    return pltpu.CompilerParams(dimension_semantics=sem, vmem_limit_bytes=vmem)


def _normalize(x):
    mu = jnp.mean(x, axis=-1, keepdims=True)
    xc = x - mu
    var = jnp.mean(xc * xc, axis=-1, keepdims=True)
    return xc * lax.rsqrt(var + LN_EPS)


def _silu(x):
    return x * jax.nn.sigmoid(x)


def _log_sigmoid(x):
    return jnp.minimum(x, 0.0) - jnp.log(1.0 + jnp.exp(-jnp.abs(x)))


_CONTRACT_LAST = (((1,), (1,)), ((), ()))
LOG2E = 1.4426950408889634


ADA_TN = 1024
ADA_ROWS = 16


def _ada_kernel(c_ref, w_ref, b_ref, o_ref):
    sc = _silu(c_ref[...]).astype(BF16)
    o_ref[...] = jnp.dot(sc, w_ref[...].astype(BF16), preferred_element_type=F32) + b_ref[...]


def _ada(c_pad, w_ada, b_ada):
    n = w_ada.shape[1]
    return pl.pallas_call(
        _ada_kernel,
        out_shape=jax.ShapeDtypeStruct((ADA_ROWS, n), F32),
        grid=(n // ADA_TN,),
        in_specs=[
            pl.BlockSpec((ADA_ROWS, D_MODEL), lambda j: (0, 0)),
            pl.BlockSpec((D_MODEL, ADA_TN), lambda j: (0, j)),
            pl.BlockSpec((1, ADA_TN), lambda j: (0, j)),
        ],
        out_specs=pl.BlockSpec((ADA_ROWS, ADA_TN), lambda j: (0, j)),
        compiler_params=_cparams(("arbitrary",)),
        name="ada_mod",
    )(c_pad, w_ada, b_ada)


LN_CHUNK = 256


def _inproj_kernel(x_ref, mod_ref, w_ref, wif_ref, p_ref, g_ref, u_ref):
    n = pl.program_id(1)

    @pl.when(n == 0)
    def _():
        shift = mod_ref[0:1, :]
        scale = 1.0 + mod_ref[1:2, :]
        wif = wif_ref[...]
        w = w_ref[...]
        for ci in range(PROJ_TM // LN_CHUNK):
            rows = slice(ci * LN_CHUNK, (ci + 1) * LN_CHUNK)
            u = (_normalize(x_ref[rows, :]) * scale + shift).astype(BF16)
            u_ref[rows, :] = u
            g_ref[rows, :] = lax.dot_general(u, wif, _CONTRACT_LAST, preferred_element_type=F32)
            p_ref[rows, :] = lax.dot_general(u, w, _CONTRACT_LAST, preferred_element_type=F32)

    @pl.when(n > 0)
    def _():
        p_ref[...] = lax.dot_general(u_ref[...], w_ref[...], _CONTRACT_LAST, preferred_element_type=F32)


def _inproj(x2, mod, w_in_t, w_if_t):
    tiles_per_batch = SEQ // PROJ_TM
    return pl.pallas_call(
        _inproj_kernel,
        out_shape=(
            jax.ShapeDtypeStruct((TOKENS, P_WIDTH), F32),
            jax.ShapeDtypeStruct((TOKENS, LANES), F32),
            jax.ShapeDtypeStruct((TOKENS, D_MODEL), BF16),
        ),
        grid=(TOKENS // PROJ_TM, P_TILES_IN),
        in_specs=[
            pl.BlockSpec((PROJ_TM, D_MODEL), lambda m, n: (m, 0)),
            pl.BlockSpec((None, 6, D_MODEL), lambda m, n: (m // tiles_per_batch, 0, 0)),
            pl.BlockSpec((PROJ_TN, D_MODEL), lambda m, n: (n, 0)),
            pl.BlockSpec((LANES, D_MODEL), lambda m, n: (0, 0)),
        ],
        out_specs=(
            pl.BlockSpec((PROJ_TM, PROJ_TN),
                         lambda m, n: (m, jnp.where(n < P_ATT_TILES, n + P_M_TILES, n - P_ATT_TILES))),
            pl.BlockSpec((PROJ_TM, LANES), lambda m, n: (m, 0)),
            pl.BlockSpec((PROJ_TM, D_MODEL), lambda m, n: (m, 0)),
        ),
        compiler_params=_cparams(("arbitrary", "arbitrary")),
        name="in_proj",
    )(x2, mod, w_in_t, w_if_t)


def _gateproj_kernel(u_ref, w_ref, b_ref, o_ref):
    acc = jnp.dot(u_ref[...], w_ref[...].astype(BF16), preferred_element_type=F32)
    o_ref[...] = jax.nn.sigmoid(acc + b_ref[...]).astype(BF16)


def _gateproj(u, w_gate, b_gate):
    n = w_gate.shape[1]
    return pl.pallas_call(
        _gateproj_kernel,
        out_shape=jax.ShapeDtypeStruct((TOKENS, n), BF16),
        grid=(TOKENS // PROJ_TM, n // GATE_TN),
        in_specs=[
            pl.BlockSpec((PROJ_TM, D_MODEL), lambda m, j: (m, 0)),
            pl.BlockSpec((D_MODEL, GATE_TN), lambda m, j: (0, j)),
            pl.BlockSpec((1, GATE_TN), lambda m, j: (0, j)),
        ],
        out_specs=pl.BlockSpec((PROJ_TM, GATE_TN), lambda m, j: (m, j)),
        compiler_params=_cparams(("arbitrary", "arbitrary")),
        name="gate_proj",
    )(u, w_gate, b_gate)


ROPE_ROWS = 256


def _rows(start, size, stride):
    return pl.ds(start, size) if stride == 1 else pl.ds(start, size, stride=stride)


def _attn_kernel(q0, k0, v0, q1, k1, v1, q2, k2, v2, cos_ref, sin_ref, y_ref,
                 qr_sc, kr_sc, o0, o1, o2, l0, l1, l2):
    scale = HEAD_DIM_A ** -0.5
    blk = ATT_BLOCK
    qi2 = lax.broadcasted_iota(jnp.int32, (blk, 2 * blk), 0)
    kc2 = lax.broadcasted_iota(jnp.int32, (blk, 2 * blk), 1)
    mask_prev_cur = (kc2 >= qi2) & (kc2 <= qi2 + blk)
    qi1 = lax.broadcasted_iota(jnp.int32, (blk, blk), 0)
    kc1 = lax.broadcasted_iota(jnp.int32, (blk, blk), 1)
    mask_cur = kc1 <= qi1
    half = HEAD_DIM_A // 2
    groups = ((q0, k0, v0, o0, l0), (q1, k1, v1, o1, l1), (q2, k2, v2, o2, l2))
    for (window, d), (q_ref, k_ref, v_ref, o_sc, l_sc) in zip(DIL_CONFIGS, groups):
        nb = SEQ // d // blk
        for src_ref, dst_sc in ((q_ref, qr_sc), (k_ref, kr_sc)):
            for c0 in range(0, SEQ, ROPE_ROWS):
                rs = slice(c0, c0 + ROPE_ROWS)
                xr = src_ref[rs, :]
                dst_sc[rs, :] = xr * cos_ref[rs, :] + pltpu.roll(xr, half, 1) * sin_ref[rs, :]
        for r in range(d):
            for j in range(nb):
                start = r + j * blk * d
                cur = _rows(start, blk, d)
                qb = qr_sc[cur, :].astype(BF16)
                if j == 0:
                    keys = cur
                    mask = mask_cur
                else:
                    keys = _rows(start - blk * d, 2 * blk, d)
                    mask = mask_prev_cur
                kw = kr_sc[keys, :].astype(BF16)
                vb = v_ref[keys, :].astype(BF16)
                vw = jnp.concatenate([vb, jnp.ones_like(vb)], axis=1)
                s = lax.dot_general(qb, kw, _CONTRACT_LAST, preferred_element_type=F32)
                s = jnp.where(mask, s, -jnp.inf)
                mx = jnp.max(s, axis=-1, keepdims=True)
                p = jnp.exp2((s - mx) * (scale * LOG2E))
                pv = jnp.dot(p.astype(BF16), vw, preferred_element_type=F32)
                den = pv[:, HEAD_DIM_A:]
                o_sc[cur, :] = pv[:, :HEAD_DIM_A] / den
                l_sc[cur, :] = mx * scale + jnp.log(den)
    for c0 in range(0, SEQ, ROPE_ROWS):
        rs = slice(c0, c0 + ROPE_ROWS)
        la = l0[rs, :]
        lb = l1[rs, :]
        lc = l2[rs, :]
        mx = jnp.maximum(jnp.maximum(la, lb), lc)
        ea = jnp.exp(la - mx)
        eb = jnp.exp(lb - mx)
        ec = jnp.exp(lc - mx)
        den = ea + eb + ec
        y_ref[rs, :] = ((ea / den) * o0[rs, :] + (eb / den) * o1[rs, :] + (ec / den) * o2[rs, :]).astype(BF16)


def _attention(p, cos_t, sin_t):
    for window, d in DIL_CONFIGS:
        assert window // d == ATT_BLOCK and SEQ % (d * ATT_BLOCK) == 0
    col0 = P_ATT_OFF // HEAD_DIM_A

    def slab(part, g):
        off = col0 + (part * A_QKV_W + g * A_GROUP_W) // HEAD_DIM_A
        return pl.BlockSpec((SEQ, HEAD_DIM_A), lambda b, h: (b, off + h))

    in_specs = [slab(part, g) for g in range(N_DIL_GROUPS) for part in range(3)]
    table = pl.BlockSpec((SEQ, HEAD_DIM_A), lambda b, h: (0, 0))
    return pl.pallas_call(
        _attn_kernel,
        out_shape=jax.ShapeDtypeStruct((TOKENS, A_GROUP_W), BF16),
        grid=(BATCH, HEADS_PER_GROUP),
        in_specs=in_specs + [table, table],
        out_specs=pl.BlockSpec((SEQ, HEAD_DIM_A), lambda b, h: (b, h)),
        scratch_shapes=[pltpu.VMEM((SEQ, HEAD_DIM_A), F32)] * 8,
        compiler_params=_cparams(("arbitrary", "arbitrary")),
        name="dil_attn",
    )(*([p] * 9), cos_t, sin_t)


CONV_HALO = SUBLANES


def _mlstm_kernel(q_ref, k_ref, v_ref, og_ref, gc_ref, gr_ref, bc_ref, br_ref, cw_ref, cb_ref, ng_ref,
                  y_ref, xq_sc, xk_sc, ct_sc, n_sc, m_sc):
    c = pl.program_id(1)
    L = M_CHUNK
    lo = CONV_HALO

    @pl.when(c == 0)
    def _():
        xq_sc[0:lo, :] = jnp.zeros((lo, M_W), F32)
        xk_sc[0:lo, :] = jnp.zeros((lo, M_W), F32)
        ct_sc[...] = jnp.zeros_like(ct_sc)
        n_sc[...] = jnp.zeros_like(n_sc)
        m_sc[...] = jnp.zeros_like(m_sc)

    xq_sc[lo:lo + L, :] = q_ref[...]
    xk_sc[lo:lo + L, :] = k_ref[...]

    def conv(x_sc, w, b):
        acc = x_sc[lo:lo + L, :] * w[CONV_K - 1:CONV_K, :] + b
        for j in range(CONV_K - 1):
            off = lo - (CONV_K - 1) + j
            acc = acc + x_sc[off:off + L, :] * w[j:j + 1, :]
        return acc

    cw = cw_ref[...]
    cb = cb_ref[...]
    qc = _silu(conv(xq_sc, cw[:, :M_W], cb[:, :M_W]))
    kc = _silu(conv(xk_sc, cw[:, M_W:], cb[:, M_W:])) * (M_HEAD_DIM ** -0.5)
    xq_sc[0:lo, :] = xq_sc[L:L + lo, :]
    xk_sc[0:lo, :] = xk_sc[L:L + lo, :]

    row = lax.broadcasted_iota(jnp.int32, (L, L), 0)
    col = lax.broadcasted_iota(jnp.int32, (L, L), 1)
    causal = row >= col
    state = [(ct_sc[h], n_sc[h, 0:1, :], m_sc[h, 0:1, 0:1]) for h in range(M_HEADS)]
    results = []
    for h in range(M_HEADS):
        hs = slice(h * M_HEAD_DIM, (h + 1) * M_HEAD_DIM)
        ct, n_row, m_prev = state[h]
        q = qc[:, hs]
        k = kc[:, hs]
        i_col = gc_ref[:, h:h + 1] + bc_ref[:, h:h + 1]
        lf_col = _log_sigmoid(gc_ref[:, M_HEADS + h:M_HEADS + h + 1] + bc_ref[:, M_HEADS + h:M_HEADS + h + 1])
        i_row = gr_ref[h:h + 1, :] + br_ref[h:h + 1, :]
        lf_row = _log_sigmoid(gr_ref[M_HEADS + h:M_HEADS + h + 1, :] + br_ref[M_HEADS + h:M_HEADS + h + 1, :])
        b_col = jnp.sum(jnp.where(causal, lf_row, 0.0), axis=1, keepdims=True)
        b_row = jnp.sum(jnp.where(row <= col, lf_col, 0.0), axis=0, keepdims=True)
        log_d = jnp.where(causal, b_col - b_row + i_row, -jnp.inf)
        log_inter = b_col + m_prev
        m_t = jnp.maximum(jnp.max(log_d, axis=1, keepdims=True), log_inter)
        qb = q.astype(BF16)
        kb = k.astype(BF16)
        vb = v_ref[:, hs].astype(BF16)
        s = lax.dot_general(qb, kb, (((1,), (1,)), ((), ())), preferred_element_type=F32) * jnp.exp(log_d - m_t)
        inter = jnp.exp(log_inter - m_t)
        num = (jnp.dot(s.astype(BF16), vb, preferred_element_type=F32)
               + inter * jnp.dot(qb, ct.astype(BF16), preferred_element_type=F32))
        den = jnp.sum(s, axis=1, keepdims=True) + inter * jnp.sum(q * n_row, axis=1, keepdims=True)
        hh = num / jnp.maximum(jnp.abs(den), jnp.exp(-m_t))
        m_new = m_t[L - 1:L, :]
        b_last = b_col[L - 1:L, :]
        w_col = jnp.exp(b_last - b_col + i_col - m_new)
        decay = jnp.exp(b_last + m_prev - m_new)
        kw = k * w_col
        ct_new = decay * ct + lax.dot_general(kw.astype(BF16), vb, (((0,), (0,)), ((), ())),
                                              preferred_element_type=F32)
        n_new = decay * n_row + jnp.sum(kw, axis=0, keepdims=True)
        z = jax.nn.sigmoid(og_ref[:, hs]) * hh
        results.append((ct_new, n_new, m_new, (_normalize(z) * ng_ref[:, hs]).astype(BF16)))
    for h, (ct_new, n_new, m_new, y) in enumerate(results):
        ct_sc[h] = ct_new
        n_sc[h] = jnp.broadcast_to(n_new, (SUBLANES, M_HEAD_DIM))
        m_sc[h] = jnp.broadcast_to(m_new, (SUBLANES, LANES))
        y_ref[:, h * M_HEAD_DIM:(h + 1) * M_HEAD_DIM] = y


def _mlstm(p, gcol, grow, bcol, brow, conv_w, conv_b, norm_g):
    nc = SEQ // M_CHUNK
    slab = lambda blk: pl.BlockSpec((M_CHUNK, M_W), lambda b, c: (b * nc + c, blk))
    return pl.pallas_call(
        _mlstm_kernel,
        out_shape=jax.ShapeDtypeStruct((TOKENS, M_W), BF16),
        grid=(BATCH, nc),
        in_specs=[
            slab(P_QM_BLK), slab(P_KM_BLK), slab(P_VM_BLK), slab(P_OM_BLK),
            pl.BlockSpec((None, M_CHUNK, 2 * M_HEADS), lambda b, c: (b, c, 0)),
            pl.BlockSpec((None, 2 * M_HEADS, M_CHUNK), lambda b, c: (b, 0, c)),
            pl.BlockSpec((1, 2 * M_HEADS), lambda b, c: (0, 0)),
            pl.BlockSpec((2 * M_HEADS, 1), lambda b, c: (0, 0)),
            pl.BlockSpec((CONV_K, 2 * M_W), lambda b, c: (0, 0)),
            pl.BlockSpec((1, 2 * M_W), lambda b, c: (0, 0)),
            pl.BlockSpec((1, M_W), lambda b, c: (0, 0)),
        ],
        out_specs=pl.BlockSpec((M_CHUNK, M_W), lambda b, c: (b * nc + c, 0)),
        scratch_shapes=[
            pltpu.VMEM((M_CHUNK + CONV_HALO, M_W), F32),
            pltpu.VMEM((M_CHUNK + CONV_HALO, M_W), F32),
            pltpu.VMEM((M_HEADS, M_HEAD_DIM, M_HEAD_DIM), F32),
            pltpu.VMEM((M_HEADS, SUBLANES, M_HEAD_DIM), F32),
            pltpu.VMEM((M_HEADS, SUBLANES, LANES), F32),
        ],
        compiler_params=_cparams(("arbitrary", "arbitrary")),
        name="mlstm",
    )(p, p, p, p, gcol, grow, bcol, brow, conv_w, conv_b, norm_g)


RT_E1, RT_E2, RT_W1, RT_W2 = 0, 1, 2, 3


def _route(logits):
    lane = lax.broadcasted_iota(jnp.int32, logits.shape, 1).astype(F32)
    big = float(LANES)
    is_g = lane < N_EXPERT_GROUPS
    gl = jnp.where(is_g, logits, -jnp.inf)
    gexp = jnp.exp(gl - jnp.max(gl, axis=1, keepdims=True))
    gprob = gexp / jnp.sum(gexp, axis=1, keepdims=True)
    g_w = jnp.max(gprob, axis=1, keepdims=True)
    g_top = jnp.min(jnp.where(is_g & (gprob == g_w), lane, big), axis=1, keepdims=True)
    lo = N_EXPERT_GROUPS + EXPERTS_PER_GROUP * g_top
    in_grp = (lane >= lo) & (lane < lo + EXPERTS_PER_GROUP)
    el = jnp.where(in_grp, logits, -jnp.inf)
    eexp = jnp.exp(el - jnp.max(el, axis=1, keepdims=True))
    eprob = eexp / jnp.sum(eexp, axis=1, keepdims=True)
    v1 = jnp.max(eprob, axis=1, keepdims=True)
    i1 = jnp.min(jnp.where(in_grp & (eprob == v1), lane, big), axis=1, keepdims=True)
    rest = jnp.where(in_grp & (lane != i1), eprob, -1.0)
    v2 = jnp.max(rest, axis=1, keepdims=True)
    i2 = jnp.min(jnp.where(rest == v2, lane, big), axis=1, keepdims=True)
    tot = v1 + v2
    w1 = g_w * (v1 / tot)
    w2 = g_w * (v2 / tot)
    e1 = i1 - N_EXPERT_GROUPS
    e2 = i2 - N_EXPERT_GROUPS
    rec = jnp.where(lane == RT_E1, e1, jnp.where(lane == RT_E2, e2, jnp.where(lane == RT_W1, w1, w2)))
    return jnp.where(lane <= RT_W2, rec, 0.0)


def _merge_kernel(ya_ref, ym_ref, g_ref, x_ref, mod_ref,
                  wpa_ref, wpm_ref, wout_ref, lng_ref, lnb_ref, wr_ref, br_ref,
                  x1_ref, u2_ref, rt_ref):
    for sb in range(MERGE_TM // MERGE_SUB):
        rows = slice(sb * MERGE_SUB, (sb + 1) * MERGE_SUB)
        pa = jnp.dot(ya_ref[rows, :], wpa_ref[...], preferred_element_type=F32)
        pm = jnp.dot(ym_ref[rows, :], wpm_ref[...], preferred_element_type=F32)
        merged = g_ref[rows, :D_MODEL].astype(F32) * pa + g_ref[rows, D_MODEL:].astype(F32) * pm
        mix = jnp.dot(merged.astype(BF16), wout_ref[...], preferred_element_type=F32)
        z = DEEPNORM_ALPHA * x_ref[rows, :] + mod_ref[2:3, :] * mix
        x1 = _normalize(z) * lng_ref[...] + lnb_ref[...]
        x1_ref[rows, :] = x1
        u2 = _normalize(x1) * (1.0 + mod_ref[4:5, :]) + mod_ref[3:4, :]
        u2_ref[rows, :] = u2
        logits = jnp.dot(u2.astype(BF16), wr_ref[...], preferred_element_type=F32) + br_ref[...]
        rt_ref[rows, :] = _route(logits)


def _merge(ya, ym, g, x2, mod, wpa, wpm, wout, lng, lnb, wr, br):
    tm = MERGE_TM
    tiles_per_batch = SEQ // tm
    rowblk = lambda w: pl.BlockSpec((tm, w), lambda m: (m, 0))
    const = lambda shape: pl.BlockSpec(shape, lambda m: (0,) * len(shape), pipeline_mode=pl.Buffered(1))
    return pl.pallas_call(
        _merge_kernel,
        out_shape=(
            jax.ShapeDtypeStruct((TOKENS, D_MODEL), F32),
            jax.ShapeDtypeStruct((TOKENS, D_MODEL), F32),
            jax.ShapeDtypeStruct((TOKENS, LANES), F32),
        ),
        grid=(TOKENS // tm,),
        in_specs=[
            rowblk(A_GROUP_W), rowblk(M_W), rowblk(2 * D_MODEL), rowblk(D_MODEL),
            pl.BlockSpec((None, 6, D_MODEL), lambda m: (m // tiles_per_batch, 0, 0)),
            const((A_GROUP_W, D_MODEL)), const((M_W, D_MODEL)), const((D_MODEL, D_MODEL)),
            const((1, D_MODEL)), const((1, D_MODEL)),
            const((D_MODEL, LANES)), const((1, LANES)),
        ],
        out_specs=(rowblk(D_MODEL), rowblk(D_MODEL), rowblk(LANES)),
        compiler_params=_cparams(("arbitrary",)),
        name="merge_ln1_route",
    )(ya, ym, g, x2, mod, wpa, wpm, wout, lng, lnb, wr, br)


CAST_ROWS = 128


def _issue_rows(src_ref, idx_ref, base, buf, slot, sem, nrows):
    def body(blk, carry):
        for j in range(SUBLANES):
            row = idx_ref[base + blk * SUBLANES + j]
            src = src_ref.at[lax.shift_right_logical(row, 3), pl.ds(row & (SUBLANES - 1), 1)]
            pltpu.make_async_copy(src, buf.at[slot, blk, pl.ds(j, 1)], sem.at[slot]).start(priority=j % 2)
        return carry

    lax.fori_loop(0, nrows // SUBLANES, body, 0)


def _wait_rows(src_ref, buf, slot, sem, nrows):
    groups = nrows // SUBLANES
    pltpu.make_async_copy(src_ref.at[pl.ds(0, groups)], buf.at[slot, pl.ds(0, groups)], sem.at[slot]).wait()


def _expert_changed(te_ref, i):
    return (i == 0) | (te_ref[i] != te_ref[jnp.maximum(i - 1, 0)])


def _moe_kernel(te_ref, tn_ref, tv_ref, nu_ref, tok_ref, u_ref, wg_ref, wu_ref, wd_ref, o_ref,
                stg, stu, std, wgb, wub, wdb, xbuf, wsem, xsem):
    i = pl.program_id(0)
    nu = nu_ref[0]
    tm = o_ref.shape[0]
    half = tm // 2

    def weight_copies(e):
        return (pltpu.make_async_copy(wg_ref.at[e], stg, wsem.at[0]),
                pltpu.make_async_copy(wu_ref.at[e], stu, wsem.at[1]),
                pltpu.make_async_copy(wd_ref.at[e], std, wsem.at[2]))

    def for_tile_rows(t, fn):
        @pl.when(tv_ref[t] <= half)
        def _():
            fn(half)

        @pl.when(tv_ref[t] > half)
        def _():
            fn(tm)

    def issue_tile(t):
        for_tile_rows(t, lambda nrows: _issue_rows(u_ref, tok_ref, t * tm, xbuf, t % MOE_ROW_SLOTS, xsem, nrows))

    @pl.when(i == 0)
    def _():
        issue_tile(0)

        @pl.when(nu > 1)
        def _():
            issue_tile(1)

        for cp in weight_copies(te_ref[0]):
            cp.start(priority=1)

    @pl.when(i < nu)
    def _():
        slot = i % MOE_ROW_SLOTS

        @pl.when(i + 2 < nu)
        def _():
            issue_tile(i + 2)

        @pl.when(_expert_changed(te_ref, i))
        def _():
            for cp in weight_copies(te_ref[i]):
                cp.wait()
            for src, dst in ((stg, wgb), (stu, wub), (std, wdb)):
                def cast_rows(ci, carry, src=src, dst=dst):
                    r = pl.multiple_of(ci * CAST_ROWS, CAST_ROWS)
                    dst[pl.ds(r, CAST_ROWS), :] = src[pl.ds(r, CAST_ROWS), :].astype(BF16)
                    return carry

                lax.fori_loop(0, src.shape[0] // CAST_ROWS, cast_rows, 0)

            @pl.when(tn_ref[i] >= 0)
            def _():
                for cp in weight_copies(tn_ref[i]):
                    cp.start(priority=1)

        def compute(nrows):
            _wait_rows(u_ref, xbuf, slot, xsem, nrows)
            x = xbuf[slot, 0:nrows // SUBLANES].reshape(nrows, D_MODEL).astype(BF16)
            a = jnp.dot(x, wgb[...], preferred_element_type=F32)
            b = jnp.dot(x, wub[...], preferred_element_type=F32)
            h = (_silu(a) * b).astype(BF16)
            o_ref[0:nrows, :] = jnp.dot(h, wdb[...], preferred_element_type=F32)
            if nrows < tm:
                o_ref[nrows:tm, :] = jnp.zeros((tm - nrows, D_MODEL), F32)

        for_tile_rows(i, compute)

    @pl.when(i >= nu)
    def _():
        o_ref[...] = jnp.zeros_like(o_ref)


def _moe(te, te_next, tile_rows, n_used, row_token, u2, w_eg, w_eu, w_ed):
    tm = MOE_TM
    return pl.pallas_call(
        _moe_kernel,
        out_shape=jax.ShapeDtypeStruct((MOE_ROWS, D_MODEL), F32),
        grid_spec=pltpu.PrefetchScalarGridSpec(
            num_scalar_prefetch=5,
            grid=(MOE_TILES,),
            in_specs=[pl.BlockSpec(memory_space=pl.ANY)] * 4,
            out_specs=pl.BlockSpec((tm, D_MODEL), lambda i, te, tn, tv, nu, tok: (i, 0)),
            scratch_shapes=[
                pltpu.VMEM((D_MODEL, D_FF_EXPERT), F32),
                pltpu.VMEM((D_MODEL, D_FF_EXPERT), F32),
                pltpu.VMEM((D_FF_EXPERT, D_MODEL), F32),
                pltpu.VMEM((D_MODEL, D_FF_EXPERT), BF16),
                pltpu.VMEM((D_MODEL, D_FF_EXPERT), BF16),
                pltpu.VMEM((D_FF_EXPERT, D_MODEL), BF16),
                pltpu.VMEM((MOE_ROW_SLOTS, tm // SUBLANES, SUBLANES, D_MODEL), F32),
                pltpu.SemaphoreType.DMA((3,)),
                pltpu.SemaphoreType.DMA((MOE_ROW_SLOTS,)),
            ],
        ),
        compiler_params=_cparams(("arbitrary",)),
        name="moe_experts",
    )(te, te_next, tile_rows, n_used, row_token, u2.reshape(TOKENS // SUBLANES, SUBLANES, D_MODEL),
      w_eg, w_eu, w_ed)


def _final_kernel(pos_ref, o_ref, x1_ref, rt_ref, mod_ref, lng_ref, lnb_ref, y_ref, buf, sem):
    i = pl.program_id(0)
    tb = x1_ref.shape[0]
    slot = i % 2

    @pl.when(i == 0)
    def _():
        _issue_rows(o_ref, pos_ref, 0, buf, 0, sem, 2 * tb)

    @pl.when(i + 1 < pl.num_programs(0))
    def _():
        _issue_rows(o_ref, pos_ref, (i + 1) * (2 * tb), buf, 1 - slot, sem, 2 * tb)

    _wait_rows(o_ref, buf, slot, sem, 2 * tb)
    rows = buf[slot].reshape(2 * tb, D_MODEL)
    ffn = rt_ref[:, RT_W1:RT_W1 + 1] * rows[0:tb, :] + rt_ref[:, RT_W2:RT_W2 + 1] * rows[tb:2 * tb, :]
    z = DEEPNORM_ALPHA * x1_ref[...] + mod_ref[5:6, :] * ffn
    y_ref[...] = _normalize(z) * lng_ref[...] + lnb_ref[...]


def _final(pos_tiles, moe_out, x1, rt, mod, lng, lnb):
    tb = FINAL_TB
    tiles_per_batch = SEQ // tb
    return pl.pallas_call(
        _final_kernel,
        out_shape=jax.ShapeDtypeStruct((TOKENS, D_MODEL), F32),
        grid_spec=pltpu.PrefetchScalarGridSpec(
            num_scalar_prefetch=1,
            grid=(TOKENS // tb,),
            in_specs=[
                pl.BlockSpec(memory_space=pl.ANY),
                pl.BlockSpec((tb, D_MODEL), lambda i, pos: (i, 0)),
                pl.BlockSpec((tb, LANES), lambda i, pos: (i, 0)),
                pl.BlockSpec((None, 6, D_MODEL), lambda i, pos: (i // tiles_per_batch, 0, 0)),
                pl.BlockSpec((1, D_MODEL), lambda i, pos: (0, 0)),
                pl.BlockSpec((1, D_MODEL), lambda i, pos: (0, 0)),
            ],
            out_specs=pl.BlockSpec((tb, D_MODEL), lambda i, pos: (i, 0)),
            scratch_shapes=[pltpu.VMEM((2, 2 * tb // SUBLANES, SUBLANES, D_MODEL), F32),
                            pltpu.SemaphoreType.DMA((2,))],
        ),
        compiler_params=_cparams(("arbitrary",)),
        name="combine_ln2",
    )(pos_tiles, moe_out.reshape(MOE_ROWS // SUBLANES, SUBLANES, D_MODEL), x1, rt, mod, lng, lnb)


def _routing_tables(e_ids):
    tm = MOE_TM
    e_flat = e_ids.reshape(-1)
    onehot = (e_flat[:, None] == jnp.arange(N_EXPERTS, dtype=jnp.int32)[None, :]).astype(jnp.int32)
    csum = jnp.cumsum(onehot, axis=0)
    rank = jnp.sum((csum - onehot) * onehot, axis=1)
    counts = csum[-1]
    padded = ((counts + tm - 1) // tm) * tm
    pend = jnp.cumsum(padded)
    pstart = pend - padded
    dest = pstart[e_flat] + rank
    tok = jnp.arange(2 * TOKENS, dtype=jnp.int32) // 2
    row_token = jnp.zeros((MOE_ROWS,), jnp.int32).at[dest].set(tok)
    n_used = (pend[-1] // tm).astype(jnp.int32)
    tile_start = jnp.arange(MOE_TILES, dtype=jnp.int32) * tm
    te = jnp.sum((pend[None, :] <= tile_start[:, None]).astype(jnp.int32), axis=1)
    te = jnp.minimum(te, N_EXPERTS - 1)
    te_last = te[jnp.maximum(n_used - 1, 0)]
    te = jnp.where(jnp.arange(MOE_TILES) < n_used, te, te_last)
    next_run = pend[te] // tm
    te_next = jnp.where(next_run < n_used, te[jnp.minimum(next_run, MOE_TILES - 1)], -1).astype(jnp.int32)
    tile_rows = jnp.clip(pstart[te] + counts[te] - tile_start, 0, tm).astype(jnp.int32)
    pos = dest.reshape(TOKENS, 2)
    return row_token, te, te_next, tile_rows, n_used.reshape(1), pos


def _rope_tables():
    inv = ROPE_THETA ** (-jnp.arange(0, HEAD_DIM_A, 2, dtype=F32) / HEAD_DIM_A)
    ang = jnp.arange(SEQ, dtype=F32)[:, None] * inv[None, :]
    cos = jnp.cos(ang)
    sin = jnp.sin(ang)
    return jnp.concatenate([cos, cos], axis=-1), jnp.concatenate([-sin, sin], axis=-1)


def kernel(x, c, w_ada, b_ada, w_in, b_mgate, conv_w, conv_b, m_norm_g, w_proj_a, w_proj_m, w_gate, b_gate,
           w_out, ln1_g, ln1_b, w_rg, b_rg, w_re, b_re, w_eg, w_eu, w_ed, ln2_g, ln2_b):
    assert x.shape == (BATCH, SEQ, D_MODEL) and w_ada.shape[0] == 1
    l = 0
    x2 = x.reshape(TOKENS, D_MODEL)

    c_pad = jnp.zeros((ADA_ROWS, D_MODEL), F32).at[:BATCH].set(c)
    mod = _ada(c_pad, w_ada[l], b_ada[l][None, :])[:BATCH].reshape(BATCH, 6, D_MODEL)

    w_in_t = jnp.swapaxes(w_in[l], 0, 1).astype(BF16)
    w_if_t = jnp.zeros((LANES, D_MODEL), BF16).at[:2 * M_HEADS].set(w_in_t[N_IN_MAIN:])
    p, gates, u = _inproj(x2, mod, w_in_t, w_if_t)
    g = _gateproj(u, w_gate[l], b_gate[l][None, :])

    cos_t, sin_t = _rope_tables()
    ya = _attention(p, cos_t, sin_t)

    gcol = gates[:, :2 * M_HEADS].reshape(BATCH, SEQ, 2 * M_HEADS)
    grow = jnp.transpose(gcol, (0, 2, 1))
    ym = _mlstm(p, gcol, grow, b_mgate[l][None, :], b_mgate[l][:, None], conv_w[l], conv_b[l][None, :],
                m_norm_g[l][None, :])

    wr = (jnp.zeros((D_MODEL, LANES), F32)
          .at[:, :N_EXPERT_GROUPS].set(w_rg[l])
          .at[:, N_EXPERT_GROUPS:N_EXPERT_GROUPS + N_EXPERTS].set(w_re[l])).astype(BF16)
    br = (jnp.zeros((1, LANES), F32)
          .at[0, :N_EXPERT_GROUPS].set(b_rg[l])
          .at[0, N_EXPERT_GROUPS:N_EXPERT_GROUPS + N_EXPERTS].set(b_re[l]))
    x1, u2, rt = _merge(ya, ym, g, x2, mod,
                        w_proj_a[l].astype(BF16), w_proj_m[l].astype(BF16), w_out[l].astype(BF16),
                        ln1_g[l][None, :], ln1_b[l][None, :], wr, br)

    e_ids = rt[:, RT_E1:RT_E2 + 1].astype(jnp.int32)
    row_token, te, te_next, tile_rows, n_used, pos = _routing_tables(e_ids)

    mo = _moe(te, te_next, tile_rows, n_used, row_token, u2,
              w_eg[l].reshape(N_EXPERTS, D_MODEL, D_FF_EXPERT), w_eu[l].reshape(N_EXPERTS, D_MODEL, D_FF_EXPERT),
              w_ed[l].reshape(N_EXPERTS, D_FF_EXPERT, D_MODEL))

    nt = TOKENS // FINAL_TB
    pos_tiles = jnp.transpose(pos.reshape(nt, FINAL_TB, 2), (0, 2, 1)).reshape(-1)
    y = _final(pos_tiles, mo, x1, rt, mod, ln2_g[l][None, :], ln2_b[l][None, :])
    return y.reshape(BATCH, SEQ, D_MODEL)
```

```python
import functools

import jax
import jax.numpy as jnp
from jax import lax
from jax.experimental import pallas as pl
from jax.experimental.pallas import tpu as pltpu

F32 = jnp.float32
BF16 = jnp.bfloat16

D_MODEL = 2048
BATCH = 4
SEQ = 2048
TOKENS = BATCH * SEQ
DIL_CONFIGS = ((128, 1), (512, 4), (2048, 16))
N_DIL_GROUPS = 3
HEADS_PER_GROUP = 4
HEAD_DIM_A = 128
ATT_BLOCK = 128
ROPE_THETA = 10000.0
A_GROUP_W = HEADS_PER_GROUP * HEAD_DIM_A
A_QKV_W = N_DIL_GROUPS * A_GROUP_W
M_HEADS = 4
M_HEAD_DIM = 256
M_W = M_HEADS * M_HEAD_DIM
M_CHUNK = 128
CONV_K = 4
N_IN_MAIN = 3 * A_QKV_W + 4 * M_W
N_EXPERT_GROUPS = 4
EXPERTS_PER_GROUP = 8
N_EXPERTS = N_EXPERT_GROUPS * EXPERTS_PER_GROUP
D_FF_EXPERT = 1024
DEEPNORM_ALPHA = 2.0 ** 0.25
LN_EPS = 1e-5

LANES = 128
SUBLANES = 8
VMEM_LIMIT_BYTES = 56 * 1024 * 1024

PROJ_TN = 256
GATE_TN = 1024
PROJ_TM = 1024
IN_TM = 2048
P_TILES_IN = N_IN_MAIN // PROJ_TN
P_ATT_TILES = 3 * A_QKV_W // PROJ_TN
P_M_TILES = P_TILES_IN - P_ATT_TILES
P_WIDTH = N_IN_MAIN
P_ATT_OFF = P_M_TILES * PROJ_TN
P_QM_BLK, P_KM_BLK, P_VM_BLK, P_OM_BLK = 0, 1, 2, 3
MERGE_TM = 256
MERGE_SUB = 256
MOE_TM = 256
MOE_ROWS = 2 * TOKENS + N_EXPERTS * MOE_TM
MOE_TILES = MOE_ROWS // MOE_TM
MOE_ROW_SLOTS = 3
FINAL_TB = 256


def _cparams(sem, vmem=VMEM_LIMIT_BYTES):
    return pltpu.CompilerParams(dimension_semantics=sem, vmem_limit_bytes=vmem)


def _normalize(x):
    mu = jnp.mean(x, axis=-1, keepdims=True)
    xc = x - mu
    var = jnp.mean(xc * xc, axis=-1, keepdims=True)
    return xc * lax.rsqrt(var + LN_EPS)


def _silu(x):
    return x * jax.nn.sigmoid(x)


def _log_sigmoid(x):
    return jnp.minimum(x, 0.0) - jnp.log(1.0 + jnp.exp(-jnp.abs(x)))


_CONTRACT_LAST = (((1,), (1,)), ((), ()))
LOG2E = 1.4426950408889634


ADA_TN = 1024
ADA_ROWS = 16


def _ada_kernel(c_ref, w_ref, b_ref, o_ref):
    sc = _silu(c_ref[...]).astype(BF16)
    o_ref[...] = jnp.dot(sc, w_ref[...].astype(BF16), preferred_element_type=F32) + b_ref[...]


def _ada(c_pad, w_ada, b_ada):
    n = w_ada.shape[1]
    return pl.pallas_call(
        _ada_kernel,
        out_shape=jax.ShapeDtypeStruct((ADA_ROWS, n), F32),
        grid=(n // ADA_TN,),
        in_specs=[
            pl.BlockSpec((ADA_ROWS, D_MODEL), lambda j: (0, 0)),
            pl.BlockSpec((D_MODEL, ADA_TN), lambda j: (0, j)),
            pl.BlockSpec((1, ADA_TN), lambda j: (0, j)),
        ],
        out_specs=pl.BlockSpec((ADA_ROWS, ADA_TN), lambda j: (0, j)),
        compiler_params=_cparams(("arbitrary",)),
        name="ada_mod",
    )(c_pad, w_ada, b_ada)


LN_CHUNK = 256


def _inproj_kernel(x_ref, mod_ref, w_ref, wif_ref, p_ref, g_ref, u_ref):
    n = pl.program_id(1)

    @pl.when(n == 0)
    def _():
        shift = mod_ref[0:1, :]
        scale = 1.0 + mod_ref[1:2, :]
        wif = wif_ref[...].astype(BF16)
        w = w_ref[...].astype(BF16)
        for ci in range(IN_TM // LN_CHUNK):
            rows = slice(ci * LN_CHUNK, (ci + 1) * LN_CHUNK)
            u = (_normalize(x_ref[rows, :]) * scale + shift).astype(BF16)
            u_ref[rows, :] = u
            g_ref[rows, :] = lax.dot_general(u, wif, _CONTRACT_LAST, preferred_element_type=F32)
            p_ref[rows, :] = lax.dot_general(u, w, _CONTRACT_LAST, preferred_element_type=F32)

    @pl.when(n > 0)
    def _():
        p_ref[...] = lax.dot_general(u_ref[...], w_ref[...].astype(BF16), _CONTRACT_LAST,
                                     preferred_element_type=F32)


def _inproj(x2, mod, w_in_t, w_if_t):
    tiles_per_batch = SEQ // IN_TM
    return pl.pallas_call(
        _inproj_kernel,
        out_shape=(
            jax.ShapeDtypeStruct((TOKENS, P_WIDTH), F32),
            jax.ShapeDtypeStruct((TOKENS, LANES), F32),
            jax.ShapeDtypeStruct((TOKENS, D_MODEL), BF16),
        ),
        grid=(TOKENS // IN_TM, P_TILES_IN),
        in_specs=[
            pl.BlockSpec((IN_TM, D_MODEL), lambda m, n: (m, 0), pipeline_mode=pl.Buffered(1)),
            pl.BlockSpec((None, 6, D_MODEL), lambda m, n: (m // tiles_per_batch, 0, 0)),
            pl.BlockSpec((PROJ_TN, D_MODEL), lambda m, n: (n, 0)),
            pl.BlockSpec((LANES, D_MODEL), lambda m, n: (0, 0)),
        ],
        out_specs=(
            pl.BlockSpec((IN_TM, PROJ_TN),
                         lambda m, n: (m, jnp.where(n < P_ATT_TILES, n + P_M_TILES, n - P_ATT_TILES))),
            pl.BlockSpec((IN_TM, LANES), lambda m, n: (m, 0)),
            pl.BlockSpec((IN_TM, D_MODEL), lambda m, n: (m, 0)),
        ),
        compiler_params=_cparams(("arbitrary", "arbitrary")),
        name="in_proj",
    )(x2, mod, w_in_t, w_if_t)


def _gateproj_kernel(u_ref, w_ref, b_ref, o_ref):
    acc = jnp.dot(u_ref[...], w_ref[...].astype(BF16), preferred_element_type=F32)
    o_ref[...] = jax.nn.sigmoid(acc + b_ref[...]).astype(BF16)


def _gateproj(u, w_gate, b_gate):
    n = w_gate.shape[1]
    return pl.pallas_call(
        _gateproj_kernel,
        out_shape=jax.ShapeDtypeStruct((TOKENS, n), BF16),
        grid=(TOKENS // PROJ_TM, n // GATE_TN),
        in_specs=[
            pl.BlockSpec((PROJ_TM, D_MODEL), lambda m, j: (m, 0)),
            pl.BlockSpec((D_MODEL, GATE_TN), lambda m, j: (0, j)),
            pl.BlockSpec((1, GATE_TN), lambda m, j: (0, j)),
        ],
        out_specs=pl.BlockSpec((PROJ_TM, GATE_TN), lambda m, j: (m, j)),
        compiler_params=_cparams(("arbitrary", "arbitrary")),
        name="gate_proj",
    )(u, w_gate, b_gate)


ROPE_ROWS = 256


def _rows(start, size, stride):
    return pl.ds(start, size) if stride == 1 else pl.ds(start, size, stride=stride)


def _attn_kernel(q0, k0, v0, q1, k1, v1, q2, k2, v2, cos_ref, sin_ref, y_ref,
                 qr_sc, kr_sc, o0, o1, o2, l0, l1, l2):
    scale = HEAD_DIM_A ** -0.5
    blk = ATT_BLOCK
    qi2 = lax.broadcasted_iota(jnp.int32, (blk, 2 * blk), 0)
    kc2 = lax.broadcasted_iota(jnp.int32, (blk, 2 * blk), 1)
    mask_prev_cur = (kc2 >= qi2) & (kc2 <= qi2 + blk)
    qi1 = lax.broadcasted_iota(jnp.int32, (blk, blk), 0)
    kc1 = lax.broadcasted_iota(jnp.int32, (blk, blk), 1)
    mask_cur = kc1 <= qi1
    half = HEAD_DIM_A // 2
    groups = ((q0, k0, v0, o0, l0), (q1, k1, v1, o1, l1), (q2, k2, v2, o2, l2))
    for (window, d), (q_ref, k_ref, v_ref, o_sc, l_sc) in zip(DIL_CONFIGS, groups):
        nb = SEQ // d // blk
        for src_ref, dst_sc in ((q_ref, qr_sc), (k_ref, kr_sc)):
            for c0 in range(0, SEQ, ROPE_ROWS):
                rs = slice(c0, c0 + ROPE_ROWS)
                xr = src_ref[rs, :]
                dst_sc[rs, :] = xr * cos_ref[rs, :] + pltpu.roll(xr, half, 1) * sin_ref[rs, :]
        for r in range(d):
            for j in range(nb):
                start = r + j * blk * d
                cur = _rows(start, blk, d)
                qb = qr_sc[cur, :].astype(BF16)
                if j == 0:
                    keys = cur
                    mask = mask_cur
                else:
                    keys = _rows(start - blk * d, 2 * blk, d)
                    mask = mask_prev_cur
                kw = kr_sc[keys, :].astype(BF16)
                vb = v_ref[keys, :].astype(BF16)
                vw = jnp.concatenate([vb, jnp.ones_like(vb)], axis=1)
                s = lax.dot_general(qb, kw, _CONTRACT_LAST, preferred_element_type=F32)
                s = jnp.where(mask, s, -jnp.inf)
                mx = jnp.max(s, axis=-1, keepdims=True)
                p = jnp.exp2((s - mx) * (scale * LOG2E))
                pv = jnp.dot(p.astype(BF16), vw, preferred_element_type=F32)
                den = pv[:, HEAD_DIM_A:]
                o_sc[cur, :] = pv[:, :HEAD_DIM_A] / den
                l_sc[cur, :] = mx * scale + jnp.log(den)
    for c0 in range(0, SEQ, ROPE_ROWS):
        rs = slice(c0, c0 + ROPE_ROWS)
        la = l0[rs, :]
        lb = l1[rs, :]
        lc = l2[rs, :]
        mx = jnp.maximum(jnp.maximum(la, lb), lc)
        ea = jnp.exp(la - mx)
        eb = jnp.exp(lb - mx)
        ec = jnp.exp(lc - mx)
        den = ea + eb + ec
        y_ref[rs, :] = ((ea / den) * o0[rs, :] + (eb / den) * o1[rs, :] + (ec / den) * o2[rs, :]).astype(BF16)


def _attention(p, cos_t, sin_t):
    for window, d in DIL_CONFIGS:
        assert window // d == ATT_BLOCK and SEQ % (d * ATT_BLOCK) == 0
    col0 = P_ATT_OFF // HEAD_DIM_A

    def slab(part, g):
        off = col0 + (part * A_QKV_W + g * A_GROUP_W) // HEAD_DIM_A
        return pl.BlockSpec((SEQ, HEAD_DIM_A), lambda b, h: (b, off + h))

    in_specs = [slab(part, g) for g in range(N_DIL_GROUPS) for part in range(3)]
    table = pl.BlockSpec((SEQ, HEAD_DIM_A), lambda b, h: (0, 0))
    return pl.pallas_call(
        _attn_kernel,
        out_shape=jax.ShapeDtypeStruct((TOKENS, A_GROUP_W), BF16),
        grid=(BATCH, HEADS_PER_GROUP),
        in_specs=in_specs + [table, table],
        out_specs=pl.BlockSpec((SEQ, HEAD_DIM_A), lambda b, h: (b, h)),
        scratch_shapes=[pltpu.VMEM((SEQ, HEAD_DIM_A), F32)] * 8,
        compiler_params=_cparams(("arbitrary", "arbitrary")),
        name="dil_attn",
    )(*([p] * 9), cos_t, sin_t)


CONV_HALO = SUBLANES


def _mlstm_kernel(q_ref, k_ref, v_ref, og_ref, gc_ref, gr_ref, bc_ref, br_ref, cw_ref, cb_ref, ng_ref,
                  y_ref, xq_sc, xk_sc, ct_sc, n_sc, m_sc):
    c = pl.program_id(1)
    L = M_CHUNK
    lo = CONV_HALO

    @pl.when(c == 0)
    def _():
        xq_sc[0:lo, :] = jnp.zeros((lo, M_W), F32)
        xk_sc[0:lo, :] = jnp.zeros((lo, M_W), F32)
        ct_sc[...] = jnp.zeros_like(ct_sc)
        n_sc[...] = jnp.zeros_like(n_sc)
        m_sc[...] = jnp.zeros_like(m_sc)

    xq_sc[lo:lo + L, :] = q_ref[...]
    xk_sc[lo:lo + L, :] = k_ref[...]

    def conv(x_sc, w, b):
        acc = x_sc[lo:lo + L, :] * w[CONV_K - 1:CONV_K, :] + b
        for j in range(CONV_K - 1):
            off = lo - (CONV_K - 1) + j
            acc = acc + x_sc[off:off + L, :] * w[j:j + 1, :]
        return acc

    cw = cw_ref[...]
    cb = cb_ref[...]
    qc = _silu(conv(xq_sc, cw[:, :M_W], cb[:, :M_W]))
    kc = _silu(conv(xk_sc, cw[:, M_W:], cb[:, M_W:])) * (M_HEAD_DIM ** -0.5)
    xq_sc[0:lo, :] = xq_sc[L:L + lo, :]
    xk_sc[0:lo, :] = xk_sc[L:L + lo, :]

    row = lax.broadcasted_iota(jnp.int32, (L, L), 0)
    col = lax.broadcasted_iota(jnp.int32, (L, L), 1)
    causal = row >= col
    state = [(ct_sc[h], n_sc[h, 0:1, :], m_sc[h, 0:1, 0:1]) for h in range(M_HEADS)]
    results = []
    for h in range(M_HEADS):
        hs = slice(h * M_HEAD_DIM, (h + 1) * M_HEAD_DIM)
        ct, n_row, m_prev = state[h]
        q = qc[:, hs]
        k = kc[:, hs]
        i_col = gc_ref[:, h:h + 1] + bc_ref[:, h:h + 1]
        lf_col = _log_sigmoid(gc_ref[:, M_HEADS + h:M_HEADS + h + 1] + bc_ref[:, M_HEADS + h:M_HEADS + h + 1])
        i_row = gr_ref[h:h + 1, :] + br_ref[h:h + 1, :]
        lf_row = _log_sigmoid(gr_ref[M_HEADS + h:M_HEADS + h + 1, :] + br_ref[M_HEADS + h:M_HEADS + h + 1, :])
        b_col = jnp.sum(jnp.where(causal, lf_row, 0.0), axis=1, keepdims=True)
        b_row = jnp.sum(jnp.where(row <= col, lf_col, 0.0), axis=0, keepdims=True)
        log_d = jnp.where(causal, b_col - b_row + i_row, -jnp.inf)
        log_inter = b_col + m_prev
        m_t = jnp.maximum(jnp.max(log_d, axis=1, keepdims=True), log_inter)
        qb = q.astype(BF16)
        kb = k.astype(BF16)
        vb = v_ref[:, hs].astype(BF16)
        s = lax.dot_general(qb, kb, (((1,), (1,)), ((), ())), preferred_element_type=F32) * jnp.exp(log_d - m_t)
        inter = jnp.exp(log_inter - m_t)
        num = (jnp.dot(s.astype(BF16), vb, preferred_element_type=F32)
               + inter * jnp.dot(qb, ct.astype(BF16), preferred_element_type=F32))
        den = jnp.sum(s, axis=1, keepdims=True) + inter * jnp.sum(q * n_row, axis=1, keepdims=True)
        hh = num / jnp.maximum(jnp.abs(den), jnp.exp(-m_t))
        m_new = m_t[L - 1:L, :]
        b_last = b_col[L - 1:L, :]
        w_col = jnp.exp(b_last - b_col + i_col - m_new)
        decay = jnp.exp(b_last + m_prev - m_new)
        kw = k * w_col
        ct_new = decay * ct + lax.dot_general(kw.astype(BF16), vb, (((0,), (0,)), ((), ())),
                                              preferred_element_type=F32)
        n_new = decay * n_row + jnp.sum(kw, axis=0, keepdims=True)
        z = jax.nn.sigmoid(og_ref[:, hs]) * hh
        results.append((ct_new, n_new, m_new, (_normalize(z) * ng_ref[:, hs]).astype(BF16)))
    for h, (ct_new, n_new, m_new, y) in enumerate(results):
        ct_sc[h] = ct_new
        n_sc[h] = jnp.broadcast_to(n_new, (SUBLANES, M_HEAD_DIM))
        m_sc[h] = jnp.broadcast_to(m_new, (SUBLANES, LANES))
        y_ref[:, h * M_HEAD_DIM:(h + 1) * M_HEAD_DIM] = y


def _mlstm(p, gcol, grow, bcol, brow, conv_w, conv_b, norm_g):
    nc = SEQ // M_CHUNK
    slab = lambda blk: pl.BlockSpec((M_CHUNK, M_W), lambda b, c: (b * nc + c, blk))
    return pl.pallas_call(
        _mlstm_kernel,
        out_shape=jax.ShapeDtypeStruct((TOKENS, M_W), BF16),
        grid=(BATCH, nc),
        in_specs=[
            slab(P_QM_BLK), slab(P_KM_BLK), slab(P_VM_BLK), slab(P_OM_BLK),
            pl.BlockSpec((None, M_CHUNK, 2 * M_HEADS), lambda b, c: (b, c, 0)),
            pl.BlockSpec((None, 2 * M_HEADS, M_CHUNK), lambda b, c: (b, 0, c)),
            pl.BlockSpec((1, 2 * M_HEADS), lambda b, c: (0, 0)),
            pl.BlockSpec((2 * M_HEADS, 1), lambda b, c: (0, 0)),
            pl.BlockSpec((CONV_K, 2 * M_W), lambda b, c: (0, 0)),
            pl.BlockSpec((1, 2 * M_W), lambda b, c: (0, 0)),
            pl.BlockSpec((1, M_W), lambda b, c: (0, 0)),
        ],
        out_specs=pl.BlockSpec((M_CHUNK, M_W), lambda b, c: (b * nc + c, 0)),
        scratch_shapes=[
            pltpu.VMEM((M_CHUNK + CONV_HALO, M_W), F32),
            pltpu.VMEM((M_CHUNK + CONV_HALO, M_W), F32),
            pltpu.VMEM((M_HEADS, M_HEAD_DIM, M_HEAD_DIM), F32),
            pltpu.VMEM((M_HEADS, SUBLANES, M_HEAD_DIM), F32),
            pltpu.VMEM((M_HEADS, SUBLANES, LANES), F32),
        ],
        compiler_params=_cparams(("arbitrary", "arbitrary")),
        name="mlstm",
    )(p, p, p, p, gcol, grow, bcol, brow, conv_w, conv_b, norm_g)


RT_E1, RT_E2, RT_W1, RT_W2 = 0, 1, 2, 3


def _route(logits):
    lane = lax.broadcasted_iota(jnp.int32, logits.shape, 1).astype(F32)
    big = float(LANES)
    is_g = lane < N_EXPERT_GROUPS
    gl = jnp.where(is_g, logits, -jnp.inf)
    gexp = jnp.exp(gl - jnp.max(gl, axis=1, keepdims=True))
    gprob = gexp / jnp.sum(gexp, axis=1, keepdims=True)
    g_w = jnp.max(gprob, axis=1, keepdims=True)
    g_top = jnp.min(jnp.where(is_g & (gprob == g_w), lane, big), axis=1, keepdims=True)
    lo = N_EXPERT_GROUPS + EXPERTS_PER_GROUP * g_top
    in_grp = (lane >= lo) & (lane < lo + EXPERTS_PER_GROUP)
    el = jnp.where(in_grp, logits, -jnp.inf)
    eexp = jnp.exp(el - jnp.max(el, axis=1, keepdims=True))
    eprob = eexp / jnp.sum(eexp, axis=1, keepdims=True)
    v1 = jnp.max(eprob, axis=1, keepdims=True)
    i1 = jnp.min(jnp.where(in_grp & (eprob == v1), lane, big), axis=1, keepdims=True)
    rest = jnp.where(in_grp & (lane != i1), eprob, -1.0)
    v2 = jnp.max(rest, axis=1, keepdims=True)
    i2 = jnp.min(jnp.where(rest == v2, lane, big), axis=1, keepdims=True)
    tot = v1 + v2
    w1 = g_w * (v1 / tot)
    w2 = g_w * (v2 / tot)
    e1 = i1 - N_EXPERT_GROUPS
    e2 = i2 - N_EXPERT_GROUPS
    rec = jnp.where(lane == RT_E1, e1, jnp.where(lane == RT_E2, e2, jnp.where(lane == RT_W1, w1, w2)))
    return jnp.where(lane <= RT_W2, rec, 0.0)


def _merge_kernel(ya_ref, ym_ref, g_ref, x_ref, mod_ref,
                  wpa_ref, wpm_ref, wout_ref, lng_ref, lnb_ref, wr_ref, br_ref,
                  x1_ref, u2_ref, rt_ref):
    for sb in range(MERGE_TM // MERGE_SUB):
        rows = slice(sb * MERGE_SUB, (sb + 1) * MERGE_SUB)
        pa = jnp.dot(ya_ref[rows, :], wpa_ref[...], preferred_element_type=F32)
        pm = jnp.dot(ym_ref[rows, :], wpm_ref[...], preferred_element_type=F32)
        merged = g_ref[rows, :D_MODEL].astype(F32) * pa + g_ref[rows, D_MODEL:].astype(F32) * pm
        mix = jnp.dot(merged.astype(BF16), wout_ref[...], preferred_element_type=F32)
        z = DEEPNORM_ALPHA * x_ref[rows, :] + mod_ref[2:3, :] * mix
        x1 = _normalize(z) * lng_ref[...] + lnb_ref[...]
        x1_ref[rows, :] = x1
        u2 = _normalize(x1) * (1.0 + mod_ref[4:5, :]) + mod_ref[3:4, :]
        u2_ref[rows, :] = u2
        logits = jnp.dot(u2.astype(BF16), wr_ref[...], preferred_element_type=F32) + br_ref[...]
        rt_ref[rows, :] = _route(logits)


def _merge(ya, ym, g, x2, mod, wpa, wpm, wout, lng, lnb, wr, br):
    tm = MERGE_TM
    tiles_per_batch = SEQ // tm
    rowblk = lambda w: pl.BlockSpec((tm, w), lambda m: (m, 0))
    const = lambda shape: pl.BlockSpec(shape, lambda m: (0,) * len(shape), pipeline_mode=pl.Buffered(1))
    return pl.pallas_call(
        _merge_kernel,
        out_shape=(
            jax.ShapeDtypeStruct((TOKENS, D_MODEL), F32),
            jax.ShapeDtypeStruct((TOKENS, D_MODEL), F32),
            jax.ShapeDtypeStruct((TOKENS, LANES), F32),
        ),
        grid=(TOKENS // tm,),
        in_specs=[
            rowblk(A_GROUP_W), rowblk(M_W), rowblk(2 * D_MODEL), rowblk(D_MODEL),
            pl.BlockSpec((None, 6, D_MODEL), lambda m: (m // tiles_per_batch, 0, 0)),
            const((A_GROUP_W, D_MODEL)), const((M_W, D_MODEL)), const((D_MODEL, D_MODEL)),
            const((1, D_MODEL)), const((1, D_MODEL)),
            const((D_MODEL, LANES)), const((1, LANES)),
        ],
        out_specs=(rowblk(D_MODEL), rowblk(D_MODEL), rowblk(LANES)),
        compiler_params=_cparams(("arbitrary",)),
        name="merge_ln1_route",
    )(ya, ym, g, x2, mod, wpa, wpm, wout, lng, lnb, wr, br)


CAST_ROWS = 128


def _issue_rows(src_ref, idx_ref, base, buf, slot, sem, nrows):
    def body(blk, carry):
        for j in range(SUBLANES):
            row = idx_ref[base + blk * SUBLANES + j]
            src = src_ref.at[lax.shift_right_logical(row, 3), pl.ds(row & (SUBLANES - 1), 1)]
            pltpu.make_async_copy(src, buf.at[slot, blk, pl.ds(j, 1)], sem.at[slot]).start(priority=j % 2)
        return carry

    lax.fori_loop(0, nrows // SUBLANES, body, 0)


def _wait_rows(src_ref, buf, slot, sem, nrows):
    groups = nrows // SUBLANES
    pltpu.make_async_copy(src_ref.at[pl.ds(0, groups)], buf.at[slot, pl.ds(0, groups)], sem.at[slot]).wait()


def _expert_changed(te_ref, i):
    return (i == 0) | (te_ref[i] != te_ref[jnp.maximum(i - 1, 0)])


def _moe_kernel(te_ref, tn_ref, tv_ref, nu_ref, tok_ref, u_ref, wg_ref, wu_ref, wd_ref, o_ref,
                stg, stu, std, wgb, wub, wdb, xbuf, wsem, xsem):
    i = pl.program_id(0)
    nu = nu_ref[0]
    tm = o_ref.shape[0]
    half = tm // 2

    def weight_copies(e):
        return (pltpu.make_async_copy(wg_ref.at[e], stg, wsem.at[0]),
                pltpu.make_async_copy(wu_ref.at[e], stu, wsem.at[1]),
                pltpu.make_async_copy(wd_ref.at[e], std, wsem.at[2]))

    def for_tile_rows(t, fn):
        @pl.when(tv_ref[t] <= half)
        def _():
            fn(half)

        @pl.when(tv_ref[t] > half)
        def _():
            fn(tm)

    def issue_tile(t):
        for_tile_rows(t, lambda nrows: _issue_rows(u_ref, tok_ref, t * tm, xbuf, t % MOE_ROW_SLOTS, xsem, nrows))

    @pl.when(i == 0)
    def _():
        issue_tile(0)

        @pl.when(nu > 1)
        def _():
            issue_tile(1)

        for cp in weight_copies(te_ref[0]):
            cp.start(priority=1)

    @pl.when(i < nu)
    def _():
        slot = i % MOE_ROW_SLOTS

        @pl.when(i + 2 < nu)
        def _():
            issue_tile(i + 2)

        @pl.when(_expert_changed(te_ref, i))
        def _():
            for cp in weight_copies(te_ref[i]):
                cp.wait()
            for src, dst in ((stg, wgb), (stu, wub), (std, wdb)):
                def cast_rows(ci, carry, src=src, dst=dst):
                    r = pl.multiple_of(ci * CAST_ROWS, CAST_ROWS)
                    dst[pl.ds(r, CAST_ROWS), :] = src[pl.ds(r, CAST_ROWS), :].astype(BF16)
                    return carry

                lax.fori_loop(0, src.shape[0] // CAST_ROWS, cast_rows, 0)

            @pl.when(tn_ref[i] >= 0)
            def _():
                for cp in weight_copies(tn_ref[i]):
                    cp.start(priority=1)

        def compute(nrows):
            _wait_rows(u_ref, xbuf, slot, xsem, nrows)
            x = xbuf[slot, 0:nrows // SUBLANES].reshape(nrows, D_MODEL).astype(BF16)
            a = jnp.dot(x, wgb[...], preferred_element_type=F32)
            b = jnp.dot(x, wub[...], preferred_element_type=F32)
            h = (_silu(a) * b).astype(BF16)
            o_ref[0:nrows, :] = jnp.dot(h, wdb[...], preferred_element_type=F32)
            if nrows < tm:
                o_ref[nrows:tm, :] = jnp.zeros((tm - nrows, D_MODEL), F32)

        for_tile_rows(i, compute)

    @pl.when(i >= nu)
    def _():
        o_ref[...] = jnp.zeros_like(o_ref)


def _moe(te, te_next, tile_rows, n_used, row_token, u2, w_eg, w_eu, w_ed):
    tm = MOE_TM
    return pl.pallas_call(
        _moe_kernel,
        out_shape=jax.ShapeDtypeStruct((MOE_ROWS, D_MODEL), F32),
        grid_spec=pltpu.PrefetchScalarGridSpec(
            num_scalar_prefetch=5,
            grid=(MOE_TILES,),
            in_specs=[pl.BlockSpec(memory_space=pl.ANY)] * 4,
            out_specs=pl.BlockSpec((tm, D_MODEL), lambda i, te, tn, tv, nu, tok: (i, 0)),
            scratch_shapes=[
                pltpu.VMEM((D_MODEL, D_FF_EXPERT), F32),
                pltpu.VMEM((D_MODEL, D_FF_EXPERT), F32),
                pltpu.VMEM((D_FF_EXPERT, D_MODEL), F32),
                pltpu.VMEM((D_MODEL, D_FF_EXPERT), BF16),
                pltpu.VMEM((D_MODEL, D_FF_EXPERT), BF16),
                pltpu.VMEM((D_FF_EXPERT, D_MODEL), BF16),
                pltpu.VMEM((MOE_ROW_SLOTS, tm // SUBLANES, SUBLANES, D_MODEL), F32),
                pltpu.SemaphoreType.DMA((3,)),
                pltpu.SemaphoreType.DMA((MOE_ROW_SLOTS,)),
            ],
        ),
        compiler_params=_cparams(("arbitrary",)),
        name="moe_experts",
    )(te, te_next, tile_rows, n_used, row_token, u2.reshape(TOKENS // SUBLANES, SUBLANES, D_MODEL),
      w_eg, w_eu, w_ed)


def _final_kernel(pos_ref, o_ref, x1_ref, rt_ref, mod_ref, lng_ref, lnb_ref, y_ref, buf, sem):
    i = pl.program_id(0)
    tb = x1_ref.shape[0]
    slot = i % 2

    @pl.when(i == 0)
    def _():
        _issue_rows(o_ref, pos_ref, 0, buf, 0, sem, 2 * tb)

    @pl.when(i + 1 < pl.num_programs(0))
    def _():
        _issue_rows(o_ref, pos_ref, (i + 1) * (2 * tb), buf, 1 - slot, sem, 2 * tb)

    _wait_rows(o_ref, buf, slot, sem, 2 * tb)
    rows = buf[slot].reshape(2 * tb, D_MODEL)
    ffn = rt_ref[:, RT_W1:RT_W1 + 1] * rows[0:tb, :] + rt_ref[:, RT_W2:RT_W2 + 1] * rows[tb:2 * tb, :]
    z = DEEPNORM_ALPHA * x1_ref[...] + mod_ref[5:6, :] * ffn
    y_ref[...] = _normalize(z) * lng_ref[...] + lnb_ref[...]


def _final(pos_tiles, moe_out, x1, rt, mod, lng, lnb):
    tb = FINAL_TB
    tiles_per_batch = SEQ // tb
    return pl.pallas_call(
        _final_kernel,
        out_shape=jax.ShapeDtypeStruct((TOKENS, D_MODEL), F32),
        grid_spec=pltpu.PrefetchScalarGridSpec(
            num_scalar_prefetch=1,
            grid=(TOKENS // tb,),
            in_specs=[
                pl.BlockSpec(memory_space=pl.ANY),
                pl.BlockSpec((tb, D_MODEL), lambda i, pos: (i, 0)),
                pl.BlockSpec((tb, LANES), lambda i, pos: (i, 0)),
                pl.BlockSpec((None, 6, D_MODEL), lambda i, pos: (i // tiles_per_batch, 0, 0)),
                pl.BlockSpec((1, D_MODEL), lambda i, pos: (0, 0)),
                pl.BlockSpec((1, D_MODEL), lambda i, pos: (0, 0)),
            ],
            out_specs=pl.BlockSpec((tb, D_MODEL), lambda i, pos: (i, 0)),
            scratch_shapes=[pltpu.VMEM((2, 2 * tb // SUBLANES, SUBLANES, D_MODEL), F32),
                            pltpu.SemaphoreType.DMA((2,))],
        ),
        compiler_params=_cparams(("arbitrary",)),
        name="combine_ln2",
    )(pos_tiles, moe_out.reshape(MOE_ROWS // SUBLANES, SUBLANES, D_MODEL), x1, rt, mod, lng, lnb)


def _routing_tables(e_ids):
    tm = MOE_TM
    e_flat = e_ids.reshape(-1)
    onehot = (e_flat[:, None] == jnp.arange(N_EXPERTS, dtype=jnp.int32)[None, :]).astype(jnp.int32)
    csum = jnp.cumsum(onehot, axis=0)
    rank = jnp.sum((csum - onehot) * onehot, axis=1)
    counts = csum[-1]
    padded = ((counts + tm - 1) // tm) * tm
    pend = jnp.cumsum(padded)
    pstart = pend - padded
    dest = pstart[e_flat] + rank
    tok = jnp.arange(2 * TOKENS, dtype=jnp.int32) // 2
    row_token = jnp.zeros((MOE_ROWS,), jnp.int32).at[dest].set(tok)
    n_used = (pend[-1] // tm).astype(jnp.int32)
    tile_start = jnp.arange(MOE_TILES, dtype=jnp.int32) * tm
    te = jnp.sum((pend[None, :] <= tile_start[:, None]).astype(jnp.int32), axis=1)
    te = jnp.minimum(te, N_EXPERTS - 1)
    te_last = te[jnp.maximum(n_used - 1, 0)]
    te = jnp.where(jnp.arange(MOE_TILES) < n_used, te, te_last)
    next_run = pend[te] // tm
    te_next = jnp.where(next_run < n_used, te[jnp.minimum(next_run, MOE_TILES - 1)], -1).astype(jnp.int32)
    tile_rows = jnp.clip(pstart[te] + counts[te] - tile_start, 0, tm).astype(jnp.int32)
    pos = dest.reshape(TOKENS, 2)
    return row_token, te, te_next, tile_rows, n_used.reshape(1), pos


def _rope_tables():
    inv = ROPE_THETA ** (-jnp.arange(0, HEAD_DIM_A, 2, dtype=F32) / HEAD_DIM_A)
    ang = jnp.arange(SEQ, dtype=F32)[:, None] * inv[None, :]
    cos = jnp.cos(ang)
    sin = jnp.sin(ang)
    return jnp.concatenate([cos, cos], axis=-1), jnp.concatenate([-sin, sin], axis=-1)


def kernel(x, c, w_ada, b_ada, w_in, b_mgate, conv_w, conv_b, m_norm_g, w_proj_a, w_proj_m, w_gate, b_gate,
           w_out, ln1_g, ln1_b, w_rg, b_rg, w_re, b_re, w_eg, w_eu, w_ed, ln2_g, ln2_b):
    assert x.shape == (BATCH, SEQ, D_MODEL) and w_ada.shape[0] == 1
    l = 0
    x2 = x.reshape(TOKENS, D_MODEL)

    c_pad = jnp.zeros((ADA_ROWS, D_MODEL), F32).at[:BATCH].set(c)
    mod = _ada(c_pad, w_ada[l], b_ada[l][None, :])[:BATCH].reshape(BATCH, 6, D_MODEL)

    w_in_t = jnp.swapaxes(w_in[l], 0, 1)
    w_if_t = jnp.zeros((LANES, D_MODEL), F32).at[:2 * M_HEADS].set(w_in_t[N_IN_MAIN:])
    p, gates, u = _inproj(x2, mod, w_in_t, w_if_t)
    g = _gateproj(u, w_gate[l], b_gate[l][None, :])

    cos_t, sin_t = _rope_tables()
    ya = _attention(p, cos_t, sin_t)

    gcol = gates[:, :2 * M_HEADS].reshape(BATCH, SEQ, 2 * M_HEADS)
    grow = jnp.transpose(gcol, (0, 2, 1))
    ym = _mlstm(p, gcol, grow, b_mgate[l][None, :], b_mgate[l][:, None], conv_w[l], conv_b[l][None, :],
                m_norm_g[l][None, :])

    wr = (jnp.zeros((D_MODEL, LANES), F32)
          .at[:, :N_EXPERT_GROUPS].set(w_rg[l])
          .at[:, N_EXPERT_GROUPS:N_EXPERT_GROUPS + N_EXPERTS].set(w_re[l])).astype(BF16)
    br = (jnp.zeros((1, LANES), F32)
          .at[0, :N_EXPERT_GROUPS].set(b_rg[l])
          .at[0, N_EXPERT_GROUPS:N_EXPERT_GROUPS + N_EXPERTS].set(b_re[l]))
    x1, u2, rt = _merge(ya, ym, g, x2, mod,
                        w_proj_a[l].astype(BF16), w_proj_m[l].astype(BF16), w_out[l].astype(BF16),
                        ln1_g[l][None, :], ln1_b[l][None, :], wr, br)

    e_ids = rt[:, RT_E1:RT_E2 + 1].astype(jnp.int32)
    row_token, te, te_next, tile_rows, n_used, pos = _routing_tables(e_ids)

    mo = _moe(te, te_next, tile_rows, n_used, row_token, u2,
              w_eg[l].reshape(N_EXPERTS, D_MODEL, D_FF_EXPERT), w_eu[l].reshape(N_EXPERTS, D_MODEL, D_FF_EXPERT),
              w_ed[l].reshape(N_EXPERTS, D_FF_EXPERT, D_MODEL))

    nt = TOKENS // FINAL_TB
    pos_tiles = jnp.transpose(pos.reshape(nt, FINAL_TB, 2), (0, 2, 1)).reshape(-1)
    y = _final(pos_tiles, mo, x1, rt, mod, ln2_g[l][None, :], ln2_b[l][None, :])
    return y.reshape(BATCH, SEQ, D_MODEL)
```

```python
import functools

import jax
import jax.numpy as jnp
from jax import lax
from jax.experimental import pallas as pl
from jax.experimental.pallas import tpu as pltpu

F32 = jnp.float32
BF16 = jnp.bfloat16

D_MODEL = 2048
BATCH = 4
SEQ = 2048
TOKENS = BATCH * SEQ
DIL_CONFIGS = ((128, 1), (512, 4), (2048, 16))
N_DIL_GROUPS = 3
HEADS_PER_GROUP = 4
HEAD_DIM_A = 128
ATT_BLOCK = 128
ROPE_THETA = 10000.0
A_GROUP_W = HEADS_PER_GROUP * HEAD_DIM_A
A_QKV_W = N_DIL_GROUPS * A_GROUP_W
M_HEADS = 4
M_HEAD_DIM = 256
M_W = M_HEADS * M_HEAD_DIM
M_CHUNK = 128
CONV_K = 4
N_IN_MAIN = 3 * A_QKV_W + 4 * M_W
N_EXPERT_GROUPS = 4
EXPERTS_PER_GROUP = 8
N_EXPERTS = N_EXPERT_GROUPS * EXPERTS_PER_GROUP
D_FF_EXPERT = 1024
DEEPNORM_ALPHA = 2.0 ** 0.25
LN_EPS = 1e-5

LANES = 128
SUBLANES = 8
VMEM_LIMIT_BYTES = 56 * 1024 * 1024

PROJ_TN = 256
GATE_TN = 1024
PROJ_TM = 1024
IN_TM = 2048
P_TILES_IN = N_IN_MAIN // PROJ_TN
P_ATT_TILES = 3 * A_QKV_W // PROJ_TN
P_M_TILES = P_TILES_IN - P_ATT_TILES
P_WIDTH = N_IN_MAIN
P_ATT_OFF = P_M_TILES * PROJ_TN
P_QM_BLK, P_KM_BLK, P_VM_BLK, P_OM_BLK = 0, 1, 2, 3
MERGE_TM = 256
MERGE_SUB = 256
MOE_TM = 256
MOE_ROWS = 2 * TOKENS + N_EXPERTS * MOE_TM
MOE_TILES = MOE_ROWS // MOE_TM
MOE_ROW_SLOTS = 3
FINAL_TB = 256


def _cparams(sem, vmem=VMEM_LIMIT_BYTES):
    return pltpu.CompilerParams(dimension_semantics=sem, vmem_limit_bytes=vmem)


def _normalize(x):
    mu = jnp.mean(x, axis=-1, keepdims=True)
    xc = x - mu
    var = jnp.mean(xc * xc, axis=-1, keepdims=True)
    return xc * lax.rsqrt(var + LN_EPS)


def _silu(x):
    return x * jax.nn.sigmoid(x)


def _log_sigmoid(x):
    return jnp.minimum(x, 0.0) - jnp.log(1.0 + jnp.exp(-jnp.abs(x)))


_CONTRACT_LAST = (((1,), (1,)), ((), ()))
LOG2E = 1.4426950408889634


ADA_TN = 1024
ADA_ROWS = 16


def _ada_kernel(c_ref, w_ref, b_ref, o_ref):
    sc = _silu(c_ref[...]).astype(BF16)
    o_ref[...] = jnp.dot(sc, w_ref[...].astype(BF16), preferred_element_type=F32) + b_ref[...]


def _ada(c_pad, w_ada, b_ada):
    n = w_ada.shape[1]
    return pl.pallas_call(
        _ada_kernel,
        out_shape=jax.ShapeDtypeStruct((ADA_ROWS, n), F32),
        grid=(n // ADA_TN,),
        in_specs=[
            pl.BlockSpec((ADA_ROWS, D_MODEL), lambda j: (0, 0)),
            pl.BlockSpec((D_MODEL, ADA_TN), lambda j: (0, j)),
            pl.BlockSpec((1, ADA_TN), lambda j: (0, j)),
        ],
        out_specs=pl.BlockSpec((ADA_ROWS, ADA_TN), lambda j: (0, j)),
        compiler_params=_cparams(("arbitrary",)),
        name="ada_mod",
    )(c_pad, w_ada, b_ada)


LN_CHUNK = 256


def _inproj_kernel(x_ref, mod_ref, w_ref, wif_ref, p_ref, g_ref, u_ref, xbuf, xsem):
    m = pl.program_id(0)
    n = pl.program_id(1)

    def x_copy(tile):
        return pltpu.make_async_copy(x_ref.at[pl.ds(pl.multiple_of(tile * IN_TM, IN_TM), IN_TM)], xbuf, xsem)

    @pl.when((m == 0) & (n == 0))
    def _():
        x_copy(0).start()

    @pl.when((n == 1) & (m + 1 < pl.num_programs(0)))
    def _():
        x_copy(m + 1).start()

    @pl.when(n == 0)
    def _():
        x_copy(m).wait()
        shift = mod_ref[0:1, :]
        scale = 1.0 + mod_ref[1:2, :]
        wif = wif_ref[...].astype(BF16)
        w = w_ref[...].astype(BF16)
        for ci in range(IN_TM // LN_CHUNK):
            rows = slice(ci * LN_CHUNK, (ci + 1) * LN_CHUNK)
            u = (_normalize(xbuf[rows, :]) * scale + shift).astype(BF16)
            u_ref[rows, :] = u
            g_ref[rows, :] = lax.dot_general(u, wif, _CONTRACT_LAST, preferred_element_type=F32)
            p_ref[rows, :] = lax.dot_general(u, w, _CONTRACT_LAST, preferred_element_type=F32)

    @pl.when(n > 0)
    def _():
        p_ref[...] = lax.dot_general(u_ref[...], w_ref[...].astype(BF16), _CONTRACT_LAST,
                                     preferred_element_type=F32)


def _inproj(x2, mod, w_in_t, w_if_t):
    tiles_per_batch = SEQ // IN_TM
    return pl.pallas_call(
        _inproj_kernel,
        out_shape=(
            jax.ShapeDtypeStruct((TOKENS, P_WIDTH), F32),
            jax.ShapeDtypeStruct((TOKENS, LANES), F32),
            jax.ShapeDtypeStruct((TOKENS, D_MODEL), BF16),
        ),
        grid=(TOKENS // IN_TM, P_TILES_IN),
        in_specs=[
            pl.BlockSpec(memory_space=pl.ANY),
            pl.BlockSpec((None, 6, D_MODEL), lambda m, n: (m // tiles_per_batch, 0, 0)),
            pl.BlockSpec((PROJ_TN, D_MODEL), lambda m, n: (n, 0)),
            pl.BlockSpec((LANES, D_MODEL), lambda m, n: (0, 0)),
        ],
        out_specs=(
            pl.BlockSpec((IN_TM, PROJ_TN),
                         lambda m, n: (m, jnp.where(n < P_ATT_TILES, n + P_M_TILES, n - P_ATT_TILES))),
            pl.BlockSpec((IN_TM, LANES), lambda m, n: (m, 0)),
            pl.BlockSpec((IN_TM, D_MODEL), lambda m, n: (m, 0)),
        ),
        scratch_shapes=[pltpu.VMEM((IN_TM, D_MODEL), F32), pltpu.SemaphoreType.DMA],
        compiler_params=_cparams(("arbitrary", "arbitrary")),
        name="in_proj",
    )(x2, mod, w_in_t, w_if_t)


def _gateproj_kernel(u_ref, w_ref, b_ref, o_ref):
    acc = jnp.dot(u_ref[...], w_ref[...].astype(BF16), preferred_element_type=F32)
    o_ref[...] = jax.nn.sigmoid(acc + b_ref[...]).astype(BF16)


def _gateproj(u, w_gate, b_gate):
    n = w_gate.shape[1]
    return pl.pallas_call(
        _gateproj_kernel,
        out_shape=jax.ShapeDtypeStruct((TOKENS, n), BF16),
        grid=(TOKENS // PROJ_TM, n // GATE_TN),
        in_specs=[
            pl.BlockSpec((PROJ_TM, D_MODEL), lambda m, j: (m, 0)),
            pl.BlockSpec((D_MODEL, GATE_TN), lambda m, j: (0, j)),
            pl.BlockSpec((1, GATE_TN), lambda m, j: (0, j)),
        ],
        out_specs=pl.BlockSpec((PROJ_TM, GATE_TN), lambda m, j: (m, j)),
        compiler_params=_cparams(("arbitrary", "arbitrary")),
        name="gate_proj",
    )(u, w_gate, b_gate)


ROPE_ROWS = 256


def _rows(start, size, stride):
    return pl.ds(start, size) if stride == 1 else pl.ds(start, size, stride=stride)


def _attn_kernel(q0, k0, v0, q1, k1, v1, q2, k2, v2, cos_ref, sin_ref, y_ref,
                 qr_sc, kr_sc, o0, o1, o2, l0, l1, l2):
    scale = HEAD_DIM_A ** -0.5
    blk = ATT_BLOCK
    qi2 = lax.broadcasted_iota(jnp.int32, (blk, 2 * blk), 0)
    kc2 = lax.broadcasted_iota(jnp.int32, (blk, 2 * blk), 1)
    mask_prev_cur = (kc2 >= qi2) & (kc2 <= qi2 + blk)
    qi1 = lax.broadcasted_iota(jnp.int32, (blk, blk), 0)
    kc1 = lax.broadcasted_iota(jnp.int32, (blk, blk), 1)
    mask_cur = kc1 <= qi1
    half = HEAD_DIM_A // 2
    groups = ((q0, k0, v0, o0, l0), (q1, k1, v1, o1, l1), (q2, k2, v2, o2, l2))
    for (window, d), (q_ref, k_ref, v_ref, o_sc, l_sc) in zip(DIL_CONFIGS, groups):
        nb = SEQ // d // blk
        for src_ref, dst_sc in ((q_ref, qr_sc), (k_ref, kr_sc)):
            for c0 in range(0, SEQ, ROPE_ROWS):
                rs = slice(c0, c0 + ROPE_ROWS)
                xr = src_ref[rs, :]
                dst_sc[rs, :] = xr * cos_ref[rs, :] + pltpu.roll(xr, half, 1) * sin_ref[rs, :]
        for r in range(d):
            for j in range(nb):
                start = r + j * blk * d
                cur = _rows(start, blk, d)
                qb = qr_sc[cur, :].astype(BF16)
                if j == 0:
                    keys = cur
                    mask = mask_cur
                else:
                    keys = _rows(start - blk * d, 2 * blk, d)
                    mask = mask_prev_cur
                kw = kr_sc[keys, :].astype(BF16)
                vb = v_ref[keys, :].astype(BF16)
                vw = jnp.concatenate([vb, jnp.ones_like(vb)], axis=1)
                s = lax.dot_general(qb, kw, _CONTRACT_LAST, preferred_element_type=F32)
                s = jnp.where(mask, s, -jnp.inf)
                mx = jnp.max(s, axis=-1, keepdims=True)
                p = jnp.exp2((s - mx) * (scale * LOG2E))
                pv = jnp.dot(p.astype(BF16), vw, preferred_element_type=F32)
                den = pv[:, HEAD_DIM_A:]
                o_sc[cur, :] = pv[:, :HEAD_DIM_A] / den
                l_sc[cur, :] = mx * scale + jnp.log(den)
    for c0 in range(0, SEQ, ROPE_ROWS):
        rs = slice(c0, c0 + ROPE_ROWS)
        la = l0[rs, :]
        lb = l1[rs, :]
        lc = l2[rs, :]
        mx = jnp.maximum(jnp.maximum(la, lb), lc)
        ea = jnp.exp(la - mx)
        eb = jnp.exp(lb - mx)
        ec = jnp.exp(lc - mx)
        den = ea + eb + ec
        y_ref[rs, :] = ((ea / den) * o0[rs, :] + (eb / den) * o1[rs, :] + (ec / den) * o2[rs, :]).astype(BF16)


def _attention(p, cos_t, sin_t):
    for window, d in DIL_CONFIGS:
        assert window // d == ATT_BLOCK and SEQ % (d * ATT_BLOCK) == 0
    col0 = P_ATT_OFF // HEAD_DIM_A

    def slab(part, g):
        off = col0 + (part * A_QKV_W + g * A_GROUP_W) // HEAD_DIM_A
        return pl.BlockSpec((SEQ, HEAD_DIM_A), lambda b, h: (b, off + h))

    in_specs = [slab(part, g) for g in range(N_DIL_GROUPS) for part in range(3)]
    table = pl.BlockSpec((SEQ, HEAD_DIM_A), lambda b, h: (0, 0))
    return pl.pallas_call(
        _attn_kernel,
        out_shape=jax.ShapeDtypeStruct((TOKENS, A_GROUP_W), BF16),
        grid=(BATCH, HEADS_PER_GROUP),
        in_specs=in_specs + [table, table],
        out_specs=pl.BlockSpec((SEQ, HEAD_DIM_A), lambda b, h: (b, h)),
        scratch_shapes=[pltpu.VMEM((SEQ, HEAD_DIM_A), F32)] * 8,
        compiler_params=_cparams(("arbitrary", "arbitrary")),
        name="dil_attn",
    )(*([p] * 9), cos_t, sin_t)


CONV_HALO = SUBLANES


def _mlstm_kernel(q_ref, k_ref, v_ref, og_ref, gc_ref, gr_ref, bc_ref, br_ref, cw_ref, cb_ref, ng_ref,
                  y_ref, xq_sc, xk_sc, ct_sc, n_sc, m_sc):
    c = pl.program_id(1)
    L = M_CHUNK
    lo = CONV_HALO

    @pl.when(c == 0)
    def _():
        xq_sc[0:lo, :] = jnp.zeros((lo, M_W), F32)
        xk_sc[0:lo, :] = jnp.zeros((lo, M_W), F32)
        ct_sc[...] = jnp.zeros_like(ct_sc)
        n_sc[...] = jnp.zeros_like(n_sc)
        m_sc[...] = jnp.zeros_like(m_sc)

    xq_sc[lo:lo + L, :] = q_ref[...]
    xk_sc[lo:lo + L, :] = k_ref[...]

    def conv(x_sc, w, b):
        acc = x_sc[lo:lo + L, :] * w[CONV_K - 1:CONV_K, :] + b
        for j in range(CONV_K - 1):
            off = lo - (CONV_K - 1) + j
            acc = acc + x_sc[off:off + L, :] * w[j:j + 1, :]
        return acc

    cw = cw_ref[...]
    cb = cb_ref[...]
    qc = _silu(conv(xq_sc, cw[:, :M_W], cb[:, :M_W]))
    kc = _silu(conv(xk_sc, cw[:, M_W:], cb[:, M_W:])) * (M_HEAD_DIM ** -0.5)
    xq_sc[0:lo, :] = xq_sc[L:L + lo, :]
    xk_sc[0:lo, :] = xk_sc[L:L + lo, :]

    row = lax.broadcasted_iota(jnp.int32, (L, L), 0)
    col = lax.broadcasted_iota(jnp.int32, (L, L), 1)
    causal = row >= col
    state = [(ct_sc[h], n_sc[h, 0:1, :], m_sc[h, 0:1, 0:1]) for h in range(M_HEADS)]
    results = []
    for h in range(M_HEADS):
        hs = slice(h * M_HEAD_DIM, (h + 1) * M_HEAD_DIM)
        ct, n_row, m_prev = state[h]
        q = qc[:, hs]
        k = kc[:, hs]
        i_col = gc_ref[:, h:h + 1] + bc_ref[:, h:h + 1]
        lf_col = _log_sigmoid(gc_ref[:, M_HEADS + h:M_HEADS + h + 1] + bc_ref[:, M_HEADS + h:M_HEADS + h + 1])
        i_row = gr_ref[h:h + 1, :] + br_ref[h:h + 1, :]
        lf_row = _log_sigmoid(gr_ref[M_HEADS + h:M_HEADS + h + 1, :] + br_ref[M_HEADS + h:M_HEADS + h + 1, :])
        b_col = jnp.sum(jnp.where(causal, lf_row, 0.0), axis=1, keepdims=True)
        b_row = jnp.sum(jnp.where(row <= col, lf_col, 0.0), axis=0, keepdims=True)
        log_d = jnp.where(causal, b_col - b_row + i_row, -jnp.inf)
        log_inter = b_col + m_prev
        m_t = jnp.maximum(jnp.max(log_d, axis=1, keepdims=True), log_inter)
        qb = q.astype(BF16)
        kb = k.astype(BF16)
        vb = v_ref[:, hs].astype(BF16)
        s = lax.dot_general(qb, kb, (((1,), (1,)), ((), ())), preferred_element_type=F32) * jnp.exp(log_d - m_t)
        inter = jnp.exp(log_inter - m_t)
        num = (jnp.dot(s.astype(BF16), vb, preferred_element_type=F32)
               + inter * jnp.dot(qb, ct.astype(BF16), preferred_element_type=F32))
        den = jnp.sum(s, axis=1, keepdims=True) + inter * jnp.sum(q * n_row, axis=1, keepdims=True)
        hh = num / jnp.maximum(jnp.abs(den), jnp.exp(-m_t))
        m_new = m_t[L - 1:L, :]
        b_last = b_col[L - 1:L, :]
        w_col = jnp.exp(b_last - b_col + i_col - m_new)
        decay = jnp.exp(b_last + m_prev - m_new)
        kw = k * w_col
        ct_new = decay * ct + lax.dot_general(kw.astype(BF16), vb, (((0,), (0,)), ((), ())),
                                              preferred_element_type=F32)
        n_new = decay * n_row + jnp.sum(kw, axis=0, keepdims=True)
        z = jax.nn.sigmoid(og_ref[:, hs]) * hh
        results.append((ct_new, n_new, m_new, (_normalize(z) * ng_ref[:, hs]).astype(BF16)))
    for h, (ct_new, n_new, m_new, y) in enumerate(results):
        ct_sc[h] = ct_new
        n_sc[h] = jnp.broadcast_to(n_new, (SUBLANES, M_HEAD_DIM))
        m_sc[h] = jnp.broadcast_to(m_new, (SUBLANES, LANES))
        y_ref[:, h * M_HEAD_DIM:(h + 1) * M_HEAD_DIM] = y


def _mlstm(p, gcol, grow, bcol, brow, conv_w, conv_b, norm_g):
    nc = SEQ // M_CHUNK
    slab = lambda blk: pl.BlockSpec((M_CHUNK, M_W), lambda b, c: (b * nc + c, blk))
    return pl.pallas_call(
        _mlstm_kernel,
        out_shape=jax.ShapeDtypeStruct((TOKENS, M_W), BF16),
        grid=(BATCH, nc),
        in_specs=[
            slab(P_QM_BLK), slab(P_KM_BLK), slab(P_VM_BLK), slab(P_OM_BLK),
            pl.BlockSpec((None, M_CHUNK, 2 * M_HEADS), lambda b, c: (b, c, 0)),
            pl.BlockSpec((None, 2 * M_HEADS, M_CHUNK), lambda b, c: (b, 0, c)),
            pl.BlockSpec((1, 2 * M_HEADS), lambda b, c: (0, 0)),
            pl.BlockSpec((2 * M_HEADS, 1), lambda b, c: (0, 0)),
            pl.BlockSpec((CONV_K, 2 * M_W), lambda b, c: (0, 0)),
            pl.BlockSpec((1, 2 * M_W), lambda b, c: (0, 0)),
            pl.BlockSpec((1, M_W), lambda b, c: (0, 0)),
        ],
        out_specs=pl.BlockSpec((M_CHUNK, M_W), lambda b, c: (b * nc + c, 0)),
        scratch_shapes=[
            pltpu.VMEM((M_CHUNK + CONV_HALO, M_W), F32),
            pltpu.VMEM((M_CHUNK + CONV_HALO, M_W), F32),
            pltpu.VMEM((M_HEADS, M_HEAD_DIM, M_HEAD_DIM), F32),
            pltpu.VMEM((M_HEADS, SUBLANES, M_HEAD_DIM), F32),
            pltpu.VMEM((M_HEADS, SUBLANES, LANES), F32),
        ],
        compiler_params=_cparams(("arbitrary", "arbitrary")),
        name="mlstm",
    )(p, p, p, p, gcol, grow, bcol, brow, conv_w, conv_b, norm_g)


RT_E1, RT_E2, RT_W1, RT_W2 = 0, 1, 2, 3


def _route(logits):
    lane = lax.broadcasted_iota(jnp.int32, logits.shape, 1).astype(F32)
    big = float(LANES)
    is_g = lane < N_EXPERT_GROUPS
    gl = jnp.where(is_g, logits, -jnp.inf)
    gexp = jnp.exp(gl - jnp.max(gl, axis=1, keepdims=True))
    gprob = gexp / jnp.sum(gexp, axis=1, keepdims=True)
    g_w = jnp.max(gprob, axis=1, keepdims=True)
    g_top = jnp.min(jnp.where(is_g & (gprob == g_w), lane, big), axis=1, keepdims=True)
    lo = N_EXPERT_GROUPS + EXPERTS_PER_GROUP * g_top
    in_grp = (lane >= lo) & (lane < lo + EXPERTS_PER_GROUP)
    el = jnp.where(in_grp, logits, -jnp.inf)
    eexp = jnp.exp(el - jnp.max(el, axis=1, keepdims=True))
    eprob = eexp / jnp.sum(eexp, axis=1, keepdims=True)
    v1 = jnp.max(eprob, axis=1, keepdims=True)
    i1 = jnp.min(jnp.where(in_grp & (eprob == v1), lane, big), axis=1, keepdims=True)
    rest = jnp.where(in_grp & (lane != i1), eprob, -1.0)
    v2 = jnp.max(rest, axis=1, keepdims=True)
    i2 = jnp.min(jnp.where(rest == v2, lane, big), axis=1, keepdims=True)
    tot = v1 + v2
    w1 = g_w * (v1 / tot)
    w2 = g_w * (v2 / tot)
    e1 = i1 - N_EXPERT_GROUPS
    e2 = i2 - N_EXPERT_GROUPS
    rec = jnp.where(lane == RT_E1, e1, jnp.where(lane == RT_E2, e2, jnp.where(lane == RT_W1, w1, w2)))
    return jnp.where(lane <= RT_W2, rec, 0.0)


def _merge_kernel(ya_ref, ym_ref, g_ref, x_ref, mod_ref,
                  wpa_ref, wpm_ref, wout_ref, lng_ref, lnb_ref, wr_ref, br_ref,
                  x1_ref, u2_ref, rt_ref):
    for sb in range(MERGE_TM // MERGE_SUB):
        rows = slice(sb * MERGE_SUB, (sb + 1) * MERGE_SUB)
        pa = jnp.dot(ya_ref[rows, :], wpa_ref[...], preferred_element_type=F32)
        pm = jnp.dot(ym_ref[rows, :], wpm_ref[...], preferred_element_type=F32)
        merged = g_ref[rows, :D_MODEL].astype(F32) * pa + g_ref[rows, D_MODEL:].astype(F32) * pm
        mix = jnp.dot(merged.astype(BF16), wout_ref[...], preferred_element_type=F32)
        z = DEEPNORM_ALPHA * x_ref[rows, :] + mod_ref[2:3, :] * mix
        x1 = _normalize(z) * lng_ref[...] + lnb_ref[...]
        x1_ref[rows, :] = x1
        u2 = _normalize(x1) * (1.0 + mod_ref[4:5, :]) + mod_ref[3:4, :]
        u2_ref[rows, :] = u2
        logits = jnp.dot(u2.astype(BF16), wr_ref[...], preferred_element_type=F32) + br_ref[...]
        rt_ref[rows, :] = _route(logits)


def _merge(ya, ym, g, x2, mod, wpa, wpm, wout, lng, lnb, wr, br):
    tm = MERGE_TM
    tiles_per_batch = SEQ // tm
    rowblk = lambda w: pl.BlockSpec((tm, w), lambda m: (m, 0))
    const = lambda shape: pl.BlockSpec(shape, lambda m: (0,) * len(shape), pipeline_mode=pl.Buffered(1))
    return pl.pallas_call(
        _merge_kernel,
        out_shape=(
            jax.ShapeDtypeStruct((TOKENS, D_MODEL), F32),
            jax.ShapeDtypeStruct((TOKENS, D_MODEL), F32),
            jax.ShapeDtypeStruct((TOKENS, LANES), F32),
        ),
        grid=(TOKENS // tm,),
        in_specs=[
            rowblk(A_GROUP_W), rowblk(M_W), rowblk(2 * D_MODEL), rowblk(D_MODEL),
            pl.BlockSpec((None, 6, D_MODEL), lambda m: (m // tiles_per_batch, 0, 0)),
            const((A_GROUP_W, D_MODEL)), const((M_W, D_MODEL)), const((D_MODEL, D_MODEL)),
            const((1, D_MODEL)), const((1, D_MODEL)),
            const((D_MODEL, LANES)), const((1, LANES)),
        ],
        out_specs=(rowblk(D_MODEL), rowblk(D_MODEL), rowblk(LANES)),
        compiler_params=_cparams(("arbitrary",)),
        name="merge_ln1_route",
    )(ya, ym, g, x2, mod, wpa, wpm, wout, lng, lnb, wr, br)


CAST_ROWS = 128


def _issue_rows(src_ref, idx_ref, base, buf, slot, sem, nrows):
    def body(blk, carry):
        for j in range(SUBLANES):
            row = idx_ref[base + blk * SUBLANES + j]
            src = src_ref.at[lax.shift_right_logical(row, 3), pl.ds(row & (SUBLANES - 1), 1)]
            pltpu.make_async_copy(src, buf.at[slot, blk, pl.ds(j, 1)], sem.at[slot]).start(priority=j % 2)
        return carry

    lax.fori_loop(0, nrows // SUBLANES, body, 0)


def _wait_rows(src_ref, buf, slot, sem, nrows):
    groups = nrows // SUBLANES
    pltpu.make_async_copy(src_ref.at[pl.ds(0, groups)], buf.at[slot, pl.ds(0, groups)], sem.at[slot]).wait()


class _TileTable:
    def __init__(self, meta_ref):
        self.ref = meta_ref

    def expert(self, t):
        return self.ref[META_TE * LANES + t]

    def next_expert(self, t):
        return self.ref[META_TN * LANES + t]

    def valid_rows(self, t):
        return self.ref[META_TV * LANES + t]

    def tiles_used(self):
        return self.ref[META_NU * LANES]


def _expert_changed(te_ref, i):
    return (i == 0) | (te_ref.expert(i) != te_ref.expert(jnp.maximum(i - 1, 0)))


def _moe_kernel(meta_ref, tok_ref, u_ref, wg_ref, wu_ref, wd_ref, o_ref,
                stg, stu, std, wgb, wub, wdb, xbuf, wsem, xsem):
    i = pl.program_id(0)
    te_ref = _TileTable(meta_ref)
    nu = te_ref.tiles_used()
    tm = o_ref.shape[0]
    half = tm // 2

    def weight_copies(e):
        return (pltpu.make_async_copy(wg_ref.at[e], stg, wsem.at[0]),
                pltpu.make_async_copy(wu_ref.at[e], stu, wsem.at[1]),
                pltpu.make_async_copy(wd_ref.at[e], std, wsem.at[2]))

    def for_tile_rows(t, fn):
        @pl.when(te_ref.valid_rows(t) <= half)
        def _():
            fn(half)

        @pl.when(te_ref.valid_rows(t) > half)
        def _():
            fn(tm)

    def issue_tile(t):
        for_tile_rows(t, lambda nrows: _issue_rows(u_ref, tok_ref, t * tm, xbuf, t % MOE_ROW_SLOTS, xsem, nrows))

    @pl.when(i == 0)
    def _():
        issue_tile(0)

        @pl.when(nu > 1)
        def _():
            issue_tile(1)

        for cp in weight_copies(te_ref.expert(0)):
            cp.start(priority=1)

    @pl.when(i < nu)
    def _():
        slot = i % MOE_ROW_SLOTS

        @pl.when(i + 2 < nu)
        def _():
            issue_tile(i + 2)

        @pl.when(_expert_changed(te_ref, i))
        def _():
            for cp in weight_copies(te_ref.expert(i)):
                cp.wait()
            for src, dst in ((stg, wgb), (stu, wub), (std, wdb)):
                def cast_rows(ci, carry, src=src, dst=dst):
                    r = pl.multiple_of(ci * CAST_ROWS, CAST_ROWS)
                    dst[pl.ds(r, CAST_ROWS), :] = src[pl.ds(r, CAST_ROWS), :].astype(BF16)
                    return carry

                lax.fori_loop(0, src.shape[0] // CAST_ROWS, cast_rows, 0)

            @pl.when(te_ref.next_expert(i) >= 0)
            def _():
                for cp in weight_copies(te_ref.next_expert(i)):
                    cp.start(priority=1)

        def compute(nrows):
            _wait_rows(u_ref, xbuf, slot, xsem, nrows)
            x = xbuf[slot, 0:nrows // SUBLANES].reshape(nrows, D_MODEL).astype(BF16)
            a = jnp.dot(x, wgb[...], preferred_element_type=F32)
            b = jnp.dot(x, wub[...], preferred_element_type=F32)
            h = (_silu(a) * b).astype(BF16)
            o_ref[0:nrows, :] = jnp.dot(h, wdb[...], preferred_element_type=F32)
            if nrows < tm:
                o_ref[nrows:tm, :] = jnp.zeros((tm - nrows, D_MODEL), F32)

        for_tile_rows(i, compute)

    @pl.when(i >= nu)
    def _():
        o_ref[...] = jnp.zeros_like(o_ref)


def _moe(meta, row_token, u2, w_eg, w_eu, w_ed):
    tm = MOE_TM
    return pl.pallas_call(
        _moe_kernel,
        out_shape=jax.ShapeDtypeStruct((MOE_ROWS, D_MODEL), F32),
        grid_spec=pltpu.PrefetchScalarGridSpec(
            num_scalar_prefetch=2,
            grid=(MOE_TILES,),
            in_specs=[pl.BlockSpec(memory_space=pl.ANY)] * 4,
            out_specs=pl.BlockSpec((tm, D_MODEL), lambda i, meta, tok: (i, 0)),
            scratch_shapes=[
                pltpu.VMEM((D_MODEL, D_FF_EXPERT), F32),
                pltpu.VMEM((D_MODEL, D_FF_EXPERT), F32),
                pltpu.VMEM((D_FF_EXPERT, D_MODEL), F32),
                pltpu.VMEM((D_MODEL, D_FF_EXPERT), BF16),
                pltpu.VMEM((D_MODEL, D_FF_EXPERT), BF16),
                pltpu.VMEM((D_FF_EXPERT, D_MODEL), BF16),
                pltpu.VMEM((MOE_ROW_SLOTS, tm // SUBLANES, SUBLANES, D_MODEL), F32),
                pltpu.SemaphoreType.DMA((3,)),
                pltpu.SemaphoreType.DMA((MOE_ROW_SLOTS,)),
            ],
        ),
        compiler_params=_cparams(("arbitrary",)),
        name="moe_experts",
    )(meta, row_token, u2.reshape(TOKENS // SUBLANES, SUBLANES, D_MODEL), w_eg, w_eu, w_ed)


def _final_kernel(pos_ref, o_ref, x1_ref, rt_ref, mod_ref, lng_ref, lnb_ref, y_ref, buf, sem):
    i = pl.program_id(0)
    tb = x1_ref.shape[0]
    slot = i % 2

    @pl.when(i == 0)
    def _():
        _issue_rows(o_ref, pos_ref, 0, buf, 0, sem, 2 * tb)

    @pl.when(i + 1 < pl.num_programs(0))
    def _():
        _issue_rows(o_ref, pos_ref, (i + 1) * (2 * tb), buf, 1 - slot, sem, 2 * tb)

    _wait_rows(o_ref, buf, slot, sem, 2 * tb)
    rows = buf[slot].reshape(2 * tb, D_MODEL)
    ffn = rt_ref[:, RT_W1:RT_W1 + 1] * rows[0:tb, :] + rt_ref[:, RT_W2:RT_W2 + 1] * rows[tb:2 * tb, :]
    z = DEEPNORM_ALPHA * x1_ref[...] + mod_ref[5:6, :] * ffn
    y_ref[...] = _normalize(z) * lng_ref[...] + lnb_ref[...]


def _final(pos_tiles, moe_out, x1, rt, mod, lng, lnb):
    tb = FINAL_TB
    tiles_per_batch = SEQ // tb
    return pl.pallas_call(
        _final_kernel,
        out_shape=jax.ShapeDtypeStruct((TOKENS, D_MODEL), F32),
        grid_spec=pltpu.PrefetchScalarGridSpec(
            num_scalar_prefetch=1,
            grid=(TOKENS // tb,),
            in_specs=[
                pl.BlockSpec(memory_space=pl.ANY),
                pl.BlockSpec((tb, D_MODEL), lambda i, pos: (i, 0)),
                pl.BlockSpec((tb, LANES), lambda i, pos: (i, 0)),
                pl.BlockSpec((None, 6, D_MODEL), lambda i, pos: (i // tiles_per_batch, 0, 0)),
                pl.BlockSpec((1, D_MODEL), lambda i, pos: (0, 0)),
                pl.BlockSpec((1, D_MODEL), lambda i, pos: (0, 0)),
            ],
            out_specs=pl.BlockSpec((tb, D_MODEL), lambda i, pos: (i, 0)),
            scratch_shapes=[pltpu.VMEM((2, 2 * tb // SUBLANES, SUBLANES, D_MODEL), F32),
                            pltpu.SemaphoreType.DMA((2,))],
        ),
        compiler_params=_cparams(("arbitrary",)),
        name="combine_ln2",
    )(pos_tiles, moe_out.reshape(MOE_ROWS // SUBLANES, SUBLANES, D_MODEL), x1, rt, mod, lng, lnb)


ROUTE_BLK = 256
META_TE, META_TN, META_TV, META_NU = 0, 1, 2, 3


def _lane_cumsum(x, lane):
    s = 1
    while s < LANES:
        x = x + jnp.where(lane >= s, pltpu.roll(x, s, 1), 0.0)
        s *= 2
    return x


def _route_tables_kernel(rt_ref, pos_ref, meta_ref, rank_sc):
    tm = float(MOE_TM)
    nblk = TOKENS // ROUTE_BLK
    lane_i = lax.broadcasted_iota(jnp.int32, (ROUTE_BLK, LANES), 1)
    lane_f = lane_i.astype(F32)
    earlier = (lax.broadcasted_iota(jnp.int32, (ROUTE_BLK, ROUTE_BLK), 0)
               > lax.broadcasted_iota(jnp.int32, (ROUTE_BLK, ROUTE_BLK), 1)).astype(BF16)

    def onehots(b):
        blk = rt_ref[pl.ds(pl.multiple_of(b * ROUTE_BLK, ROUTE_BLK), ROUTE_BLK), :]
        return blk[:, RT_E1:RT_E1 + 1] == lane_f, blk[:, RT_E2:RT_E2 + 1] == lane_f

    def count_pass(b, carry):
        c1, c2 = carry
        o1, o2 = onehots(b)
        f1 = jnp.where(o1, 1.0, 0.0)
        f2 = jnp.where(o2, 1.0, 0.0)
        p1 = jnp.dot(earlier, f1.astype(BF16), preferred_element_type=F32) + c1
        p2 = jnp.dot(earlier, f2.astype(BF16), preferred_element_type=F32) + c2
        r1 = jnp.sum(jnp.where(o1, p1, 0.0), axis=1, keepdims=True)
        r2 = jnp.sum(jnp.where(o2, p2, 0.0), axis=1, keepdims=True)
        rank_sc[pl.ds(pl.multiple_of(b * ROUTE_BLK, ROUTE_BLK), ROUTE_BLK), :] = jnp.where(
            lane_i == 0, r1, jnp.where(lane_i == 1, r2, 0.0))
        return c1 + jnp.sum(f1, axis=0, keepdims=True), c2 + jnp.sum(f2, axis=0, keepdims=True)

    zero = jnp.zeros((1, LANES), F32)
    c1, c2 = lax.fori_loop(0, nblk, count_pass, (zero, zero))

    lane8 = lax.broadcasted_iota(jnp.int32, (SUBLANES, LANES), 1)
    counts = jnp.broadcast_to(c1 + c2, (SUBLANES, LANES))
    padded = jnp.floor((counts + (tm - 1.0)) * (1.0 / tm)) * tm
    pend = _lane_cumsum(padded, lane8)
    pstart = pend - padded
    start1 = pstart[0:1, :]
    start2 = start1 + c1

    def place_pass(b, carry):
        o1, o2 = onehots(b)
        rows = pl.ds(pl.multiple_of(b * ROUTE_BLK, ROUTE_BLK), ROUTE_BLK)
        rk = rank_sc[rows, :]
        d1 = jnp.sum(jnp.where(o1, start1, 0.0), axis=1, keepdims=True) + rk[:, 0:1]
        d2 = jnp.sum(jnp.where(o2, start2, 0.0), axis=1, keepdims=True) + rk[:, 1:2]
        pos_ref[rows, :] = jnp.where(lane_i == 0, d1, jnp.where(lane_i == 1, d2, 0.0)).astype(jnp.int32)
        return carry

    lax.fori_loop(0, nblk, place_pass, 0)

    tile = lax.broadcasted_iota(jnp.int32, (LANES, LANES), 0).astype(F32)
    lane = lax.broadcasted_iota(jnp.int32, (LANES, LANES), 1)
    expert_lane = lane < N_EXPERTS
    pend_b = jnp.broadcast_to(pend[0:1, :], (LANES, LANES))
    pstart_b = jnp.broadcast_to(pstart[0:1, :], (LANES, LANES))
    counts_b = jnp.broadcast_to(counts[0:1, :], (LANES, LANES))
    n_used = jnp.sum(jnp.where(lane == N_EXPERTS - 1, pend_b, 0.0), axis=1, keepdims=True) * (1.0 / tm)
    te = jnp.sum(jnp.where(expert_lane & (pend_b <= tile * tm), 1.0, 0.0), axis=1, keepdims=True)
    te = jnp.minimum(te, float(N_EXPERTS - 1))
    te_last = jnp.sum(jnp.where(tile[:, 0:1] == n_used - 1.0, te, 0.0), axis=0, keepdims=True)
    te = jnp.where(tile[:, 0:1] < n_used, te, te_last)
    of_tile = lane.astype(F32) == te
    pend_te = jnp.sum(jnp.where(of_tile, pend_b, 0.0), axis=1, keepdims=True)
    last_row = jnp.sum(jnp.where(of_tile, pstart_b + counts_b, 0.0), axis=1, keepdims=True)
    tile_rows = jnp.clip(last_row - tile[:, 0:1] * tm, 0.0, tm)
    next_run = pend_te * (1.0 / tm)
    te_by_lane = jnp.broadcast_to(te, (LANES, LANES)).T
    te_at_next = jnp.sum(jnp.where(lane.astype(F32) == next_run, te_by_lane, 0.0), axis=1, keepdims=True)
    te_next = jnp.where(next_run < n_used, te_at_next, -1.0)
    cols = jnp.where(lane == META_TE, te, jnp.where(lane == META_TN, te_next,
                     jnp.where(lane == META_TV, tile_rows, jnp.where(lane == META_NU, n_used, 0.0))))
    meta_ref[...] = cols.T[0:SUBLANES, :].astype(jnp.int32)


def _routing_tables(rt):
    pos, meta = pl.pallas_call(
        _route_tables_kernel,
        out_shape=(
            jax.ShapeDtypeStruct((TOKENS, LANES), jnp.int32),
            jax.ShapeDtypeStruct((SUBLANES, LANES), jnp.int32),
        ),
        scratch_shapes=[pltpu.VMEM((TOKENS, LANES), F32)],
        compiler_params=_cparams(None),
        name="route_tables",
    )(rt)
    pos = pos[:, 0:2]
    tok = jnp.arange(2 * TOKENS, dtype=jnp.int32) // 2
    row_token = jnp.zeros((MOE_ROWS,), jnp.int32).at[pos.reshape(-1)].set(tok)
    return row_token, meta.reshape(-1), pos


def _rope_tables():
    inv = ROPE_THETA ** (-jnp.arange(0, HEAD_DIM_A, 2, dtype=F32) / HEAD_DIM_A)
    ang = jnp.arange(SEQ, dtype=F32)[:, None] * inv[None, :]
    cos = jnp.cos(ang)
    sin = jnp.sin(ang)
    return jnp.concatenate([cos, cos], axis=-1), jnp.concatenate([-sin, sin], axis=-1)


def kernel(x, c, w_ada, b_ada, w_in, b_mgate, conv_w, conv_b, m_norm_g, w_proj_a, w_proj_m, w_gate, b_gate,
           w_out, ln1_g, ln1_b, w_rg, b_rg, w_re, b_re, w_eg, w_eu, w_ed, ln2_g, ln2_b):
    assert x.shape == (BATCH, SEQ, D_MODEL) and w_ada.shape[0] == 1
    l = 0
    x2 = x.reshape(TOKENS, D_MODEL)

    c_pad = jnp.zeros((ADA_ROWS, D_MODEL), F32).at[:BATCH].set(c)
    mod = _ada(c_pad, w_ada[l], b_ada[l][None, :])[:BATCH].reshape(BATCH, 6, D_MODEL)

    w_in_t = jnp.swapaxes(w_in[l], 0, 1)
    w_if_t = jnp.zeros((LANES, D_MODEL), F32).at[:2 * M_HEADS].set(w_in_t[N_IN_MAIN:])
    p, gates, u = _inproj(x2, mod, w_in_t, w_if_t)
    g = _gateproj(u, w_gate[l], b_gate[l][None, :])

    cos_t, sin_t = _rope_tables()
    ya = _attention(p, cos_t, sin_t)

    gcol = gates[:, :2 * M_HEADS].reshape(BATCH, SEQ, 2 * M_HEADS)
    grow = jnp.transpose(gcol, (0, 2, 1))
    ym = _mlstm(p, gcol, grow, b_mgate[l][None, :], b_mgate[l][:, None], conv_w[l], conv_b[l][None, :],
                m_norm_g[l][None, :])

    wr = (jnp.zeros((D_MODEL, LANES), F32)
          .at[:, :N_EXPERT_GROUPS].set(w_rg[l])
          .at[:, N_EXPERT_GROUPS:N_EXPERT_GROUPS + N_EXPERTS].set(w_re[l])).astype(BF16)
    br = (jnp.zeros((1, LANES), F32)
          .at[0, :N_EXPERT_GROUPS].set(b_rg[l])
          .at[0, N_EXPERT_GROUPS:N_EXPERT_GROUPS + N_EXPERTS].set(b_re[l]))
    x1, u2, rt = _merge(ya, ym, g, x2, mod,
                        w_proj_a[l].astype(BF16), w_proj_m[l].astype(BF16), w_out[l].astype(BF16),
                        ln1_g[l][None, :], ln1_b[l][None, :], wr, br)

    row_token, meta, pos = _routing_tables(rt)

    mo = _moe(meta, row_token, u2,
              w_eg[l].reshape(N_EXPERTS, D_MODEL, D_FF_EXPERT), w_eu[l].reshape(N_EXPERTS, D_MODEL, D_FF_EXPERT),
              w_ed[l].reshape(N_EXPERTS, D_FF_EXPERT, D_MODEL))

    nt = TOKENS // FINAL_TB
    pos_tiles = jnp.transpose(pos.reshape(nt, FINAL_TB, 2), (0, 2, 1)).reshape(-1)
    y = _final(pos_tiles, mo, x1, rt, mod, ln2_g[l][None, :], ln2_b[l][None, :])
    return y.reshape(BATCH, SEQ, D_MODEL)
```

```python
import functools

import jax
import jax.numpy as jnp
from jax import lax
from jax.experimental import pallas as pl
from jax.experimental.pallas import tpu as pltpu

F32 = jnp.float32
BF16 = jnp.bfloat16

D_MODEL = 2048
BATCH = 4
SEQ = 2048
TOKENS = BATCH * SEQ
DIL_CONFIGS = ((128, 1), (512, 4), (2048, 16))
N_DIL_GROUPS = 3
HEADS_PER_GROUP = 4
HEAD_DIM_A = 128
ATT_BLOCK = 128
ROPE_THETA = 10000.0
A_GROUP_W = HEADS_PER_GROUP * HEAD_DIM_A
A_QKV_W = N_DIL_GROUPS * A_GROUP_W
M_HEADS = 4
M_HEAD_DIM = 256
M_W = M_HEADS * M_HEAD_DIM
M_CHUNK = 128
CONV_K = 4
N_IN_MAIN = 3 * A_QKV_W + 4 * M_W
N_EXPERT_GROUPS = 4
EXPERTS_PER_GROUP = 8
N_EXPERTS = N_EXPERT_GROUPS * EXPERTS_PER_GROUP
D_FF_EXPERT = 1024
DEEPNORM_ALPHA = 2.0 ** 0.25
LN_EPS = 1e-5

LANES = 128
SUBLANES = 8
VMEM_LIMIT_BYTES = 56 * 1024 * 1024

PROJ_TN = 256
GATE_TN = 1024
PROJ_TM = 1024
IN_TM = 2048
P_TILES_IN = N_IN_MAIN // PROJ_TN
P_ATT_TILES = 3 * A_QKV_W // PROJ_TN
P_M_TILES = P_TILES_IN - P_ATT_TILES
P_WIDTH = N_IN_MAIN
P_ATT_OFF = P_M_TILES * PROJ_TN
P_QM_BLK, P_KM_BLK, P_VM_BLK, P_OM_BLK = 0, 1, 2, 3
MERGE_TM = 256
MERGE_SUB = 256
MOE_TM = 256
MOE_ROWS = 2 * TOKENS + N_EXPERTS * MOE_TM
MOE_TILES = MOE_ROWS // MOE_TM
MOE_ROW_SLOTS = 3
FINAL_TB = 256


def _cparams(sem, vmem=VMEM_LIMIT_BYTES):
    return pltpu.CompilerParams(dimension_semantics=sem, vmem_limit_bytes=vmem)


def _normalize(x):
    mu = jnp.mean(x, axis=-1, keepdims=True)
    xc = x - mu
    var = jnp.mean(xc * xc, axis=-1, keepdims=True)
    return xc * lax.rsqrt(var + LN_EPS)


def _silu(x):
    return x * jax.nn.sigmoid(x)


def _log_sigmoid(x):
    return jnp.minimum(x, 0.0) - jnp.log(1.0 + jnp.exp(-jnp.abs(x)))


_CONTRACT_LAST = (((1,), (1,)), ((), ()))
LOG2E = 1.4426950408889634


ADA_TN = 1024
ADA_ROWS = 16


def _ada_kernel(c_ref, w_ref, b_ref, o_ref):
    sc = _silu(c_ref[...]).astype(BF16)
    o_ref[...] = jnp.dot(sc, w_ref[...].astype(BF16), preferred_element_type=F32) + b_ref[...]


def _ada(c_pad, w_ada, b_ada):
    n = w_ada.shape[1]
    return pl.pallas_call(
        _ada_kernel,
        out_shape=jax.ShapeDtypeStruct((ADA_ROWS, n), F32),
        grid=(n // ADA_TN,),
        in_specs=[
            pl.BlockSpec((ADA_ROWS, D_MODEL), lambda j: (0, 0)),
            pl.BlockSpec((D_MODEL, ADA_TN), lambda j: (0, j)),
            pl.BlockSpec((1, ADA_TN), lambda j: (0, j)),
        ],
        out_specs=pl.BlockSpec((ADA_ROWS, ADA_TN), lambda j: (0, j)),
        compiler_params=_cparams(("arbitrary",)),
        name="ada_mod",
    )(c_pad, w_ada, b_ada)


LN_CHUNK = 256


def _inproj_kernel(x_ref, mod_ref, w_ref, wif_ref, p_ref, g_ref, u_ref, xbuf, xsem):
    m = pl.program_id(0)
    n = pl.program_id(1)

    def x_copy(tile):
        return pltpu.make_async_copy(x_ref.at[pl.ds(pl.multiple_of(tile * IN_TM, IN_TM), IN_TM)], xbuf, xsem)

    @pl.when((m == 0) & (n == 0))
    def _():
        x_copy(0).start()

    @pl.when((n == 1) & (m + 1 < pl.num_programs(0)))
    def _():
        x_copy(m + 1).start()

    @pl.when(n == 0)
    def _():
        x_copy(m).wait()
        shift = mod_ref[0:1, :]
        scale = 1.0 + mod_ref[1:2, :]
        wif = wif_ref[...].astype(BF16)
        w = w_ref[...].astype(BF16)
        for ci in range(IN_TM // LN_CHUNK):
            rows = slice(ci * LN_CHUNK, (ci + 1) * LN_CHUNK)
            u = (_normalize(xbuf[rows, :]) * scale + shift).astype(BF16)
            u_ref[rows, :] = u
            g_ref[rows, :] = lax.dot_general(u, wif, _CONTRACT_LAST, preferred_element_type=F32)
            p_ref[rows, :] = lax.dot_general(u, w, _CONTRACT_LAST, preferred_element_type=F32)

    @pl.when(n > 0)
    def _():
        p_ref[...] = lax.dot_general(u_ref[...], w_ref[...].astype(BF16), _CONTRACT_LAST,
                                     preferred_element_type=F32)


def _inproj(x2, mod, w_in_t, w_if_t):
    tiles_per_batch = SEQ // IN_TM
    return pl.pallas_call(
        _inproj_kernel,
        out_shape=(
            jax.ShapeDtypeStruct((TOKENS, P_WIDTH), F32),
            jax.ShapeDtypeStruct((TOKENS, LANES), F32),
            jax.ShapeDtypeStruct((TOKENS, D_MODEL), BF16),
        ),
        grid=(TOKENS // IN_TM, P_TILES_IN),
        in_specs=[
            pl.BlockSpec(memory_space=pl.ANY),
            pl.BlockSpec((None, 6, D_MODEL), lambda m, n: (m // tiles_per_batch, 0, 0)),
            pl.BlockSpec((PROJ_TN, D_MODEL), lambda m, n: (n, 0)),
            pl.BlockSpec((LANES, D_MODEL), lambda m, n: (0, 0)),
        ],
        out_specs=(
            pl.BlockSpec((IN_TM, PROJ_TN),
                         lambda m, n: (m, jnp.where(n < P_ATT_TILES, n + P_M_TILES, n - P_ATT_TILES))),
            pl.BlockSpec((IN_TM, LANES), lambda m, n: (m, 0)),
            pl.BlockSpec((IN_TM, D_MODEL), lambda m, n: (m, 0)),
        ),
        scratch_shapes=[pltpu.VMEM((IN_TM, D_MODEL), F32), pltpu.SemaphoreType.DMA],
        compiler_params=_cparams(("arbitrary", "arbitrary")),
        name="in_proj",
    )(x2, mod, w_in_t, w_if_t)


def _gateproj_kernel(u_ref, w_ref, b_ref, o_ref):
    acc = jnp.dot(u_ref[...], w_ref[...].astype(BF16), preferred_element_type=F32)
    o_ref[...] = jax.nn.sigmoid(acc + b_ref[...]).astype(BF16)


def _gateproj(u, w_gate, b_gate):
    n = w_gate.shape[1]
    return pl.pallas_call(
        _gateproj_kernel,
        out_shape=jax.ShapeDtypeStruct((TOKENS, n), BF16),
        grid=(TOKENS // PROJ_TM, n // GATE_TN),
        in_specs=[
            pl.BlockSpec((PROJ_TM, D_MODEL), lambda m, j: (m, 0)),
            pl.BlockSpec((D_MODEL, GATE_TN), lambda m, j: (0, j)),
            pl.BlockSpec((1, GATE_TN), lambda m, j: (0, j)),
        ],
        out_specs=pl.BlockSpec((PROJ_TM, GATE_TN), lambda m, j: (m, j)),
        compiler_params=_cparams(("arbitrary", "arbitrary")),
        name="gate_proj",
    )(u, w_gate, b_gate)


ROPE_ROWS = 256
ATT_LOCKSTEP = 8


def _rows(start, size, stride):
    return pl.ds(start, size) if stride == 1 else pl.ds(start, size, stride=stride)


def _attn_kernel(q0, k0, v0, q1, k1, v1, q2, k2, v2, cos_ref, sin_ref, y_ref,
                 qr_sc, kr_sc, o0, o1, o2, l0, l1, l2):
    scale = HEAD_DIM_A ** -0.5
    blk = ATT_BLOCK
    qi2 = lax.broadcasted_iota(jnp.int32, (blk, 2 * blk), 0)
    kc2 = lax.broadcasted_iota(jnp.int32, (blk, 2 * blk), 1)
    mask_prev_cur = (kc2 >= qi2) & (kc2 <= qi2 + blk)
    qi1 = lax.broadcasted_iota(jnp.int32, (blk, blk), 0)
    kc1 = lax.broadcasted_iota(jnp.int32, (blk, blk), 1)
    mask_cur = kc1 <= qi1
    half = HEAD_DIM_A // 2
    groups = ((q0, k0, v0, o0, l0), (q1, k1, v1, o1, l1), (q2, k2, v2, o2, l2))
    for (window, d), (q_ref, k_ref, v_ref, o_sc, l_sc) in zip(DIL_CONFIGS, groups):
        nb = SEQ // d // blk
        for src_ref, dst_sc in ((q_ref, qr_sc), (k_ref, kr_sc)):
            for c0 in range(0, SEQ, ROPE_ROWS):
                rs = slice(c0, c0 + ROPE_ROWS)
                xr = src_ref[rs, :]
                dst_sc[rs, :] = xr * cos_ref[rs, :] + pltpu.roll(xr, half, 1) * sin_ref[rs, :]
        blocks = [(r, j) for r in range(d) for j in range(nb)]
        for b0 in range(0, len(blocks), ATT_LOCKSTEP):
            batch = blocks[b0:b0 + ATT_LOCKSTEP]
            cur, keys, mask = [], [], []
            for r, j in batch:
                start = r + j * blk * d
                cur.append(_rows(start, blk, d))
                keys.append(cur[-1] if j == 0 else _rows(start - blk * d, 2 * blk, d))
                mask.append(mask_cur if j == 0 else mask_prev_cur)
            nbb = range(len(batch))
            qb = [qr_sc[cur[i], :].astype(BF16) for i in nbb]
            kw = [kr_sc[keys[i], :].astype(BF16) for i in nbb]
            s = [lax.dot_general(qb[i], kw[i], _CONTRACT_LAST, preferred_element_type=F32) for i in nbb]
            s = [jnp.where(mask[i], s[i], -jnp.inf) for i in nbb]
            mx = [jnp.max(s[i], axis=-1, keepdims=True) for i in nbb]
            p = [jnp.exp2((s[i] - mx[i]) * (scale * LOG2E)).astype(BF16) for i in nbb]
            vb = [v_ref[keys[i], :].astype(BF16) for i in nbb]
            pv = [jnp.dot(p[i], jnp.concatenate([vb[i], jnp.ones_like(vb[i])], axis=1),
                          preferred_element_type=F32) for i in nbb]
            for i in nbb:
                den = pv[i][:, HEAD_DIM_A:]
                o_sc[cur[i], :] = pv[i][:, :HEAD_DIM_A] / den
                l_sc[cur[i], :] = mx[i] * scale + jnp.log(den)
    for c0 in range(0, SEQ, ROPE_ROWS):
        rs = slice(c0, c0 + ROPE_ROWS)
        la = l0[rs, :]
        lb = l1[rs, :]
        lc = l2[rs, :]
        mx = jnp.maximum(jnp.maximum(la, lb), lc)
        ea = jnp.exp(la - mx)
        eb = jnp.exp(lb - mx)
        ec = jnp.exp(lc - mx)
        den = ea + eb + ec
        y_ref[rs, :] = ((ea / den) * o0[rs, :] + (eb / den) * o1[rs, :] + (ec / den) * o2[rs, :]).astype(BF16)


def _attention(p, cos_t, sin_t):
    for window, d in DIL_CONFIGS:
        assert window // d == ATT_BLOCK and SEQ % (d * ATT_BLOCK) == 0
    col0 = P_ATT_OFF // HEAD_DIM_A

    def slab(part, g):
        off = col0 + (part * A_QKV_W + g * A_GROUP_W) // HEAD_DIM_A
        return pl.BlockSpec((SEQ, HEAD_DIM_A), lambda b, h: (b, off + h))

    in_specs = [slab(part, g) for g in range(N_DIL_GROUPS) for part in range(3)]
    table = pl.BlockSpec((SEQ, HEAD_DIM_A), lambda b, h: (0, 0))
    return pl.pallas_call(
        _attn_kernel,
        out_shape=jax.ShapeDtypeStruct((TOKENS, A_GROUP_W), BF16),
        grid=(BATCH, HEADS_PER_GROUP),
        in_specs=in_specs + [table, table],
        out_specs=pl.BlockSpec((SEQ, HEAD_DIM_A), lambda b, h: (b, h)),
        scratch_shapes=[pltpu.VMEM((SEQ, HEAD_DIM_A), F32)] * 8,
        compiler_params=_cparams(("arbitrary", "arbitrary")),
        name="dil_attn",
    )(*([p] * 9), cos_t, sin_t)


CONV_HALO = SUBLANES


def _mlstm_kernel(q_ref, k_ref, v_ref, og_ref, gc_ref, gr_ref, bc_ref, br_ref, cw_ref, cb_ref, ng_ref,
                  y_ref, xq_sc, xk_sc, ct_sc, n_sc, m_sc):
    c = pl.program_id(1)
    L = M_CHUNK
    lo = CONV_HALO

    @pl.when(c == 0)
    def _():
        xq_sc[0:lo, :] = jnp.zeros((lo, M_W), F32)
        xk_sc[0:lo, :] = jnp.zeros((lo, M_W), F32)
        ct_sc[...] = jnp.zeros_like(ct_sc)
        n_sc[...] = jnp.zeros_like(n_sc)
        m_sc[...] = jnp.zeros_like(m_sc)

    xq_sc[lo:lo + L, :] = q_ref[...]
    xk_sc[lo:lo + L, :] = k_ref[...]

    def conv(x_sc, w, b):
        acc = x_sc[lo:lo + L, :] * w[CONV_K - 1:CONV_K, :] + b
        for j in range(CONV_K - 1):
            off = lo - (CONV_K - 1) + j
            acc = acc + x_sc[off:off + L, :] * w[j:j + 1, :]
        return acc

    cw = cw_ref[...]
    cb = cb_ref[...]
    qc = _silu(conv(xq_sc, cw[:, :M_W], cb[:, :M_W]))
    kc = _silu(conv(xk_sc, cw[:, M_W:], cb[:, M_W:])) * (M_HEAD_DIM ** -0.5)
    xq_sc[0:lo, :] = xq_sc[L:L + lo, :]
    xk_sc[0:lo, :] = xk_sc[L:L + lo, :]

    row = lax.broadcasted_iota(jnp.int32, (L, L), 0)
    col = lax.broadcasted_iota(jnp.int32, (L, L), 1)
    causal = row >= col
    heads = range(M_HEADS)
    hsl = [slice(h * M_HEAD_DIM, (h + 1) * M_HEAD_DIM) for h in heads]
    ct = [ct_sc[h] for h in heads]
    n_row = [n_sc[h, 0:1, :] for h in heads]
    m_prev = [m_sc[h, 0:1, 0:1] for h in heads]
    q = [qc[:, hsl[h]] for h in heads]
    k = [kc[:, hsl[h]] for h in heads]
    qb = [q[h].astype(BF16) for h in heads]
    kb = [k[h].astype(BF16) for h in heads]
    vb = [v_ref[:, hsl[h]].astype(BF16) for h in heads]
    qk = [lax.dot_general(qb[h], kb[h], _CONTRACT_LAST, preferred_element_type=F32) for h in heads]
    qc_state = [jnp.dot(qb[h], ct[h].astype(BF16), preferred_element_type=F32) for h in heads]

    i_col = [gc_ref[:, h:h + 1] + bc_ref[:, h:h + 1] for h in heads]
    lf_col = [_log_sigmoid(gc_ref[:, M_HEADS + h:M_HEADS + h + 1] + bc_ref[:, M_HEADS + h:M_HEADS + h + 1])
              for h in heads]
    i_row = [gr_ref[h:h + 1, :] + br_ref[h:h + 1, :] for h in heads]
    lf_row = [_log_sigmoid(gr_ref[M_HEADS + h:M_HEADS + h + 1, :] + br_ref[M_HEADS + h:M_HEADS + h + 1, :])
              for h in heads]
    b_col = [jnp.sum(jnp.where(causal, lf_row[h], 0.0), axis=1, keepdims=True) for h in heads]
    b_row = [jnp.sum(jnp.where(row <= col, lf_col[h], 0.0), axis=0, keepdims=True) for h in heads]
    log_d = [jnp.where(causal, b_col[h] - b_row[h] + i_row[h], -jnp.inf) for h in heads]
    log_inter = [b_col[h] + m_prev[h] for h in heads]
    m_t = [jnp.maximum(jnp.max(log_d[h], axis=1, keepdims=True), log_inter[h]) for h in heads]
    s = [qk[h] * jnp.exp(log_d[h] - m_t[h]) for h in heads]
    inter = [jnp.exp(log_inter[h] - m_t[h]) for h in heads]
    sv = [jnp.dot(s[h].astype(BF16), vb[h], preferred_element_type=F32) for h in heads]

    m_new = [m_t[h][L - 1:L, :] for h in heads]
    b_last = [b_col[h][L - 1:L, :] for h in heads]
    w_col = [jnp.exp(b_last[h] - b_col[h] + i_col[h] - m_new[h]) for h in heads]
    decay = [jnp.exp(b_last[h] + m_prev[h] - m_new[h]) for h in heads]
    kw = [k[h] * w_col[h] for h in heads]
    kv = [lax.dot_general(kw[h].astype(BF16), vb[h], (((0,), (0,)), ((), ())), preferred_element_type=F32)
          for h in heads]

    num = [sv[h] + inter[h] * qc_state[h] for h in heads]
    den = [jnp.sum(s[h], axis=1, keepdims=True) + inter[h] * jnp.sum(q[h] * n_row[h], axis=1, keepdims=True)
           for h in heads]
    hh = [num[h] / jnp.maximum(jnp.abs(den[h]), jnp.exp(-m_t[h])) for h in heads]
    z = [jax.nn.sigmoid(og_ref[:, hsl[h]]) * hh[h] for h in heads]
    y = [(_normalize(z[h]) * ng_ref[:, hsl[h]]).astype(BF16) for h in heads]
    ct_new = [decay[h] * ct[h] + kv[h] for h in heads]
    n_new = [decay[h] * n_row[h] + jnp.sum(kw[h], axis=0, keepdims=True) for h in heads]
    for h in heads:
        ct_sc[h] = ct_new[h]
        n_sc[h] = jnp.broadcast_to(n_new[h], (SUBLANES, M_HEAD_DIM))
        m_sc[h] = jnp.broadcast_to(m_new[h], (SUBLANES, LANES))
        y_ref[:, hsl[h]] = y[h]


def _mlstm(p, gcol, grow, bcol, brow, conv_w, conv_b, norm_g):
    nc = SEQ // M_CHUNK
    slab = lambda blk: pl.BlockSpec((M_CHUNK, M_W), lambda b, c: (b * nc + c, blk))
    return pl.pallas_call(
        _mlstm_kernel,
        out_shape=jax.ShapeDtypeStruct((TOKENS, M_W), BF16),
        grid=(BATCH, nc),
        in_specs=[
            slab(P_QM_BLK), slab(P_KM_BLK), slab(P_VM_BLK), slab(P_OM_BLK),
            pl.BlockSpec((None, M_CHUNK, 2 * M_HEADS), lambda b, c: (b, c, 0)),
            pl.BlockSpec((None, 2 * M_HEADS, M_CHUNK), lambda b, c: (b, 0, c)),
            pl.BlockSpec((1, 2 * M_HEADS), lambda b, c: (0, 0)),
            pl.BlockSpec((2 * M_HEADS, 1), lambda b, c: (0, 0)),
            pl.BlockSpec((CONV_K, 2 * M_W), lambda b, c: (0, 0)),
            pl.BlockSpec((1, 2 * M_W), lambda b, c: (0, 0)),
            pl.BlockSpec((1, M_W), lambda b, c: (0, 0)),
        ],
        out_specs=pl.BlockSpec((M_CHUNK, M_W), lambda b, c: (b * nc + c, 0)),
        scratch_shapes=[
            pltpu.VMEM((M_CHUNK + CONV_HALO, M_W), F32),
            pltpu.VMEM((M_CHUNK + CONV_HALO, M_W), F32),
            pltpu.VMEM((M_HEADS, M_HEAD_DIM, M_HEAD_DIM), F32),
            pltpu.VMEM((M_HEADS, SUBLANES, M_HEAD_DIM), F32),
            pltpu.VMEM((M_HEADS, SUBLANES, LANES), F32),
        ],
        compiler_params=_cparams(("arbitrary", "arbitrary")),
        name="mlstm",
    )(p, p, p, p, gcol, grow, bcol, brow, conv_w, conv_b, norm_g)


RT_E1, RT_E2, RT_W1, RT_W2 = 0, 1, 2, 3


def _route(logits):
    lane = lax.broadcasted_iota(jnp.int32, logits.shape, 1).astype(F32)
    big = float(LANES)
    is_g = lane < N_EXPERT_GROUPS
    gl = jnp.where(is_g, logits, -jnp.inf)
    gexp = jnp.exp(gl - jnp.max(gl, axis=1, keepdims=True))
    gprob = gexp / jnp.sum(gexp, axis=1, keepdims=True)
    g_w = jnp.max(gprob, axis=1, keepdims=True)
    g_top = jnp.min(jnp.where(is_g & (gprob == g_w), lane, big), axis=1, keepdims=True)
    lo = N_EXPERT_GROUPS + EXPERTS_PER_GROUP * g_top
    in_grp = (lane >= lo) & (lane < lo + EXPERTS_PER_GROUP)
    el = jnp.where(in_grp, logits, -jnp.inf)
    eexp = jnp.exp(el - jnp.max(el, axis=1, keepdims=True))
    eprob = eexp / jnp.sum(eexp, axis=1, keepdims=True)
    v1 = jnp.max(eprob, axis=1, keepdims=True)
    i1 = jnp.min(jnp.where(in_grp & (eprob == v1), lane, big), axis=1, keepdims=True)
    rest = jnp.where(in_grp & (lane != i1), eprob, -1.0)
    v2 = jnp.max(rest, axis=1, keepdims=True)
    i2 = jnp.min(jnp.where(rest == v2, lane, big), axis=1, keepdims=True)
    tot = v1 + v2
    w1 = g_w * (v1 / tot)
    w2 = g_w * (v2 / tot)
    e1 = i1 - N_EXPERT_GROUPS
    e2 = i2 - N_EXPERT_GROUPS
    rec = jnp.where(lane == RT_E1, e1, jnp.where(lane == RT_E2, e2, jnp.where(lane == RT_W1, w1, w2)))
    return jnp.where(lane <= RT_W2, rec, 0.0)


def _merge_kernel(ya_ref, ym_ref, g_ref, x_ref, mod_ref,
                  wpa_ref, wpm_ref, wout_ref, lng_ref, lnb_ref, wr_ref, br_ref,
                  x1_ref, u2_ref, rt_ref):
    for sb in range(MERGE_TM // MERGE_SUB):
        rows = slice(sb * MERGE_SUB, (sb + 1) * MERGE_SUB)
        pa = jnp.dot(ya_ref[rows, :], wpa_ref[...], preferred_element_type=F32)
        pm = jnp.dot(ym_ref[rows, :], wpm_ref[...], preferred_element_type=F32)
        merged = g_ref[rows, :D_MODEL].astype(F32) * pa + g_ref[rows, D_MODEL:].astype(F32) * pm
        mix = jnp.dot(merged.astype(BF16), wout_ref[...], preferred_element_type=F32)
        z = DEEPNORM_ALPHA * x_ref[rows, :] + mod_ref[2:3, :] * mix
        x1 = _normalize(z) * lng_ref[...] + lnb_ref[...]
        x1_ref[rows, :] = x1
        u2 = _normalize(x1) * (1.0 + mod_ref[4:5, :]) + mod_ref[3:4, :]
        u2_ref[rows, :] = u2
        logits = jnp.dot(u2.astype(BF16), wr_ref[...], preferred_element_type=F32) + br_ref[...]
        rt_ref[rows, :] = _route(logits)


def _merge(ya, ym, g, x2, mod, wpa, wpm, wout, lng, lnb, wr, br):
    tm = MERGE_TM
    tiles_per_batch = SEQ // tm
    rowblk = lambda w: pl.BlockSpec((tm, w), lambda m: (m, 0))
    const = lambda shape: pl.BlockSpec(shape, lambda m: (0,) * len(shape), pipeline_mode=pl.Buffered(1))
    return pl.pallas_call(
        _merge_kernel,
        out_shape=(
            jax.ShapeDtypeStruct((TOKENS, D_MODEL), F32),
            jax.ShapeDtypeStruct((TOKENS, D_MODEL), F32),
            jax.ShapeDtypeStruct((TOKENS, LANES), F32),
        ),
        grid=(TOKENS // tm,),
        in_specs=[
            rowblk(A_GROUP_W), rowblk(M_W), rowblk(2 * D_MODEL), rowblk(D_MODEL),
            pl.BlockSpec((None, 6, D_MODEL), lambda m: (m // tiles_per_batch, 0, 0)),
            const((A_GROUP_W, D_MODEL)), const((M_W, D_MODEL)), const((D_MODEL, D_MODEL)),
            const((1, D_MODEL)), const((1, D_MODEL)),
            const((D_MODEL, LANES)), const((1, LANES)),
        ],
        out_specs=(rowblk(D_MODEL), rowblk(D_MODEL), rowblk(LANES)),
        compiler_params=_cparams(("arbitrary",)),
        name="merge_ln1_route",
    )(ya, ym, g, x2, mod, wpa, wpm, wout, lng, lnb, wr, br)


CAST_ROWS = 128


def _issue_rows(src_ref, idx_ref, base, buf, slot, sem, nrows):
    def body(blk, carry):
        for j in range(SUBLANES):
            row = idx_ref[base + blk * SUBLANES + j]
            src = src_ref.at[lax.shift_right_logical(row, 3), pl.ds(row & (SUBLANES - 1), 1)]
            pltpu.make_async_copy(src, buf.at[slot, blk, pl.ds(j, 1)], sem.at[slot]).start(priority=j % 2)
        return carry

    lax.fori_loop(0, nrows // SUBLANES, body, 0)


def _wait_rows(src_ref, buf, slot, sem, nrows):
    groups = nrows // SUBLANES
    pltpu.make_async_copy(src_ref.at[pl.ds(0, groups)], buf.at[slot, pl.ds(0, groups)], sem.at[slot]).wait()


class _TileTable:
    def __init__(self, meta_ref):
        self.ref = meta_ref

    def expert(self, t):
        return self.ref[META_TE * LANES + t]

    def next_expert(self, t):
        return self.ref[META_TN * LANES + t]

    def valid_rows(self, t):
        return self.ref[META_TV * LANES + t]

    def tiles_used(self):
        return self.ref[META_NU * LANES]


def _expert_changed(te_ref, i):
    return (i == 0) | (te_ref.expert(i) != te_ref.expert(jnp.maximum(i - 1, 0)))


def _moe_kernel(meta_ref, tok_ref, u_ref, wg_ref, wu_ref, wd_ref, o_ref,
                stg, stu, std, wgb, wub, wdb, xbuf, wsem, xsem):
    i = pl.program_id(0)
    te_ref = _TileTable(meta_ref)
    nu = te_ref.tiles_used()
    tm = o_ref.shape[0]
    half = tm // 2

    def weight_copies(e):
        return (pltpu.make_async_copy(wg_ref.at[e], stg, wsem.at[0]),
                pltpu.make_async_copy(wu_ref.at[e], stu, wsem.at[1]),
                pltpu.make_async_copy(wd_ref.at[e], std, wsem.at[2]))

    def for_tile_rows(t, fn):
        @pl.when(te_ref.valid_rows(t) <= half)
        def _():
            fn(half)

        @pl.when(te_ref.valid_rows(t) > half)
        def _():
            fn(tm)

    def issue_tile(t):
        for_tile_rows(t, lambda nrows: _issue_rows(u_ref, tok_ref, t * tm, xbuf, t % MOE_ROW_SLOTS, xsem, nrows))

    @pl.when(i == 0)
    def _():
        issue_tile(0)

        @pl.when(nu > 1)
        def _():
            issue_tile(1)

        for cp in weight_copies(te_ref.expert(0)):
            cp.start(priority=1)

    @pl.when(i < nu)
    def _():
        slot = i % MOE_ROW_SLOTS

        @pl.when(i + 2 < nu)
        def _():
            issue_tile(i + 2)

        @pl.when(_expert_changed(te_ref, i))
        def _():
            for cp in weight_copies(te_ref.expert(i)):
                cp.wait()
            for src, dst in ((stg, wgb), (stu, wub), (std, wdb)):
                def cast_rows(ci, carry, src=src, dst=dst):
                    r = pl.multiple_of(ci * CAST_ROWS, CAST_ROWS)
                    dst[pl.ds(r, CAST_ROWS), :] = src[pl.ds(r, CAST_ROWS), :].astype(BF16)
                    return carry

                lax.fori_loop(0, src.shape[0] // CAST_ROWS, cast_rows, 0)

            @pl.when(te_ref.next_expert(i) >= 0)
            def _():
                for cp in weight_copies(te_ref.next_expert(i)):
                    cp.start(priority=1)

        def compute(nrows):
            _wait_rows(u_ref, xbuf, slot, xsem, nrows)
            x = xbuf[slot, 0:nrows // SUBLANES].reshape(nrows, D_MODEL).astype(BF16)
            a = jnp.dot(x, wgb[...], preferred_element_type=F32)
            b = jnp.dot(x, wub[...], preferred_element_type=F32)
            h = (_silu(a) * b).astype(BF16)
            o_ref[0:nrows, :] = jnp.dot(h, wdb[...], preferred_element_type=F32)
            if nrows < tm:
                o_ref[nrows:tm, :] = jnp.zeros((tm - nrows, D_MODEL), F32)

        for_tile_rows(i, compute)

    @pl.when(i >= nu)
    def _():
        o_ref[...] = jnp.zeros_like(o_ref)


def _moe(meta, row_token, u2, w_eg, w_eu, w_ed):
    tm = MOE_TM
    return pl.pallas_call(
        _moe_kernel,
        out_shape=jax.ShapeDtypeStruct((MOE_ROWS, D_MODEL), F32),
        grid_spec=pltpu.PrefetchScalarGridSpec(
            num_scalar_prefetch=2,
            grid=(MOE_TILES,),
            in_specs=[pl.BlockSpec(memory_space=pl.ANY)] * 4,
            out_specs=pl.BlockSpec((tm, D_MODEL), lambda i, meta, tok: (i, 0)),
            scratch_shapes=[
                pltpu.VMEM((D_MODEL, D_FF_EXPERT), F32),
                pltpu.VMEM((D_MODEL, D_FF_EXPERT), F32),
                pltpu.VMEM((D_FF_EXPERT, D_MODEL), F32),
                pltpu.VMEM((D_MODEL, D_FF_EXPERT), BF16),
                pltpu.VMEM((D_MODEL, D_FF_EXPERT), BF16),
                pltpu.VMEM((D_FF_EXPERT, D_MODEL), BF16),
                pltpu.VMEM((MOE_ROW_SLOTS, tm // SUBLANES, SUBLANES, D_MODEL), F32),
                pltpu.SemaphoreType.DMA((3,)),
                pltpu.SemaphoreType.DMA((MOE_ROW_SLOTS,)),
            ],
        ),
        compiler_params=_cparams(("arbitrary",)),
        name="moe_experts",
    )(meta, row_token, u2.reshape(TOKENS // SUBLANES, SUBLANES, D_MODEL), w_eg, w_eu, w_ed)


def _final_kernel(pos_ref, o_ref, x1_ref, rt_ref, mod_ref, lng_ref, lnb_ref, y_ref, buf, sem):
    i = pl.program_id(0)
    tb = x1_ref.shape[0]
    slot = i % 2

    @pl.when(i == 0)
    def _():
        _issue_rows(o_ref, pos_ref, 0, buf, 0, sem, 2 * tb)

    @pl.when(i + 1 < pl.num_programs(0))
    def _():
        _issue_rows(o_ref, pos_ref, (i + 1) * (2 * tb), buf, 1 - slot, sem, 2 * tb)

    _wait_rows(o_ref, buf, slot, sem, 2 * tb)
    rows = buf[slot].reshape(2 * tb, D_MODEL)
    ffn = rt_ref[:, RT_W1:RT_W1 + 1] * rows[0:tb, :] + rt_ref[:, RT_W2:RT_W2 + 1] * rows[tb:2 * tb, :]
    z = DEEPNORM_ALPHA * x1_ref[...] + mod_ref[5:6, :] * ffn
    y_ref[...] = _normalize(z) * lng_ref[...] + lnb_ref[...]


def _final(pos_tiles, moe_out, x1, rt, mod, lng, lnb):
    tb = FINAL_TB
    tiles_per_batch = SEQ // tb
    return pl.pallas_call(
        _final_kernel,
        out_shape=jax.ShapeDtypeStruct((TOKENS, D_MODEL), F32),
        grid_spec=pltpu.PrefetchScalarGridSpec(
            num_scalar_prefetch=1,
            grid=(TOKENS // tb,),
            in_specs=[
                pl.BlockSpec(memory_space=pl.ANY),
                pl.BlockSpec((tb, D_MODEL), lambda i, pos: (i, 0)),
                pl.BlockSpec((tb, LANES), lambda i, pos: (i, 0)),
                pl.BlockSpec((None, 6, D_MODEL), lambda i, pos: (i // tiles_per_batch, 0, 0)),
                pl.BlockSpec((1, D_MODEL), lambda i, pos: (0, 0)),
                pl.BlockSpec((1, D_MODEL), lambda i, pos: (0, 0)),
            ],
            out_specs=pl.BlockSpec((tb, D_MODEL), lambda i, pos: (i, 0)),
            scratch_shapes=[pltpu.VMEM((2, 2 * tb // SUBLANES, SUBLANES, D_MODEL), F32),
                            pltpu.SemaphoreType.DMA((2,))],
        ),
        compiler_params=_cparams(("arbitrary",)),
        name="combine_ln2",
    )(pos_tiles, moe_out.reshape(MOE_ROWS // SUBLANES, SUBLANES, D_MODEL), x1, rt, mod, lng, lnb)


ROUTE_BLK = 256
META_TE, META_TN, META_TV, META_NU = 0, 1, 2, 3


def _lane_cumsum(x, lane):
    s = 1
    while s < LANES:
        x = x + jnp.where(lane >= s, pltpu.roll(x, s, 1), 0.0)
        s *= 2
    return x


def _route_tables_kernel(rt_ref, pos_ref, meta_ref, rank_sc):
    tm = float(MOE_TM)
    nblk = TOKENS // ROUTE_BLK
    lane_i = lax.broadcasted_iota(jnp.int32, (ROUTE_BLK, LANES), 1)
    lane_f = lane_i.astype(F32)
    earlier = (lax.broadcasted_iota(jnp.int32, (ROUTE_BLK, ROUTE_BLK), 0)
               > lax.broadcasted_iota(jnp.int32, (ROUTE_BLK, ROUTE_BLK), 1)).astype(BF16)

    def onehots(b):
        blk = rt_ref[pl.ds(pl.multiple_of(b * ROUTE_BLK, ROUTE_BLK), ROUTE_BLK), :]
        return blk[:, RT_E1:RT_E1 + 1] == lane_f, blk[:, RT_E2:RT_E2 + 1] == lane_f

    def count_pass(b, carry):
        c1, c2 = carry
        o1, o2 = onehots(b)
        f1 = jnp.where(o1, 1.0, 0.0)
        f2 = jnp.where(o2, 1.0, 0.0)
        p1 = jnp.dot(earlier, f1.astype(BF16), preferred_element_type=F32) + c1
        p2 = jnp.dot(earlier, f2.astype(BF16), preferred_element_type=F32) + c2
        r1 = jnp.sum(jnp.where(o1, p1, 0.0), axis=1, keepdims=True)
        r2 = jnp.sum(jnp.where(o2, p2, 0.0), axis=1, keepdims=True)
        rank_sc[pl.ds(pl.multiple_of(b * ROUTE_BLK, ROUTE_BLK), ROUTE_BLK), :] = jnp.where(
            lane_i == 0, r1, jnp.where(lane_i == 1, r2, 0.0))
        return c1 + jnp.sum(f1, axis=0, keepdims=True), c2 + jnp.sum(f2, axis=0, keepdims=True)

    zero = jnp.zeros((1, LANES), F32)
    c1, c2 = lax.fori_loop(0, nblk, count_pass, (zero, zero))

    lane8 = lax.broadcasted_iota(jnp.int32, (SUBLANES, LANES), 1)
    counts = jnp.broadcast_to(c1 + c2, (SUBLANES, LANES))
    padded = jnp.floor((counts + (tm - 1.0)) * (1.0 / tm)) * tm
    pend = _lane_cumsum(padded, lane8)
    pstart = pend - padded
    start1 = pstart[0:1, :]
    start2 = start1 + c1

    def place_pass(b, carry):
        o1, o2 = onehots(b)
        rows = pl.ds(pl.multiple_of(b * ROUTE_BLK, ROUTE_BLK), ROUTE_BLK)
        rk = rank_sc[rows, :]
        d1 = jnp.sum(jnp.where(o1, start1, 0.0), axis=1, keepdims=True) + rk[:, 0:1]
        d2 = jnp.sum(jnp.where(o2, start2, 0.0), axis=1, keepdims=True) + rk[:, 1:2]
        pos_ref[rows, :] = jnp.where(lane_i == 0, d1, jnp.where(lane_i == 1, d2, 0.0)).astype(jnp.int32)
        return carry

    lax.fori_loop(0, nblk, place_pass, 0)

    tile = lax.broadcasted_iota(jnp.int32, (LANES, LANES), 0).astype(F32)
    lane = lax.broadcasted_iota(jnp.int32, (LANES, LANES), 1)
    expert_lane = lane < N_EXPERTS
    pend_b = jnp.broadcast_to(pend[0:1, :], (LANES, LANES))
    pstart_b = jnp.broadcast_to(pstart[0:1, :], (LANES, LANES))
    counts_b = jnp.broadcast_to(counts[0:1, :], (LANES, LANES))
    n_used = jnp.sum(jnp.where(lane == N_EXPERTS - 1, pend_b, 0.0), axis=1, keepdims=True) * (1.0 / tm)
    te = jnp.sum(jnp.where(expert_lane & (pend_b <= tile * tm), 1.0, 0.0), axis=1, keepdims=True)
    te = jnp.minimum(te, float(N_EXPERTS - 1))
    te_last = jnp.sum(jnp.where(tile[:, 0:1] == n_used - 1.0, te, 0.0), axis=0, keepdims=True)
    te = jnp.where(tile[:, 0:1] < n_used, te, te_last)
    of_tile = lane.astype(F32) == te
    pend_te = jnp.sum(jnp.where(of_tile, pend_b, 0.0), axis=1, keepdims=True)
    last_row = jnp.sum(jnp.where(of_tile, pstart_b + counts_b, 0.0), axis=1, keepdims=True)
    tile_rows = jnp.clip(last_row - tile[:, 0:1] * tm, 0.0, tm)
    next_run = pend_te * (1.0 / tm)
    te_by_lane = jnp.broadcast_to(te, (LANES, LANES)).T
    te_at_next = jnp.sum(jnp.where(lane.astype(F32) == next_run, te_by_lane, 0.0), axis=1, keepdims=True)
    te_next = jnp.where(next_run < n_used, te_at_next, -1.0)
    cols = jnp.where(lane == META_TE, te, jnp.where(lane == META_TN, te_next,
                     jnp.where(lane == META_TV, tile_rows, jnp.where(lane == META_NU, n_used, 0.0))))
    meta_ref[...] = cols.T[0:SUBLANES, :].astype(jnp.int32)


def _routing_tables(rt):
    pos, meta = pl.pallas_call(
        _route_tables_kernel,
        out_shape=(
            jax.ShapeDtypeStruct((TOKENS, LANES), jnp.int32),
            jax.ShapeDtypeStruct((SUBLANES, LANES), jnp.int32),
        ),
        scratch_shapes=[pltpu.VMEM((TOKENS, LANES), F32)],
        compiler_params=_cparams(None),
        name="route_tables",
    )(rt)
    pos = pos[:, 0:2]
    tok = jnp.arange(2 * TOKENS, dtype=jnp.int32) // 2
    row_token = jnp.zeros((MOE_ROWS,), jnp.int32).at[pos.reshape(-1)].set(tok)
    return row_token, meta.reshape(-1), pos


def _rope_tables():
    inv = ROPE_THETA ** (-jnp.arange(0, HEAD_DIM_A, 2, dtype=F32) / HEAD_DIM_A)
    ang = jnp.arange(SEQ, dtype=F32)[:, None] * inv[None, :]
    cos = jnp.cos(ang)
    sin = jnp.sin(ang)
    return jnp.concatenate([cos, cos], axis=-1), jnp.concatenate([-sin, sin], axis=-1)


def kernel(x, c, w_ada, b_ada, w_in, b_mgate, conv_w, conv_b, m_norm_g, w_proj_a, w_proj_m, w_gate, b_gate,
           w_out, ln1_g, ln1_b, w_rg, b_rg, w_re, b_re, w_eg, w_eu, w_ed, ln2_g, ln2_b):
    assert x.shape == (BATCH, SEQ, D_MODEL) and w_ada.shape[0] == 1
    l = 0
    x2 = x.reshape(TOKENS, D_MODEL)

    c_pad = jnp.zeros((ADA_ROWS, D_MODEL), F32).at[:BATCH].set(c)
    mod = _ada(c_pad, w_ada[l], b_ada[l][None, :])[:BATCH].reshape(BATCH, 6, D_MODEL)

    w_in_t = jnp.swapaxes(w_in[l], 0, 1)
    w_if_t = jnp.zeros((LANES, D_MODEL), F32).at[:2 * M_HEADS].set(w_in_t[N_IN_MAIN:])
    p, gates, u = _inproj(x2, mod, w_in_t, w_if_t)
    g = _gateproj(u, w_gate[l], b_gate[l][None, :])

    cos_t, sin_t = _rope_tables()
    ya = _attention(p, cos_t, sin_t)

    gcol = gates[:, :2 * M_HEADS].reshape(BATCH, SEQ, 2 * M_HEADS)
    grow = jnp.transpose(gcol, (0, 2, 1))
    ym = _mlstm(p, gcol, grow, b_mgate[l][None, :], b_mgate[l][:, None], conv_w[l], conv_b[l][None, :],
                m_norm_g[l][None, :])

    wr = (jnp.zeros((D_MODEL, LANES), F32)
          .at[:, :N_EXPERT_GROUPS].set(w_rg[l])
          .at[:, N_EXPERT_GROUPS:N_EXPERT_GROUPS + N_EXPERTS].set(w_re[l])).astype(BF16)
    br = (jnp.zeros((1, LANES), F32)
          .at[0, :N_EXPERT_GROUPS].set(b_rg[l])
          .at[0, N_EXPERT_GROUPS:N_EXPERT_GROUPS + N_EXPERTS].set(b_re[l]))
    x1, u2, rt = _merge(ya, ym, g, x2, mod,
                        w_proj_a[l].astype(BF16), w_proj_m[l].astype(BF16), w_out[l].astype(BF16),
                        ln1_g[l][None, :], ln1_b[l][None, :], wr, br)

    row_token, meta, pos = _routing_tables(rt)

    mo = _moe(meta, row_token, u2,
              w_eg[l].reshape(N_EXPERTS, D_MODEL, D_FF_EXPERT), w_eu[l].reshape(N_EXPERTS, D_MODEL, D_FF_EXPERT),
              w_ed[l].reshape(N_EXPERTS, D_FF_EXPERT, D_MODEL))

    nt = TOKENS // FINAL_TB
    pos_tiles = jnp.transpose(pos.reshape(nt, FINAL_TB, 2), (0, 2, 1)).reshape(-1)
    y = _final(pos_tiles, mo, x1, rt, mod, ln2_g[l][None, :], ln2_b[l][None, :])
    return y.reshape(BATCH, SEQ, D_MODEL)
```

```python
import functools

import jax
import jax.numpy as jnp
from jax import lax
from jax.experimental import pallas as pl
from jax.experimental.pallas import tpu as pltpu

F32 = jnp.float32
BF16 = jnp.bfloat16

D_MODEL = 2048
BATCH = 4
SEQ = 2048
TOKENS = BATCH * SEQ
DIL_CONFIGS = ((128, 1), (512, 4), (2048, 16))
N_DIL_GROUPS = 3
HEADS_PER_GROUP = 4
HEAD_DIM_A = 128
ATT_BLOCK = 128
ROPE_THETA = 10000.0
A_GROUP_W = HEADS_PER_GROUP * HEAD_DIM_A
A_QKV_W = N_DIL_GROUPS * A_GROUP_W
M_HEADS = 4
M_HEAD_DIM = 256
M_W = M_HEADS * M_HEAD_DIM
M_CHUNK = 128
CONV_K = 4
N_IN_MAIN = 3 * A_QKV_W + 4 * M_W
N_EXPERT_GROUPS = 4
EXPERTS_PER_GROUP = 8
N_EXPERTS = N_EXPERT_GROUPS * EXPERTS_PER_GROUP
D_FF_EXPERT = 1024
DEEPNORM_ALPHA = 2.0 ** 0.25
LN_EPS = 1e-5

LANES = 128
SUBLANES = 8
VMEM_LIMIT_BYTES = 56 * 1024 * 1024

PROJ_TN = 256
GATE_TN = 1024
PROJ_TM = 1024
IN_TM = 2048
P_TILES_IN = N_IN_MAIN // PROJ_TN
P_ATT_TILES = 3 * A_QKV_W // PROJ_TN
P_M_TILES = P_TILES_IN - P_ATT_TILES
P_WIDTH = N_IN_MAIN
P_ATT_OFF = P_M_TILES * PROJ_TN
P_QM_BLK, P_KM_BLK, P_VM_BLK, P_OM_BLK = 0, 1, 2, 3
MERGE_TM = 256
MERGE_SUB = 256
MOE_TM = 256
MOE_ROWS = 2 * TOKENS + N_EXPERTS * MOE_TM
MOE_TILES = MOE_ROWS // MOE_TM
MOE_ROW_SLOTS = 3
FINAL_TB = 256


def _cparams(sem, vmem=VMEM_LIMIT_BYTES):
    return pltpu.CompilerParams(dimension_semantics=sem, vmem_limit_bytes=vmem)


def _normalize(x):
    mu = jnp.mean(x, axis=-1, keepdims=True)
    xc = x - mu
    var = jnp.mean(xc * xc, axis=-1, keepdims=True)
    return xc * lax.rsqrt(var + LN_EPS)


def _silu(x):
    return x * jax.nn.sigmoid(x)


def _log_sigmoid(x):
    return jnp.minimum(x, 0.0) - jnp.log(1.0 + jnp.exp(-jnp.abs(x)))


_CONTRACT_LAST = (((1,), (1,)), ((), ()))
LOG2E = 1.4426950408889634


ADA_TN = 1024
ADA_ROWS = 16


def _ada_kernel(c_ref, w_ref, b_ref, o_ref):
    sc = _silu(c_ref[...]).astype(BF16)
    o_ref[...] = jnp.dot(sc, w_ref[...].astype(BF16), preferred_element_type=F32) + b_ref[...]


def _ada(c_pad, w_ada, b_ada):
    n = w_ada.shape[1]
    return pl.pallas_call(
        _ada_kernel,
        out_shape=jax.ShapeDtypeStruct((ADA_ROWS, n), F32),
        grid=(n // ADA_TN,),
        in_specs=[
            pl.BlockSpec((ADA_ROWS, D_MODEL), lambda j: (0, 0)),
            pl.BlockSpec((D_MODEL, ADA_TN), lambda j: (0, j)),
            pl.BlockSpec((1, ADA_TN), lambda j: (0, j)),
        ],
        out_specs=pl.BlockSpec((ADA_ROWS, ADA_TN), lambda j: (0, j)),
        compiler_params=_cparams(("arbitrary",)),
        name="ada_mod",
    )(c_pad, w_ada, b_ada)


LN_CHUNK = 256


def _inproj_kernel(x_ref, mod_ref, w_ref, wif_ref, p_ref, g_ref, u_ref, xbuf, xsem):
    m = pl.program_id(0)
    n = pl.program_id(1)

    def x_copy(tile):
        return pltpu.make_async_copy(x_ref.at[pl.ds(pl.multiple_of(tile * IN_TM, IN_TM), IN_TM)], xbuf, xsem)

    @pl.when((m == 0) & (n == 0))
    def _():
        x_copy(0).start()

    @pl.when((n == 1) & (m + 1 < pl.num_programs(0)))
    def _():
        x_copy(m + 1).start()

    @pl.when(n == 0)
    def _():
        x_copy(m).wait()
        shift = mod_ref[0:1, :]
        scale = 1.0 + mod_ref[1:2, :]
        wif = wif_ref[...].astype(BF16)
        w = w_ref[...].astype(BF16)
        for ci in range(IN_TM // LN_CHUNK):
            rows = slice(ci * LN_CHUNK, (ci + 1) * LN_CHUNK)
            u = (_normalize(xbuf[rows, :]) * scale + shift).astype(BF16)
            u_ref[rows, :] = u
            g_ref[rows, :] = lax.dot_general(u, wif, _CONTRACT_LAST, preferred_element_type=F32)
            p_ref[rows, :] = lax.dot_general(u, w, _CONTRACT_LAST, preferred_element_type=F32)

    @pl.when(n > 0)
    def _():
        p_ref[...] = lax.dot_general(u_ref[...], w_ref[...].astype(BF16), _CONTRACT_LAST,
                                     preferred_element_type=F32)


def _inproj(x2, mod, w_in_t, w_if_t):
    tiles_per_batch = SEQ // IN_TM
    return pl.pallas_call(
        _inproj_kernel,
        out_shape=(
            jax.ShapeDtypeStruct((TOKENS, P_WIDTH), F32),
            jax.ShapeDtypeStruct((TOKENS, LANES), F32),
            jax.ShapeDtypeStruct((TOKENS, D_MODEL), BF16),
        ),
        grid=(TOKENS // IN_TM, P_TILES_IN),
        in_specs=[
            pl.BlockSpec(memory_space=pl.ANY),
            pl.BlockSpec((None, 6, D_MODEL), lambda m, n: (m // tiles_per_batch, 0, 0)),
            pl.BlockSpec((PROJ_TN, D_MODEL), lambda m, n: (n, 0)),
            pl.BlockSpec((LANES, D_MODEL), lambda m, n: (0, 0)),
        ],
        out_specs=(
            pl.BlockSpec((IN_TM, PROJ_TN),
                         lambda m, n: (m, jnp.where(n < P_ATT_TILES, n + P_M_TILES, n - P_ATT_TILES))),
            pl.BlockSpec((IN_TM, LANES), lambda m, n: (m, 0)),
            pl.BlockSpec((IN_TM, D_MODEL), lambda m, n: (m, 0)),
        ),
        scratch_shapes=[pltpu.VMEM((IN_TM, D_MODEL), F32), pltpu.SemaphoreType.DMA],
        compiler_params=_cparams(("arbitrary", "arbitrary")),
        name="in_proj",
    )(x2, mod, w_in_t, w_if_t)


def _gateproj_kernel(u_ref, w_ref, b_ref, o_ref):
    acc = jnp.dot(u_ref[...], w_ref[...].astype(BF16), preferred_element_type=F32)
    o_ref[...] = jax.nn.sigmoid(acc + b_ref[...]).astype(BF16)


def _gateproj(u, w_gate, b_gate):
    n = w_gate.shape[1]
    return pl.pallas_call(
        _gateproj_kernel,
        out_shape=jax.ShapeDtypeStruct((TOKENS, n), BF16),
        grid=(TOKENS // PROJ_TM, n // GATE_TN),
        in_specs=[
            pl.BlockSpec((PROJ_TM, D_MODEL), lambda m, j: (m, 0)),
            pl.BlockSpec((D_MODEL, GATE_TN), lambda m, j: (0, j)),
            pl.BlockSpec((1, GATE_TN), lambda m, j: (0, j)),
        ],
        out_specs=pl.BlockSpec((PROJ_TM, GATE_TN), lambda m, j: (m, j)),
        compiler_params=_cparams(("arbitrary", "arbitrary")),
        name="gate_proj",
    )(u, w_gate, b_gate)


ROPE_ROWS = 256
ATT_LOCKSTEP = 8


def _rows(start, size, stride):
    return pl.ds(start, size) if stride == 1 else pl.ds(start, size, stride=stride)


def _attn_kernel(q0, k0, v0, q1, k1, v1, q2, k2, v2, cos_ref, sin_ref, y_ref,
                 qr_sc, kr_sc, o0, o1, o2, l0, l1, l2):
    scale = HEAD_DIM_A ** -0.5
    blk = ATT_BLOCK
    qi2 = lax.broadcasted_iota(jnp.int32, (blk, 2 * blk), 0)
    kc2 = lax.broadcasted_iota(jnp.int32, (blk, 2 * blk), 1)
    mask_prev_cur = (kc2 >= qi2) & (kc2 <= qi2 + blk)
    qi1 = lax.broadcasted_iota(jnp.int32, (blk, blk), 0)
    kc1 = lax.broadcasted_iota(jnp.int32, (blk, blk), 1)
    mask_cur = kc1 <= qi1
    half = HEAD_DIM_A // 2
    groups = ((q0, k0, v0, o0, l0), (q1, k1, v1, o1, l1), (q2, k2, v2, o2, l2))
    for (window, d), (q_ref, k_ref, v_ref, o_sc, l_sc) in zip(DIL_CONFIGS, groups):
        nb = SEQ // d // blk
        for src_ref, dst_sc in ((q_ref, qr_sc), (k_ref, kr_sc)):
            for c0 in range(0, SEQ, ROPE_ROWS):
                rs = slice(c0, c0 + ROPE_ROWS)
                xr = src_ref[rs, :]
                dst_sc[rs, :] = xr * cos_ref[rs, :] + pltpu.roll(xr, half, 1) * sin_ref[rs, :]
        blocks = [(r, j) for r in range(d) for j in range(nb)]
        for b0 in range(0, len(blocks), ATT_LOCKSTEP):
            batch = blocks[b0:b0 + ATT_LOCKSTEP]
            cur, keys, mask = [], [], []
            for r, j in batch:
                start = r + j * blk * d
                cur.append(_rows(start, blk, d))
                keys.append(cur[-1] if j == 0 else _rows(start - blk * d, 2 * blk, d))
                mask.append(mask_cur if j == 0 else mask_prev_cur)
            nbb = range(len(batch))
            qb = [qr_sc[cur[i], :].astype(BF16) for i in nbb]
            kw = [kr_sc[keys[i], :].astype(BF16) for i in nbb]
            s = [lax.dot_general(qb[i], kw[i], _CONTRACT_LAST, preferred_element_type=F32) for i in nbb]
            s = [jnp.where(mask[i], s[i], -jnp.inf) for i in nbb]
            mx = [jnp.max(s[i], axis=-1, keepdims=True) for i in nbb]
            p = [jnp.exp2((s[i] - mx[i]) * (scale * LOG2E)).astype(BF16) for i in nbb]
            vb = [v_ref[keys[i], :].astype(BF16) for i in nbb]
            pv = [jnp.dot(p[i], jnp.concatenate([vb[i], jnp.ones_like(vb[i])], axis=1),
                          preferred_element_type=F32) for i in nbb]
            for i in nbb:
                den = pv[i][:, HEAD_DIM_A:]
                o_sc[cur[i], :] = pv[i][:, :HEAD_DIM_A] / den
                l_sc[cur[i], :] = mx[i] * scale + jnp.log(den)
    for c0 in range(0, SEQ, ROPE_ROWS):
        rs = slice(c0, c0 + ROPE_ROWS)
        la = l0[rs, :]
        lb = l1[rs, :]
        lc = l2[rs, :]
        mx = jnp.maximum(jnp.maximum(la, lb), lc)
        ea = jnp.exp(la - mx)
        eb = jnp.exp(lb - mx)
        ec = jnp.exp(lc - mx)
        den = ea + eb + ec
        y_ref[rs, :] = ((ea / den) * o0[rs, :] + (eb / den) * o1[rs, :] + (ec / den) * o2[rs, :]).astype(BF16)


def _attention(p, cos_t, sin_t):
    for window, d in DIL_CONFIGS:
        assert window // d == ATT_BLOCK and SEQ % (d * ATT_BLOCK) == 0
    col0 = P_ATT_OFF // HEAD_DIM_A

    def slab(part, g):
        off = col0 + (part * A_QKV_W + g * A_GROUP_W) // HEAD_DIM_A
        return pl.BlockSpec((SEQ, HEAD_DIM_A), lambda b, h: (b, off + h))

    in_specs = [slab(part, g) for g in range(N_DIL_GROUPS) for part in range(3)]
    table = pl.BlockSpec((SEQ, HEAD_DIM_A), lambda b, h: (0, 0))
    return pl.pallas_call(
        _attn_kernel,
        out_shape=jax.ShapeDtypeStruct((TOKENS, A_GROUP_W), BF16),
        grid=(BATCH, HEADS_PER_GROUP),
        in_specs=in_specs + [table, table],
        out_specs=pl.BlockSpec((SEQ, HEAD_DIM_A), lambda b, h: (b, h)),
        scratch_shapes=[pltpu.VMEM((SEQ, HEAD_DIM_A), F32)] * 8,
        compiler_params=_cparams(("arbitrary", "arbitrary")),
        name="dil_attn",
    )(*([p] * 9), cos_t, sin_t)


CONV_HALO = SUBLANES


def _mlstm_kernel(q_ref, k_ref, v_ref, og_ref, gc_ref, gr_ref, bc_ref, br_ref, cw_ref, cb_ref, ng_ref,
                  y_ref, xq_sc, xk_sc, ct_sc, n_sc, m_sc):
    c = pl.program_id(1)
    L = M_CHUNK
    lo = CONV_HALO

    @pl.when(c == 0)
    def _():
        xq_sc[0:lo, :] = jnp.zeros((lo, M_W), F32)
        xk_sc[0:lo, :] = jnp.zeros((lo, M_W), F32)
        ct_sc[...] = jnp.zeros_like(ct_sc)
        n_sc[...] = jnp.zeros_like(n_sc)
        m_sc[...] = jnp.zeros_like(m_sc)

    xq_sc[lo:lo + L, :] = q_ref[...]
    xk_sc[lo:lo + L, :] = k_ref[...]

    def conv(x_sc, w, b):
        acc = x_sc[lo:lo + L, :] * w[CONV_K - 1:CONV_K, :] + b
        for j in range(CONV_K - 1):
            off = lo - (CONV_K - 1) + j
            acc = acc + x_sc[off:off + L, :] * w[j:j + 1, :]
        return acc

    cw = cw_ref[...]
    cb = cb_ref[...]
    qc = _silu(conv(xq_sc, cw[:, :M_W], cb[:, :M_W]))
    kc = _silu(conv(xk_sc, cw[:, M_W:], cb[:, M_W:])) * (M_HEAD_DIM ** -0.5)
    xq_sc[0:lo, :] = xq_sc[L:L + lo, :]
    xk_sc[0:lo, :] = xk_sc[L:L + lo, :]

    row = lax.broadcasted_iota(jnp.int32, (L, L), 0)
    col = lax.broadcasted_iota(jnp.int32, (L, L), 1)
    causal = row >= col
    heads = range(M_HEADS)
    hsl = [slice(h * M_HEAD_DIM, (h + 1) * M_HEAD_DIM) for h in heads]
    ct = [ct_sc[h] for h in heads]
    n_row = [n_sc[h, 0:1, :] for h in heads]
    m_prev = [m_sc[h, 0:1, 0:1] for h in heads]
    q = [qc[:, hsl[h]] for h in heads]
    k = [kc[:, hsl[h]] for h in heads]
    qb = [q[h].astype(BF16) for h in heads]
    kb = [k[h].astype(BF16) for h in heads]
    vb = [v_ref[:, hsl[h]].astype(BF16) for h in heads]
    qk = [lax.dot_general(qb[h], kb[h], _CONTRACT_LAST, preferred_element_type=F32) for h in heads]
    qc_state = [jnp.dot(qb[h], ct[h].astype(BF16), preferred_element_type=F32) for h in heads]

    i_col = [gc_ref[:, h:h + 1] + bc_ref[:, h:h + 1] for h in heads]
    lf_col = [_log_sigmoid(gc_ref[:, M_HEADS + h:M_HEADS + h + 1] + bc_ref[:, M_HEADS + h:M_HEADS + h + 1])
              for h in heads]
    i_row = [gr_ref[h:h + 1, :] + br_ref[h:h + 1, :] for h in heads]
    lf_row = [_log_sigmoid(gr_ref[M_HEADS + h:M_HEADS + h + 1, :] + br_ref[M_HEADS + h:M_HEADS + h + 1, :])
              for h in heads]
    b_col = [jnp.sum(jnp.where(causal, lf_row[h], 0.0), axis=1, keepdims=True) for h in heads]
    b_row = [jnp.sum(jnp.where(row <= col, lf_col[h], 0.0), axis=0, keepdims=True) for h in heads]
    log_d = [jnp.where(causal, b_col[h] - b_row[h] + i_row[h], -jnp.inf) for h in heads]
    log_inter = [b_col[h] + m_prev[h] for h in heads]
    m_t = [jnp.maximum(jnp.max(log_d[h], axis=1, keepdims=True), log_inter[h]) for h in heads]
    s = [qk[h] * jnp.exp(log_d[h] - m_t[h]) for h in heads]
    inter = [jnp.exp(log_inter[h] - m_t[h]) for h in heads]
    sv = [jnp.dot(s[h].astype(BF16), vb[h], preferred_element_type=F32) for h in heads]

    m_new = [m_t[h][L - 1:L, :] for h in heads]
    b_last = [b_col[h][L - 1:L, :] for h in heads]
    w_col = [jnp.exp(b_last[h] - b_col[h] + i_col[h] - m_new[h]) for h in heads]
    decay = [jnp.exp(b_last[h] + m_prev[h] - m_new[h]) for h in heads]
    kw = [k[h] * w_col[h] for h in heads]
    kv = [lax.dot_general(kw[h].astype(BF16), vb[h], (((0,), (0,)), ((), ())), preferred_element_type=F32)
          for h in heads]

    num = [sv[h] + inter[h] * qc_state[h] for h in heads]
    den = [jnp.sum(s[h], axis=1, keepdims=True) + inter[h] * jnp.sum(q[h] * n_row[h], axis=1, keepdims=True)
           for h in heads]
    hh = [num[h] / jnp.maximum(jnp.abs(den[h]), jnp.exp(-m_t[h])) for h in heads]
    z = [jax.nn.sigmoid(og_ref[:, hsl[h]]) * hh[h] for h in heads]
    y = [(_normalize(z[h]) * ng_ref[:, hsl[h]]).astype(BF16) for h in heads]
    ct_new = [decay[h] * ct[h] + kv[h] for h in heads]
    n_new = [decay[h] * n_row[h] + jnp.sum(kw[h], axis=0, keepdims=True) for h in heads]
    for h in heads:
        ct_sc[h] = ct_new[h]
        n_sc[h] = jnp.broadcast_to(n_new[h], (SUBLANES, M_HEAD_DIM))
        m_sc[h] = jnp.broadcast_to(m_new[h], (SUBLANES, LANES))
        y_ref[:, hsl[h]] = y[h]


def _mlstm(p, gcol, grow, bcol, brow, conv_w, conv_b, norm_g):
    nc = SEQ // M_CHUNK
    slab = lambda blk: pl.BlockSpec((M_CHUNK, M_W), lambda b, c: (b * nc + c, blk))
    return pl.pallas_call(
        _mlstm_kernel,
        out_shape=jax.ShapeDtypeStruct((TOKENS, M_W), BF16),
        grid=(BATCH, nc),
        in_specs=[
            slab(P_QM_BLK), slab(P_KM_BLK), slab(P_VM_BLK), slab(P_OM_BLK),
            pl.BlockSpec((None, M_CHUNK, 2 * M_HEADS), lambda b, c: (b, c, 0)),
            pl.BlockSpec((None, 2 * M_HEADS, M_CHUNK), lambda b, c: (b, 0, c)),
            pl.BlockSpec((1, 2 * M_HEADS), lambda b, c: (0, 0)),
            pl.BlockSpec((2 * M_HEADS, 1), lambda b, c: (0, 0)),
            pl.BlockSpec((CONV_K, 2 * M_W), lambda b, c: (0, 0)),
            pl.BlockSpec((1, 2 * M_W), lambda b, c: (0, 0)),
            pl.BlockSpec((1, M_W), lambda b, c: (0, 0)),
        ],
        out_specs=pl.BlockSpec((M_CHUNK, M_W), lambda b, c: (b * nc + c, 0)),
        scratch_shapes=[
            pltpu.VMEM((M_CHUNK + CONV_HALO, M_W), F32),
            pltpu.VMEM((M_CHUNK + CONV_HALO, M_W), F32),
            pltpu.VMEM((M_HEADS, M_HEAD_DIM, M_HEAD_DIM), F32),
            pltpu.VMEM((M_HEADS, SUBLANES, M_HEAD_DIM), F32),
            pltpu.VMEM((M_HEADS, SUBLANES, LANES), F32),
        ],
        compiler_params=_cparams(("arbitrary", "arbitrary")),
        name="mlstm",
    )(p, p, p, p, gcol, grow, bcol, brow, conv_w, conv_b, norm_g)


RT_E1, RT_E2, RT_W1, RT_W2 = 0, 1, 2, 3


def _route(logits):
    lane = lax.broadcasted_iota(jnp.int32, logits.shape, 1).astype(F32)
    big = float(LANES)
    is_g = lane < N_EXPERT_GROUPS
    gl = jnp.where(is_g, logits, -jnp.inf)
    gexp = jnp.exp(gl - jnp.max(gl, axis=1, keepdims=True))
    gprob = gexp / jnp.sum(gexp, axis=1, keepdims=True)
    g_w = jnp.max(gprob, axis=1, keepdims=True)
    g_top = jnp.min(jnp.where(is_g & (gprob == g_w), lane, big), axis=1, keepdims=True)
    lo = N_EXPERT_GROUPS + EXPERTS_PER_GROUP * g_top
    in_grp = (lane >= lo) & (lane < lo + EXPERTS_PER_GROUP)
    el = jnp.where(in_grp, logits, -jnp.inf)
    eexp = jnp.exp(el - jnp.max(el, axis=1, keepdims=True))
    eprob = eexp / jnp.sum(eexp, axis=1, keepdims=True)
    v1 = jnp.max(eprob, axis=1, keepdims=True)
    i1 = jnp.min(jnp.where(in_grp & (eprob == v1), lane, big), axis=1, keepdims=True)
    rest = jnp.where(in_grp & (lane != i1), eprob, -1.0)
    v2 = jnp.max(rest, axis=1, keepdims=True)
    i2 = jnp.min(jnp.where(rest == v2, lane, big), axis=1, keepdims=True)
    tot = v1 + v2
    w1 = g_w * (v1 / tot)
    w2 = g_w * (v2 / tot)
    e1 = i1 - N_EXPERT_GROUPS
    e2 = i2 - N_EXPERT_GROUPS
    rec = jnp.where(lane == RT_E1, e1, jnp.where(lane == RT_E2, e2, jnp.where(lane == RT_W1, w1, w2)))
    return jnp.where(lane <= RT_W2, rec, 0.0)


def _merge_kernel(ya_ref, ym_ref, g_ref, x_ref, mod_ref,
                  wpa_ref, wpm_ref, wout_ref, lng_ref, lnb_ref, wr_ref, br_ref,
                  x1_ref, u2_ref, rt_ref):
    nsb = MERGE_TM // MERGE_SUB
    rows = [slice(sb * MERGE_SUB, (sb + 1) * MERGE_SUB) for sb in range(nsb)]

    def matmul_part(i):
        pa = jnp.dot(ya_ref[rows[i], :], wpa_ref[...], preferred_element_type=F32)
        pm = jnp.dot(ym_ref[rows[i], :], wpm_ref[...], preferred_element_type=F32)
        merged = g_ref[rows[i], :D_MODEL].astype(F32) * pa + g_ref[rows[i], D_MODEL:].astype(F32) * pm
        return jnp.dot(merged.astype(BF16), wout_ref[...], preferred_element_type=F32)

    def norm_part(i, mix):
        z = DEEPNORM_ALPHA * x_ref[rows[i], :] + mod_ref[2:3, :] * mix
        x1 = _normalize(z) * lng_ref[...] + lnb_ref[...]
        x1_ref[rows[i], :] = x1
        u2 = _normalize(x1) * (1.0 + mod_ref[4:5, :]) + mod_ref[3:4, :]
        u2_ref[rows[i], :] = u2
        logits = jnp.dot(u2.astype(BF16), wr_ref[...], preferred_element_type=F32) + br_ref[...]
        rt_ref[rows[i], :] = _route(logits)

    mix = matmul_part(0)
    for i in range(nsb):
        nxt = matmul_part(i + 1) if i + 1 < nsb else None
        norm_part(i, mix)
        mix = nxt


def _merge(ya, ym, g, x2, mod, wpa, wpm, wout, lng, lnb, wr, br):
    tm = MERGE_TM
    tiles_per_batch = SEQ // tm
    rowblk = lambda w: pl.BlockSpec((tm, w), lambda m: (m, 0))
    const = lambda shape: pl.BlockSpec(shape, lambda m: (0,) * len(shape), pipeline_mode=pl.Buffered(1))
    return pl.pallas_call(
        _merge_kernel,
        out_shape=(
            jax.ShapeDtypeStruct((TOKENS, D_MODEL), F32),
            jax.ShapeDtypeStruct((TOKENS, D_MODEL), F32),
            jax.ShapeDtypeStruct((TOKENS, LANES), F32),
        ),
        grid=(TOKENS // tm,),
        in_specs=[
            rowblk(A_GROUP_W), rowblk(M_W), rowblk(2 * D_MODEL), rowblk(D_MODEL),
            pl.BlockSpec((None, 6, D_MODEL), lambda m: (m // tiles_per_batch, 0, 0)),
            const((A_GROUP_W, D_MODEL)), const((M_W, D_MODEL)), const((D_MODEL, D_MODEL)),
            const((1, D_MODEL)), const((1, D_MODEL)),
            const((D_MODEL, LANES)), const((1, LANES)),
        ],
        out_specs=(rowblk(D_MODEL), rowblk(D_MODEL), rowblk(LANES)),
        compiler_params=_cparams(("arbitrary",)),
        name="merge_ln1_route",
    )(ya, ym, g, x2, mod, wpa, wpm, wout, lng, lnb, wr, br)


CAST_ROWS = 128


def _issue_rows(src_ref, idx_ref, base, buf, slot, sem, nrows):
    def body(blk, carry):
        for j in range(SUBLANES):
            row = idx_ref[base + blk * SUBLANES + j]
            src = src_ref.at[lax.shift_right_logical(row, 3), pl.ds(row & (SUBLANES - 1), 1)]
            pltpu.make_async_copy(src, buf.at[slot, blk, pl.ds(j, 1)], sem.at[slot]).start(priority=j % 2)
        return carry

    lax.fori_loop(0, nrows // SUBLANES, body, 0)


def _wait_rows(src_ref, buf, slot, sem, nrows):
    groups = nrows // SUBLANES
    pltpu.make_async_copy(src_ref.at[pl.ds(0, groups)], buf.at[slot, pl.ds(0, groups)], sem.at[slot]).wait()


class _TileTable:
    def __init__(self, meta_ref):
        self.ref = meta_ref

    def expert(self, t):
        return self.ref[META_TE * LANES + t]

    def next_expert(self, t):
        return self.ref[META_TN * LANES + t]

    def valid_rows(self, t):
        return self.ref[META_TV * LANES + t]

    def tiles_used(self):
        return self.ref[META_NU * LANES]


def _expert_changed(te_ref, i):
    return (i == 0) | (te_ref.expert(i) != te_ref.expert(jnp.maximum(i - 1, 0)))


def _moe_kernel(meta_ref, tok_ref, u_ref, wg_ref, wu_ref, wd_ref, o_ref,
                stg, stu, std, wgb, wub, wdb, xbuf, wsem, xsem):
    i = pl.program_id(0)
    te_ref = _TileTable(meta_ref)
    nu = te_ref.tiles_used()
    tm = o_ref.shape[0]
    half = tm // 2

    def weight_copies(e):
        return (pltpu.make_async_copy(wg_ref.at[e], stg, wsem.at[0]),
                pltpu.make_async_copy(wu_ref.at[e], stu, wsem.at[1]),
                pltpu.make_async_copy(wd_ref.at[e], std, wsem.at[2]))

    def for_tile_rows(t, fn):
        @pl.when(te_ref.valid_rows(t) <= half)
        def _():
            fn(half)

        @pl.when(te_ref.valid_rows(t) > half)
        def _():
            fn(tm)

    def issue_tile(t):
        for_tile_rows(t, lambda nrows: _issue_rows(u_ref, tok_ref, t * tm, xbuf, t % MOE_ROW_SLOTS, xsem, nrows))

    @pl.when(i == 0)
    def _():
        issue_tile(0)

        @pl.when(nu > 1)
        def _():
            issue_tile(1)

        for cp in weight_copies(te_ref.expert(0)):
            cp.start(priority=1)

    @pl.when(i < nu)
    def _():
        slot = i % MOE_ROW_SLOTS

        @pl.when(i + 2 < nu)
        def _():
            issue_tile(i + 2)

        @pl.when(_expert_changed(te_ref, i))
        def _():
            for cp in weight_copies(te_ref.expert(i)):
                cp.wait()
            for src, dst in ((stg, wgb), (stu, wub), (std, wdb)):
                def cast_rows(ci, carry, src=src, dst=dst):
                    r = pl.multiple_of(ci * CAST_ROWS, CAST_ROWS)
                    dst[pl.ds(r, CAST_ROWS), :] = src[pl.ds(r, CAST_ROWS), :].astype(BF16)
                    return carry

                lax.fori_loop(0, src.shape[0] // CAST_ROWS, cast_rows, 0)

            @pl.when(te_ref.next_expert(i) >= 0)
            def _():
                for cp in weight_copies(te_ref.next_expert(i)):
                    cp.start(priority=1)

        def compute(nrows):
            _wait_rows(u_ref, xbuf, slot, xsem, nrows)
            x = xbuf[slot, 0:nrows // SUBLANES].reshape(nrows, D_MODEL).astype(BF16)
            a = jnp.dot(x, wgb[...], preferred_element_type=F32)
            b = jnp.dot(x, wub[...], preferred_element_type=F32)
            h = (_silu(a) * b).astype(BF16)
            o_ref[0:nrows, :] = jnp.dot(h, wdb[...], preferred_element_type=F32)
            if nrows < tm:
                o_ref[nrows:tm, :] = jnp.zeros((tm - nrows, D_MODEL), F32)

        for_tile_rows(i, compute)

    @pl.when(i >= nu)
    def _():
        o_ref[...] = jnp.zeros_like(o_ref)


def _moe(meta, row_token, u2, w_eg, w_eu, w_ed):
    tm = MOE_TM
    return pl.pallas_call(
        _moe_kernel,
        out_shape=jax.ShapeDtypeStruct((MOE_ROWS, D_MODEL), F32),
        grid_spec=pltpu.PrefetchScalarGridSpec(
            num_scalar_prefetch=2,
            grid=(MOE_TILES,),
            in_specs=[pl.BlockSpec(memory_space=pl.ANY)] * 4,
            out_specs=pl.BlockSpec((tm, D_MODEL), lambda i, meta, tok: (i, 0)),
            scratch_shapes=[
                pltpu.VMEM((D_MODEL, D_FF_EXPERT), F32),
                pltpu.VMEM((D_MODEL, D_FF_EXPERT), F32),
                pltpu.VMEM((D_FF_EXPERT, D_MODEL), F32),
                pltpu.VMEM((D_MODEL, D_FF_EXPERT), BF16),
                pltpu.VMEM((D_MODEL, D_FF_EXPERT), BF16),
                pltpu.VMEM((D_FF_EXPERT, D_MODEL), BF16),
                pltpu.VMEM((MOE_ROW_SLOTS, tm // SUBLANES, SUBLANES, D_MODEL), F32),
                pltpu.SemaphoreType.DMA((3,)),
                pltpu.SemaphoreType.DMA((MOE_ROW_SLOTS,)),
            ],
        ),
        compiler_params=_cparams(("arbitrary",)),
        name="moe_experts",
    )(meta, row_token, u2.reshape(TOKENS // SUBLANES, SUBLANES, D_MODEL), w_eg, w_eu, w_ed)


def _final_kernel(pos_ref, o_ref, x1_ref, rt_ref, mod_ref, lng_ref, lnb_ref, y_ref, buf, sem):
    i = pl.program_id(0)
    tb = x1_ref.shape[0]
    slot = i % 2

    @pl.when(i == 0)
    def _():
        _issue_rows(o_ref, pos_ref, 0, buf, 0, sem, 2 * tb)

    @pl.when(i + 1 < pl.num_programs(0))
    def _():
        _issue_rows(o_ref, pos_ref, (i + 1) * (2 * tb), buf, 1 - slot, sem, 2 * tb)

    _wait_rows(o_ref, buf, slot, sem, 2 * tb)
    rows = buf[slot].reshape(2 * tb, D_MODEL)
    ffn = rt_ref[:, RT_W1:RT_W1 + 1] * rows[0:tb, :] + rt_ref[:, RT_W2:RT_W2 + 1] * rows[tb:2 * tb, :]
    z = DEEPNORM_ALPHA * x1_ref[...] + mod_ref[5:6, :] * ffn
    y_ref[...] = _normalize(z) * lng_ref[...] + lnb_ref[...]


def _final(pos_tiles, moe_out, x1, rt, mod, lng, lnb):
    tb = FINAL_TB
    tiles_per_batch = SEQ // tb
    return pl.pallas_call(
        _final_kernel,
        out_shape=jax.ShapeDtypeStruct((TOKENS, D_MODEL), F32),
        grid_spec=pltpu.PrefetchScalarGridSpec(
            num_scalar_prefetch=1,
            grid=(TOKENS // tb,),
            in_specs=[
                pl.BlockSpec(memory_space=pl.ANY),
                pl.BlockSpec((tb, D_MODEL), lambda i, pos: (i, 0)),
                pl.BlockSpec((tb, LANES), lambda i, pos: (i, 0)),
                pl.BlockSpec((None, 6, D_MODEL), lambda i, pos: (i // tiles_per_batch, 0, 0)),
                pl.BlockSpec((1, D_MODEL), lambda i, pos: (0, 0)),
                pl.BlockSpec((1, D_MODEL), lambda i, pos: (0, 0)),
            ],
            out_specs=pl.BlockSpec((tb, D_MODEL), lambda i, pos: (i, 0)),
            scratch_shapes=[pltpu.VMEM((2, 2 * tb // SUBLANES, SUBLANES, D_MODEL), F32),
                            pltpu.SemaphoreType.DMA((2,))],
        ),
        compiler_params=_cparams(("arbitrary",)),
        name="combine_ln2",
    )(pos_tiles, moe_out.reshape(MOE_ROWS // SUBLANES, SUBLANES, D_MODEL), x1, rt, mod, lng, lnb)


ROUTE_BLK = 256
ROUTE_LOCKSTEP = 4
META_TE, META_TN, META_TV, META_NU = 0, 1, 2, 3


def _lane_cumsum(x, lane):
    s = 1
    while s < LANES:
        x = x + jnp.where(lane >= s, pltpu.roll(x, s, 1), 0.0)
        s *= 2
    return x


def _route_tables_kernel(rt_ref, pos_ref, meta_ref, rank_sc):
    tm = float(MOE_TM)
    nblk = TOKENS // ROUTE_BLK
    lane_i = lax.broadcasted_iota(jnp.int32, (ROUTE_BLK, LANES), 1)
    lane_f = lane_i.astype(F32)
    earlier = (lax.broadcasted_iota(jnp.int32, (ROUTE_BLK, ROUTE_BLK), 0)
               > lax.broadcasted_iota(jnp.int32, (ROUTE_BLK, ROUTE_BLK), 1)).astype(BF16)

    def onehots(b):
        blk = rt_ref[pl.ds(pl.multiple_of(b * ROUTE_BLK, ROUTE_BLK), ROUTE_BLK), :]
        return blk[:, RT_E1:RT_E1 + 1] == lane_f, blk[:, RT_E2:RT_E2 + 1] == lane_f

    def count_pass(g, carry):
        c1, c2 = carry
        bs = [g * ROUTE_LOCKSTEP + i for i in range(ROUTE_LOCKSTEP)]
        oh = [onehots(b) for b in bs]
        f = [(jnp.where(o1, 1.0, 0.0), jnp.where(o2, 1.0, 0.0)) for o1, o2 in oh]
        p = [(jnp.dot(earlier, f1.astype(BF16), preferred_element_type=F32),
              jnp.dot(earlier, f2.astype(BF16), preferred_element_type=F32)) for f1, f2 in f]
        tot = [(jnp.sum(f1, axis=0, keepdims=True), jnp.sum(f2, axis=0, keepdims=True)) for f1, f2 in f]
        for i, b in enumerate(bs):
            r1 = jnp.sum(jnp.where(oh[i][0], p[i][0] + c1, 0.0), axis=1, keepdims=True)
            r2 = jnp.sum(jnp.where(oh[i][1], p[i][1] + c2, 0.0), axis=1, keepdims=True)
            rank_sc[pl.ds(pl.multiple_of(b * ROUTE_BLK, ROUTE_BLK), ROUTE_BLK), :] = jnp.where(
                lane_i == 0, r1, jnp.where(lane_i == 1, r2, 0.0))
            c1 = c1 + tot[i][0]
            c2 = c2 + tot[i][1]
        return c1, c2

    zero = jnp.zeros((1, LANES), F32)
    c1, c2 = lax.fori_loop(0, nblk // ROUTE_LOCKSTEP, count_pass, (zero, zero))

    lane8 = lax.broadcasted_iota(jnp.int32, (SUBLANES, LANES), 1)
    counts = jnp.broadcast_to(c1 + c2, (SUBLANES, LANES))
    padded = jnp.floor((counts + (tm - 1.0)) * (1.0 / tm)) * tm
    pend = _lane_cumsum(padded, lane8)
    pstart = pend - padded
    start1 = pstart[0:1, :]
    start2 = start1 + c1

    def place_pass(g, carry):
        bs = [g * ROUTE_LOCKSTEP + i for i in range(ROUTE_LOCKSTEP)]
        oh = [onehots(b) for b in bs]
        rows = [pl.ds(pl.multiple_of(b * ROUTE_BLK, ROUTE_BLK), ROUTE_BLK) for b in bs]
        rk = [rank_sc[r, :] for r in rows]
        d1 = [jnp.sum(jnp.where(oh[i][0], start1, 0.0), axis=1, keepdims=True) + rk[i][:, 0:1]
              for i in range(ROUTE_LOCKSTEP)]
        d2 = [jnp.sum(jnp.where(oh[i][1], start2, 0.0), axis=1, keepdims=True) + rk[i][:, 1:2]
              for i in range(ROUTE_LOCKSTEP)]
        for i in range(ROUTE_LOCKSTEP):
            pos_ref[rows[i], :] = jnp.where(lane_i == 0, d1[i], jnp.where(lane_i == 1, d2[i], 0.0)).astype(jnp.int32)
        return carry

    lax.fori_loop(0, nblk // ROUTE_LOCKSTEP, place_pass, 0)

    tile = lax.broadcasted_iota(jnp.int32, (LANES, LANES), 0).astype(F32)
    lane = lax.broadcasted_iota(jnp.int32, (LANES, LANES), 1)
    expert_lane = lane < N_EXPERTS
    pend_b = jnp.broadcast_to(pend[0:1, :], (LANES, LANES))
    pstart_b = jnp.broadcast_to(pstart[0:1, :], (LANES, LANES))
    counts_b = jnp.broadcast_to(counts[0:1, :], (LANES, LANES))
    n_used = jnp.sum(jnp.where(lane == N_EXPERTS - 1, pend_b, 0.0), axis=1, keepdims=True) * (1.0 / tm)
    te = jnp.sum(jnp.where(expert_lane & (pend_b <= tile * tm), 1.0, 0.0), axis=1, keepdims=True)
    te = jnp.minimum(te, float(N_EXPERTS - 1))
    te_last = jnp.sum(jnp.where(tile[:, 0:1] == n_used - 1.0, te, 0.0), axis=0, keepdims=True)
    te = jnp.where(tile[:, 0:1] < n_used, te, te_last)
    of_tile = lane.astype(F32) == te
    pend_te = jnp.sum(jnp.where(of_tile, pend_b, 0.0), axis=1, keepdims=True)
    last_row = jnp.sum(jnp.where(of_tile, pstart_b + counts_b, 0.0), axis=1, keepdims=True)
    tile_rows = jnp.clip(last_row - tile[:, 0:1] * tm, 0.0, tm)
    next_run = pend_te * (1.0 / tm)
    te_by_lane = jnp.broadcast_to(te, (LANES, LANES)).T
    te_at_next = jnp.sum(jnp.where(lane.astype(F32) == next_run, te_by_lane, 0.0), axis=1, keepdims=True)
    te_next = jnp.where(next_run < n_used, te_at_next, -1.0)
    cols = jnp.where(lane == META_TE, te, jnp.where(lane == META_TN, te_next,
                     jnp.where(lane == META_TV, tile_rows, jnp.where(lane == META_NU, n_used, 0.0))))
    meta_ref[...] = cols.T[0:SUBLANES, :].astype(jnp.int32)


def _routing_tables(rt):
    pos, meta = pl.pallas_call(
        _route_tables_kernel,
        out_shape=(
            jax.ShapeDtypeStruct((TOKENS, LANES), jnp.int32),
            jax.ShapeDtypeStruct((SUBLANES, LANES), jnp.int32),
        ),
        scratch_shapes=[pltpu.VMEM((TOKENS, LANES), F32)],
        compiler_params=_cparams(None),
        name="route_tables",
    )(rt)
    pos = pos[:, 0:2]
    tok = jnp.arange(2 * TOKENS, dtype=jnp.int32) // 2
    row_token = jnp.zeros((MOE_ROWS,), jnp.int32).at[pos.reshape(-1)].set(
        tok, unique_indices=True, mode="promise_in_bounds")
    return row_token, meta.reshape(-1), pos


def _rope_tables():
    inv = ROPE_THETA ** (-jnp.arange(0, HEAD_DIM_A, 2, dtype=F32) / HEAD_DIM_A)
    ang = jnp.arange(SEQ, dtype=F32)[:, None] * inv[None, :]
    cos = jnp.cos(ang)
    sin = jnp.sin(ang)
    return jnp.concatenate([cos, cos], axis=-1), jnp.concatenate([-sin, sin], axis=-1)


def kernel(x, c, w_ada, b_ada, w_in, b_mgate, conv_w, conv_b, m_norm_g, w_proj_a, w_proj_m, w_gate, b_gate,
           w_out, ln1_g, ln1_b, w_rg, b_rg, w_re, b_re, w_eg, w_eu, w_ed, ln2_g, ln2_b):
    assert x.shape == (BATCH, SEQ, D_MODEL) and w_ada.shape[0] == 1
    l = 0
    x2 = x.reshape(TOKENS, D_MODEL)

    c_pad = jnp.zeros((ADA_ROWS, D_MODEL), F32).at[:BATCH].set(c)
    mod = _ada(c_pad, w_ada[l], b_ada[l][None, :])[:BATCH].reshape(BATCH, 6, D_MODEL)

    w_in_t = jnp.swapaxes(w_in[l], 0, 1)
    w_if_t = jnp.zeros((LANES, D_MODEL), F32).at[:2 * M_HEADS].set(w_in_t[N_IN_MAIN:])
    p, gates, u = _inproj(x2, mod, w_in_t, w_if_t)
    g = _gateproj(u, w_gate[l], b_gate[l][None, :])

    cos_t, sin_t = _rope_tables()
    ya = _attention(p, cos_t, sin_t)

    gcol = gates[:, :2 * M_HEADS].reshape(BATCH, SEQ, 2 * M_HEADS)
    grow = jnp.transpose(gcol, (0, 2, 1))
    ym = _mlstm(p, gcol, grow, b_mgate[l][None, :], b_mgate[l][:, None], conv_w[l], conv_b[l][None, :],
                m_norm_g[l][None, :])

    wr = (jnp.zeros((D_MODEL, LANES), F32)
          .at[:, :N_EXPERT_GROUPS].set(w_rg[l])
          .at[:, N_EXPERT_GROUPS:N_EXPERT_GROUPS + N_EXPERTS].set(w_re[l])).astype(BF16)
    br = (jnp.zeros((1, LANES), F32)
          .at[0, :N_EXPERT_GROUPS].set(b_rg[l])
          .at[0, N_EXPERT_GROUPS:N_EXPERT_GROUPS + N_EXPERTS].set(b_re[l]))
    x1, u2, rt = _merge(ya, ym, g, x2, mod,
                        w_proj_a[l].astype(BF16), w_proj_m[l].astype(BF16), w_out[l].astype(BF16),
                        ln1_g[l][None, :], ln1_b[l][None, :], wr, br)

    row_token, meta, pos = _routing_tables(rt)

    mo = _moe(meta, row_token, u2,
              w_eg[l].reshape(N_EXPERTS, D_MODEL, D_FF_EXPERT), w_eu[l].reshape(N_EXPERTS, D_MODEL, D_FF_EXPERT),
              w_ed[l].reshape(N_EXPERTS, D_FF_EXPERT, D_MODEL))

    nt = TOKENS // FINAL_TB
    pos_tiles = jnp.transpose(pos.reshape(nt, FINAL_TB, 2), (0, 2, 1)).reshape(-1)
    y = _final(pos_tiles, mo, x1, rt, mod, ln2_g[l][None, :], ln2_b[l][None, :])
    return y.reshape(BATCH, SEQ, D_MODEL)
```

```python
import functools

import jax
import jax.numpy as jnp
from jax import lax
from jax.experimental import pallas as pl
from jax.experimental.pallas import tpu as pltpu

F32 = jnp.float32
BF16 = jnp.bfloat16

D_MODEL = 2048
BATCH = 4
SEQ = 2048
TOKENS = BATCH * SEQ
DIL_CONFIGS = ((128, 1), (512, 4), (2048, 16))
N_DIL_GROUPS = 3
HEADS_PER_GROUP = 4
HEAD_DIM_A = 128
ATT_BLOCK = 128
ROPE_THETA = 10000.0
A_GROUP_W = HEADS_PER_GROUP * HEAD_DIM_A
A_QKV_W = N_DIL_GROUPS * A_GROUP_W
M_HEADS = 4
M_HEAD_DIM = 256
M_W = M_HEADS * M_HEAD_DIM
M_CHUNK = 128
CONV_K = 4
N_IN_MAIN = 3 * A_QKV_W + 4 * M_W
N_EXPERT_GROUPS = 4
EXPERTS_PER_GROUP = 8
N_EXPERTS = N_EXPERT_GROUPS * EXPERTS_PER_GROUP
D_FF_EXPERT = 1024
DEEPNORM_ALPHA = 2.0 ** 0.25
LN_EPS = 1e-5

LANES = 128
SUBLANES = 8
VMEM_LIMIT_BYTES = 56 * 1024 * 1024

PROJ_TN = 256
GATE_TN = 1024
PROJ_TM = 1024
IN_TM = 2048
P_TILES_IN = N_IN_MAIN // PROJ_TN
P_ATT_TILES = 3 * A_QKV_W // PROJ_TN
P_M_TILES = P_TILES_IN - P_ATT_TILES
P_WIDTH = N_IN_MAIN
P_ATT_OFF = P_M_TILES * PROJ_TN
P_QM_BLK, P_KM_BLK, P_VM_BLK, P_OM_BLK = 0, 1, 2, 3
MERGE_TM = 256
MERGE_SUB = 256
MOE_TM = 256
MOE_ROWS = 2 * TOKENS + N_EXPERTS * MOE_TM
MOE_TILES = MOE_ROWS // MOE_TM
MOE_ROW_SLOTS = 3
FINAL_TB = 256


def _cparams(sem, vmem=VMEM_LIMIT_BYTES):
    return pltpu.CompilerParams(dimension_semantics=sem, vmem_limit_bytes=vmem)


def _normalize(x):
    mu = jnp.mean(x, axis=-1, keepdims=True)
    xc = x - mu
    var = jnp.mean(xc * xc, axis=-1, keepdims=True)
    return xc * lax.rsqrt(var + LN_EPS)


def _silu(x):
    return x * jax.nn.sigmoid(x)


def _log_sigmoid(x):
    return jnp.minimum(x, 0.0) - jnp.log(1.0 + jnp.exp(-jnp.abs(x)))


_CONTRACT_LAST = (((1,), (1,)), ((), ()))
LOG2E = 1.4426950408889634


ADA_TN = 1024
ADA_ROWS = 16


def _ada_kernel(c_ref, w_ref, b_ref, o_ref):
    sc = _silu(c_ref[...]).astype(BF16)
    o_ref[...] = jnp.dot(sc, w_ref[...].astype(BF16), preferred_element_type=F32) + b_ref[...]


def _ada(c_pad, w_ada, b_ada):
    n = w_ada.shape[1]
    return pl.pallas_call(
        _ada_kernel,
        out_shape=jax.ShapeDtypeStruct((ADA_ROWS, n), F32),
        grid=(n // ADA_TN,),
        in_specs=[
            pl.BlockSpec((ADA_ROWS, D_MODEL), lambda j: (0, 0)),
            pl.BlockSpec((D_MODEL, ADA_TN), lambda j: (0, j)),
            pl.BlockSpec((1, ADA_TN), lambda j: (0, j)),
        ],
        out_specs=pl.BlockSpec((ADA_ROWS, ADA_TN), lambda j: (0, j)),
        compiler_params=_cparams(("arbitrary",)),
        name="ada_mod",
    )(c_pad, w_ada, b_ada)


LN_CHUNK = 256


def _inproj_kernel(x_ref, mod_ref, w_ref, wif_ref, p_ref, g_ref, u_ref, xbuf, xsem):
    m = pl.program_id(0)
    n = pl.program_id(1)

    def x_copy(tile):
        return pltpu.make_async_copy(x_ref.at[pl.ds(pl.multiple_of(tile * IN_TM, IN_TM), IN_TM)], xbuf, xsem)

    @pl.when((m == 0) & (n == 0))
    def _():
        x_copy(0).start()

    @pl.when((n == 1) & (m + 1 < pl.num_programs(0)))
    def _():
        x_copy(m + 1).start()

    @pl.when(n == 0)
    def _():
        x_copy(m).wait()
        shift = mod_ref[0:1, :]
        scale = 1.0 + mod_ref[1:2, :]
        wif = wif_ref[...].astype(BF16)
        w = w_ref[...].astype(BF16)
        for ci in range(IN_TM // LN_CHUNK):
            rows = slice(ci * LN_CHUNK, (ci + 1) * LN_CHUNK)
            u = (_normalize(xbuf[rows, :]) * scale + shift).astype(BF16)
            u_ref[rows, :] = u
            g_ref[rows, :] = lax.dot_general(u, wif, _CONTRACT_LAST, preferred_element_type=F32)
            p_ref[rows, :] = lax.dot_general(u, w, _CONTRACT_LAST, preferred_element_type=F32)

    @pl.when(n > 0)
    def _():
        p_ref[...] = lax.dot_general(u_ref[...], w_ref[...].astype(BF16), _CONTRACT_LAST,
                                     preferred_element_type=F32)


def _inproj(x2, mod, w_in_t, w_if_t):
    tiles_per_batch = SEQ // IN_TM
    return pl.pallas_call(
        _inproj_kernel,
        out_shape=(
            jax.ShapeDtypeStruct((TOKENS, P_WIDTH), F32),
            jax.ShapeDtypeStruct((TOKENS, LANES), F32),
            jax.ShapeDtypeStruct((TOKENS, D_MODEL), BF16),
        ),
        grid=(TOKENS // IN_TM, P_TILES_IN),
        in_specs=[
            pl.BlockSpec(memory_space=pl.ANY),
            pl.BlockSpec((None, 6, D_MODEL), lambda m, n: (m // tiles_per_batch, 0, 0)),
            pl.BlockSpec((PROJ_TN, D_MODEL), lambda m, n: (n, 0)),
            pl.BlockSpec((LANES, D_MODEL), lambda m, n: (0, 0)),
        ],
        out_specs=(
            pl.BlockSpec((IN_TM, PROJ_TN),
                         lambda m, n: (m, jnp.where(n < P_ATT_TILES, n + P_M_TILES, n - P_ATT_TILES))),
            pl.BlockSpec((IN_TM, LANES), lambda m, n: (m, 0)),
            pl.BlockSpec((IN_TM, D_MODEL), lambda m, n: (m, 0)),
        ),
        scratch_shapes=[pltpu.VMEM((IN_TM, D_MODEL), F32), pltpu.SemaphoreType.DMA],
        compiler_params=_cparams(("arbitrary", "arbitrary")),
        name="in_proj",
    )(x2, mod, w_in_t, w_if_t)


def _gateproj_kernel(u_ref, w_ref, b_ref, o_ref):
    acc = jnp.dot(u_ref[...], w_ref[...].astype(BF16), preferred_element_type=F32)
    o_ref[...] = jax.nn.sigmoid(acc + b_ref[...]).astype(BF16)


def _gateproj(u, w_gate, b_gate):
    n = w_gate.shape[1]
    return pl.pallas_call(
        _gateproj_kernel,
        out_shape=jax.ShapeDtypeStruct((TOKENS, n), BF16),
        grid=(TOKENS // PROJ_TM, n // GATE_TN),
        in_specs=[
            pl.BlockSpec((PROJ_TM, D_MODEL), lambda m, j: (m, 0)),
            pl.BlockSpec((D_MODEL, GATE_TN), lambda m, j: (0, j)),
            pl.BlockSpec((1, GATE_TN), lambda m, j: (0, j)),
        ],
        out_specs=pl.BlockSpec((PROJ_TM, GATE_TN), lambda m, j: (m, j)),
        compiler_params=_cparams(("arbitrary", "arbitrary")),
        name="gate_proj",
    )(u, w_gate, b_gate)


ROPE_ROWS = 256
ATT_LOCKSTEP = 8


def _rows(start, size, stride):
    return pl.ds(start, size) if stride == 1 else pl.ds(start, size, stride=stride)


def _attn_kernel(q0, k0, v0, q1, k1, v1, q2, k2, v2, cos_ref, sin_ref, y_ref,
                 qr_sc, kr_sc, o0, o1, o2, l0, l1, l2):
    scale = HEAD_DIM_A ** -0.5
    blk = ATT_BLOCK
    qi2 = lax.broadcasted_iota(jnp.int32, (blk, 2 * blk), 0)
    kc2 = lax.broadcasted_iota(jnp.int32, (blk, 2 * blk), 1)
    mask_prev_cur = (kc2 >= qi2) & (kc2 <= qi2 + blk)
    qi1 = lax.broadcasted_iota(jnp.int32, (blk, blk), 0)
    kc1 = lax.broadcasted_iota(jnp.int32, (blk, blk), 1)
    mask_cur = kc1 <= qi1
    half = HEAD_DIM_A // 2
    groups = ((q0, k0, v0, o0, l0), (q1, k1, v1, o1, l1), (q2, k2, v2, o2, l2))
    for (window, d), (q_ref, k_ref, v_ref, o_sc, l_sc) in zip(DIL_CONFIGS, groups):
        nb = SEQ // d // blk
        for src_ref, dst_sc in ((q_ref, qr_sc), (k_ref, kr_sc)):
            for c0 in range(0, SEQ, ROPE_ROWS):
                rs = slice(c0, c0 + ROPE_ROWS)
                xr = src_ref[rs, :]
                dst_sc[rs, :] = xr * cos_ref[rs, :] + pltpu.roll(xr, half, 1) * sin_ref[rs, :]
        blocks = [(r, j) for r in range(d) for j in range(nb)]
        for b0 in range(0, len(blocks), ATT_LOCKSTEP):
            batch = blocks[b0:b0 + ATT_LOCKSTEP]
            cur, keys, mask = [], [], []
            for r, j in batch:
                start = r + j * blk * d
                cur.append(_rows(start, blk, d))
                keys.append(cur[-1] if j == 0 else _rows(start - blk * d, 2 * blk, d))
                mask.append(mask_cur if j == 0 else mask_prev_cur)
            nbb = range(len(batch))
            qb = [qr_sc[cur[i], :].astype(BF16) for i in nbb]
            kw = [kr_sc[keys[i], :].astype(BF16) for i in nbb]
            s = [lax.dot_general(qb[i], kw[i], _CONTRACT_LAST, preferred_element_type=F32) for i in nbb]
            s = [jnp.where(mask[i], s[i], -jnp.inf) for i in nbb]
            mx = [jnp.max(s[i], axis=-1, keepdims=True) for i in nbb]
            p = [jnp.exp2((s[i] - mx[i]) * (scale * LOG2E)).astype(BF16) for i in nbb]
            vb = [v_ref[keys[i], :].astype(BF16) for i in nbb]
            pv = [jnp.dot(p[i], jnp.concatenate([vb[i], jnp.ones_like(vb[i])], axis=1),
                          preferred_element_type=F32) for i in nbb]
            for i in nbb:
                den = pv[i][:, HEAD_DIM_A:]
                o_sc[cur[i], :] = pv[i][:, :HEAD_DIM_A] / den
                l_sc[cur[i], :] = mx[i] * scale + jnp.log(den)
    for c0 in range(0, SEQ, ROPE_ROWS):
        rs = slice(c0, c0 + ROPE_ROWS)
        la = l0[rs, :]
        lb = l1[rs, :]
        lc = l2[rs, :]
        mx = jnp.maximum(jnp.maximum(la, lb), lc)
        ea = jnp.exp(la - mx)
        eb = jnp.exp(lb - mx)
        ec = jnp.exp(lc - mx)
        den = ea + eb + ec
        y_ref[rs, :] = ((ea / den) * o0[rs, :] + (eb / den) * o1[rs, :] + (ec / den) * o2[rs, :]).astype(BF16)


def _attention(p, cos_t, sin_t):
    for window, d in DIL_CONFIGS:
        assert window // d == ATT_BLOCK and SEQ % (d * ATT_BLOCK) == 0
    col0 = P_ATT_OFF // HEAD_DIM_A

    def slab(part, g):
        off = col0 + (part * A_QKV_W + g * A_GROUP_W) // HEAD_DIM_A
        return pl.BlockSpec((SEQ, HEAD_DIM_A), lambda b, h: (b, off + h))

    in_specs = [slab(part, g) for g in range(N_DIL_GROUPS) for part in range(3)]
    table = pl.BlockSpec((SEQ, HEAD_DIM_A), lambda b, h: (0, 0))
    return pl.pallas_call(
        _attn_kernel,
        out_shape=jax.ShapeDtypeStruct((TOKENS, A_GROUP_W), BF16),
        grid=(BATCH, HEADS_PER_GROUP),
        in_specs=in_specs + [table, table],
        out_specs=pl.BlockSpec((SEQ, HEAD_DIM_A), lambda b, h: (b, h)),
        scratch_shapes=[pltpu.VMEM((SEQ, HEAD_DIM_A), F32)] * 8,
        compiler_params=_cparams(("arbitrary", "arbitrary")),
        name="dil_attn",
    )(*([p] * 9), cos_t, sin_t)


CONV_HALO = SUBLANES


def _mlstm_kernel(q_ref, k_ref, v_ref, og_ref, gc_ref, gr_ref, bc_ref, br_ref, cw_ref, cb_ref, ng_ref,
                  y_ref, xq_sc, xk_sc, ct_sc, n_sc, m_sc):
    c = pl.program_id(1)
    L = M_CHUNK
    lo = CONV_HALO

    @pl.when(c == 0)
    def _():
        xq_sc[0:lo, :] = jnp.zeros((lo, M_W), F32)
        xk_sc[0:lo, :] = jnp.zeros((lo, M_W), F32)
        ct_sc[...] = jnp.zeros_like(ct_sc)
        n_sc[...] = jnp.zeros_like(n_sc)
        m_sc[...] = jnp.zeros_like(m_sc)

    xq_sc[lo:lo + L, :] = q_ref[...]
    xk_sc[lo:lo + L, :] = k_ref[...]

    def conv(x_sc, w, b):
        acc = x_sc[lo:lo + L, :] * w[CONV_K - 1:CONV_K, :] + b
        for j in range(CONV_K - 1):
            off = lo - (CONV_K - 1) + j
            acc = acc + x_sc[off:off + L, :] * w[j:j + 1, :]
        return acc

    cw = cw_ref[...]
    cb = cb_ref[...]
    qc = _silu(conv(xq_sc, cw[:, :M_W], cb[:, :M_W]))
    kc = _silu(conv(xk_sc, cw[:, M_W:], cb[:, M_W:])) * (M_HEAD_DIM ** -0.5)
    xq_sc[0:lo, :] = xq_sc[L:L + lo, :]
    xk_sc[0:lo, :] = xk_sc[L:L + lo, :]

    row = lax.broadcasted_iota(jnp.int32, (L, L), 0)
    col = lax.broadcasted_iota(jnp.int32, (L, L), 1)
    causal = row >= col
    heads = range(M_HEADS)
    hsl = [slice(h * M_HEAD_DIM, (h + 1) * M_HEAD_DIM) for h in heads]
    ct = [ct_sc[h] for h in heads]
    n_row = [n_sc[h, 0:1, :] for h in heads]
    m_prev = [m_sc[h, 0:1, 0:1] for h in heads]
    q = [qc[:, hsl[h]] for h in heads]
    k = [kc[:, hsl[h]] for h in heads]
    qb = [q[h].astype(BF16) for h in heads]
    kb = [k[h].astype(BF16) for h in heads]
    vb = [v_ref[:, hsl[h]].astype(BF16) for h in heads]
    qk = [lax.dot_general(qb[h], kb[h], _CONTRACT_LAST, preferred_element_type=F32) for h in heads]
    qc_state = [jnp.dot(qb[h], ct[h].astype(BF16), preferred_element_type=F32) for h in heads]

    i_col = [gc_ref[:, h:h + 1] + bc_ref[:, h:h + 1] for h in heads]
    lf_col = [_log_sigmoid(gc_ref[:, M_HEADS + h:M_HEADS + h + 1] + bc_ref[:, M_HEADS + h:M_HEADS + h + 1])
              for h in heads]
    i_row = [gr_ref[h:h + 1, :] + br_ref[h:h + 1, :] for h in heads]
    lf_row = [_log_sigmoid(gr_ref[M_HEADS + h:M_HEADS + h + 1, :] + br_ref[M_HEADS + h:M_HEADS + h + 1, :])
              for h in heads]
    b_col = [jnp.sum(jnp.where(causal, lf_row[h], 0.0), axis=1, keepdims=True) for h in heads]
    b_row = [jnp.sum(jnp.where(row <= col, lf_col[h], 0.0), axis=0, keepdims=True) for h in heads]
    log_d = [jnp.where(causal, b_col[h] - b_row[h] + i_row[h], -jnp.inf) for h in heads]
    log_inter = [b_col[h] + m_prev[h] for h in heads]
    m_t = [jnp.maximum(jnp.max(log_d[h], axis=1, keepdims=True), log_inter[h]) for h in heads]
    s = [qk[h] * jnp.exp(log_d[h] - m_t[h]) for h in heads]
    inter = [jnp.exp(log_inter[h] - m_t[h]) for h in heads]
    sv = [jnp.dot(s[h].astype(BF16), vb[h], preferred_element_type=F32) for h in heads]

    m_new = [m_t[h][L - 1:L, :] for h in heads]
    b_last = [b_col[h][L - 1:L, :] for h in heads]
    w_col = [jnp.exp(b_last[h] - b_col[h] + i_col[h] - m_new[h]) for h in heads]
    decay = [jnp.exp(b_last[h] + m_prev[h] - m_new[h]) for h in heads]
    kw = [k[h] * w_col[h] for h in heads]
    kv = [lax.dot_general(kw[h].astype(BF16), vb[h], (((0,), (0,)), ((), ())), preferred_element_type=F32)
          for h in heads]

    num = [sv[h] + inter[h] * qc_state[h] for h in heads]
    den = [jnp.sum(s[h], axis=1, keepdims=True) + inter[h] * jnp.sum(q[h] * n_row[h], axis=1, keepdims=True)
           for h in heads]
    hh = [num[h] / jnp.maximum(jnp.abs(den[h]), jnp.exp(-m_t[h])) for h in heads]
    z = [jax.nn.sigmoid(og_ref[:, hsl[h]]) * hh[h] for h in heads]
    y = [(_normalize(z[h]) * ng_ref[:, hsl[h]]).astype(BF16) for h in heads]
    ct_new = [decay[h] * ct[h] + kv[h] for h in heads]
    n_new = [decay[h] * n_row[h] + jnp.sum(kw[h], axis=0, keepdims=True) for h in heads]
    for h in heads:
        ct_sc[h] = ct_new[h]
        n_sc[h] = jnp.broadcast_to(n_new[h], (SUBLANES, M_HEAD_DIM))
        m_sc[h] = jnp.broadcast_to(m_new[h], (SUBLANES, LANES))
        y_ref[:, hsl[h]] = y[h]


def _mlstm(p, gcol, grow, bcol, brow, conv_w, conv_b, norm_g):
    nc = SEQ // M_CHUNK
    slab = lambda blk: pl.BlockSpec((M_CHUNK, M_W), lambda b, c: (b * nc + c, blk))
    return pl.pallas_call(
        _mlstm_kernel,
        out_shape=jax.ShapeDtypeStruct((TOKENS, M_W), BF16),
        grid=(BATCH, nc),
        in_specs=[
            slab(P_QM_BLK), slab(P_KM_BLK), slab(P_VM_BLK), slab(P_OM_BLK),
            pl.BlockSpec((None, M_CHUNK, 2 * M_HEADS), lambda b, c: (b, c, 0)),
            pl.BlockSpec((None, 2 * M_HEADS, M_CHUNK), lambda b, c: (b, 0, c)),
            pl.BlockSpec((1, 2 * M_HEADS), lambda b, c: (0, 0)),
            pl.BlockSpec((2 * M_HEADS, 1), lambda b, c: (0, 0)),
            pl.BlockSpec((CONV_K, 2 * M_W), lambda b, c: (0, 0)),
            pl.BlockSpec((1, 2 * M_W), lambda b, c: (0, 0)),
            pl.BlockSpec((1, M_W), lambda b, c: (0, 0)),
        ],
        out_specs=pl.BlockSpec((M_CHUNK, M_W), lambda b, c: (b * nc + c, 0)),
        scratch_shapes=[
            pltpu.VMEM((M_CHUNK + CONV_HALO, M_W), F32),
            pltpu.VMEM((M_CHUNK + CONV_HALO, M_W), F32),
            pltpu.VMEM((M_HEADS, M_HEAD_DIM, M_HEAD_DIM), F32),
            pltpu.VMEM((M_HEADS, SUBLANES, M_HEAD_DIM), F32),
            pltpu.VMEM((M_HEADS, SUBLANES, LANES), F32),
        ],
        compiler_params=_cparams(("arbitrary", "arbitrary")),
        name="mlstm",
    )(p, p, p, p, gcol, grow, bcol, brow, conv_w, conv_b, norm_g)


RT_E1, RT_E2, RT_W1, RT_W2 = 0, 1, 2, 3


def _route(logits):
    lane = lax.broadcasted_iota(jnp.int32, logits.shape, 1).astype(F32)
    big = float(LANES)
    is_g = lane < N_EXPERT_GROUPS
    gl = jnp.where(is_g, logits, -jnp.inf)
    gexp = jnp.exp(gl - jnp.max(gl, axis=1, keepdims=True))
    gprob = gexp / jnp.sum(gexp, axis=1, keepdims=True)
    g_w = jnp.max(gprob, axis=1, keepdims=True)
    g_top = jnp.min(jnp.where(is_g & (gprob == g_w), lane, big), axis=1, keepdims=True)
    lo = N_EXPERT_GROUPS + EXPERTS_PER_GROUP * g_top
    in_grp = (lane >= lo) & (lane < lo + EXPERTS_PER_GROUP)
    el = jnp.where(in_grp, logits, -jnp.inf)
    eexp = jnp.exp(el - jnp.max(el, axis=1, keepdims=True))
    eprob = eexp / jnp.sum(eexp, axis=1, keepdims=True)
    v1 = jnp.max(eprob, axis=1, keepdims=True)
    i1 = jnp.min(jnp.where(in_grp & (eprob == v1), lane, big), axis=1, keepdims=True)
    rest = jnp.where(in_grp & (lane != i1), eprob, -1.0)
    v2 = jnp.max(rest, axis=1, keepdims=True)
    i2 = jnp.min(jnp.where(rest == v2, lane, big), axis=1, keepdims=True)
    tot = v1 + v2
    w1 = g_w * (v1 / tot)
    w2 = g_w * (v2 / tot)
    e1 = i1 - N_EXPERT_GROUPS
    e2 = i2 - N_EXPERT_GROUPS
    rec = jnp.where(lane == RT_E1, e1, jnp.where(lane == RT_E2, e2, jnp.where(lane == RT_W1, w1, w2)))
    return jnp.where(lane <= RT_W2, rec, 0.0)


def _merge_kernel(ya_ref, ym_ref, g_ref, x_ref, mod_ref,
                  wpa_ref, wpm_ref, wout_ref, lng_ref, lnb_ref, wr_ref, br_ref,
                  x1_ref, u2_ref, rt_ref):
    nsb = MERGE_TM // MERGE_SUB
    rows = [slice(sb * MERGE_SUB, (sb + 1) * MERGE_SUB) for sb in range(nsb)]

    def matmul_part(i):
        pa = jnp.dot(ya_ref[rows[i], :], wpa_ref[...], preferred_element_type=F32)
        pm = jnp.dot(ym_ref[rows[i], :], wpm_ref[...], preferred_element_type=F32)
        merged = g_ref[rows[i], :D_MODEL].astype(F32) * pa + g_ref[rows[i], D_MODEL:].astype(F32) * pm
        return jnp.dot(merged.astype(BF16), wout_ref[...], preferred_element_type=F32)

    def norm_part(i, mix):
        z = DEEPNORM_ALPHA * x_ref[rows[i], :] + mod_ref[2:3, :] * mix
        x1 = _normalize(z) * lng_ref[...] + lnb_ref[...]
        x1_ref[rows[i], :] = x1
        u2 = _normalize(x1) * (1.0 + mod_ref[4:5, :]) + mod_ref[3:4, :]
        u2_ref[rows[i], :] = u2
        logits = jnp.dot(u2.astype(BF16), wr_ref[...], preferred_element_type=F32) + br_ref[...]
        rt_ref[rows[i], :] = _route(logits)

    mix = matmul_part(0)
    for i in range(nsb):
        nxt = matmul_part(i + 1) if i + 1 < nsb else None
        norm_part(i, mix)
        mix = nxt


def _merge(ya, ym, g, x2, mod, wpa, wpm, wout, lng, lnb, wr, br):
    tm = MERGE_TM
    tiles_per_batch = SEQ // tm
    rowblk = lambda w: pl.BlockSpec((tm, w), lambda m: (m, 0))
    const = lambda shape: pl.BlockSpec(shape, lambda m: (0,) * len(shape), pipeline_mode=pl.Buffered(1))
    return pl.pallas_call(
        _merge_kernel,
        out_shape=(
            jax.ShapeDtypeStruct((TOKENS, D_MODEL), F32),
            jax.ShapeDtypeStruct((TOKENS, D_MODEL), F32),
            jax.ShapeDtypeStruct((TOKENS, LANES), F32),
        ),
        grid=(TOKENS // tm,),
        in_specs=[
            rowblk(A_GROUP_W), rowblk(M_W), rowblk(2 * D_MODEL), rowblk(D_MODEL),
            pl.BlockSpec((None, 6, D_MODEL), lambda m: (m // tiles_per_batch, 0, 0)),
            const((A_GROUP_W, D_MODEL)), const((M_W, D_MODEL)), const((D_MODEL, D_MODEL)),
            const((1, D_MODEL)), const((1, D_MODEL)),
            const((D_MODEL, LANES)), const((1, LANES)),
        ],
        out_specs=(rowblk(D_MODEL), rowblk(D_MODEL), rowblk(LANES)),
        compiler_params=_cparams(("arbitrary",)),
        name="merge_ln1_route",
    )(ya, ym, g, x2, mod, wpa, wpm, wout, lng, lnb, wr, br)


CAST_ROWS = 128


def _issue_rows(src_ref, idx_ref, base, buf, slot, sem, nrows):
    def body(blk, carry):
        for j in range(SUBLANES):
            row = idx_ref[base + blk * SUBLANES + j]
            src = src_ref.at[lax.shift_right_logical(row, 3), pl.ds(row & (SUBLANES - 1), 1)]
            pltpu.make_async_copy(src, buf.at[slot, blk, pl.ds(j, 1)], sem.at[slot]).start(priority=j % 2)
        return carry

    lax.fori_loop(0, nrows // SUBLANES, body, 0)


def _wait_rows(src_ref, buf, slot, sem, nrows):
    groups = nrows // SUBLANES
    pltpu.make_async_copy(src_ref.at[pl.ds(0, groups)], buf.at[slot, pl.ds(0, groups)], sem.at[slot]).wait()


class _TileTable:
    def __init__(self, meta_ref):
        self.ref = meta_ref

    def expert(self, t):
        return self.ref[META_TE * LANES + t]

    def next_expert(self, t):
        return self.ref[META_TN * LANES + t]

    def valid_rows(self, t):
        return self.ref[META_TV * LANES + t]

    def tiles_used(self):
        return self.ref[META_NU * LANES]


def _expert_changed(te_ref, i):
    return (i == 0) | (te_ref.expert(i) != te_ref.expert(jnp.maximum(i - 1, 0)))


def _moe_kernel(meta_ref, tok_ref, u_ref, wg_ref, wu_ref, wd_ref, o_ref,
                stg, stu, std, wgb, wub, wdb, xbuf, wsem, xsem):
    i = pl.program_id(0)
    te_ref = _TileTable(meta_ref)
    nu = te_ref.tiles_used()
    tm = o_ref.shape[0]
    half = tm // 2

    def weight_copies(e):
        return (pltpu.make_async_copy(wg_ref.at[e], stg, wsem.at[0]),
                pltpu.make_async_copy(wu_ref.at[e], stu, wsem.at[1]),
                pltpu.make_async_copy(wd_ref.at[e], std, wsem.at[2]))

    def for_tile_rows(t, fn):
        @pl.when(te_ref.valid_rows(t) <= half)
        def _():
            fn(half)

        @pl.when(te_ref.valid_rows(t) > half)
        def _():
            fn(tm)

    def issue_tile(t):
        for_tile_rows(t, lambda nrows: _issue_rows(u_ref, tok_ref, t * tm, xbuf, t % MOE_ROW_SLOTS, xsem, nrows))

    @pl.when(i == 0)
    def _():
        issue_tile(0)

        @pl.when(nu > 1)
        def _():
            issue_tile(1)

        for cp in weight_copies(te_ref.expert(0)):
            cp.start(priority=1)

    @pl.when(i < nu)
    def _():
        slot = i % MOE_ROW_SLOTS

        @pl.when(i + 2 < nu)
        def _():
            issue_tile(i + 2)

        changed = _expert_changed(te_ref, i)

        @pl.when(changed)
        def _():
            for cp in weight_copies(te_ref.expert(i)):
                cp.wait()
            for src, dst in ((stg, wgb), (stu, wub)):
                def cast_rows(ci, carry, src=src, dst=dst):
                    r = pl.multiple_of(ci * CAST_ROWS, CAST_ROWS)
                    dst[pl.ds(r, CAST_ROWS), :] = src[pl.ds(r, CAST_ROWS), :].astype(BF16)
                    return carry

                lax.fori_loop(0, src.shape[0] // CAST_ROWS, cast_rows, 0)

        def compute(nrows, new_expert):
            _wait_rows(u_ref, xbuf, slot, xsem, nrows)
            x = xbuf[slot, 0:nrows // SUBLANES].reshape(nrows, D_MODEL).astype(BF16)
            a = jnp.dot(x, wgb[...], preferred_element_type=F32)
            b = jnp.dot(x, wub[...], preferred_element_type=F32)
            if new_expert:
                for r0 in range(0, D_FF_EXPERT, CAST_ROWS):
                    wdb[r0:r0 + CAST_ROWS, :] = std[r0:r0 + CAST_ROWS, :].astype(BF16)

                @pl.when(te_ref.next_expert(i) >= 0)
                def _():
                    for cp in weight_copies(te_ref.next_expert(i)):
                        cp.start(priority=1)
            h = (_silu(a) * b).astype(BF16)
            o_ref[0:nrows, :] = jnp.dot(h, wdb[...], preferred_element_type=F32)
            if nrows < tm:
                o_ref[nrows:tm, :] = jnp.zeros((tm - nrows, D_MODEL), F32)

        @pl.when(changed)
        def _():
            for_tile_rows(i, functools.partial(compute, new_expert=True))

        @pl.when(jnp.logical_not(changed))
        def _():
            for_tile_rows(i, functools.partial(compute, new_expert=False))

    @pl.when(i >= nu)
    def _():
        o_ref[...] = jnp.zeros_like(o_ref)


def _moe(meta, row_token, u2, w_eg, w_eu, w_ed):
    tm = MOE_TM
    return pl.pallas_call(
        _moe_kernel,
        out_shape=jax.ShapeDtypeStruct((MOE_ROWS, D_MODEL), F32),
        grid_spec=pltpu.PrefetchScalarGridSpec(
            num_scalar_prefetch=2,
            grid=(MOE_TILES,),
            in_specs=[pl.BlockSpec(memory_space=pl.ANY)] * 4,
            out_specs=pl.BlockSpec((tm, D_MODEL), lambda i, meta, tok: (i, 0)),
            scratch_shapes=[
                pltpu.VMEM((D_MODEL, D_FF_EXPERT), F32),
                pltpu.VMEM((D_MODEL, D_FF_EXPERT), F32),
                pltpu.VMEM((D_FF_EXPERT, D_MODEL), F32),
                pltpu.VMEM((D_MODEL, D_FF_EXPERT), BF16),
                pltpu.VMEM((D_MODEL, D_FF_EXPERT), BF16),
                pltpu.VMEM((D_FF_EXPERT, D_MODEL), BF16),
                pltpu.VMEM((MOE_ROW_SLOTS, tm // SUBLANES, SUBLANES, D_MODEL), F32),
                pltpu.SemaphoreType.DMA((3,)),
                pltpu.SemaphoreType.DMA((MOE_ROW_SLOTS,)),
            ],
        ),
        compiler_params=_cparams(("arbitrary",)),
        name="moe_experts",
    )(meta, row_token, u2.reshape(TOKENS // SUBLANES, SUBLANES, D_MODEL), w_eg, w_eu, w_ed)


def _final_kernel(pos_ref, o_ref, x1_ref, rt_ref, mod_ref, lng_ref, lnb_ref, y_ref, buf, sem):
    i = pl.program_id(0)
    tb = x1_ref.shape[0]
    slot = i % 2

    @pl.when(i == 0)
    def _():
        _issue_rows(o_ref, pos_ref, 0, buf, 0, sem, 2 * tb)

    @pl.when(i + 1 < pl.num_programs(0))
    def _():
        _issue_rows(o_ref, pos_ref, (i + 1) * (2 * tb), buf, 1 - slot, sem, 2 * tb)

    _wait_rows(o_ref, buf, slot, sem, 2 * tb)
    rows = buf[slot].reshape(2 * tb, D_MODEL)
    ffn = rt_ref[:, RT_W1:RT_W1 + 1] * rows[0:tb, :] + rt_ref[:, RT_W2:RT_W2 + 1] * rows[tb:2 * tb, :]
    z = DEEPNORM_ALPHA * x1_ref[...] + mod_ref[5:6, :] * ffn
    y_ref[...] = _normalize(z) * lng_ref[...] + lnb_ref[...]


def _final(pos_tiles, moe_out, x1, rt, mod, lng, lnb):
    tb = FINAL_TB
    tiles_per_batch = SEQ // tb
    return pl.pallas_call(
        _final_kernel,
        out_shape=jax.ShapeDtypeStruct((TOKENS, D_MODEL), F32),
        grid_spec=pltpu.PrefetchScalarGridSpec(
            num_scalar_prefetch=1,
            grid=(TOKENS // tb,),
            in_specs=[
                pl.BlockSpec(memory_space=pl.ANY),
                pl.BlockSpec((tb, D_MODEL), lambda i, pos: (i, 0)),
                pl.BlockSpec((tb, LANES), lambda i, pos: (i, 0)),
                pl.BlockSpec((None, 6, D_MODEL), lambda i, pos: (i // tiles_per_batch, 0, 0)),
                pl.BlockSpec((1, D_MODEL), lambda i, pos: (0, 0)),
                pl.BlockSpec((1, D_MODEL), lambda i, pos: (0, 0)),
            ],
            out_specs=pl.BlockSpec((tb, D_MODEL), lambda i, pos: (i, 0)),
            scratch_shapes=[pltpu.VMEM((2, 2 * tb // SUBLANES, SUBLANES, D_MODEL), F32),
                            pltpu.SemaphoreType.DMA((2,))],
        ),
        compiler_params=_cparams(("arbitrary",)),
        name="combine_ln2",
    )(pos_tiles, moe_out.reshape(MOE_ROWS // SUBLANES, SUBLANES, D_MODEL), x1, rt, mod, lng, lnb)


ROUTE_BLK = 256
ROUTE_LOCKSTEP = 4
META_TE, META_TN, META_TV, META_NU = 0, 1, 2, 3


def _lane_cumsum(x, lane):
    s = 1
    while s < LANES:
        x = x + jnp.where(lane >= s, pltpu.roll(x, s, 1), 0.0)
        s *= 2
    return x


def _route_tables_kernel(rt_ref, pos_ref, meta_ref, rank_sc):
    tm = float(MOE_TM)
    nblk = TOKENS // ROUTE_BLK
    lane_i = lax.broadcasted_iota(jnp.int32, (ROUTE_BLK, LANES), 1)
    lane_f = lane_i.astype(F32)
    earlier = (lax.broadcasted_iota(jnp.int32, (ROUTE_BLK, ROUTE_BLK), 0)
               > lax.broadcasted_iota(jnp.int32, (ROUTE_BLK, ROUTE_BLK), 1)).astype(BF16)

    def onehots(b):
        blk = rt_ref[pl.ds(pl.multiple_of(b * ROUTE_BLK, ROUTE_BLK), ROUTE_BLK), :]
        return blk[:, RT_E1:RT_E1 + 1] == lane_f, blk[:, RT_E2:RT_E2 + 1] == lane_f

    def count_pass(g, carry):
        c1, c2 = carry
        bs = [g * ROUTE_LOCKSTEP + i for i in range(ROUTE_LOCKSTEP)]
        oh = [onehots(b) for b in bs]
        f = [(jnp.where(o1, 1.0, 0.0), jnp.where(o2, 1.0, 0.0)) for o1, o2 in oh]
        p = [(jnp.dot(earlier, f1.astype(BF16), preferred_element_type=F32),
              jnp.dot(earlier, f2.astype(BF16), preferred_element_type=F32)) for f1, f2 in f]
        tot = [(jnp.sum(f1, axis=0, keepdims=True), jnp.sum(f2, axis=0, keepdims=True)) for f1, f2 in f]
        for i, b in enumerate(bs):
            r1 = jnp.sum(jnp.where(oh[i][0], p[i][0] + c1, 0.0), axis=1, keepdims=True)
            r2 = jnp.sum(jnp.where(oh[i][1], p[i][1] + c2, 0.0), axis=1, keepdims=True)
            rank_sc[pl.ds(pl.multiple_of(b * ROUTE_BLK, ROUTE_BLK), ROUTE_BLK), :] = jnp.where(
                lane_i == 0, r1, jnp.where(lane_i == 1, r2, 0.0))
            c1 = c1 + tot[i][0]
            c2 = c2 + tot[i][1]
        return c1, c2

    zero = jnp.zeros((1, LANES), F32)
    c1, c2 = lax.fori_loop(0, nblk // ROUTE_LOCKSTEP, count_pass, (zero, zero))

    lane8 = lax.broadcasted_iota(jnp.int32, (SUBLANES, LANES), 1)
    counts = jnp.broadcast_to(c1 + c2, (SUBLANES, LANES))
    padded = jnp.floor((counts + (tm - 1.0)) * (1.0 / tm)) * tm
    pend = _lane_cumsum(padded, lane8)
    pstart = pend - padded
    start1 = pstart[0:1, :]
    start2 = start1 + c1

    def place_pass(g, carry):
        bs = [g * ROUTE_LOCKSTEP + i for i in range(ROUTE_LOCKSTEP)]
        oh = [onehots(b) for b in bs]
        rows = [pl.ds(pl.multiple_of(b * ROUTE_BLK, ROUTE_BLK), ROUTE_BLK) for b in bs]
        rk = [rank_sc[r, :] for r in rows]
        d1 = [jnp.sum(jnp.where(oh[i][0], start1, 0.0), axis=1, keepdims=True) + rk[i][:, 0:1]
              for i in range(ROUTE_LOCKSTEP)]
        d2 = [jnp.sum(jnp.where(oh[i][1], start2, 0.0), axis=1, keepdims=True) + rk[i][:, 1:2]
              for i in range(ROUTE_LOCKSTEP)]
        for i in range(ROUTE_LOCKSTEP):
            pos_ref[rows[i], :] = jnp.where(lane_i == 0, d1[i], jnp.where(lane_i == 1, d2[i], 0.0)).astype(jnp.int32)
        return carry

    lax.fori_loop(0, nblk // ROUTE_LOCKSTEP, place_pass, 0)

    tile = lax.broadcasted_iota(jnp.int32, (LANES, LANES), 0).astype(F32)
    lane = lax.broadcasted_iota(jnp.int32, (LANES, LANES), 1)
    expert_lane = lane < N_EXPERTS
    pend_b = jnp.broadcast_to(pend[0:1, :], (LANES, LANES))
    pstart_b = jnp.broadcast_to(pstart[0:1, :], (LANES, LANES))
    counts_b = jnp.broadcast_to(counts[0:1, :], (LANES, LANES))
    n_used = jnp.sum(jnp.where(lane == N_EXPERTS - 1, pend_b, 0.0), axis=1, keepdims=True) * (1.0 / tm)
    te = jnp.sum(jnp.where(expert_lane & (pend_b <= tile * tm), 1.0, 0.0), axis=1, keepdims=True)
    te = jnp.minimum(te, float(N_EXPERTS - 1))
    te_last = jnp.sum(jnp.where(tile[:, 0:1] == n_used - 1.0, te, 0.0), axis=0, keepdims=True)
    te = jnp.where(tile[:, 0:1] < n_used, te, te_last)
    of_tile = lane.astype(F32) == te
    pend_te = jnp.sum(jnp.where(of_tile, pend_b, 0.0), axis=1, keepdims=True)
    last_row = jnp.sum(jnp.where(of_tile, pstart_b + counts_b, 0.0), axis=1, keepdims=True)
    tile_rows = jnp.clip(last_row - tile[:, 0:1] * tm, 0.0, tm)
    next_run = pend_te * (1.0 / tm)
    te_by_lane = jnp.broadcast_to(te, (LANES, LANES)).T
    te_at_next = jnp.sum(jnp.where(lane.astype(F32) == next_run, te_by_lane, 0.0), axis=1, keepdims=True)
    te_next = jnp.where(next_run < n_used, te_at_next, -1.0)
    cols = jnp.where(lane == META_TE, te, jnp.where(lane == META_TN, te_next,
                     jnp.where(lane == META_TV, tile_rows, jnp.where(lane == META_NU, n_used, 0.0))))
    meta_ref[...] = cols.T[0:SUBLANES, :].astype(jnp.int32)


def _routing_tables(rt):
    pos, meta = pl.pallas_call(
        _route_tables_kernel,
        out_shape=(
            jax.ShapeDtypeStruct((TOKENS, LANES), jnp.int32),
            jax.ShapeDtypeStruct((SUBLANES, LANES), jnp.int32),
        ),
        scratch_shapes=[pltpu.VMEM((TOKENS, LANES), F32)],
        compiler_params=_cparams(None),
        name="route_tables",
    )(rt)
    pos = pos[:, 0:2]
    tok = jnp.arange(2 * TOKENS, dtype=jnp.int32) // 2
    row_token = jnp.zeros((MOE_ROWS,), jnp.int32).at[pos.reshape(-1)].set(
        tok, unique_indices=True, mode="promise_in_bounds")
    return row_token, meta.reshape(-1), pos


def _rope_tables():
    inv = ROPE_THETA ** (-jnp.arange(0, HEAD_DIM_A, 2, dtype=F32) / HEAD_DIM_A)
    ang = jnp.arange(SEQ, dtype=F32)[:, None] * inv[None, :]
    cos = jnp.cos(ang)
    sin = jnp.sin(ang)
    return jnp.concatenate([cos, cos], axis=-1), jnp.concatenate([-sin, sin], axis=-1)


def kernel(x, c, w_ada, b_ada, w_in, b_mgate, conv_w, conv_b, m_norm_g, w_proj_a, w_proj_m, w_gate, b_gate,
           w_out, ln1_g, ln1_b, w_rg, b_rg, w_re, b_re, w_eg, w_eu, w_ed, ln2_g, ln2_b):
    assert x.shape == (BATCH, SEQ, D_MODEL) and w_ada.shape[0] == 1
    l = 0
    x2 = x.reshape(TOKENS, D_MODEL)

    c_pad = jnp.zeros((ADA_ROWS, D_MODEL), F32).at[:BATCH].set(c)
    mod = _ada(c_pad, w_ada[l], b_ada[l][None, :])[:BATCH].reshape(BATCH, 6, D_MODEL)

    w_in_t = jnp.swapaxes(w_in[l], 0, 1)
    w_if_t = jnp.zeros((LANES, D_MODEL), F32).at[:2 * M_HEADS].set(w_in_t[N_IN_MAIN:])
    p, gates, u = _inproj(x2, mod, w_in_t, w_if_t)
    g = _gateproj(u, w_gate[l], b_gate[l][None, :])

    cos_t, sin_t = _rope_tables()
    ya = _attention(p, cos_t, sin_t)

    gcol = gates[:, :2 * M_HEADS].reshape(BATCH, SEQ, 2 * M_HEADS)
    grow = jnp.transpose(gcol, (0, 2, 1))
    ym = _mlstm(p, gcol, grow, b_mgate[l][None, :], b_mgate[l][:, None], conv_w[l], conv_b[l][None, :],
                m_norm_g[l][None, :])

    wr = (jnp.zeros((D_MODEL, LANES), F32)
          .at[:, :N_EXPERT_GROUPS].set(w_rg[l])
          .at[:, N_EXPERT_GROUPS:N_EXPERT_GROUPS + N_EXPERTS].set(w_re[l])).astype(BF16)
    br = (jnp.zeros((1, LANES), F32)
          .at[0, :N_EXPERT_GROUPS].set(b_rg[l])
          .at[0, N_EXPERT_GROUPS:N_EXPERT_GROUPS + N_EXPERTS].set(b_re[l]))
    x1, u2, rt = _merge(ya, ym, g, x2, mod,
                        w_proj_a[l].astype(BF16), w_proj_m[l].astype(BF16), w_out[l].astype(BF16),
                        ln1_g[l][None, :], ln1_b[l][None, :], wr, br)

    row_token, meta, pos = _routing_tables(rt)

    mo = _moe(meta, row_token, u2,
              w_eg[l].reshape(N_EXPERTS, D_MODEL, D_FF_EXPERT), w_eu[l].reshape(N_EXPERTS, D_MODEL, D_FF_EXPERT),
              w_ed[l].reshape(N_EXPERTS, D_FF_EXPERT, D_MODEL))

    nt = TOKENS // FINAL_TB
    pos_tiles = jnp.transpose(pos.reshape(nt, FINAL_TB, 2), (0, 2, 1)).reshape(-1)
    y = _final(pos_tiles, mo, x1, rt, mod, ln2_g[l][None, :], ln2_b[l][None, :])
    return y.reshape(BATCH, SEQ, D_MODEL)
```

```python
import functools

import jax
import jax.numpy as jnp
from jax import lax
from jax.experimental import pallas as pl
from jax.experimental.pallas import tpu as pltpu

F32 = jnp.float32
BF16 = jnp.bfloat16

D_MODEL = 2048
BATCH = 4
SEQ = 2048
TOKENS = BATCH * SEQ
DIL_CONFIGS = ((128, 1), (512, 4), (2048, 16))
N_DIL_GROUPS = 3
HEADS_PER_GROUP = 4
HEAD_DIM_A = 128
ATT_BLOCK = 128
ROPE_THETA = 10000.0
A_GROUP_W = HEADS_PER_GROUP * HEAD_DIM_A
A_QKV_W = N_DIL_GROUPS * A_GROUP_W
M_HEADS = 4
M_HEAD_DIM = 256
M_W = M_HEADS * M_HEAD_DIM
M_CHUNK = 128
CONV_K = 4
N_IN_MAIN = 3 * A_QKV_W + 4 * M_W
N_EXPERT_GROUPS = 4
EXPERTS_PER_GROUP = 8
N_EXPERTS = N_EXPERT_GROUPS * EXPERTS_PER_GROUP
D_FF_EXPERT = 1024
DEEPNORM_ALPHA = 2.0 ** 0.25
LN_EPS = 1e-5

LANES = 128
SUBLANES = 8
VMEM_LIMIT_BYTES = 56 * 1024 * 1024

PROJ_TN = 256
GATE_TN = 1024
PROJ_TM = 1024
IN_TM = 2048
P_TILES_IN = N_IN_MAIN // PROJ_TN
P_ATT_TILES = 3 * A_QKV_W // PROJ_TN
P_M_TILES = P_TILES_IN - P_ATT_TILES
P_WIDTH = N_IN_MAIN
P_ATT_OFF = P_M_TILES * PROJ_TN
P_QM_BLK, P_KM_BLK, P_VM_BLK, P_OM_BLK = 0, 1, 2, 3
MERGE_TM = 256
MERGE_SUB = 256
MOE_TM = 256
MOE_ROWS = 2 * TOKENS + N_EXPERTS * MOE_TM
MOE_TILES = MOE_ROWS // MOE_TM
MOE_ROW_SLOTS = 3
FINAL_TB = 256


def _cparams(sem, vmem=VMEM_LIMIT_BYTES):
    return pltpu.CompilerParams(dimension_semantics=sem, vmem_limit_bytes=vmem)


def _normalize(x):
    mu = jnp.mean(x, axis=-1, keepdims=True)
    xc = x - mu
    var = jnp.mean(xc * xc, axis=-1, keepdims=True)
    return xc * lax.rsqrt(var + LN_EPS)


def _silu(x):
    return x * jax.nn.sigmoid(x)


def _log_sigmoid(x):
    return jnp.minimum(x, 0.0) - jnp.log(1.0 + jnp.exp(-jnp.abs(x)))


_CONTRACT_LAST = (((1,), (1,)), ((), ()))
LOG2E = 1.4426950408889634


ADA_TN = 1024
ADA_ROWS = 16


def _ada_kernel(c_ref, w_ref, b_ref, o_ref):
    sc = _silu(c_ref[...]).astype(BF16)
    o_ref[...] = jnp.dot(sc, w_ref[...].astype(BF16), preferred_element_type=F32) + b_ref[...]


def _ada(c_pad, w_ada, b_ada):
    n = w_ada.shape[1]
    return pl.pallas_call(
        _ada_kernel,
        out_shape=jax.ShapeDtypeStruct((ADA_ROWS, n), F32),
        grid=(n // ADA_TN,),
        in_specs=[
            pl.BlockSpec((ADA_ROWS, D_MODEL), lambda j: (0, 0)),
            pl.BlockSpec((D_MODEL, ADA_TN), lambda j: (0, j)),
            pl.BlockSpec((1, ADA_TN), lambda j: (0, j)),
        ],
        out_specs=pl.BlockSpec((ADA_ROWS, ADA_TN), lambda j: (0, j)),
        compiler_params=_cparams(("arbitrary",)),
        name="ada_mod",
    )(c_pad, w_ada, b_ada)


LN_CHUNK = 256


def _inproj_kernel(x_ref, mod_ref, w_ref, wif_ref, p_ref, g_ref, u_ref, xbuf, xsem):
    m = pl.program_id(0)
    n = pl.program_id(1)

    def x_copy(tile):
        return pltpu.make_async_copy(x_ref.at[pl.ds(pl.multiple_of(tile * IN_TM, IN_TM), IN_TM)], xbuf, xsem)

    @pl.when((m == 0) & (n == 0))
    def _():
        x_copy(0).start()

    @pl.when((n == 1) & (m + 1 < pl.num_programs(0)))
    def _():
        x_copy(m + 1).start()

    @pl.when(n == 0)
    def _():
        x_copy(m).wait()
        shift = mod_ref[0:1, :]
        scale = 1.0 + mod_ref[1:2, :]
        wif = wif_ref[...].astype(BF16)
        w = w_ref[...].astype(BF16)
        for ci in range(IN_TM // LN_CHUNK):
            rows = slice(ci * LN_CHUNK, (ci + 1) * LN_CHUNK)
            u = (_normalize(xbuf[rows, :]) * scale + shift).astype(BF16)
            u_ref[rows, :] = u
            g_ref[rows, :] = lax.dot_general(u, wif, _CONTRACT_LAST, preferred_element_type=F32)
            p_ref[rows, :] = lax.dot_general(u, w, _CONTRACT_LAST, preferred_element_type=F32)

    @pl.when(n > 0)
    def _():
        p_ref[...] = lax.dot_general(u_ref[...], w_ref[...].astype(BF16), _CONTRACT_LAST,
                                     preferred_element_type=F32)


def _inproj(x2, mod, w_in_t, w_if_t):
    tiles_per_batch = SEQ // IN_TM
    return pl.pallas_call(
        _inproj_kernel,
        out_shape=(
            jax.ShapeDtypeStruct((TOKENS, P_WIDTH), F32),
            jax.ShapeDtypeStruct((TOKENS, LANES), F32),
            jax.ShapeDtypeStruct((TOKENS, D_MODEL), BF16),
        ),
        grid=(TOKENS // IN_TM, P_TILES_IN),
        in_specs=[
            pl.BlockSpec(memory_space=pl.ANY),
            pl.BlockSpec((None, 6, D_MODEL), lambda m, n: (m // tiles_per_batch, 0, 0)),
            pl.BlockSpec((PROJ_TN, D_MODEL), lambda m, n: (n, 0)),
            pl.BlockSpec((LANES, D_MODEL), lambda m, n: (0, 0)),
        ],
        out_specs=(
            pl.BlockSpec((IN_TM, PROJ_TN),
                         lambda m, n: (m, jnp.where(n < P_ATT_TILES, n + P_M_TILES, n - P_ATT_TILES))),
            pl.BlockSpec((IN_TM, LANES), lambda m, n: (m, 0)),
            pl.BlockSpec((IN_TM, D_MODEL), lambda m, n: (m, 0)),
        ),
        scratch_shapes=[pltpu.VMEM((IN_TM, D_MODEL), F32), pltpu.SemaphoreType.DMA],
        compiler_params=_cparams(("arbitrary", "arbitrary")),
        name="in_proj",
    )(x2, mod, w_in_t, w_if_t)


def _gateproj_kernel(u_ref, w_ref, b_ref, o_ref):
    acc = jnp.dot(u_ref[...], w_ref[...].astype(BF16), preferred_element_type=F32)
    o_ref[...] = jax.nn.sigmoid(acc + b_ref[...]).astype(BF16)


def _gateproj(u, w_gate, b_gate):
    n = w_gate.shape[1]
    return pl.pallas_call(
        _gateproj_kernel,
        out_shape=jax.ShapeDtypeStruct((TOKENS, n), BF16),
        grid=(TOKENS // PROJ_TM, n // GATE_TN),
        in_specs=[
            pl.BlockSpec((PROJ_TM, D_MODEL), lambda m, j: (m, 0)),
            pl.BlockSpec((D_MODEL, GATE_TN), lambda m, j: (0, j)),
            pl.BlockSpec((1, GATE_TN), lambda m, j: (0, j)),
        ],
        out_specs=pl.BlockSpec((PROJ_TM, GATE_TN), lambda m, j: (m, j)),
        compiler_params=_cparams(("arbitrary", "arbitrary")),
        name="gate_proj",
    )(u, w_gate, b_gate)


ROPE_ROWS = 256
ATT_LOCKSTEP = 8


def _rows(start, size, stride):
    return pl.ds(start, size) if stride == 1 else pl.ds(start, size, stride=stride)


def _attn_kernel(q0, k0, v0, q1, k1, v1, q2, k2, v2, cos_ref, sin_ref, y_ref,
                 qr_sc, kr_sc, o0, o1, o2, l0, l1, l2):
    scale = HEAD_DIM_A ** -0.5
    blk = ATT_BLOCK
    qi2 = lax.broadcasted_iota(jnp.int32, (blk, 2 * blk), 0)
    kc2 = lax.broadcasted_iota(jnp.int32, (blk, 2 * blk), 1)
    mask_prev_cur = (kc2 >= qi2) & (kc2 <= qi2 + blk)
    qi1 = lax.broadcasted_iota(jnp.int32, (blk, blk), 0)
    kc1 = lax.broadcasted_iota(jnp.int32, (blk, blk), 1)
    mask_cur = kc1 <= qi1
    half = HEAD_DIM_A // 2
    groups = ((q0, k0, v0, o0, l0), (q1, k1, v1, o1, l1), (q2, k2, v2, o2, l2))
    for (window, d), (q_ref, k_ref, v_ref, o_sc, l_sc) in zip(DIL_CONFIGS, groups):
        nb = SEQ // d // blk
        for src_ref, dst_sc in ((q_ref, qr_sc), (k_ref, kr_sc)):
            for c0 in range(0, SEQ, ROPE_ROWS):
                rs = slice(c0, c0 + ROPE_ROWS)
                xr = src_ref[rs, :]
                dst_sc[rs, :] = xr * cos_ref[rs, :] + pltpu.roll(xr, half, 1) * sin_ref[rs, :]
        blocks = [(r, j) for r in range(d) for j in range(nb)]
        for b0 in range(0, len(blocks), ATT_LOCKSTEP):
            batch = blocks[b0:b0 + ATT_LOCKSTEP]
            cur, keys, mask = [], [], []
            for r, j in batch:
                start = r + j * blk * d
                cur.append(_rows(start, blk, d))
                keys.append(cur[-1] if j == 0 else _rows(start - blk * d, 2 * blk, d))
                mask.append(mask_cur if j == 0 else mask_prev_cur)
            nbb = range(len(batch))
            qb = [qr_sc[cur[i], :].astype(BF16) for i in nbb]
            kw = [kr_sc[keys[i], :].astype(BF16) for i in nbb]
            s = [lax.dot_general(qb[i], kw[i], _CONTRACT_LAST, preferred_element_type=F32) for i in nbb]
            s = [jnp.where(mask[i], s[i], -jnp.inf) for i in nbb]
            mx = [jnp.max(s[i], axis=-1, keepdims=True) for i in nbb]
            p = [jnp.exp2((s[i] - mx[i]) * (scale * LOG2E)).astype(BF16) for i in nbb]
            vb = [v_ref[keys[i], :].astype(BF16) for i in nbb]
            pv = [jnp.dot(p[i], jnp.concatenate([vb[i], jnp.ones_like(vb[i])], axis=1),
                          preferred_element_type=F32) for i in nbb]
            for i in nbb:
                den = pv[i][:, HEAD_DIM_A:]
                o_sc[cur[i], :] = pv[i][:, :HEAD_DIM_A] / den
                l_sc[cur[i], :] = mx[i] * scale + jnp.log(den)
    for c0 in range(0, SEQ, ROPE_ROWS):
        rs = slice(c0, c0 + ROPE_ROWS)
        la = l0[rs, :]
        lb = l1[rs, :]
        lc = l2[rs, :]
        mx = jnp.maximum(jnp.maximum(la, lb), lc)
        ea = jnp.exp(la - mx)
        eb = jnp.exp(lb - mx)
        ec = jnp.exp(lc - mx)
        den = ea + eb + ec
        y_ref[rs, :] = ((ea / den) * o0[rs, :] + (eb / den) * o1[rs, :] + (ec / den) * o2[rs, :]).astype(BF16)


def _attention(p, cos_t, sin_t):
    for window, d in DIL_CONFIGS:
        assert window // d == ATT_BLOCK and SEQ % (d * ATT_BLOCK) == 0
    col0 = P_ATT_OFF // HEAD_DIM_A

    def slab(part, g):
        off = col0 + (part * A_QKV_W + g * A_GROUP_W) // HEAD_DIM_A
        return pl.BlockSpec((SEQ, HEAD_DIM_A), lambda b, h: (b, off + h))

    in_specs = [slab(part, g) for g in range(N_DIL_GROUPS) for part in range(3)]
    table = pl.BlockSpec((SEQ, HEAD_DIM_A), lambda b, h: (0, 0))
    return pl.pallas_call(
        _attn_kernel,
        out_shape=jax.ShapeDtypeStruct((TOKENS, A_GROUP_W), BF16),
        grid=(BATCH, HEADS_PER_GROUP),
        in_specs=in_specs + [table, table],
        out_specs=pl.BlockSpec((SEQ, HEAD_DIM_A), lambda b, h: (b, h)),
        scratch_shapes=[pltpu.VMEM((SEQ, HEAD_DIM_A), F32)] * 8,
        compiler_params=_cparams(("arbitrary", "arbitrary")),
        name="dil_attn",
    )(*([p] * 9), cos_t, sin_t)


CONV_HALO = SUBLANES
M_BATCH = 2


def _mlstm_kernel(q_ref, k_ref, v_ref, og_ref, gc_ref, gr_ref, bc_ref, br_ref, cw_ref, cb_ref, ng_ref,
                  y_ref, xq_sc, xk_sc, ct_sc, n_sc, m_sc):
    c = pl.program_id(1)
    L = M_CHUNK
    lo = CONV_HALO

    nbat = q_ref.shape[0]

    @pl.when(c == 0)
    def _():
        xq_sc[:, 0:lo, :] = jnp.zeros((nbat, lo, M_W), F32)
        xk_sc[:, 0:lo, :] = jnp.zeros((nbat, lo, M_W), F32)
        ct_sc[...] = jnp.zeros_like(ct_sc)
        n_sc[...] = jnp.zeros_like(n_sc)
        m_sc[...] = jnp.zeros_like(m_sc)

    for bi in range(nbat):
        xq_sc[bi, lo:lo + L, :] = q_ref[bi]
        xk_sc[bi, lo:lo + L, :] = k_ref[bi]

    def conv(x_sc, bi, w, b):
        acc = x_sc[bi, lo:lo + L, :] * w[CONV_K - 1:CONV_K, :] + b
        for j in range(CONV_K - 1):
            off = lo - (CONV_K - 1) + j
            acc = acc + x_sc[bi, off:off + L, :] * w[j:j + 1, :]
        return acc

    cw = cw_ref[...]
    cb = cb_ref[...]
    qc = [_silu(conv(xq_sc, bi, cw[:, :M_W], cb[:, :M_W])) for bi in range(nbat)]
    kc = [_silu(conv(xk_sc, bi, cw[:, M_W:], cb[:, M_W:])) * (M_HEAD_DIM ** -0.5) for bi in range(nbat)]
    for bi in range(nbat):
        xq_sc[bi, 0:lo, :] = xq_sc[bi, L:L + lo, :]
        xk_sc[bi, 0:lo, :] = xk_sc[bi, L:L + lo, :]

    row = lax.broadcasted_iota(jnp.int32, (L, L), 0)
    col = lax.broadcasted_iota(jnp.int32, (L, L), 1)
    causal = row >= col
    heads = range(nbat * M_HEADS)
    bat = [ch // M_HEADS for ch in heads]
    hd = [ch % M_HEADS for ch in heads]
    hsl = [slice(hd[ch] * M_HEAD_DIM, (hd[ch] + 1) * M_HEAD_DIM) for ch in heads]
    ct = [ct_sc[h] for h in heads]
    n_row = [n_sc[h, 0:1, :] for h in heads]
    m_prev = [m_sc[h, 0:1, 0:1] for h in heads]
    q = [qc[bat[h]][:, hsl[h]] for h in heads]
    k = [kc[bat[h]][:, hsl[h]] for h in heads]
    qb = [q[h].astype(BF16) for h in heads]
    kb = [k[h].astype(BF16) for h in heads]
    vb = [v_ref[bat[h], :, hsl[h]].astype(BF16) for h in heads]
    qk = [lax.dot_general(qb[h], kb[h], _CONTRACT_LAST, preferred_element_type=F32) for h in heads]
    qc_state = [jnp.dot(qb[h], ct[h].astype(BF16), preferred_element_type=F32) for h in heads]

    def gate_col(h, j):
        return gc_ref[bat[h], :, j:j + 1] + bc_ref[:, j:j + 1]

    def gate_row(h, j):
        return gr_ref[bat[h], j:j + 1, :] + br_ref[j:j + 1, :]

    i_col = [gate_col(h, hd[h]) for h in heads]
    lf_col = [_log_sigmoid(gate_col(h, M_HEADS + hd[h])) for h in heads]
    i_row = [gate_row(h, hd[h]) for h in heads]
    lf_row = [_log_sigmoid(gate_row(h, M_HEADS + hd[h])) for h in heads]
    b_col = [jnp.sum(jnp.where(causal, lf_row[h], 0.0), axis=1, keepdims=True) for h in heads]
    b_row = [jnp.sum(jnp.where(row <= col, lf_col[h], 0.0), axis=0, keepdims=True) for h in heads]
    log_d = [jnp.where(causal, b_col[h] - b_row[h] + i_row[h], -jnp.inf) for h in heads]
    log_inter = [b_col[h] + m_prev[h] for h in heads]
    m_t = [jnp.maximum(jnp.max(log_d[h], axis=1, keepdims=True), log_inter[h]) for h in heads]
    s = [qk[h] * jnp.exp(log_d[h] - m_t[h]) for h in heads]
    inter = [jnp.exp(log_inter[h] - m_t[h]) for h in heads]
    sv = [jnp.dot(s[h].astype(BF16), vb[h], preferred_element_type=F32) for h in heads]

    m_new = [m_t[h][L - 1:L, :] for h in heads]
    b_last = [b_col[h][L - 1:L, :] for h in heads]
    w_col = [jnp.exp(b_last[h] - b_col[h] + i_col[h] - m_new[h]) for h in heads]
    decay = [jnp.exp(b_last[h] + m_prev[h] - m_new[h]) for h in heads]
    kw = [k[h] * w_col[h] for h in heads]
    kv = [lax.dot_general(kw[h].astype(BF16), vb[h], (((0,), (0,)), ((), ())), preferred_element_type=F32)
          for h in heads]

    num = [sv[h] + inter[h] * qc_state[h] for h in heads]
    den = [jnp.sum(s[h], axis=1, keepdims=True) + inter[h] * jnp.sum(q[h] * n_row[h], axis=1, keepdims=True)
           for h in heads]
    hh = [num[h] / jnp.maximum(jnp.abs(den[h]), jnp.exp(-m_t[h])) for h in heads]
    z = [jax.nn.sigmoid(og_ref[bat[h], :, hsl[h]]) * hh[h] for h in heads]
    y = [(_normalize(z[h]) * ng_ref[:, hsl[h]]).astype(BF16) for h in heads]
    ct_new = [decay[h] * ct[h] + kv[h] for h in heads]
    n_new = [decay[h] * n_row[h] + jnp.sum(kw[h], axis=0, keepdims=True) for h in heads]
    for h in heads:
        ct_sc[h] = ct_new[h]
        n_sc[h] = jnp.broadcast_to(n_new[h], (SUBLANES, M_HEAD_DIM))
        m_sc[h] = jnp.broadcast_to(m_new[h], (SUBLANES, LANES))
        y_ref[bat[h], :, hsl[h]] = y[h]


def _mlstm(p, gcol, grow, bcol, brow, conv_w, conv_b, norm_g):
    nc = SEQ // M_CHUNK
    mb = M_BATCH
    p3 = p.reshape(BATCH, SEQ, P_WIDTH)
    slab = lambda blk: pl.BlockSpec((mb, M_CHUNK, M_W), lambda b, c: (b, c, blk))
    y = pl.pallas_call(
        _mlstm_kernel,
        out_shape=jax.ShapeDtypeStruct((BATCH, SEQ, M_W), BF16),
        grid=(BATCH // mb, nc),
        in_specs=[
            slab(P_QM_BLK), slab(P_KM_BLK), slab(P_VM_BLK), slab(P_OM_BLK),
            pl.BlockSpec((mb, M_CHUNK, 2 * M_HEADS), lambda b, c: (b, c, 0)),
            pl.BlockSpec((mb, 2 * M_HEADS, M_CHUNK), lambda b, c: (b, 0, c)),
            pl.BlockSpec((1, 2 * M_HEADS), lambda b, c: (0, 0)),
            pl.BlockSpec((2 * M_HEADS, 1), lambda b, c: (0, 0)),
            pl.BlockSpec((CONV_K, 2 * M_W), lambda b, c: (0, 0)),
            pl.BlockSpec((1, 2 * M_W), lambda b, c: (0, 0)),
            pl.BlockSpec((1, M_W), lambda b, c: (0, 0)),
        ],
        out_specs=pl.BlockSpec((mb, M_CHUNK, M_W), lambda b, c: (b, c, 0)),
        scratch_shapes=[
            pltpu.VMEM((mb, M_CHUNK + CONV_HALO, M_W), F32),
            pltpu.VMEM((mb, M_CHUNK + CONV_HALO, M_W), F32),
            pltpu.VMEM((mb * M_HEADS, M_HEAD_DIM, M_HEAD_DIM), F32),
            pltpu.VMEM((mb * M_HEADS, SUBLANES, M_HEAD_DIM), F32),
            pltpu.VMEM((mb * M_HEADS, SUBLANES, LANES), F32),
        ],
        compiler_params=_cparams(("arbitrary", "arbitrary")),
        name="mlstm",
    )(p3, p3, p3, p3, gcol, grow, bcol, brow, conv_w, conv_b, norm_g)
    return y.reshape(TOKENS, M_W)


RT_E1, RT_E2, RT_W1, RT_W2 = 0, 1, 2, 3


def _route(logits):
    lane = lax.broadcasted_iota(jnp.int32, logits.shape, 1).astype(F32)
    big = float(LANES)
    is_g = lane < N_EXPERT_GROUPS
    gl = jnp.where(is_g, logits, -jnp.inf)
    gexp = jnp.exp(gl - jnp.max(gl, axis=1, keepdims=True))
    gprob = gexp / jnp.sum(gexp, axis=1, keepdims=True)
    g_w = jnp.max(gprob, axis=1, keepdims=True)
    g_top = jnp.min(jnp.where(is_g & (gprob == g_w), lane, big), axis=1, keepdims=True)
    lo = N_EXPERT_GROUPS + EXPERTS_PER_GROUP * g_top
    in_grp = (lane >= lo) & (lane < lo + EXPERTS_PER_GROUP)
    el = jnp.where(in_grp, logits, -jnp.inf)
    eexp = jnp.exp(el - jnp.max(el, axis=1, keepdims=True))
    eprob = eexp / jnp.sum(eexp, axis=1, keepdims=True)
    v1 = jnp.max(eprob, axis=1, keepdims=True)
    i1 = jnp.min(jnp.where(in_grp & (eprob == v1), lane, big), axis=1, keepdims=True)
    rest = jnp.where(in_grp & (lane != i1), eprob, -1.0)
    v2 = jnp.max(rest, axis=1, keepdims=True)
    i2 = jnp.min(jnp.where(rest == v2, lane, big), axis=1, keepdims=True)
    tot = v1 + v2
    w1 = g_w * (v1 / tot)
    w2 = g_w * (v2 / tot)
    e1 = i1 - N_EXPERT_GROUPS
    e2 = i2 - N_EXPERT_GROUPS
    rec = jnp.where(lane == RT_E1, e1, jnp.where(lane == RT_E2, e2, jnp.where(lane == RT_W1, w1, w2)))
    return jnp.where(lane <= RT_W2, rec, 0.0)


def _merge_kernel(ya_ref, ym_ref, g_ref, x_ref, mod_ref,
                  wpa_ref, wpm_ref, wout_ref, lng_ref, lnb_ref, wr_ref, br_ref,
                  x1_ref, u2_ref, rt_ref):
    nsb = MERGE_TM // MERGE_SUB
    rows = [slice(sb * MERGE_SUB, (sb + 1) * MERGE_SUB) for sb in range(nsb)]

    def matmul_part(i):
        pa = jnp.dot(ya_ref[rows[i], :], wpa_ref[...], preferred_element_type=F32)
        pm = jnp.dot(ym_ref[rows[i], :], wpm_ref[...], preferred_element_type=F32)
        merged = g_ref[rows[i], :D_MODEL].astype(F32) * pa + g_ref[rows[i], D_MODEL:].astype(F32) * pm
        return jnp.dot(merged.astype(BF16), wout_ref[...], preferred_element_type=F32)

    def norm_part(i, mix):
        z = DEEPNORM_ALPHA * x_ref[rows[i], :] + mod_ref[2:3, :] * mix
        x1 = _normalize(z) * lng_ref[...] + lnb_ref[...]
        x1_ref[rows[i], :] = x1
        u2 = _normalize(x1) * (1.0 + mod_ref[4:5, :]) + mod_ref[3:4, :]
        u2_ref[rows[i], :] = u2
        logits = jnp.dot(u2.astype(BF16), wr_ref[...], preferred_element_type=F32) + br_ref[...]
        rt_ref[rows[i], :] = _route(logits)

    mix = matmul_part(0)
    for i in range(nsb):
        nxt = matmul_part(i + 1) if i + 1 < nsb else None
        norm_part(i, mix)
        mix = nxt


def _merge(ya, ym, g, x2, mod, wpa, wpm, wout, lng, lnb, wr, br):
    tm = MERGE_TM
    tiles_per_batch = SEQ // tm
    rowblk = lambda w: pl.BlockSpec((tm, w), lambda m: (m, 0))
    const = lambda shape: pl.BlockSpec(shape, lambda m: (0,) * len(shape), pipeline_mode=pl.Buffered(1))
    return pl.pallas_call(
        _merge_kernel,
        out_shape=(
            jax.ShapeDtypeStruct((TOKENS, D_MODEL), F32),
            jax.ShapeDtypeStruct((TOKENS, D_MODEL), F32),
            jax.ShapeDtypeStruct((TOKENS, LANES), F32),
        ),
        grid=(TOKENS // tm,),
        in_specs=[
            rowblk(A_GROUP_W), rowblk(M_W), rowblk(2 * D_MODEL), rowblk(D_MODEL),
            pl.BlockSpec((None, 6, D_MODEL), lambda m: (m // tiles_per_batch, 0, 0)),
            const((A_GROUP_W, D_MODEL)), const((M_W, D_MODEL)), const((D_MODEL, D_MODEL)),
            const((1, D_MODEL)), const((1, D_MODEL)),
            const((D_MODEL, LANES)), const((1, LANES)),
        ],
        out_specs=(rowblk(D_MODEL), rowblk(D_MODEL), rowblk(LANES)),
        compiler_params=_cparams(("arbitrary",)),
        name="merge_ln1_route",
    )(ya, ym, g, x2, mod, wpa, wpm, wout, lng, lnb, wr, br)


CAST_ROWS = 128


def _issue_rows(src_ref, idx_ref, base, buf, slot, sem, nrows):
    def body(blk, carry):
        for j in range(SUBLANES):
            row = idx_ref[base + blk * SUBLANES + j]
            src = src_ref.at[lax.shift_right_logical(row, 3), pl.ds(row & (SUBLANES - 1), 1)]
            pltpu.make_async_copy(src, buf.at[slot, blk, pl.ds(j, 1)], sem.at[slot]).start(priority=j % 2)
        return carry

    lax.fori_loop(0, nrows // SUBLANES, body, 0)


def _wait_rows(src_ref, buf, slot, sem, nrows):
    groups = nrows // SUBLANES
    pltpu.make_async_copy(src_ref.at[pl.ds(0, groups)], buf.at[slot, pl.ds(0, groups)], sem.at[slot]).wait()


class _TileTable:
    def __init__(self, meta_ref):
        self.ref = meta_ref

    def expert(self, t):
        return self.ref[META_TE * LANES + t]

    def next_expert(self, t):
        return self.ref[META_TN * LANES + t]

    def valid_rows(self, t):
        return self.ref[META_TV * LANES + t]

    def tiles_used(self):
        return self.ref[META_NU * LANES]


def _expert_changed(te_ref, i):
    return (i == 0) | (te_ref.expert(i) != te_ref.expert(jnp.maximum(i - 1, 0)))


def _moe_kernel(meta_ref, tok_ref, u_ref, wg_ref, wu_ref, wd_ref, o_ref,
                stg, stu, std, wgb, wub, wdb, xbuf, wsem, xsem):
    i = pl.program_id(0)
    te_ref = _TileTable(meta_ref)
    nu = te_ref.tiles_used()
    tm = o_ref.shape[0]
    half = tm // 2

    def weight_copies(e):
        return (pltpu.make_async_copy(wg_ref.at[e], stg, wsem.at[0]),
                pltpu.make_async_copy(wu_ref.at[e], stu, wsem.at[1]),
                pltpu.make_async_copy(wd_ref.at[e], std, wsem.at[2]))

    def for_tile_rows(t, fn):
        @pl.when(te_ref.valid_rows(t) <= half)
        def _():
            fn(half)

        @pl.when(te_ref.valid_rows(t) > half)
        def _():
            fn(tm)

    def issue_tile(t):
        for_tile_rows(t, lambda nrows: _issue_rows(u_ref, tok_ref, t * tm, xbuf, t % MOE_ROW_SLOTS, xsem, nrows))

    @pl.when(i == 0)
    def _():
        issue_tile(0)

        @pl.when(nu > 1)
        def _():
            issue_tile(1)

        for cp in weight_copies(te_ref.expert(0)):
            cp.start(priority=1)

    @pl.when(i < nu)
    def _():
        slot = i % MOE_ROW_SLOTS

        @pl.when(i + 2 < nu)
        def _():
            issue_tile(i + 2)

        changed = _expert_changed(te_ref, i)

        @pl.when(changed)
        def _():
            for cp in weight_copies(te_ref.expert(i)):
                cp.wait()
            def cast_rows(ci, carry):
                r = pl.multiple_of(ci * CAST_ROWS, CAST_ROWS)
                wgb[pl.ds(r, CAST_ROWS), :] = stg[pl.ds(r, CAST_ROWS), :].astype(BF16)
                return carry

            lax.fori_loop(0, D_MODEL // CAST_ROWS, cast_rows, 0)

        def convert_inline(src, dst):
            for r0 in range(0, src.shape[0], CAST_ROWS):
                dst[r0:r0 + CAST_ROWS, :] = src[r0:r0 + CAST_ROWS, :].astype(BF16)

        def compute(nrows, new_expert):
            _wait_rows(u_ref, xbuf, slot, xsem, nrows)
            x = xbuf[slot, 0:nrows // SUBLANES].reshape(nrows, D_MODEL).astype(BF16)
            a = jnp.dot(x, wgb[...], preferred_element_type=F32)
            if new_expert:
                convert_inline(stu, wub)
            b = jnp.dot(x, wub[...], preferred_element_type=F32)
            if new_expert:
                convert_inline(std, wdb)

                @pl.when(te_ref.next_expert(i) >= 0)
                def _():
                    for cp in weight_copies(te_ref.next_expert(i)):
                        cp.start(priority=1)
            h = (_silu(a) * b).astype(BF16)
            o_ref[0:nrows, :] = jnp.dot(h, wdb[...], preferred_element_type=F32)
            if nrows < tm:
                o_ref[nrows:tm, :] = jnp.zeros((tm - nrows, D_MODEL), F32)

        @pl.when(changed)
        def _():
            for_tile_rows(i, functools.partial(compute, new_expert=True))

        @pl.when(jnp.logical_not(changed))
        def _():
            for_tile_rows(i, functools.partial(compute, new_expert=False))

    @pl.when(i >= nu)
    def _():
        o_ref[...] = jnp.zeros_like(o_ref)


def _moe(meta, row_token, u2, w_eg, w_eu, w_ed):
    tm = MOE_TM
    return pl.pallas_call(
        _moe_kernel,
        out_shape=jax.ShapeDtypeStruct((MOE_ROWS, D_MODEL), F32),
        grid_spec=pltpu.PrefetchScalarGridSpec(
            num_scalar_prefetch=2,
            grid=(MOE_TILES,),
            in_specs=[pl.BlockSpec(memory_space=pl.ANY)] * 4,
            out_specs=pl.BlockSpec((tm, D_MODEL), lambda i, meta, tok: (i, 0)),
            scratch_shapes=[
                pltpu.VMEM((D_MODEL, D_FF_EXPERT), F32),
                pltpu.VMEM((D_MODEL, D_FF_EXPERT), F32),
                pltpu.VMEM((D_FF_EXPERT, D_MODEL), F32),
                pltpu.VMEM((D_MODEL, D_FF_EXPERT), BF16),
                pltpu.VMEM((D_MODEL, D_FF_EXPERT), BF16),
                pltpu.VMEM((D_FF_EXPERT, D_MODEL), BF16),
                pltpu.VMEM((MOE_ROW_SLOTS, tm // SUBLANES, SUBLANES, D_MODEL), F32),
                pltpu.SemaphoreType.DMA((3,)),
                pltpu.SemaphoreType.DMA((MOE_ROW_SLOTS,)),
            ],
        ),
        compiler_params=_cparams(("arbitrary",)),
        name="moe_experts",
    )(meta, row_token, u2.reshape(TOKENS // SUBLANES, SUBLANES, D_MODEL), w_eg, w_eu, w_ed)


def _final_kernel(pos_ref, o_ref, x1_ref, rt_ref, mod_ref, lng_ref, lnb_ref, y_ref, buf, sem):
    i = pl.program_id(0)
    tb = x1_ref.shape[0]
    slot = i % 2

    @pl.when(i == 0)
    def _():
        _issue_rows(o_ref, pos_ref, 0, buf, 0, sem, 2 * tb)

    @pl.when(i + 1 < pl.num_programs(0))
    def _():
        _issue_rows(o_ref, pos_ref, (i + 1) * (2 * tb), buf, 1 - slot, sem, 2 * tb)

    _wait_rows(o_ref, buf, slot, sem, 2 * tb)
    rows = buf[slot].reshape(2 * tb, D_MODEL)
    ffn = rt_ref[:, RT_W1:RT_W1 + 1] * rows[0:tb, :] + rt_ref[:, RT_W2:RT_W2 + 1] * rows[tb:2 * tb, :]
    z = DEEPNORM_ALPHA * x1_ref[...] + mod_ref[5:6, :] * ffn
    y_ref[...] = _normalize(z) * lng_ref[...] + lnb_ref[...]


def _final(pos_tiles, moe_out, x1, rt, mod, lng, lnb):
    tb = FINAL_TB
    tiles_per_batch = SEQ // tb
    return pl.pallas_call(
        _final_kernel,
        out_shape=jax.ShapeDtypeStruct((TOKENS, D_MODEL), F32),
        grid_spec=pltpu.PrefetchScalarGridSpec(
            num_scalar_prefetch=1,
            grid=(TOKENS // tb,),
            in_specs=[
                pl.BlockSpec(memory_space=pl.ANY),
                pl.BlockSpec((tb, D_MODEL), lambda i, pos: (i, 0)),
                pl.BlockSpec((tb, LANES), lambda i, pos: (i, 0)),
                pl.BlockSpec((None, 6, D_MODEL), lambda i, pos: (i // tiles_per_batch, 0, 0)),
                pl.BlockSpec((1, D_MODEL), lambda i, pos: (0, 0)),
                pl.BlockSpec((1, D_MODEL), lambda i, pos: (0, 0)),
            ],
            out_specs=pl.BlockSpec((tb, D_MODEL), lambda i, pos: (i, 0)),
            scratch_shapes=[pltpu.VMEM((2, 2 * tb // SUBLANES, SUBLANES, D_MODEL), F32),
                            pltpu.SemaphoreType.DMA((2,))],
        ),
        compiler_params=_cparams(("arbitrary",)),
        name="combine_ln2",
    )(pos_tiles, moe_out.reshape(MOE_ROWS // SUBLANES, SUBLANES, D_MODEL), x1, rt, mod, lng, lnb)


ROUTE_BLK = 256
ROUTE_LOCKSTEP = 4
META_TE, META_TN, META_TV, META_NU = 0, 1, 2, 3


def _lane_cumsum(x, lane):
    s = 1
    while s < LANES:
        x = x + jnp.where(lane >= s, pltpu.roll(x, s, 1), 0.0)
        s *= 2
    return x


def _route_tables_kernel(rt_ref, pos_ref, meta_ref, rank_sc):
    tm = float(MOE_TM)
    nblk = TOKENS // ROUTE_BLK
    lane_i = lax.broadcasted_iota(jnp.int32, (ROUTE_BLK, LANES), 1)
    lane_f = lane_i.astype(F32)
    earlier = (lax.broadcasted_iota(jnp.int32, (ROUTE_BLK, ROUTE_BLK), 0)
               > lax.broadcasted_iota(jnp.int32, (ROUTE_BLK, ROUTE_BLK), 1)).astype(BF16)

    def onehots(b):
        blk = rt_ref[pl.ds(pl.multiple_of(b * ROUTE_BLK, ROUTE_BLK), ROUTE_BLK), :]
        return blk[:, RT_E1:RT_E1 + 1] == lane_f, blk[:, RT_E2:RT_E2 + 1] == lane_f

    def count_pass(g, carry):
        c1, c2 = carry
        bs = [g * ROUTE_LOCKSTEP + i for i in range(ROUTE_LOCKSTEP)]
        oh = [onehots(b) for b in bs]
        f = [(jnp.where(o1, 1.0, 0.0), jnp.where(o2, 1.0, 0.0)) for o1, o2 in oh]
        p = [(jnp.dot(earlier, f1.astype(BF16), preferred_element_type=F32),
              jnp.dot(earlier, f2.astype(BF16), preferred_element_type=F32)) for f1, f2 in f]
        tot = [(jnp.sum(f1, axis=0, keepdims=True), jnp.sum(f2, axis=0, keepdims=True)) for f1, f2 in f]
        for i, b in enumerate(bs):
            r1 = jnp.sum(jnp.where(oh[i][0], p[i][0] + c1, 0.0), axis=1, keepdims=True)
            r2 = jnp.sum(jnp.where(oh[i][1], p[i][1] + c2, 0.0), axis=1, keepdims=True)
            rank_sc[pl.ds(pl.multiple_of(b * ROUTE_BLK, ROUTE_BLK), ROUTE_BLK), :] = jnp.where(
                lane_i == 0, r1, jnp.where(lane_i == 1, r2, 0.0))
            c1 = c1 + tot[i][0]
            c2 = c2 + tot[i][1]
        return c1, c2

    zero = jnp.zeros((1, LANES), F32)
    c1, c2 = lax.fori_loop(0, nblk // ROUTE_LOCKSTEP, count_pass, (zero, zero))

    lane8 = lax.broadcasted_iota(jnp.int32, (SUBLANES, LANES), 1)
    counts = jnp.broadcast_to(c1 + c2, (SUBLANES, LANES))
    padded = jnp.floor((counts + (tm - 1.0)) * (1.0 / tm)) * tm
    pend = _lane_cumsum(padded, lane8)
    pstart = pend - padded
    start1 = pstart[0:1, :]
    start2 = start1 + c1

    def place_pass(g, carry):
        bs = [g * ROUTE_LOCKSTEP + i for i in range(ROUTE_LOCKSTEP)]
        oh = [onehots(b) for b in bs]
        rows = [pl.ds(pl.multiple_of(b * ROUTE_BLK, ROUTE_BLK), ROUTE_BLK) for b in bs]
        rk = [rank_sc[r, :] for r in rows]
        d1 = [jnp.sum(jnp.where(oh[i][0], start1, 0.0), axis=1, keepdims=True) + rk[i][:, 0:1]
              for i in range(ROUTE_LOCKSTEP)]
        d2 = [jnp.sum(jnp.where(oh[i][1], start2, 0.0), axis=1, keepdims=True) + rk[i][:, 1:2]
              for i in range(ROUTE_LOCKSTEP)]
        for i in range(ROUTE_LOCKSTEP):
            pos_ref[rows[i], :] = jnp.where(lane_i == 0, d1[i], jnp.where(lane_i == 1, d2[i], 0.0)).astype(jnp.int32)
        return carry

    lax.fori_loop(0, nblk // ROUTE_LOCKSTEP, place_pass, 0)

    tile = lax.broadcasted_iota(jnp.int32, (LANES, LANES), 0).astype(F32)
    lane = lax.broadcasted_iota(jnp.int32, (LANES, LANES), 1)
    expert_lane = lane < N_EXPERTS
    pend_b = jnp.broadcast_to(pend[0:1, :], (LANES, LANES))
    pstart_b = jnp.broadcast_to(pstart[0:1, :], (LANES, LANES))
    counts_b = jnp.broadcast_to(counts[0:1, :], (LANES, LANES))
    n_used = jnp.sum(jnp.where(lane == N_EXPERTS - 1, pend_b, 0.0), axis=1, keepdims=True) * (1.0 / tm)
    te = jnp.sum(jnp.where(expert_lane & (pend_b <= tile * tm), 1.0, 0.0), axis=1, keepdims=True)
    te = jnp.minimum(te, float(N_EXPERTS - 1))
    te_last = jnp.sum(jnp.where(tile[:, 0:1] == n_used - 1.0, te, 0.0), axis=0, keepdims=True)
    te = jnp.where(tile[:, 0:1] < n_used, te, te_last)
    of_tile = lane.astype(F32) == te
    pend_te = jnp.sum(jnp.where(of_tile, pend_b, 0.0), axis=1, keepdims=True)
    last_row = jnp.sum(jnp.where(of_tile, pstart_b + counts_b, 0.0), axis=1, keepdims=True)
    tile_rows = jnp.clip(last_row - tile[:, 0:1] * tm, 0.0, tm)
    next_run = pend_te * (1.0 / tm)
    te_by_lane = jnp.broadcast_to(te, (LANES, LANES)).T
    te_at_next = jnp.sum(jnp.where(lane.astype(F32) == next_run, te_by_lane, 0.0), axis=1, keepdims=True)
    te_next = jnp.where(next_run < n_used, te_at_next, -1.0)
    cols = jnp.where(lane == META_TE, te, jnp.where(lane == META_TN, te_next,
                     jnp.where(lane == META_TV, tile_rows, jnp.where(lane == META_NU, n_used, 0.0))))
    meta_ref[...] = cols.T[0:SUBLANES, :].astype(jnp.int32)


def _routing_tables(rt):
    pos, meta = pl.pallas_call(
        _route_tables_kernel,
        out_shape=(
            jax.ShapeDtypeStruct((TOKENS, LANES), jnp.int32),
            jax.ShapeDtypeStruct((SUBLANES, LANES), jnp.int32),
        ),
        scratch_shapes=[pltpu.VMEM((TOKENS, LANES), F32)],
        compiler_params=_cparams(None),
        name="route_tables",
    )(rt)
    pos = pos[:, 0:2]
    tok = jnp.arange(2 * TOKENS, dtype=jnp.int32) // 2
    row_token = jnp.zeros((MOE_ROWS,), jnp.int32).at[pos.reshape(-1)].set(
        tok, unique_indices=True, mode="promise_in_bounds")
    return row_token, meta.reshape(-1), pos


def _rope_tables():
    inv = ROPE_THETA ** (-jnp.arange(0, HEAD_DIM_A, 2, dtype=F32) / HEAD_DIM_A)
    ang = jnp.arange(SEQ, dtype=F32)[:, None] * inv[None, :]
    cos = jnp.cos(ang)
    sin = jnp.sin(ang)
    return jnp.concatenate([cos, cos], axis=-1), jnp.concatenate([-sin, sin], axis=-1)


def kernel(x, c, w_ada, b_ada, w_in, b_mgate, conv_w, conv_b, m_norm_g, w_proj_a, w_proj_m, w_gate, b_gate,
           w_out, ln1_g, ln1_b, w_rg, b_rg, w_re, b_re, w_eg, w_eu, w_ed, ln2_g, ln2_b):
    assert x.shape == (BATCH, SEQ, D_MODEL) and w_ada.shape[0] == 1
    l = 0
    x2 = x.reshape(TOKENS, D_MODEL)

    c_pad = jnp.zeros((ADA_ROWS, D_MODEL), F32).at[:BATCH].set(c)
    mod = _ada(c_pad, w_ada[l], b_ada[l][None, :])[:BATCH].reshape(BATCH, 6, D_MODEL)

    w_in_t = jnp.swapaxes(w_in[l], 0, 1)
    w_if_t = jnp.zeros((LANES, D_MODEL), F32).at[:2 * M_HEADS].set(w_in_t[N_IN_MAIN:])
    p, gates, u = _inproj(x2, mod, w_in_t, w_if_t)
    g = _gateproj(u, w_gate[l], b_gate[l][None, :])

    cos_t, sin_t = _rope_tables()
    ya = _attention(p, cos_t, sin_t)

    gcol = gates[:, :2 * M_HEADS].reshape(BATCH, SEQ, 2 * M_HEADS)
    grow = jnp.transpose(gcol, (0, 2, 1))
    ym = _mlstm(p, gcol, grow, b_mgate[l][None, :], b_mgate[l][:, None], conv_w[l], conv_b[l][None, :],
                m_norm_g[l][None, :])

    wr = (jnp.zeros((D_MODEL, LANES), F32)
          .at[:, :N_EXPERT_GROUPS].set(w_rg[l])
          .at[:, N_EXPERT_GROUPS:N_EXPERT_GROUPS + N_EXPERTS].set(w_re[l])).astype(BF16)
    br = (jnp.zeros((1, LANES), F32)
          .at[0, :N_EXPERT_GROUPS].set(b_rg[l])
          .at[0, N_EXPERT_GROUPS:N_EXPERT_GROUPS + N_EXPERTS].set(b_re[l]))
    x1, u2, rt = _merge(ya, ym, g, x2, mod,
                        w_proj_a[l].astype(BF16), w_proj_m[l].astype(BF16), w_out[l].astype(BF16),
                        ln1_g[l][None, :], ln1_b[l][None, :], wr, br)

    row_token, meta, pos = _routing_tables(rt)

    mo = _moe(meta, row_token, u2,
              w_eg[l].reshape(N_EXPERTS, D_MODEL, D_FF_EXPERT), w_eu[l].reshape(N_EXPERTS, D_MODEL, D_FF_EXPERT),
              w_ed[l].reshape(N_EXPERTS, D_FF_EXPERT, D_MODEL))

    nt = TOKENS // FINAL_TB
    pos_tiles = jnp.transpose(pos.reshape(nt, FINAL_TB, 2), (0, 2, 1)).reshape(-1)
    y = _final(pos_tiles, mo, x1, rt, mod, ln2_g[l][None, :], ln2_b[l][None, :])
    return y.reshape(BATCH, SEQ, D_MODEL)
```

```python
import functools

import jax
import jax.numpy as jnp
from jax import lax
from jax.experimental import pallas as pl
from jax.experimental.pallas import tpu as pltpu

F32 = jnp.float32
BF16 = jnp.bfloat16

D_MODEL = 2048
BATCH = 4
SEQ = 2048
TOKENS = BATCH * SEQ
DIL_CONFIGS = ((128, 1), (512, 4), (2048, 16))
N_DIL_GROUPS = 3
HEADS_PER_GROUP = 4
HEAD_DIM_A = 128
ATT_BLOCK = 128
ROPE_THETA = 10000.0
A_GROUP_W = HEADS_PER_GROUP * HEAD_DIM_A
A_QKV_W = N_DIL_GROUPS * A_GROUP_W
M_HEADS = 4
M_HEAD_DIM = 256
M_W = M_HEADS * M_HEAD_DIM
M_CHUNK = 128
CONV_K = 4
N_IN_MAIN = 3 * A_QKV_W + 4 * M_W
N_EXPERT_GROUPS = 4
EXPERTS_PER_GROUP = 8
N_EXPERTS = N_EXPERT_GROUPS * EXPERTS_PER_GROUP
D_FF_EXPERT = 1024
DEEPNORM_ALPHA = 2.0 ** 0.25
LN_EPS = 1e-5

LANES = 128
SUBLANES = 8
VMEM_LIMIT_BYTES = 56 * 1024 * 1024

PROJ_TN = 256
GATE_TN = 1024
PROJ_TM = 1024
IN_TM = 2048
P_TILES_IN = N_IN_MAIN // PROJ_TN
P_ATT_TILES = 3 * A_QKV_W // PROJ_TN
P_M_TILES = P_TILES_IN - P_ATT_TILES
P_WIDTH = N_IN_MAIN
P_ATT_OFF = P_M_TILES * PROJ_TN
P_QM_BLK, P_KM_BLK, P_VM_BLK, P_OM_BLK = 0, 1, 2, 3
MERGE_TM = 256
MOE_TM = 256
MOE_ROWS = 2 * TOKENS + N_EXPERTS * MOE_TM
MOE_TILES = MOE_ROWS // MOE_TM
MOE_ROW_SLOTS = 3
FINAL_TB = 256


def _cparams(sem, vmem=VMEM_LIMIT_BYTES):
    return pltpu.CompilerParams(dimension_semantics=sem, vmem_limit_bytes=vmem)


def _normalize(x):
    mu = jnp.mean(x, axis=-1, keepdims=True)
    xc = x - mu
    var = jnp.mean(xc * xc, axis=-1, keepdims=True)
    return xc * lax.rsqrt(var + LN_EPS)


def _silu(x):
    return x * jax.nn.sigmoid(x)


def _log_sigmoid(x):
    return jnp.minimum(x, 0.0) - jnp.log(1.0 + jnp.exp(-jnp.abs(x)))


_CONTRACT_LAST = (((1,), (1,)), ((), ()))
LOG2E = 1.4426950408889634


ADA_TN = 1024
ADA_ROWS = 16


def _ada_kernel(c_ref, w_ref, b_ref, o_ref):
    sc = _silu(c_ref[...]).astype(BF16)
    o_ref[...] = jnp.dot(sc, w_ref[...].astype(BF16), preferred_element_type=F32) + b_ref[...]


def _ada(c_pad, w_ada, b_ada):
    n = w_ada.shape[1]
    return pl.pallas_call(
        _ada_kernel,
        out_shape=jax.ShapeDtypeStruct((ADA_ROWS, n), F32),
        grid=(n // ADA_TN,),
        in_specs=[
            pl.BlockSpec((ADA_ROWS, D_MODEL), lambda j: (0, 0)),
            pl.BlockSpec((D_MODEL, ADA_TN), lambda j: (0, j)),
            pl.BlockSpec((1, ADA_TN), lambda j: (0, j)),
        ],
        out_specs=pl.BlockSpec((ADA_ROWS, ADA_TN), lambda j: (0, j)),
        compiler_params=_cparams(("arbitrary",)),
        name="ada_mod",
    )(c_pad, w_ada, b_ada)


LN_CHUNK = 256


def _inproj_kernel(x_ref, mod_ref, w_ref, wif_ref, p_ref, g_ref, u_ref, xbuf, xsem):
    m = pl.program_id(0)
    n = pl.program_id(1)

    def x_copy(tile):
        return pltpu.make_async_copy(x_ref.at[pl.ds(pl.multiple_of(tile * IN_TM, IN_TM), IN_TM)], xbuf, xsem)

    @pl.when((m == 0) & (n == 0))
    def _():
        x_copy(0).start()

    @pl.when((n == 1) & (m + 1 < pl.num_programs(0)))
    def _():
        x_copy(m + 1).start()

    @pl.when(n == 0)
    def _():
        x_copy(m).wait()
        shift = mod_ref[0:1, :]
        scale = 1.0 + mod_ref[1:2, :]
        wif = wif_ref[...].astype(BF16)
        w = w_ref[...].astype(BF16)
        for ci in range(IN_TM // LN_CHUNK):
            rows = slice(ci * LN_CHUNK, (ci + 1) * LN_CHUNK)
            u = (_normalize(xbuf[rows, :]) * scale + shift).astype(BF16)
            u_ref[rows, :] = u
            g_ref[rows, :] = lax.dot_general(u, wif, _CONTRACT_LAST, preferred_element_type=F32)
            p_ref[rows, :] = lax.dot_general(u, w, _CONTRACT_LAST, preferred_element_type=F32)

    @pl.when(n > 0)
    def _():
        p_ref[...] = lax.dot_general(u_ref[...], w_ref[...].astype(BF16), _CONTRACT_LAST,
                                     preferred_element_type=F32)


def _inproj(x2, mod, w_in_t, w_if_t):
    tiles_per_batch = SEQ // IN_TM
    return pl.pallas_call(
        _inproj_kernel,
        out_shape=(
            jax.ShapeDtypeStruct((TOKENS, P_WIDTH), F32),
            jax.ShapeDtypeStruct((TOKENS, LANES), F32),
            jax.ShapeDtypeStruct((TOKENS, D_MODEL), BF16),
        ),
        grid=(TOKENS // IN_TM, P_TILES_IN),
        in_specs=[
            pl.BlockSpec(memory_space=pl.ANY),
            pl.BlockSpec((None, 6, D_MODEL), lambda m, n: (m // tiles_per_batch, 0, 0)),
            pl.BlockSpec((PROJ_TN, D_MODEL), lambda m, n: (n, 0)),
            pl.BlockSpec((LANES, D_MODEL), lambda m, n: (0, 0)),
        ],
        out_specs=(
            pl.BlockSpec((IN_TM, PROJ_TN),
                         lambda m, n: (m, jnp.where(n < P_ATT_TILES, n + P_M_TILES, n - P_ATT_TILES))),
            pl.BlockSpec((IN_TM, LANES), lambda m, n: (m, 0)),
            pl.BlockSpec((IN_TM, D_MODEL), lambda m, n: (m, 0)),
        ),
        scratch_shapes=[pltpu.VMEM((IN_TM, D_MODEL), F32), pltpu.SemaphoreType.DMA],
        compiler_params=_cparams(("arbitrary", "arbitrary")),
        name="in_proj",
    )(x2, mod, w_in_t, w_if_t)


def _gateproj_kernel(u_ref, w_ref, b_ref, o_ref):
    acc = jnp.dot(u_ref[...], w_ref[...].astype(BF16), preferred_element_type=F32)
    o_ref[...] = jax.nn.sigmoid(acc + b_ref[...]).astype(BF16)


def _gateproj(u, w_gate, b_gate):
    n = w_gate.shape[1]
    return pl.pallas_call(
        _gateproj_kernel,
        out_shape=jax.ShapeDtypeStruct((TOKENS, n), BF16),
        grid=(TOKENS // PROJ_TM, n // GATE_TN),
        in_specs=[
            pl.BlockSpec((PROJ_TM, D_MODEL), lambda m, j: (m, 0)),
            pl.BlockSpec((D_MODEL, GATE_TN), lambda m, j: (0, j)),
            pl.BlockSpec((1, GATE_TN), lambda m, j: (0, j)),
        ],
        out_specs=pl.BlockSpec((PROJ_TM, GATE_TN), lambda m, j: (m, j)),
        compiler_params=_cparams(("arbitrary", "arbitrary")),
        name="gate_proj",
    )(u, w_gate, b_gate)


ROPE_ROWS = 256
ATT_LOCKSTEP = 8


def _rows(start, size, stride):
    return pl.ds(start, size) if stride == 1 else pl.ds(start, size, stride=stride)


def _attn_kernel(q0, k0, v0, q1, k1, v1, q2, k2, v2, cos_ref, sin_ref, y_ref,
                 qr_sc, kr_sc, o0, o1, o2, l0, l1, l2):
    scale = HEAD_DIM_A ** -0.5
    blk = ATT_BLOCK
    qi2 = lax.broadcasted_iota(jnp.int32, (blk, 2 * blk), 0)
    kc2 = lax.broadcasted_iota(jnp.int32, (blk, 2 * blk), 1)
    mask_prev_cur = (kc2 >= qi2) & (kc2 <= qi2 + blk)
    qi1 = lax.broadcasted_iota(jnp.int32, (blk, blk), 0)
    kc1 = lax.broadcasted_iota(jnp.int32, (blk, blk), 1)
    mask_cur = kc1 <= qi1
    half = HEAD_DIM_A // 2
    groups = ((q0, k0, v0, o0, l0), (q1, k1, v1, o1, l1), (q2, k2, v2, o2, l2))
    for (window, d), (q_ref, k_ref, v_ref, o_sc, l_sc) in zip(DIL_CONFIGS, groups):
        nb = SEQ // d // blk
        for src_ref, dst_sc in ((q_ref, qr_sc), (k_ref, kr_sc)):
            for c0 in range(0, SEQ, ROPE_ROWS):
                rs = slice(c0, c0 + ROPE_ROWS)
                xr = src_ref[rs, :]
                dst_sc[rs, :] = xr * cos_ref[rs, :] + pltpu.roll(xr, half, 1) * sin_ref[rs, :]
        blocks = [(r, j) for r in range(d) for j in range(nb)]
        for b0 in range(0, len(blocks), ATT_LOCKSTEP):
            batch = blocks[b0:b0 + ATT_LOCKSTEP]
            cur, keys, mask = [], [], []
            for r, j in batch:
                start = r + j * blk * d
                cur.append(_rows(start, blk, d))
                keys.append(cur[-1] if j == 0 else _rows(start - blk * d, 2 * blk, d))
                mask.append(mask_cur if j == 0 else mask_prev_cur)
            nbb = range(len(batch))
            qb = [qr_sc[cur[i], :].astype(BF16) for i in nbb]
            kw = [kr_sc[keys[i], :].astype(BF16) for i in nbb]
            s = [lax.dot_general(qb[i], kw[i], _CONTRACT_LAST, preferred_element_type=F32) for i in nbb]
            s = [jnp.where(mask[i], s[i], -jnp.inf) for i in nbb]
            mx = [jnp.max(s[i], axis=-1, keepdims=True) for i in nbb]
            p = [jnp.exp2((s[i] - mx[i]) * (scale * LOG2E)).astype(BF16) for i in nbb]
            vb = [v_ref[keys[i], :].astype(BF16) for i in nbb]
            pv = [jnp.dot(p[i], jnp.concatenate([vb[i], jnp.ones_like(vb[i])], axis=1),
                          preferred_element_type=F32) for i in nbb]
            for i in nbb:
                den = pv[i][:, HEAD_DIM_A:]
                o_sc[cur[i], :] = pv[i][:, :HEAD_DIM_A] / den
                l_sc[cur[i], :] = mx[i] * scale + jnp.log(den)
    for c0 in range(0, SEQ, ROPE_ROWS):
        rs = slice(c0, c0 + ROPE_ROWS)
        la = l0[rs, :]
        lb = l1[rs, :]
        lc = l2[rs, :]
        mx = jnp.maximum(jnp.maximum(la, lb), lc)
        ea = jnp.exp(la - mx)
        eb = jnp.exp(lb - mx)
        ec = jnp.exp(lc - mx)
        den = ea + eb + ec
        y_ref[rs, :] = ((ea / den) * o0[rs, :] + (eb / den) * o1[rs, :] + (ec / den) * o2[rs, :]).astype(BF16)


def _attention(p, cos_t, sin_t):
    for window, d in DIL_CONFIGS:
        assert window // d == ATT_BLOCK and SEQ % (d * ATT_BLOCK) == 0
    col0 = P_ATT_OFF // HEAD_DIM_A

    def slab(part, g):
        off = col0 + (part * A_QKV_W + g * A_GROUP_W) // HEAD_DIM_A
        return pl.BlockSpec((SEQ, HEAD_DIM_A), lambda b, h: (b, off + h))

    in_specs = [slab(part, g) for g in range(N_DIL_GROUPS) for part in range(3)]
    table = pl.BlockSpec((SEQ, HEAD_DIM_A), lambda b, h: (0, 0))
    return pl.pallas_call(
        _attn_kernel,
        out_shape=jax.ShapeDtypeStruct((TOKENS, A_GROUP_W), BF16),
        grid=(BATCH, HEADS_PER_GROUP),
        in_specs=in_specs + [table, table],
        out_specs=pl.BlockSpec((SEQ, HEAD_DIM_A), lambda b, h: (b, h)),
        scratch_shapes=[pltpu.VMEM((SEQ, HEAD_DIM_A), F32)] * 8,
        compiler_params=_cparams(("arbitrary", "arbitrary")),
        name="dil_attn",
    )(*([p] * 9), cos_t, sin_t)


CONV_HALO = SUBLANES
M_BATCH = 2


def _mlstm_kernel(q_ref, k_ref, v_ref, og_ref, gc_ref, gr_ref, bc_ref, br_ref, cw_ref, cb_ref, ng_ref,
                  y_ref, xq_sc, xk_sc, ct_sc, n_sc, m_sc):
    c = pl.program_id(1)
    L = M_CHUNK
    lo = CONV_HALO

    nbat = q_ref.shape[0]

    @pl.when(c == 0)
    def _():
        xq_sc[:, 0:lo, :] = jnp.zeros((nbat, lo, M_W), F32)
        xk_sc[:, 0:lo, :] = jnp.zeros((nbat, lo, M_W), F32)
        ct_sc[...] = jnp.zeros_like(ct_sc)
        n_sc[...] = jnp.zeros_like(n_sc)
        m_sc[...] = jnp.zeros_like(m_sc)

    for bi in range(nbat):
        xq_sc[bi, lo:lo + L, :] = q_ref[bi]
        xk_sc[bi, lo:lo + L, :] = k_ref[bi]

    def conv(x_sc, bi, w, b):
        acc = x_sc[bi, lo:lo + L, :] * w[CONV_K - 1:CONV_K, :] + b
        for j in range(CONV_K - 1):
            off = lo - (CONV_K - 1) + j
            acc = acc + x_sc[bi, off:off + L, :] * w[j:j + 1, :]
        return acc

    cw = cw_ref[...]
    cb = cb_ref[...]
    qc = [_silu(conv(xq_sc, bi, cw[:, :M_W], cb[:, :M_W])) for bi in range(nbat)]
    kc = [_silu(conv(xk_sc, bi, cw[:, M_W:], cb[:, M_W:])) * (M_HEAD_DIM ** -0.5) for bi in range(nbat)]
    for bi in range(nbat):
        xq_sc[bi, 0:lo, :] = xq_sc[bi, L:L + lo, :]
        xk_sc[bi, 0:lo, :] = xk_sc[bi, L:L + lo, :]

    row = lax.broadcasted_iota(jnp.int32, (L, L), 0)
    col = lax.broadcasted_iota(jnp.int32, (L, L), 1)
    causal = row >= col
    heads = range(nbat * M_HEADS)
    bat = [ch // M_HEADS for ch in heads]
    hd = [ch % M_HEADS for ch in heads]
    hsl = [slice(hd[ch] * M_HEAD_DIM, (hd[ch] + 1) * M_HEAD_DIM) for ch in heads]
    ct = [ct_sc[h] for h in heads]
    n_row = [n_sc[h, 0:1, :] for h in heads]
    m_prev = [m_sc[h, 0:1, 0:1] for h in heads]
    q = [qc[bat[h]][:, hsl[h]] for h in heads]
    k = [kc[bat[h]][:, hsl[h]] for h in heads]
    qb = [q[h].astype(BF16) for h in heads]
    kb = [k[h].astype(BF16) for h in heads]
    vb = [v_ref[bat[h], :, hsl[h]].astype(BF16) for h in heads]
    qk = [lax.dot_general(qb[h], kb[h], _CONTRACT_LAST, preferred_element_type=F32) for h in heads]
    qc_state = [jnp.dot(qb[h], ct[h].astype(BF16), preferred_element_type=F32) for h in heads]

    def gate_col(h, j):
        return gc_ref[bat[h], :, j:j + 1] + bc_ref[:, j:j + 1]

    def gate_row(h, j):
        return gr_ref[bat[h], j:j + 1, :] + br_ref[j:j + 1, :]

    i_col = [gate_col(h, hd[h]) for h in heads]
    lf_col = [_log_sigmoid(gate_col(h, M_HEADS + hd[h])) for h in heads]
    i_row = [gate_row(h, hd[h]) for h in heads]
    lf_row = [_log_sigmoid(gate_row(h, M_HEADS + hd[h])) for h in heads]
    b_col = [jnp.sum(jnp.where(causal, lf_row[h], 0.0), axis=1, keepdims=True) for h in heads]
    b_row = [jnp.sum(jnp.where(row <= col, lf_col[h], 0.0), axis=0, keepdims=True) for h in heads]
    log_d = [jnp.where(causal, b_col[h] - b_row[h] + i_row[h], -jnp.inf) for h in heads]
    log_inter = [b_col[h] + m_prev[h] for h in heads]
    m_t = [jnp.maximum(jnp.max(log_d[h], axis=1, keepdims=True), log_inter[h]) for h in heads]
    s = [qk[h] * jnp.exp(log_d[h] - m_t[h]) for h in heads]
    inter = [jnp.exp(log_inter[h] - m_t[h]) for h in heads]
    sv = [jnp.dot(s[h].astype(BF16), vb[h], preferred_element_type=F32) for h in heads]

    m_new = [m_t[h][L - 1:L, :] for h in heads]
    b_last = [b_col[h][L - 1:L, :] for h in heads]
    w_col = [jnp.exp(b_last[h] - b_col[h] + i_col[h] - m_new[h]) for h in heads]
    decay = [jnp.exp(b_last[h] + m_prev[h] - m_new[h]) for h in heads]
    kw = [k[h] * w_col[h] for h in heads]
    kv = [lax.dot_general(kw[h].astype(BF16), vb[h], (((0,), (0,)), ((), ())), preferred_element_type=F32)
          for h in heads]

    num = [sv[h] + inter[h] * qc_state[h] for h in heads]
    den = [jnp.sum(s[h], axis=1, keepdims=True) + inter[h] * jnp.sum(q[h] * n_row[h], axis=1, keepdims=True)
           for h in heads]
    hh = [num[h] / jnp.maximum(jnp.abs(den[h]), jnp.exp(-m_t[h])) for h in heads]
    z = [jax.nn.sigmoid(og_ref[bat[h], :, hsl[h]]) * hh[h] for h in heads]
    y = [(_normalize(z[h]) * ng_ref[:, hsl[h]]).astype(BF16) for h in heads]
    ct_new = [decay[h] * ct[h] + kv[h] for h in heads]
    n_new = [decay[h] * n_row[h] + jnp.sum(kw[h], axis=0, keepdims=True) for h in heads]
    for h in heads:
        ct_sc[h] = ct_new[h]
        n_sc[h] = jnp.broadcast_to(n_new[h], (SUBLANES, M_HEAD_DIM))
        m_sc[h] = jnp.broadcast_to(m_new[h], (SUBLANES, LANES))
        y_ref[bat[h], :, hsl[h]] = y[h]


def _mlstm(p, gcol, grow, bcol, brow, conv_w, conv_b, norm_g):
    nc = SEQ // M_CHUNK
    mb = M_BATCH
    p3 = p.reshape(BATCH, SEQ, P_WIDTH)
    slab = lambda blk: pl.BlockSpec((mb, M_CHUNK, M_W), lambda b, c: (b, c, blk))
    y = pl.pallas_call(
        _mlstm_kernel,
        out_shape=jax.ShapeDtypeStruct((BATCH, SEQ, M_W), BF16),
        grid=(BATCH // mb, nc),
        in_specs=[
            slab(P_QM_BLK), slab(P_KM_BLK), slab(P_VM_BLK), slab(P_OM_BLK),
            pl.BlockSpec((mb, M_CHUNK, 2 * M_HEADS), lambda b, c: (b, c, 0)),
            pl.BlockSpec((mb, 2 * M_HEADS, M_CHUNK), lambda b, c: (b, 0, c)),
            pl.BlockSpec((1, 2 * M_HEADS), lambda b, c: (0, 0)),
            pl.BlockSpec((2 * M_HEADS, 1), lambda b, c: (0, 0)),
            pl.BlockSpec((CONV_K, 2 * M_W), lambda b, c: (0, 0)),
            pl.BlockSpec((1, 2 * M_W), lambda b, c: (0, 0)),
            pl.BlockSpec((1, M_W), lambda b, c: (0, 0)),
        ],
        out_specs=pl.BlockSpec((mb, M_CHUNK, M_W), lambda b, c: (b, c, 0)),
        scratch_shapes=[
            pltpu.VMEM((mb, M_CHUNK + CONV_HALO, M_W), F32),
            pltpu.VMEM((mb, M_CHUNK + CONV_HALO, M_W), F32),
            pltpu.VMEM((mb * M_HEADS, M_HEAD_DIM, M_HEAD_DIM), F32),
            pltpu.VMEM((mb * M_HEADS, SUBLANES, M_HEAD_DIM), F32),
            pltpu.VMEM((mb * M_HEADS, SUBLANES, LANES), F32),
        ],
        compiler_params=_cparams(("arbitrary", "arbitrary")),
        name="mlstm",
    )(p3, p3, p3, p3, gcol, grow, bcol, brow, conv_w, conv_b, norm_g)
    return y.reshape(TOKENS, M_W)


RT_E1, RT_E2, RT_W1, RT_W2 = 0, 1, 2, 3


def _route(logits):
    lane = lax.broadcasted_iota(jnp.int32, logits.shape, 1).astype(F32)
    big = float(LANES)
    is_g = lane < N_EXPERT_GROUPS
    gl = jnp.where(is_g, logits, -jnp.inf)
    gexp = jnp.exp(gl - jnp.max(gl, axis=1, keepdims=True))
    gprob = gexp / jnp.sum(gexp, axis=1, keepdims=True)
    g_w = jnp.max(gprob, axis=1, keepdims=True)
    g_top = jnp.min(jnp.where(is_g & (gprob == g_w), lane, big), axis=1, keepdims=True)
    lo = N_EXPERT_GROUPS + EXPERTS_PER_GROUP * g_top
    in_grp = (lane >= lo) & (lane < lo + EXPERTS_PER_GROUP)
    el = jnp.where(in_grp, logits, -jnp.inf)
    eexp = jnp.exp(el - jnp.max(el, axis=1, keepdims=True))
    eprob = eexp / jnp.sum(eexp, axis=1, keepdims=True)
    v1 = jnp.max(eprob, axis=1, keepdims=True)
    i1 = jnp.min(jnp.where(in_grp & (eprob == v1), lane, big), axis=1, keepdims=True)
    rest = jnp.where(in_grp & (lane != i1), eprob, -1.0)
    v2 = jnp.max(rest, axis=1, keepdims=True)
    i2 = jnp.min(jnp.where(rest == v2, lane, big), axis=1, keepdims=True)
    tot = v1 + v2
    w1 = g_w * (v1 / tot)
    w2 = g_w * (v2 / tot)
    e1 = i1 - N_EXPERT_GROUPS
    e2 = i2 - N_EXPERT_GROUPS
    rec = jnp.where(lane == RT_E1, e1, jnp.where(lane == RT_E2, e2, jnp.where(lane == RT_W1, w1, w2)))
    return jnp.where(lane <= RT_W2, rec, 0.0)


def _merge_kernel(ya_ref, ym_ref, g_ref, x_ref, mod_ref,
                  wpa_ref, wpm_ref, wout_ref, lng_ref, lnb_ref, wr_ref, br_ref,
                  x1_ref, u2_ref, rt_ref):
    pa = jnp.dot(ya_ref[...], wpa_ref[...], preferred_element_type=F32)
    pm = jnp.dot(ym_ref[...], wpm_ref[...], preferred_element_type=F32)
    merged = g_ref[:, :D_MODEL].astype(F32) * pa + g_ref[:, D_MODEL:].astype(F32) * pm
    mix = jnp.dot(merged.astype(BF16), wout_ref[...], preferred_element_type=F32)
    z = DEEPNORM_ALPHA * x_ref[...] + mod_ref[2:3, :] * mix
    x1 = _normalize(z) * lng_ref[...] + lnb_ref[...]
    x1_ref[...] = x1
    u2 = _normalize(x1) * (1.0 + mod_ref[4:5, :]) + mod_ref[3:4, :]
    u2_ref[...] = u2
    logits = jnp.dot(u2.astype(BF16), wr_ref[...], preferred_element_type=F32) + br_ref[...]
    rt_ref[...] = _route(logits)


def _merge(ya, ym, g, x2, mod, wpa, wpm, wout, lng, lnb, wr, br):
    tm = MERGE_TM
    tiles_per_batch = SEQ // tm
    rowblk = lambda w: pl.BlockSpec((tm, w), lambda m: (m, 0))
    const = lambda shape: pl.BlockSpec(shape, lambda m: (0,) * len(shape), pipeline_mode=pl.Buffered(1))
    return pl.pallas_call(
        _merge_kernel,
        out_shape=(
            jax.ShapeDtypeStruct((TOKENS, D_MODEL), F32),
            jax.ShapeDtypeStruct((TOKENS, D_MODEL), F32),
            jax.ShapeDtypeStruct((TOKENS, LANES), F32),
        ),
        grid=(TOKENS // tm,),
        in_specs=[
            rowblk(A_GROUP_W), rowblk(M_W), rowblk(2 * D_MODEL), rowblk(D_MODEL),
            pl.BlockSpec((None, 6, D_MODEL), lambda m: (m // tiles_per_batch, 0, 0)),
            const((A_GROUP_W, D_MODEL)), const((M_W, D_MODEL)), const((D_MODEL, D_MODEL)),
            const((1, D_MODEL)), const((1, D_MODEL)),
            const((D_MODEL, LANES)), const((1, LANES)),
        ],
        out_specs=(rowblk(D_MODEL), rowblk(D_MODEL), rowblk(LANES)),
        compiler_params=_cparams(("arbitrary",)),
        name="merge_ln1_route",
    )(ya, ym, g, x2, mod, wpa, wpm, wout, lng, lnb, wr, br)


CAST_ROWS = 128


def _issue_rows(src_ref, idx_ref, base, buf, slot, sem, nrows):
    def body(blk, carry):
        for j in range(SUBLANES):
            row = idx_ref[base + blk * SUBLANES + j]
            src = src_ref.at[lax.shift_right_logical(row, 3), pl.ds(row & (SUBLANES - 1), 1)]
            pltpu.make_async_copy(src, buf.at[slot, blk, pl.ds(j, 1)], sem.at[slot]).start(priority=j % 2)
        return carry

    lax.fori_loop(0, nrows // SUBLANES, body, 0)


def _wait_rows(src_ref, buf, slot, sem, nrows):
    groups = nrows // SUBLANES
    pltpu.make_async_copy(src_ref.at[pl.ds(0, groups)], buf.at[slot, pl.ds(0, groups)], sem.at[slot]).wait()


class _TileTable:
    def __init__(self, meta_ref):
        self.ref = meta_ref

    def expert(self, t):
        return self.ref[META_TE * LANES + t]

    def next_expert(self, t):
        return self.ref[META_TN * LANES + t]

    def valid_rows(self, t):
        return self.ref[META_TV * LANES + t]

    def tiles_used(self):
        return self.ref[META_NU * LANES]


def _expert_changed(te_ref, i):
    return (i == 0) | (te_ref.expert(i) != te_ref.expert(jnp.maximum(i - 1, 0)))


def _moe_kernel(meta_ref, tok_ref, u_ref, wg_ref, wu_ref, wd_ref, o_ref,
                stg, stu, std, wgb, wub, wdb, xbuf, wsem, xsem):
    i = pl.program_id(0)
    te_ref = _TileTable(meta_ref)
    nu = te_ref.tiles_used()
    tm = o_ref.shape[0]
    half = tm // 2

    def weight_copies(e):
        return (pltpu.make_async_copy(wg_ref.at[e], stg, wsem.at[0]),
                pltpu.make_async_copy(wu_ref.at[e], stu, wsem.at[1]),
                pltpu.make_async_copy(wd_ref.at[e], std, wsem.at[2]))

    def for_tile_rows(t, fn):
        @pl.when(te_ref.valid_rows(t) <= half)
        def _():
            fn(half)

        @pl.when(te_ref.valid_rows(t) > half)
        def _():
            fn(tm)

    def issue_tile(t):
        for_tile_rows(t, lambda nrows: _issue_rows(u_ref, tok_ref, t * tm, xbuf, t % MOE_ROW_SLOTS, xsem, nrows))

    @pl.when(i == 0)
    def _():
        issue_tile(0)

        @pl.when(nu > 1)
        def _():
            issue_tile(1)

        for cp in weight_copies(te_ref.expert(0)):
            cp.start(priority=1)

    @pl.when(i < nu)
    def _():
        slot = i % MOE_ROW_SLOTS

        @pl.when(i + 2 < nu)
        def _():
            issue_tile(i + 2)

        changed = _expert_changed(te_ref, i)

        @pl.when(changed)
        def _():
            for cp in weight_copies(te_ref.expert(i)):
                cp.wait()
            def cast_rows(ci, carry):
                r = pl.multiple_of(ci * CAST_ROWS, CAST_ROWS)
                wgb[pl.ds(r, CAST_ROWS), :] = stg[pl.ds(r, CAST_ROWS), :].astype(BF16)
                return carry

            lax.fori_loop(0, D_MODEL // CAST_ROWS, cast_rows, 0)

        def convert_inline(src, dst):
            for r0 in range(0, src.shape[0], CAST_ROWS):
                dst[r0:r0 + CAST_ROWS, :] = src[r0:r0 + CAST_ROWS, :].astype(BF16)

        def compute(nrows, new_expert):
            _wait_rows(u_ref, xbuf, slot, xsem, nrows)
            x = xbuf[slot, 0:nrows // SUBLANES].reshape(nrows, D_MODEL).astype(BF16)
            a = jnp.dot(x, wgb[...], preferred_element_type=F32)
            if new_expert:
                convert_inline(stu, wub)
            b = jnp.dot(x, wub[...], preferred_element_type=F32)
            if new_expert:
                convert_inline(std, wdb)

                @pl.when(te_ref.next_expert(i) >= 0)
                def _():
                    for cp in weight_copies(te_ref.next_expert(i)):
                        cp.start(priority=1)
            h = (_silu(a) * b).astype(BF16)
            o_ref[0:nrows, :] = jnp.dot(h, wdb[...], preferred_element_type=F32)
            if nrows < tm:
                o_ref[nrows:tm, :] = jnp.zeros((tm - nrows, D_MODEL), F32)

        @pl.when(changed)
        def _():
            for_tile_rows(i, functools.partial(compute, new_expert=True))

        @pl.when(jnp.logical_not(changed))
        def _():
            for_tile_rows(i, functools.partial(compute, new_expert=False))

    @pl.when(i >= nu)
    def _():
        o_ref[...] = jnp.zeros_like(o_ref)


def _moe(meta, row_token, u2, w_eg, w_eu, w_ed):
    tm = MOE_TM
    return pl.pallas_call(
        _moe_kernel,
        out_shape=jax.ShapeDtypeStruct((MOE_ROWS, D_MODEL), F32),
        grid_spec=pltpu.PrefetchScalarGridSpec(
            num_scalar_prefetch=2,
            grid=(MOE_TILES,),
            in_specs=[pl.BlockSpec(memory_space=pl.ANY)] * 4,
            out_specs=pl.BlockSpec((tm, D_MODEL), lambda i, meta, tok: (i, 0)),
            scratch_shapes=[
                pltpu.VMEM((D_MODEL, D_FF_EXPERT), F32),
                pltpu.VMEM((D_MODEL, D_FF_EXPERT), F32),
                pltpu.VMEM((D_FF_EXPERT, D_MODEL), F32),
                pltpu.VMEM((D_MODEL, D_FF_EXPERT), BF16),
                pltpu.VMEM((D_MODEL, D_FF_EXPERT), BF16),
                pltpu.VMEM((D_FF_EXPERT, D_MODEL), BF16),
                pltpu.VMEM((MOE_ROW_SLOTS, tm // SUBLANES, SUBLANES, D_MODEL), F32),
                pltpu.SemaphoreType.DMA((3,)),
                pltpu.SemaphoreType.DMA((MOE_ROW_SLOTS,)),
            ],
        ),
        compiler_params=_cparams(("arbitrary",)),
        name="moe_experts",
    )(meta, row_token, u2.reshape(TOKENS // SUBLANES, SUBLANES, D_MODEL), w_eg, w_eu, w_ed)


def _final_kernel(pos_ref, o_ref, x1_ref, rt_ref, mod_ref, lng_ref, lnb_ref, y_ref, buf, sem):
    i = pl.program_id(0)
    tb = x1_ref.shape[0]
    slot = i % 2

    @pl.when(i == 0)
    def _():
        _issue_rows(o_ref, pos_ref, 0, buf, 0, sem, 2 * tb)

    @pl.when(i + 1 < pl.num_programs(0))
    def _():
        _issue_rows(o_ref, pos_ref, (i + 1) * (2 * tb), buf, 1 - slot, sem, 2 * tb)

    _wait_rows(o_ref, buf, slot, sem, 2 * tb)
    rows = buf[slot].reshape(2 * tb, D_MODEL)
    ffn = rt_ref[:, RT_W1:RT_W1 + 1] * rows[0:tb, :] + rt_ref[:, RT_W2:RT_W2 + 1] * rows[tb:2 * tb, :]
    z = DEEPNORM_ALPHA * x1_ref[...] + mod_ref[5:6, :] * ffn
    y_ref[...] = _normalize(z) * lng_ref[...] + lnb_ref[...]


def _final(pos_tiles, moe_out, x1, rt, mod, lng, lnb):
    tb = FINAL_TB
    tiles_per_batch = SEQ // tb
    return pl.pallas_call(
        _final_kernel,
        out_shape=jax.ShapeDtypeStruct((TOKENS, D_MODEL), F32),
        grid_spec=pltpu.PrefetchScalarGridSpec(
            num_scalar_prefetch=1,
            grid=(TOKENS // tb,),
            in_specs=[
                pl.BlockSpec(memory_space=pl.ANY),
                pl.BlockSpec((tb, D_MODEL), lambda i, pos: (i, 0)),
                pl.BlockSpec((tb, LANES), lambda i, pos: (i, 0)),
                pl.BlockSpec((None, 6, D_MODEL), lambda i, pos: (i // tiles_per_batch, 0, 0)),
                pl.BlockSpec((1, D_MODEL), lambda i, pos: (0, 0)),
                pl.BlockSpec((1, D_MODEL), lambda i, pos: (0, 0)),
            ],
            out_specs=pl.BlockSpec((tb, D_MODEL), lambda i, pos: (i, 0)),
            scratch_shapes=[pltpu.VMEM((2, 2 * tb // SUBLANES, SUBLANES, D_MODEL), F32),
                            pltpu.SemaphoreType.DMA((2,))],
        ),
        compiler_params=_cparams(("arbitrary",)),
        name="combine_ln2",
    )(pos_tiles, moe_out.reshape(MOE_ROWS // SUBLANES, SUBLANES, D_MODEL), x1, rt, mod, lng, lnb)


ROUTE_BLK = 256
ROUTE_LOCKSTEP = 4
META_TE, META_TN, META_TV, META_NU = 0, 1, 2, 3


def _lane_cumsum(x, lane):
    s = 1
    while s < LANES:
        x = x + jnp.where(lane >= s, pltpu.roll(x, s, 1), 0.0)
        s *= 2
    return x


def _route_tables_kernel(rt_ref, pos_ref, meta_ref, rank_sc):
    tm = float(MOE_TM)
    nblk = TOKENS // ROUTE_BLK
    lane_i = lax.broadcasted_iota(jnp.int32, (ROUTE_BLK, LANES), 1)
    lane_f = lane_i.astype(F32)
    earlier = (lax.broadcasted_iota(jnp.int32, (ROUTE_BLK, ROUTE_BLK), 0)
               > lax.broadcasted_iota(jnp.int32, (ROUTE_BLK, ROUTE_BLK), 1)).astype(BF16)

    def onehots(b):
        blk = rt_ref[pl.ds(pl.multiple_of(b * ROUTE_BLK, ROUTE_BLK), ROUTE_BLK), :]
        return blk[:, RT_E1:RT_E1 + 1] == lane_f, blk[:, RT_E2:RT_E2 + 1] == lane_f

    def count_pass(g, carry):
        c1, c2 = carry
        bs = [g * ROUTE_LOCKSTEP + i for i in range(ROUTE_LOCKSTEP)]
        oh = [onehots(b) for b in bs]
        f = [(jnp.where(o1, 1.0, 0.0), jnp.where(o2, 1.0, 0.0)) for o1, o2 in oh]
        p = [(jnp.dot(earlier, f1.astype(BF16), preferred_element_type=F32),
              jnp.dot(earlier, f2.astype(BF16), preferred_element_type=F32)) for f1, f2 in f]
        tot = [(jnp.sum(f1, axis=0, keepdims=True), jnp.sum(f2, axis=0, keepdims=True)) for f1, f2 in f]
        for i, b in enumerate(bs):
            r1 = jnp.sum(jnp.where(oh[i][0], p[i][0] + c1, 0.0), axis=1, keepdims=True)
            r2 = jnp.sum(jnp.where(oh[i][1], p[i][1] + c2, 0.0), axis=1, keepdims=True)
            rank_sc[pl.ds(pl.multiple_of(b * ROUTE_BLK, ROUTE_BLK), ROUTE_BLK), :] = jnp.where(
                lane_i == 0, r1, jnp.where(lane_i == 1, r2, 0.0))
            c1 = c1 + tot[i][0]
            c2 = c2 + tot[i][1]
        return c1, c2

    zero = jnp.zeros((1, LANES), F32)
    c1, c2 = lax.fori_loop(0, nblk // ROUTE_LOCKSTEP, count_pass, (zero, zero))

    lane8 = lax.broadcasted_iota(jnp.int32, (SUBLANES, LANES), 1)
    counts = jnp.broadcast_to(c1 + c2, (SUBLANES, LANES))
    padded = jnp.floor((counts + (tm - 1.0)) * (1.0 / tm)) * tm
    pend = _lane_cumsum(padded, lane8)
    pstart = pend - padded
    start1 = pstart[0:1, :]
    start2 = start1 + c1

    def place_pass(g, carry):
        bs = [g * ROUTE_LOCKSTEP + i for i in range(ROUTE_LOCKSTEP)]
        oh = [onehots(b) for b in bs]
        rows = [pl.ds(pl.multiple_of(b * ROUTE_BLK, ROUTE_BLK), ROUTE_BLK) for b in bs]
        rk = [rank_sc[r, :] for r in rows]
        d1 = [jnp.sum(jnp.where(oh[i][0], start1, 0.0), axis=1, keepdims=True) + rk[i][:, 0:1]
              for i in range(ROUTE_LOCKSTEP)]
        d2 = [jnp.sum(jnp.where(oh[i][1], start2, 0.0), axis=1, keepdims=True) + rk[i][:, 1:2]
              for i in range(ROUTE_LOCKSTEP)]
        for i in range(ROUTE_LOCKSTEP):
            pos_ref[rows[i], :] = jnp.where(lane_i == 0, d1[i], jnp.where(lane_i == 1, d2[i], 0.0)).astype(jnp.int32)
        return carry

    lax.fori_loop(0, nblk // ROUTE_LOCKSTEP, place_pass, 0)

    tile = lax.broadcasted_iota(jnp.int32, (LANES, LANES), 0).astype(F32)
    lane = lax.broadcasted_iota(jnp.int32, (LANES, LANES), 1)
    expert_lane = lane < N_EXPERTS
    pend_b = jnp.broadcast_to(pend[0:1, :], (LANES, LANES))
    pstart_b = jnp.broadcast_to(pstart[0:1, :], (LANES, LANES))
    counts_b = jnp.broadcast_to(counts[0:1, :], (LANES, LANES))
    n_used = jnp.sum(jnp.where(lane == N_EXPERTS - 1, pend_b, 0.0), axis=1, keepdims=True) * (1.0 / tm)
    te = jnp.sum(jnp.where(expert_lane & (pend_b <= tile * tm), 1.0, 0.0), axis=1, keepdims=True)
    te = jnp.minimum(te, float(N_EXPERTS - 1))
    te_last = jnp.sum(jnp.where(tile[:, 0:1] == n_used - 1.0, te, 0.0), axis=0, keepdims=True)
    te = jnp.where(tile[:, 0:1] < n_used, te, te_last)
    of_tile = lane.astype(F32) == te
    pend_te = jnp.sum(jnp.where(of_tile, pend_b, 0.0), axis=1, keepdims=True)
    last_row = jnp.sum(jnp.where(of_tile, pstart_b + counts_b, 0.0), axis=1, keepdims=True)
    tile_rows = jnp.clip(last_row - tile[:, 0:1] * tm, 0.0, tm)
    next_run = pend_te * (1.0 / tm)
    te_by_lane = jnp.broadcast_to(te, (LANES, LANES)).T
    te_at_next = jnp.sum(jnp.where(lane.astype(F32) == next_run, te_by_lane, 0.0), axis=1, keepdims=True)
    te_next = jnp.where(next_run < n_used, te_at_next, -1.0)
    cols = jnp.where(lane == META_TE, te, jnp.where(lane == META_TN, te_next,
                     jnp.where(lane == META_TV, tile_rows, jnp.where(lane == META_NU, n_used, 0.0))))
    meta_ref[...] = cols.T[0:SUBLANES, :].astype(jnp.int32)


def _routing_tables(rt):
    pos, meta = pl.pallas_call(
        _route_tables_kernel,
        out_shape=(
            jax.ShapeDtypeStruct((TOKENS, LANES), jnp.int32),
            jax.ShapeDtypeStruct((SUBLANES, LANES), jnp.int32),
        ),
        scratch_shapes=[pltpu.VMEM((TOKENS, LANES), F32)],
        compiler_params=_cparams(None),
        name="route_tables",
    )(rt)
    pos = pos[:, 0:2]
    tok = jnp.arange(2 * TOKENS, dtype=jnp.int32) // 2
    row_token = jnp.zeros((MOE_ROWS,), jnp.int32).at[pos.reshape(-1)].set(
        tok, unique_indices=True, mode="promise_in_bounds")
    return row_token, meta.reshape(-1), pos


def _rope_tables():
    inv = ROPE_THETA ** (-jnp.arange(0, HEAD_DIM_A, 2, dtype=F32) / HEAD_DIM_A)
    ang = jnp.arange(SEQ, dtype=F32)[:, None] * inv[None, :]
    cos = jnp.cos(ang)
    sin = jnp.sin(ang)
    return jnp.concatenate([cos, cos], axis=-1), jnp.concatenate([-sin, sin], axis=-1)


def kernel(x, c, w_ada, b_ada, w_in, b_mgate, conv_w, conv_b, m_norm_g, w_proj_a, w_proj_m, w_gate, b_gate,
           w_out, ln1_g, ln1_b, w_rg, b_rg, w_re, b_re, w_eg, w_eu, w_ed, ln2_g, ln2_b):
    assert x.shape == (BATCH, SEQ, D_MODEL) and w_ada.shape[0] == 1
    l = 0
    x2 = x.reshape(TOKENS, D_MODEL)

    c_pad = jnp.zeros((ADA_ROWS, D_MODEL), F32).at[:BATCH].set(c)
    mod = _ada(c_pad, w_ada[l], b_ada[l][None, :])[:BATCH].reshape(BATCH, 6, D_MODEL)

    w_in_t = jnp.swapaxes(w_in[l], 0, 1)
    w_if_t = jnp.zeros((LANES, D_MODEL), F32).at[:2 * M_HEADS].set(w_in_t[N_IN_MAIN:])
    p, gates, u = _inproj(x2, mod, w_in_t, w_if_t)
    g = _gateproj(u, w_gate[l], b_gate[l][None, :])

    cos_t, sin_t = _rope_tables()
    ya = _attention(p, cos_t, sin_t)

    gcol = gates[:, :2 * M_HEADS].reshape(BATCH, SEQ, 2 * M_HEADS)
    grow = jnp.transpose(gcol, (0, 2, 1))
    ym = _mlstm(p, gcol, grow, b_mgate[l][None, :], b_mgate[l][:, None], conv_w[l], conv_b[l][None, :],
                m_norm_g[l][None, :])

    wr = (jnp.zeros((D_MODEL, LANES), F32)
          .at[:, :N_EXPERT_GROUPS].set(w_rg[l])
          .at[:, N_EXPERT_GROUPS:N_EXPERT_GROUPS + N_EXPERTS].set(w_re[l])).astype(BF16)
    br = (jnp.zeros((1, LANES), F32)
          .at[0, :N_EXPERT_GROUPS].set(b_rg[l])
          .at[0, N_EXPERT_GROUPS:N_EXPERT_GROUPS + N_EXPERTS].set(b_re[l]))
    x1, u2, rt = _merge(ya, ym, g, x2, mod,
                        w_proj_a[l].astype(BF16), w_proj_m[l].astype(BF16), w_out[l].astype(BF16),
                        ln1_g[l][None, :], ln1_b[l][None, :], wr, br)

    row_token, meta, pos = _routing_tables(rt)

    mo = _moe(meta, row_token, u2,
              w_eg[l].reshape(N_EXPERTS, D_MODEL, D_FF_EXPERT), w_eu[l].reshape(N_EXPERTS, D_MODEL, D_FF_EXPERT),
              w_ed[l].reshape(N_EXPERTS, D_FF_EXPERT, D_MODEL))

    nt = TOKENS // FINAL_TB
    pos_tiles = jnp.transpose(pos.reshape(nt, FINAL_TB, 2), (0, 2, 1)).reshape(-1)
    y = _final(pos_tiles, mo, x1, rt, mod, ln2_g[l][None, :], ln2_b[l][None, :])
    return y.reshape(BATCH, SEQ, D_MODEL)
```

```python
import functools

import jax
import jax.numpy as jnp
from jax import lax
from jax.experimental import pallas as pl
from jax.experimental.pallas import tpu as pltpu

F32 = jnp.float32
BF16 = jnp.bfloat16

D_MODEL = 2048
BATCH = 4
SEQ = 2048
TOKENS = BATCH * SEQ
DIL_CONFIGS = ((128, 1), (512, 4), (2048, 16))
N_DIL_GROUPS = 3
HEADS_PER_GROUP = 4
HEAD_DIM_A = 128
ATT_BLOCK = 128
ROPE_THETA = 10000.0
A_GROUP_W = HEADS_PER_GROUP * HEAD_DIM_A
A_QKV_W = N_DIL_GROUPS * A_GROUP_W
M_HEADS = 4
M_HEAD_DIM = 256
M_W = M_HEADS * M_HEAD_DIM
M_CHUNK = 128
CONV_K = 4
N_IN_MAIN = 3 * A_QKV_W + 4 * M_W
N_EXPERT_GROUPS = 4
EXPERTS_PER_GROUP = 8
N_EXPERTS = N_EXPERT_GROUPS * EXPERTS_PER_GROUP
D_FF_EXPERT = 1024
DEEPNORM_ALPHA = 2.0 ** 0.25
LN_EPS = 1e-5

LANES = 128
SUBLANES = 8
VMEM_LIMIT_BYTES = 56 * 1024 * 1024

PROJ_TN = 256
GATE_TN = 1024
PROJ_TM = 1024
IN_TM = 2048
P_TILES_IN = N_IN_MAIN // PROJ_TN
P_ATT_TILES = 3 * A_QKV_W // PROJ_TN
P_M_TILES = P_TILES_IN - P_ATT_TILES
P_WIDTH = N_IN_MAIN
P_ATT_OFF = P_M_TILES * PROJ_TN
P_QM_BLK, P_KM_BLK, P_VM_BLK, P_OM_BLK = 0, 1, 2, 3
MERGE_TM = 256
MOE_TM = 256
MOE_ROWS = 2 * TOKENS + N_EXPERTS * MOE_TM
MOE_TILES = MOE_ROWS // MOE_TM
MOE_ROW_AHEAD = 3
MOE_ROW_SLOTS = MOE_ROW_AHEAD + 1
FINAL_TB = 256


def _cparams(sem, vmem=VMEM_LIMIT_BYTES):
    return pltpu.CompilerParams(dimension_semantics=sem, vmem_limit_bytes=vmem)


def _normalize(x):
    mu = jnp.mean(x, axis=-1, keepdims=True)
    xc = x - mu
    var = jnp.mean(xc * xc, axis=-1, keepdims=True)
    return xc * lax.rsqrt(var + LN_EPS)


def _silu(x):
    return x * jax.nn.sigmoid(x)


def _log_sigmoid(x):
    return jnp.minimum(x, 0.0) - jnp.log(1.0 + jnp.exp(-jnp.abs(x)))


_CONTRACT_LAST = (((1,), (1,)), ((), ()))
LOG2E = 1.4426950408889634


ADA_TN = 1024
ADA_ROWS = 16


def _ada_kernel(c_ref, w_ref, b_ref, o_ref):
    sc = _silu(c_ref[...]).astype(BF16)
    o_ref[...] = jnp.dot(sc, w_ref[...].astype(BF16), preferred_element_type=F32) + b_ref[...]


def _ada(c_pad, w_ada, b_ada):
    n = w_ada.shape[1]
    return pl.pallas_call(
        _ada_kernel,
        out_shape=jax.ShapeDtypeStruct((ADA_ROWS, n), F32),
        grid=(n // ADA_TN,),
        in_specs=[
            pl.BlockSpec((ADA_ROWS, D_MODEL), lambda j: (0, 0)),
            pl.BlockSpec((D_MODEL, ADA_TN), lambda j: (0, j)),
            pl.BlockSpec((1, ADA_TN), lambda j: (0, j)),
        ],
        out_specs=pl.BlockSpec((ADA_ROWS, ADA_TN), lambda j: (0, j)),
        compiler_params=_cparams(("arbitrary",)),
        name="ada_mod",
    )(c_pad, w_ada, b_ada)


LN_CHUNK = 256


def _inproj_kernel(x_ref, mod_ref, w_ref, wif_ref, p_ref, g_ref, u_ref, xbuf, xsem):
    m = pl.program_id(0)
    n = pl.program_id(1)

    def x_copy(tile):
        return pltpu.make_async_copy(x_ref.at[pl.ds(pl.multiple_of(tile * IN_TM, IN_TM), IN_TM)], xbuf, xsem)

    @pl.when((m == 0) & (n == 0))
    def _():
        x_copy(0).start()

    @pl.when((n == 1) & (m + 1 < pl.num_programs(0)))
    def _():
        x_copy(m + 1).start()

    @pl.when(n == 0)
    def _():
        x_copy(m).wait()
        shift = mod_ref[0:1, :]
        scale = 1.0 + mod_ref[1:2, :]
        wif = wif_ref[...].astype(BF16)
        w = w_ref[...].astype(BF16)
        for ci in range(IN_TM // LN_CHUNK):
            rows = slice(ci * LN_CHUNK, (ci + 1) * LN_CHUNK)
            u = (_normalize(xbuf[rows, :]) * scale + shift).astype(BF16)
            u_ref[rows, :] = u
            g_ref[rows, :] = lax.dot_general(u, wif, _CONTRACT_LAST, preferred_element_type=F32)
            p_ref[rows, :] = lax.dot_general(u, w, _CONTRACT_LAST, preferred_element_type=F32)

    @pl.when(n > 0)
    def _():
        p_ref[...] = lax.dot_general(u_ref[...], w_ref[...].astype(BF16), _CONTRACT_LAST,
                                     preferred_element_type=F32)


def _inproj(x2, mod, w_in_t, w_if_t):
    tiles_per_batch = SEQ // IN_TM
    return pl.pallas_call(
        _inproj_kernel,
        out_shape=(
            jax.ShapeDtypeStruct((TOKENS, P_WIDTH), F32),
            jax.ShapeDtypeStruct((TOKENS, LANES), F32),
            jax.ShapeDtypeStruct((TOKENS, D_MODEL), BF16),
        ),
        grid=(TOKENS // IN_TM, P_TILES_IN),
        in_specs=[
            pl.BlockSpec(memory_space=pl.ANY),
            pl.BlockSpec((None, 6, D_MODEL), lambda m, n: (m // tiles_per_batch, 0, 0)),
            pl.BlockSpec((PROJ_TN, D_MODEL), lambda m, n: (n, 0)),
            pl.BlockSpec((LANES, D_MODEL), lambda m, n: (0, 0)),
        ],
        out_specs=(
            pl.BlockSpec((IN_TM, PROJ_TN),
                         lambda m, n: (m, jnp.where(n < P_ATT_TILES, n + P_M_TILES, n - P_ATT_TILES))),
            pl.BlockSpec((IN_TM, LANES), lambda m, n: (m, 0)),
            pl.BlockSpec((IN_TM, D_MODEL), lambda m, n: (m, 0)),
        ),
        scratch_shapes=[pltpu.VMEM((IN_TM, D_MODEL), F32), pltpu.SemaphoreType.DMA],
        compiler_params=_cparams(("arbitrary", "arbitrary")),
        name="in_proj",
    )(x2, mod, w_in_t, w_if_t)


def _gateproj_kernel(u_ref, w_ref, b_ref, o_ref):
    acc = jnp.dot(u_ref[...], w_ref[...].astype(BF16), preferred_element_type=F32)
    o_ref[...] = jax.nn.sigmoid(acc + b_ref[...]).astype(BF16)


def _gateproj(u, w_gate, b_gate):
    n = w_gate.shape[1]
    return pl.pallas_call(
        _gateproj_kernel,
        out_shape=jax.ShapeDtypeStruct((TOKENS, n), BF16),
        grid=(TOKENS // PROJ_TM, n // GATE_TN),
        in_specs=[
            pl.BlockSpec((PROJ_TM, D_MODEL), lambda m, j: (m, 0)),
            pl.BlockSpec((D_MODEL, GATE_TN), lambda m, j: (0, j)),
            pl.BlockSpec((1, GATE_TN), lambda m, j: (0, j)),
        ],
        out_specs=pl.BlockSpec((PROJ_TM, GATE_TN), lambda m, j: (m, j)),
        compiler_params=_cparams(("arbitrary", "arbitrary")),
        name="gate_proj",
    )(u, w_gate, b_gate)


ROPE_ROWS = 256
ATT_LOCKSTEP = 8


def _rows(start, size, stride):
    return pl.ds(start, size) if stride == 1 else pl.ds(start, size, stride=stride)


def _attn_kernel(q0, k0, v0, q1, k1, v1, q2, k2, v2, cos_ref, sin_ref, y_ref,
                 qr_sc, kr_sc, o0, o1, o2, l0, l1, l2):
    scale = HEAD_DIM_A ** -0.5
    blk = ATT_BLOCK
    qi2 = lax.broadcasted_iota(jnp.int32, (blk, 2 * blk), 0)
    kc2 = lax.broadcasted_iota(jnp.int32, (blk, 2 * blk), 1)
    mask_prev_cur = (kc2 >= qi2) & (kc2 <= qi2 + blk)
    qi1 = lax.broadcasted_iota(jnp.int32, (blk, blk), 0)
    kc1 = lax.broadcasted_iota(jnp.int32, (blk, blk), 1)
    mask_cur = kc1 <= qi1
    half = HEAD_DIM_A // 2
    groups = ((q0, k0, v0, o0, l0), (q1, k1, v1, o1, l1), (q2, k2, v2, o2, l2))
    for (window, d), (q_ref, k_ref, v_ref, o_sc, l_sc) in zip(DIL_CONFIGS, groups):
        nb = SEQ // d // blk
        for src_ref, dst_sc in ((q_ref, qr_sc), (k_ref, kr_sc)):
            for c0 in range(0, SEQ, ROPE_ROWS):
                rs = slice(c0, c0 + ROPE_ROWS)
                xr = src_ref[rs, :]
                dst_sc[rs, :] = xr * cos_ref[rs, :] + pltpu.roll(xr, half, 1) * sin_ref[rs, :]
        blocks = [(r, j) for r in range(d) for j in range(nb)]
        for b0 in range(0, len(blocks), ATT_LOCKSTEP):
            batch = blocks[b0:b0 + ATT_LOCKSTEP]
            cur, keys, mask = [], [], []
            for r, j in batch:
                start = r + j * blk * d
                cur.append(_rows(start, blk, d))
                keys.append(cur[-1] if j == 0 else _rows(start - blk * d, 2 * blk, d))
                mask.append(mask_cur if j == 0 else mask_prev_cur)
            nbb = range(len(batch))
            qb = [qr_sc[cur[i], :].astype(BF16) for i in nbb]
            kw = [kr_sc[keys[i], :].astype(BF16) for i in nbb]
            s = [lax.dot_general(qb[i], kw[i], _CONTRACT_LAST, preferred_element_type=F32) for i in nbb]
            s = [jnp.where(mask[i], s[i], -jnp.inf) for i in nbb]
            mx = [jnp.max(s[i], axis=-1, keepdims=True) for i in nbb]
            p = [jnp.exp2((s[i] - mx[i]) * (scale * LOG2E)).astype(BF16) for i in nbb]
            vb = [v_ref[keys[i], :].astype(BF16) for i in nbb]
            pv = [jnp.dot(p[i], jnp.concatenate([vb[i], jnp.ones_like(vb[i])], axis=1),
                          preferred_element_type=F32) for i in nbb]
            for i in nbb:
                den = pv[i][:, HEAD_DIM_A:]
                o_sc[cur[i], :] = pv[i][:, :HEAD_DIM_A] / den
                l_sc[cur[i], :] = mx[i] * scale + jnp.log(den)
    for c0 in range(0, SEQ, ROPE_ROWS):
        rs = slice(c0, c0 + ROPE_ROWS)
        la = l0[rs, :]
        lb = l1[rs, :]
        lc = l2[rs, :]
        mx = jnp.maximum(jnp.maximum(la, lb), lc)
        ea = jnp.exp(la - mx)
        eb = jnp.exp(lb - mx)
        ec = jnp.exp(lc - mx)
        den = ea + eb + ec
        y_ref[rs, :] = ((ea / den) * o0[rs, :] + (eb / den) * o1[rs, :] + (ec / den) * o2[rs, :]).astype(BF16)


def _attention(p, cos_t, sin_t):
    for window, d in DIL_CONFIGS:
        assert window // d == ATT_BLOCK and SEQ % (d * ATT_BLOCK) == 0
    col0 = P_ATT_OFF // HEAD_DIM_A

    def slab(part, g):
        off = col0 + (part * A_QKV_W + g * A_GROUP_W) // HEAD_DIM_A
        return pl.BlockSpec((SEQ, HEAD_DIM_A), lambda b, h: (b, off + h))

    in_specs = [slab(part, g) for g in range(N_DIL_GROUPS) for part in range(3)]
    table = pl.BlockSpec((SEQ, HEAD_DIM_A), lambda b, h: (0, 0))
    return pl.pallas_call(
        _attn_kernel,
        out_shape=jax.ShapeDtypeStruct((TOKENS, A_GROUP_W), BF16),
        grid=(BATCH, HEADS_PER_GROUP),
        in_specs=in_specs + [table, table],
        out_specs=pl.BlockSpec((SEQ, HEAD_DIM_A), lambda b, h: (b, h)),
        scratch_shapes=[pltpu.VMEM((SEQ, HEAD_DIM_A), F32)] * 8,
        compiler_params=_cparams(("arbitrary", "arbitrary")),
        name="dil_attn",
    )(*([p] * 9), cos_t, sin_t)


CONV_HALO = SUBLANES
M_BATCH = 2


def _mlstm_kernel(q_ref, k_ref, v_ref, og_ref, gc_ref, gr_ref, bc_ref, br_ref, cw_ref, cb_ref, ng_ref,
                  y_ref, xq_sc, xk_sc, ct_sc, n_sc, m_sc):
    c = pl.program_id(1)
    L = M_CHUNK
    lo = CONV_HALO

    nbat = q_ref.shape[0]

    @pl.when(c == 0)
    def _():
        xq_sc[:, 0:lo, :] = jnp.zeros((nbat, lo, M_W), F32)
        xk_sc[:, 0:lo, :] = jnp.zeros((nbat, lo, M_W), F32)
        ct_sc[...] = jnp.zeros_like(ct_sc)
        n_sc[...] = jnp.zeros_like(n_sc)
        m_sc[...] = jnp.zeros_like(m_sc)

    for bi in range(nbat):
        xq_sc[bi, lo:lo + L, :] = q_ref[bi]
        xk_sc[bi, lo:lo + L, :] = k_ref[bi]

    def conv(x_sc, bi, w, b):
        acc = x_sc[bi, lo:lo + L, :] * w[CONV_K - 1:CONV_K, :] + b
        for j in range(CONV_K - 1):
            off = lo - (CONV_K - 1) + j
            acc = acc + x_sc[bi, off:off + L, :] * w[j:j + 1, :]
        return acc

    cw = cw_ref[...]
    cb = cb_ref[...]
    qc = [_silu(conv(xq_sc, bi, cw[:, :M_W], cb[:, :M_W])) for bi in range(nbat)]
    kc = [_silu(conv(xk_sc, bi, cw[:, M_W:], cb[:, M_W:])) * (M_HEAD_DIM ** -0.5) for bi in range(nbat)]
    for bi in range(nbat):
        xq_sc[bi, 0:lo, :] = xq_sc[bi, L:L + lo, :]
        xk_sc[bi, 0:lo, :] = xk_sc[bi, L:L + lo, :]

    row = lax.broadcasted_iota(jnp.int32, (L, L), 0)
    col = lax.broadcasted_iota(jnp.int32, (L, L), 1)
    causal = row >= col
    heads = range(nbat * M_HEADS)
    bat = [ch // M_HEADS for ch in heads]
    hd = [ch % M_HEADS for ch in heads]
    hsl = [slice(hd[ch] * M_HEAD_DIM, (hd[ch] + 1) * M_HEAD_DIM) for ch in heads]
    ct = [ct_sc[h] for h in heads]
    n_row = [n_sc[h, 0:1, :] for h in heads]
    m_prev = [m_sc[h, 0:1, 0:1] for h in heads]
    q = [qc[bat[h]][:, hsl[h]] for h in heads]
    k = [kc[bat[h]][:, hsl[h]] for h in heads]
    qb = [q[h].astype(BF16) for h in heads]
    kb = [k[h].astype(BF16) for h in heads]
    vb = [v_ref[bat[h], :, hsl[h]].astype(BF16) for h in heads]
    qk = [lax.dot_general(qb[h], kb[h], _CONTRACT_LAST, preferred_element_type=F32) for h in heads]
    qc_state = [jnp.dot(qb[h], ct[h].astype(BF16), preferred_element_type=F32) for h in heads]

    def gate_col(h, j):
        return gc_ref[bat[h], :, j:j + 1] + bc_ref[:, j:j + 1]

    def gate_row(h, j):
        return gr_ref[bat[h], j:j + 1, :] + br_ref[j:j + 1, :]

    i_col = [gate_col(h, hd[h]) for h in heads]
    lf_col = [_log_sigmoid(gate_col(h, M_HEADS + hd[h])) for h in heads]
    i_row = [gate_row(h, hd[h]) for h in heads]
    lf_row = [_log_sigmoid(gate_row(h, M_HEADS + hd[h])) for h in heads]
    b_col = [jnp.sum(jnp.where(causal, lf_row[h], 0.0), axis=1, keepdims=True) for h in heads]
    b_row = [jnp.sum(jnp.where(row <= col, lf_col[h], 0.0), axis=0, keepdims=True) for h in heads]
    log_d = [jnp.where(causal, b_col[h] - b_row[h] + i_row[h], -jnp.inf) for h in heads]
    log_inter = [b_col[h] + m_prev[h] for h in heads]
    m_t = [jnp.maximum(jnp.max(log_d[h], axis=1, keepdims=True), log_inter[h]) for h in heads]
    s = [qk[h] * jnp.exp(log_d[h] - m_t[h]) for h in heads]
    inter = [jnp.exp(log_inter[h] - m_t[h]) for h in heads]
    sv = [jnp.dot(s[h].astype(BF16), vb[h], preferred_element_type=F32) for h in heads]

    m_new = [m_t[h][L - 1:L, :] for h in heads]
    b_last = [b_col[h][L - 1:L, :] for h in heads]
    w_col = [jnp.exp(b_last[h] - b_col[h] + i_col[h] - m_new[h]) for h in heads]
    decay = [jnp.exp(b_last[h] + m_prev[h] - m_new[h]) for h in heads]
    kw = [k[h] * w_col[h] for h in heads]
    kv = [lax.dot_general(kw[h].astype(BF16), vb[h], (((0,), (0,)), ((), ())), preferred_element_type=F32)
          for h in heads]

    num = [sv[h] + inter[h] * qc_state[h] for h in heads]
    den = [jnp.sum(s[h], axis=1, keepdims=True) + inter[h] * jnp.sum(q[h] * n_row[h], axis=1, keepdims=True)
           for h in heads]
    hh = [num[h] / jnp.maximum(jnp.abs(den[h]), jnp.exp(-m_t[h])) for h in heads]
    z = [jax.nn.sigmoid(og_ref[bat[h], :, hsl[h]]) * hh[h] for h in heads]
    y = [(_normalize(z[h]) * ng_ref[:, hsl[h]]).astype(BF16) for h in heads]
    ct_new = [decay[h] * ct[h] + kv[h] for h in heads]
    n_new = [decay[h] * n_row[h] + jnp.sum(kw[h], axis=0, keepdims=True) for h in heads]
    for h in heads:
        ct_sc[h] = ct_new[h]
        n_sc[h] = jnp.broadcast_to(n_new[h], (SUBLANES, M_HEAD_DIM))
        m_sc[h] = jnp.broadcast_to(m_new[h], (SUBLANES, LANES))
        y_ref[bat[h], :, hsl[h]] = y[h]


def _mlstm(p, gcol, grow, bcol, brow, conv_w, conv_b, norm_g):
    nc = SEQ // M_CHUNK
    mb = M_BATCH
    p3 = p.reshape(BATCH, SEQ, P_WIDTH)
    slab = lambda blk: pl.BlockSpec((mb, M_CHUNK, M_W), lambda b, c: (b, c, blk))
    y = pl.pallas_call(
        _mlstm_kernel,
        out_shape=jax.ShapeDtypeStruct((BATCH, SEQ, M_W), BF16),
        grid=(BATCH // mb, nc),
        in_specs=[
            slab(P_QM_BLK), slab(P_KM_BLK), slab(P_VM_BLK), slab(P_OM_BLK),
            pl.BlockSpec((mb, M_CHUNK, 2 * M_HEADS), lambda b, c: (b, c, 0)),
            pl.BlockSpec((mb, 2 * M_HEADS, M_CHUNK), lambda b, c: (b, 0, c)),
            pl.BlockSpec((1, 2 * M_HEADS), lambda b, c: (0, 0)),
            pl.BlockSpec((2 * M_HEADS, 1), lambda b, c: (0, 0)),
            pl.BlockSpec((CONV_K, 2 * M_W), lambda b, c: (0, 0)),
            pl.BlockSpec((1, 2 * M_W), lambda b, c: (0, 0)),
            pl.BlockSpec((1, M_W), lambda b, c: (0, 0)),
        ],
        out_specs=pl.BlockSpec((mb, M_CHUNK, M_W), lambda b, c: (b, c, 0)),
        scratch_shapes=[
            pltpu.VMEM((mb, M_CHUNK + CONV_HALO, M_W), F32),
            pltpu.VMEM((mb, M_CHUNK + CONV_HALO, M_W), F32),
            pltpu.VMEM((mb * M_HEADS, M_HEAD_DIM, M_HEAD_DIM), F32),
            pltpu.VMEM((mb * M_HEADS, SUBLANES, M_HEAD_DIM), F32),
            pltpu.VMEM((mb * M_HEADS, SUBLANES, LANES), F32),
        ],
        compiler_params=_cparams(("arbitrary", "arbitrary")),
        name="mlstm",
    )(p3, p3, p3, p3, gcol, grow, bcol, brow, conv_w, conv_b, norm_g)
    return y.reshape(TOKENS, M_W)


RT_E1, RT_E2, RT_W1, RT_W2 = 0, 1, 2, 3


def _route(logits):
    lane = lax.broadcasted_iota(jnp.int32, logits.shape, 1).astype(F32)
    big = float(LANES)
    is_g = lane < N_EXPERT_GROUPS
    gl = jnp.where(is_g, logits, -jnp.inf)
    gexp = jnp.exp(gl - jnp.max(gl, axis=1, keepdims=True))
    gprob = gexp / jnp.sum(gexp, axis=1, keepdims=True)
    g_w = jnp.max(gprob, axis=1, keepdims=True)
    g_top = jnp.min(jnp.where(is_g & (gprob == g_w), lane, big), axis=1, keepdims=True)
    lo = N_EXPERT_GROUPS + EXPERTS_PER_GROUP * g_top
    in_grp = (lane >= lo) & (lane < lo + EXPERTS_PER_GROUP)
    el = jnp.where(in_grp, logits, -jnp.inf)
    eexp = jnp.exp(el - jnp.max(el, axis=1, keepdims=True))
    eprob = eexp / jnp.sum(eexp, axis=1, keepdims=True)
    v1 = jnp.max(eprob, axis=1, keepdims=True)
    i1 = jnp.min(jnp.where(in_grp & (eprob == v1), lane, big), axis=1, keepdims=True)
    rest = jnp.where(in_grp & (lane != i1), eprob, -1.0)
    v2 = jnp.max(rest, axis=1, keepdims=True)
    i2 = jnp.min(jnp.where(rest == v2, lane, big), axis=1, keepdims=True)
    tot = v1 + v2
    w1 = g_w * (v1 / tot)
    w2 = g_w * (v2 / tot)
    e1 = i1 - N_EXPERT_GROUPS
    e2 = i2 - N_EXPERT_GROUPS
    rec = jnp.where(lane == RT_E1, e1, jnp.where(lane == RT_E2, e2, jnp.where(lane == RT_W1, w1, w2)))
    return jnp.where(lane <= RT_W2, rec, 0.0)


def _merge_kernel(ya_ref, ym_ref, g_ref, x_ref, mod_ref,
                  wpa_ref, wpm_ref, wout_ref, lng_ref, lnb_ref, wr_ref, br_ref,
                  x1_ref, u2_ref, rt_ref):
    pa = jnp.dot(ya_ref[...], wpa_ref[...], preferred_element_type=F32)
    pm = jnp.dot(ym_ref[...], wpm_ref[...], preferred_element_type=F32)
    merged = g_ref[:, :D_MODEL].astype(F32) * pa + g_ref[:, D_MODEL:].astype(F32) * pm
    mix = jnp.dot(merged.astype(BF16), wout_ref[...], preferred_element_type=F32)
    z = DEEPNORM_ALPHA * x_ref[...] + mod_ref[2:3, :] * mix
    x1 = _normalize(z) * lng_ref[...] + lnb_ref[...]
    x1_ref[...] = x1
    u2 = _normalize(x1) * (1.0 + mod_ref[4:5, :]) + mod_ref[3:4, :]
    u2_ref[...] = u2
    logits = jnp.dot(u2.astype(BF16), wr_ref[...], preferred_element_type=F32) + br_ref[...]
    rt_ref[...] = _route(logits)


def _merge(ya, ym, g, x2, mod, wpa, wpm, wout, lng, lnb, wr, br):
    tm = MERGE_TM
    tiles_per_batch = SEQ // tm
    rowblk = lambda w: pl.BlockSpec((tm, w), lambda m: (m, 0))
    const = lambda shape: pl.BlockSpec(shape, lambda m: (0,) * len(shape), pipeline_mode=pl.Buffered(1))
    return pl.pallas_call(
        _merge_kernel,
        out_shape=(
            jax.ShapeDtypeStruct((TOKENS, D_MODEL), F32),
            jax.ShapeDtypeStruct((TOKENS, D_MODEL), F32),
            jax.ShapeDtypeStruct((TOKENS, LANES), F32),
        ),
        grid=(TOKENS // tm,),
        in_specs=[
            rowblk(A_GROUP_W), rowblk(M_W), rowblk(2 * D_MODEL), rowblk(D_MODEL),
            pl.BlockSpec((None, 6, D_MODEL), lambda m: (m // tiles_per_batch, 0, 0)),
            const((A_GROUP_W, D_MODEL)), const((M_W, D_MODEL)), const((D_MODEL, D_MODEL)),
            const((1, D_MODEL)), const((1, D_MODEL)),
            const((D_MODEL, LANES)), const((1, LANES)),
        ],
        out_specs=(rowblk(D_MODEL), rowblk(D_MODEL), rowblk(LANES)),
        compiler_params=_cparams(("arbitrary",)),
        name="merge_ln1_route",
    )(ya, ym, g, x2, mod, wpa, wpm, wout, lng, lnb, wr, br)


CAST_ROWS = 128


def _issue_rows(src_ref, idx_ref, base, buf, slot, sem, nrows):
    def body(blk, carry):
        for j in range(SUBLANES):
            row = idx_ref[base + blk * SUBLANES + j]
            src = src_ref.at[lax.shift_right_logical(row, 3), pl.ds(row & (SUBLANES - 1), 1)]
            pltpu.make_async_copy(src, buf.at[slot, blk, pl.ds(j, 1)], sem.at[slot]).start(priority=j % 2)
        return carry

    lax.fori_loop(0, nrows // SUBLANES, body, 0)


def _wait_rows(src_ref, buf, slot, sem, nrows):
    groups = nrows // SUBLANES
    pltpu.make_async_copy(src_ref.at[pl.ds(0, groups)], buf.at[slot, pl.ds(0, groups)], sem.at[slot]).wait()


class _TileTable:
    def __init__(self, meta_ref):
        self.ref = meta_ref

    def expert(self, t):
        return self.ref[META_TE * LANES + t]

    def next_expert(self, t):
        return self.ref[META_TN * LANES + t]

    def valid_rows(self, t):
        return self.ref[META_TV * LANES + t]

    def tiles_used(self):
        return self.ref[META_NU * LANES]


def _expert_changed(te_ref, i):
    return (i == 0) | (te_ref.expert(i) != te_ref.expert(jnp.maximum(i - 1, 0)))


def _moe_kernel(meta_ref, tok_ref, u_ref, wg_ref, wu_ref, wd_ref, o_ref,
                stg, stu, std, wgb, wub, wdb, xbuf, wsem, xsem):
    i = pl.program_id(0)
    te_ref = _TileTable(meta_ref)
    nu = te_ref.tiles_used()
    tm = o_ref.shape[0]
    half = tm // 2

    def weight_copies(e):
        return (pltpu.make_async_copy(wg_ref.at[e], stg, wsem.at[0]),
                pltpu.make_async_copy(wu_ref.at[e], stu, wsem.at[1]),
                pltpu.make_async_copy(wd_ref.at[e], std, wsem.at[2]))

    def for_tile_rows(t, fn):
        @pl.when(te_ref.valid_rows(t) <= half)
        def _():
            fn(half)

        @pl.when(te_ref.valid_rows(t) > half)
        def _():
            fn(tm)

    def issue_tile(t):
        for_tile_rows(t, lambda nrows: _issue_rows(u_ref, tok_ref, t * tm, xbuf, t % MOE_ROW_SLOTS, xsem, nrows))

    @pl.when(i == 0)
    def _():
        issue_tile(0)
        for t in range(1, MOE_ROW_AHEAD):
            @pl.when(nu > t)
            def _(t=t):
                issue_tile(t)

        for cp in weight_copies(te_ref.expert(0)):
            cp.start(priority=1)

    @pl.when(i < nu)
    def _():
        slot = i % MOE_ROW_SLOTS

        @pl.when(i + MOE_ROW_AHEAD < nu)
        def _():
            issue_tile(i + MOE_ROW_AHEAD)

        changed = _expert_changed(te_ref, i)

        @pl.when(changed)
        def _():
            for cp in weight_copies(te_ref.expert(i)):
                cp.wait()
            def cast_rows(ci, carry):
                r = pl.multiple_of(ci * CAST_ROWS, CAST_ROWS)
                wgb[pl.ds(r, CAST_ROWS), :] = stg[pl.ds(r, CAST_ROWS), :].astype(BF16)
                return carry

            lax.fori_loop(0, D_MODEL // CAST_ROWS, cast_rows, 0)

        def convert_inline(src, dst):
            for r0 in range(0, src.shape[0], CAST_ROWS):
                dst[r0:r0 + CAST_ROWS, :] = src[r0:r0 + CAST_ROWS, :].astype(BF16)

        def compute(nrows, new_expert):
            _wait_rows(u_ref, xbuf, slot, xsem, nrows)
            x = xbuf[slot, 0:nrows // SUBLANES].reshape(nrows, D_MODEL).astype(BF16)
            a = jnp.dot(x, wgb[...], preferred_element_type=F32)
            if new_expert:
                convert_inline(stu, wub)
            b = jnp.dot(x, wub[...], preferred_element_type=F32)
            if new_expert:
                convert_inline(std, wdb)

                @pl.when(te_ref.next_expert(i) >= 0)
                def _():
                    for cp in weight_copies(te_ref.next_expert(i)):
                        cp.start(priority=1)
            h = (_silu(a) * b).astype(BF16)
            o_ref[0:nrows, :] = jnp.dot(h, wdb[...], preferred_element_type=F32)
            if nrows < tm:
                o_ref[nrows:tm, :] = jnp.zeros((tm - nrows, D_MODEL), F32)

        @pl.when(changed)
        def _():
            for_tile_rows(i, functools.partial(compute, new_expert=True))

        @pl.when(jnp.logical_not(changed))
        def _():
            for_tile_rows(i, functools.partial(compute, new_expert=False))

    @pl.when(i >= nu)
    def _():
        o_ref[...] = jnp.zeros_like(o_ref)


def _moe(meta, row_token, u2, w_eg, w_eu, w_ed):
    tm = MOE_TM
    return pl.pallas_call(
        _moe_kernel,
        out_shape=jax.ShapeDtypeStruct((MOE_ROWS, D_MODEL), F32),
        grid_spec=pltpu.PrefetchScalarGridSpec(
            num_scalar_prefetch=2,
            grid=(MOE_TILES,),
            in_specs=[pl.BlockSpec(memory_space=pl.ANY)] * 4,
            out_specs=pl.BlockSpec((tm, D_MODEL), lambda i, meta, tok: (i, 0)),
            scratch_shapes=[
                pltpu.VMEM((D_MODEL, D_FF_EXPERT), F32),
                pltpu.VMEM((D_MODEL, D_FF_EXPERT), F32),
                pltpu.VMEM((D_FF_EXPERT, D_MODEL), F32),
                pltpu.VMEM((D_MODEL, D_FF_EXPERT), BF16),
                pltpu.VMEM((D_MODEL, D_FF_EXPERT), BF16),
                pltpu.VMEM((D_FF_EXPERT, D_MODEL), BF16),
                pltpu.VMEM((MOE_ROW_SLOTS, tm // SUBLANES, SUBLANES, D_MODEL), F32),
                pltpu.SemaphoreType.DMA((3,)),
                pltpu.SemaphoreType.DMA((MOE_ROW_SLOTS,)),
            ],
        ),
        compiler_params=_cparams(("arbitrary",)),
        name="moe_experts",
    )(meta, row_token, u2.reshape(TOKENS // SUBLANES, SUBLANES, D_MODEL), w_eg, w_eu, w_ed)


def _final_kernel(pos_ref, o_ref, x1_ref, rt_ref, mod_ref, lng_ref, lnb_ref, y_ref, buf, sem):
    i = pl.program_id(0)
    tb = x1_ref.shape[0]
    slot = i % 2

    @pl.when(i == 0)
    def _():
        _issue_rows(o_ref, pos_ref, 0, buf, 0, sem, 2 * tb)

    @pl.when(i + 1 < pl.num_programs(0))
    def _():
        _issue_rows(o_ref, pos_ref, (i + 1) * (2 * tb), buf, 1 - slot, sem, 2 * tb)

    _wait_rows(o_ref, buf, slot, sem, 2 * tb)
    rows = buf[slot].reshape(2 * tb, D_MODEL)
    ffn = rt_ref[:, RT_W1:RT_W1 + 1] * rows[0:tb, :] + rt_ref[:, RT_W2:RT_W2 + 1] * rows[tb:2 * tb, :]
    z = DEEPNORM_ALPHA * x1_ref[...] + mod_ref[5:6, :] * ffn
    y_ref[...] = _normalize(z) * lng_ref[...] + lnb_ref[...]


def _final(pos_tiles, moe_out, x1, rt, mod, lng, lnb):
    tb = FINAL_TB
    tiles_per_batch = SEQ // tb
    return pl.pallas_call(
        _final_kernel,
        out_shape=jax.ShapeDtypeStruct((TOKENS, D_MODEL), F32),
        grid_spec=pltpu.PrefetchScalarGridSpec(
            num_scalar_prefetch=1,
            grid=(TOKENS // tb,),
            in_specs=[
                pl.BlockSpec(memory_space=pl.ANY),
                pl.BlockSpec((tb, D_MODEL), lambda i, pos: (i, 0)),
                pl.BlockSpec((tb, LANES), lambda i, pos: (i, 0)),
                pl.BlockSpec((None, 6, D_MODEL), lambda i, pos: (i // tiles_per_batch, 0, 0)),
                pl.BlockSpec((1, D_MODEL), lambda i, pos: (0, 0)),
                pl.BlockSpec((1, D_MODEL), lambda i, pos: (0, 0)),
            ],
            out_specs=pl.BlockSpec((tb, D_MODEL), lambda i, pos: (i, 0)),
            scratch_shapes=[pltpu.VMEM((2, 2 * tb // SUBLANES, SUBLANES, D_MODEL), F32),
                            pltpu.SemaphoreType.DMA((2,))],
        ),
        compiler_params=_cparams(("arbitrary",)),
        name="combine_ln2",
    )(pos_tiles, moe_out.reshape(MOE_ROWS // SUBLANES, SUBLANES, D_MODEL), x1, rt, mod, lng, lnb)


ROUTE_BLK = 256
ROUTE_LOCKSTEP = 4
META_TE, META_TN, META_TV, META_NU = 0, 1, 2, 3


def _lane_cumsum(x, lane):
    s = 1
    while s < LANES:
        x = x + jnp.where(lane >= s, pltpu.roll(x, s, 1), 0.0)
        s *= 2
    return x


def _route_tables_kernel(rt_ref, pos_ref, meta_ref, rank_sc):
    tm = float(MOE_TM)
    nblk = TOKENS // ROUTE_BLK
    lane_i = lax.broadcasted_iota(jnp.int32, (ROUTE_BLK, LANES), 1)
    lane_f = lane_i.astype(F32)
    earlier = (lax.broadcasted_iota(jnp.int32, (ROUTE_BLK, ROUTE_BLK), 0)
               > lax.broadcasted_iota(jnp.int32, (ROUTE_BLK, ROUTE_BLK), 1)).astype(BF16)

    def onehots(b):
        blk = rt_ref[pl.ds(pl.multiple_of(b * ROUTE_BLK, ROUTE_BLK), ROUTE_BLK), :]
        return blk[:, RT_E1:RT_E1 + 1] == lane_f, blk[:, RT_E2:RT_E2 + 1] == lane_f

    def count_pass(g, carry):
        c1, c2 = carry
        bs = [g * ROUTE_LOCKSTEP + i for i in range(ROUTE_LOCKSTEP)]
        oh = [onehots(b) for b in bs]
        f = [(jnp.where(o1, 1.0, 0.0), jnp.where(o2, 1.0, 0.0)) for o1, o2 in oh]
        p = [(jnp.dot(earlier, f1.astype(BF16), preferred_element_type=F32),
              jnp.dot(earlier, f2.astype(BF16), preferred_element_type=F32)) for f1, f2 in f]
        tot = [(jnp.sum(f1, axis=0, keepdims=True), jnp.sum(f2, axis=0, keepdims=True)) for f1, f2 in f]
        for i, b in enumerate(bs):
            r1 = jnp.sum(jnp.where(oh[i][0], p[i][0] + c1, 0.0), axis=1, keepdims=True)
            r2 = jnp.sum(jnp.where(oh[i][1], p[i][1] + c2, 0.0), axis=1, keepdims=True)
            rank_sc[pl.ds(pl.multiple_of(b * ROUTE_BLK, ROUTE_BLK), ROUTE_BLK), :] = jnp.where(
                lane_i == 0, r1, jnp.where(lane_i == 1, r2, 0.0))
            c1 = c1 + tot[i][0]
            c2 = c2 + tot[i][1]
        return c1, c2

    zero = jnp.zeros((1, LANES), F32)
    c1, c2 = lax.fori_loop(0, nblk // ROUTE_LOCKSTEP, count_pass, (zero, zero))

    lane8 = lax.broadcasted_iota(jnp.int32, (SUBLANES, LANES), 1)
    counts = jnp.broadcast_to(c1 + c2, (SUBLANES, LANES))
    padded = jnp.floor((counts + (tm - 1.0)) * (1.0 / tm)) * tm
    pend = _lane_cumsum(padded, lane8)
    pstart = pend - padded
    start1 = pstart[0:1, :]
    start2 = start1 + c1

    def place_pass(g, carry):
        bs = [g * ROUTE_LOCKSTEP + i for i in range(ROUTE_LOCKSTEP)]
        oh = [onehots(b) for b in bs]
        rows = [pl.ds(pl.multiple_of(b * ROUTE_BLK, ROUTE_BLK), ROUTE_BLK) for b in bs]
        rk = [rank_sc[r, :] for r in rows]
        d1 = [jnp.sum(jnp.where(oh[i][0], start1, 0.0), axis=1, keepdims=True) + rk[i][:, 0:1]
              for i in range(ROUTE_LOCKSTEP)]
        d2 = [jnp.sum(jnp.where(oh[i][1], start2, 0.0), axis=1, keepdims=True) + rk[i][:, 1:2]
              for i in range(ROUTE_LOCKSTEP)]
        for i in range(ROUTE_LOCKSTEP):
            pos_ref[rows[i], :] = jnp.where(lane_i == 0, d1[i], jnp.where(lane_i == 1, d2[i], 0.0)).astype(jnp.int32)
        return carry

    lax.fori_loop(0, nblk // ROUTE_LOCKSTEP, place_pass, 0)

    tile = lax.broadcasted_iota(jnp.int32, (LANES, LANES), 0).astype(F32)
    lane = lax.broadcasted_iota(jnp.int32, (LANES, LANES), 1)
    expert_lane = lane < N_EXPERTS
    pend_b = jnp.broadcast_to(pend[0:1, :], (LANES, LANES))
    pstart_b = jnp.broadcast_to(pstart[0:1, :], (LANES, LANES))
    counts_b = jnp.broadcast_to(counts[0:1, :], (LANES, LANES))
    n_used = jnp.sum(jnp.where(lane == N_EXPERTS - 1, pend_b, 0.0), axis=1, keepdims=True) * (1.0 / tm)
    te = jnp.sum(jnp.where(expert_lane & (pend_b <= tile * tm), 1.0, 0.0), axis=1, keepdims=True)
    te = jnp.minimum(te, float(N_EXPERTS - 1))
    te_last = jnp.sum(jnp.where(tile[:, 0:1] == n_used - 1.0, te, 0.0), axis=0, keepdims=True)
    te = jnp.where(tile[:, 0:1] < n_used, te, te_last)
    of_tile = lane.astype(F32) == te
    pend_te = jnp.sum(jnp.where(of_tile, pend_b, 0.0), axis=1, keepdims=True)
    last_row = jnp.sum(jnp.where(of_tile, pstart_b + counts_b, 0.0), axis=1, keepdims=True)
    tile_rows = jnp.clip(last_row - tile[:, 0:1] * tm, 0.0, tm)
    next_run = pend_te * (1.0 / tm)
    te_by_lane = jnp.broadcast_to(te, (LANES, LANES)).T
    te_at_next = jnp.sum(jnp.where(lane.astype(F32) == next_run, te_by_lane, 0.0), axis=1, keepdims=True)
    te_next = jnp.where(next_run < n_used, te_at_next, -1.0)
    cols = jnp.where(lane == META_TE, te, jnp.where(lane == META_TN, te_next,
                     jnp.where(lane == META_TV, tile_rows, jnp.where(lane == META_NU, n_used, 0.0))))
    meta_ref[...] = cols.T[0:SUBLANES, :].astype(jnp.int32)


def _routing_tables(rt):
    pos, meta = pl.pallas_call(
        _route_tables_kernel,
        out_shape=(
            jax.ShapeDtypeStruct((TOKENS, LANES), jnp.int32),
            jax.ShapeDtypeStruct((SUBLANES, LANES), jnp.int32),
        ),
        scratch_shapes=[pltpu.VMEM((TOKENS, LANES), F32)],
        compiler_params=_cparams(None),
        name="route_tables",
    )(rt)
    pos = pos[:, 0:2]
    tok = jnp.arange(2 * TOKENS, dtype=jnp.int32) // 2
    row_token = jnp.zeros((MOE_ROWS,), jnp.int32).at[pos.reshape(-1)].set(
        tok, unique_indices=True, mode="promise_in_bounds")
    return row_token, meta.reshape(-1), pos


def _rope_tables():
    inv = ROPE_THETA ** (-jnp.arange(0, HEAD_DIM_A, 2, dtype=F32) / HEAD_DIM_A)
    ang = jnp.arange(SEQ, dtype=F32)[:, None] * inv[None, :]
    cos = jnp.cos(ang)
    sin = jnp.sin(ang)
    return jnp.concatenate([cos, cos], axis=-1), jnp.concatenate([-sin, sin], axis=-1)


def kernel(x, c, w_ada, b_ada, w_in, b_mgate, conv_w, conv_b, m_norm_g, w_proj_a, w_proj_m, w_gate, b_gate,
           w_out, ln1_g, ln1_b, w_rg, b_rg, w_re, b_re, w_eg, w_eu, w_ed, ln2_g, ln2_b):
    assert x.shape == (BATCH, SEQ, D_MODEL) and w_ada.shape[0] == 1
    l = 0
    x2 = x.reshape(TOKENS, D_MODEL)

    c_pad = jnp.zeros((ADA_ROWS, D_MODEL), F32).at[:BATCH].set(c)
    mod = _ada(c_pad, w_ada[l], b_ada[l][None, :])[:BATCH].reshape(BATCH, 6, D_MODEL)

    w_in_t = jnp.swapaxes(w_in[l], 0, 1)
    w_if_t = jnp.zeros((LANES, D_MODEL), F32).at[:2 * M_HEADS].set(w_in_t[N_IN_MAIN:])
    p, gates, u = _inproj(x2, mod, w_in_t, w_if_t)
    g = _gateproj(u, w_gate[l], b_gate[l][None, :])

    cos_t, sin_t = _rope_tables()
    ya = _attention(p, cos_t, sin_t)

    gcol = gates[:, :2 * M_HEADS].reshape(BATCH, SEQ, 2 * M_HEADS)
    grow = jnp.transpose(gcol, (0, 2, 1))
    ym = _mlstm(p, gcol, grow, b_mgate[l][None, :], b_mgate[l][:, None], conv_w[l], conv_b[l][None, :],
                m_norm_g[l][None, :])

    wr = (jnp.zeros((D_MODEL, LANES), F32)
          .at[:, :N_EXPERT_GROUPS].set(w_rg[l])
          .at[:, N_EXPERT_GROUPS:N_EXPERT_GROUPS + N_EXPERTS].set(w_re[l])).astype(BF16)
    br = (jnp.zeros((1, LANES), F32)
          .at[0, :N_EXPERT_GROUPS].set(b_rg[l])
          .at[0, N_EXPERT_GROUPS:N_EXPERT_GROUPS + N_EXPERTS].set(b_re[l]))
    x1, u2, rt = _merge(ya, ym, g, x2, mod,
                        w_proj_a[l].astype(BF16), w_proj_m[l].astype(BF16), w_out[l].astype(BF16),
                        ln1_g[l][None, :], ln1_b[l][None, :], wr, br)

    row_token, meta, pos = _routing_tables(rt)

    mo = _moe(meta, row_token, u2,
              w_eg[l].reshape(N_EXPERTS, D_MODEL, D_FF_EXPERT), w_eu[l].reshape(N_EXPERTS, D_MODEL, D_FF_EXPERT),
              w_ed[l].reshape(N_EXPERTS, D_FF_EXPERT, D_MODEL))

    nt = TOKENS // FINAL_TB
    pos_tiles = jnp.transpose(pos.reshape(nt, FINAL_TB, 2), (0, 2, 1)).reshape(-1)
    y = _final(pos_tiles, mo, x1, rt, mod, ln2_g[l][None, :], ln2_b[l][None, :])
    return y.reshape(BATCH, SEQ, D_MODEL)
```

```python
import functools

import jax
import jax.numpy as jnp
from jax import lax
from jax.experimental import pallas as pl
from jax.experimental.pallas import tpu as pltpu

F32 = jnp.float32
BF16 = jnp.bfloat16

D_MODEL = 2048
BATCH = 4
SEQ = 2048
TOKENS = BATCH * SEQ
DIL_CONFIGS = ((128, 1), (512, 4), (2048, 16))
N_DIL_GROUPS = 3
HEADS_PER_GROUP = 4
HEAD_DIM_A = 128
ATT_BLOCK = 128
ROPE_THETA = 10000.0
A_GROUP_W = HEADS_PER_GROUP * HEAD_DIM_A
A_QKV_W = N_DIL_GROUPS * A_GROUP_W
M_HEADS = 4
M_HEAD_DIM = 256
M_W = M_HEADS * M_HEAD_DIM
M_CHUNK = 128
CONV_K = 4
N_IN_MAIN = 3 * A_QKV_W + 4 * M_W
N_EXPERT_GROUPS = 4
EXPERTS_PER_GROUP = 8
N_EXPERTS = N_EXPERT_GROUPS * EXPERTS_PER_GROUP
D_FF_EXPERT = 1024
DEEPNORM_ALPHA = 2.0 ** 0.25
LN_EPS = 1e-5

LANES = 128
SUBLANES = 8
VMEM_LIMIT_BYTES = 56 * 1024 * 1024

PROJ_TN = 256
GATE_TN = 1024
PROJ_TM = 1024
IN_TM = 2048
P_TILES_IN = N_IN_MAIN // PROJ_TN
P_ATT_TILES = 3 * A_QKV_W // PROJ_TN
P_M_TILES = P_TILES_IN - P_ATT_TILES
P_WIDTH = N_IN_MAIN
P_ATT_OFF = P_M_TILES * PROJ_TN
P_QM_BLK, P_KM_BLK, P_VM_BLK, P_OM_BLK = 0, 1, 2, 3
MERGE_TM = 256
MOE_TM = 256
MOE_ROWS = 2 * TOKENS + N_EXPERTS * MOE_TM
MOE_TILES = MOE_ROWS // MOE_TM
MOE_ROW_SLOTS = 3
FINAL_TB = 512


def _cparams(sem, vmem=VMEM_LIMIT_BYTES):
    return pltpu.CompilerParams(dimension_semantics=sem, vmem_limit_bytes=vmem)


def _normalize(x):
    mu = jnp.mean(x, axis=-1, keepdims=True)
    xc = x - mu
    var = jnp.mean(xc * xc, axis=-1, keepdims=True)
    return xc * lax.rsqrt(var + LN_EPS)


def _silu(x):
    return x * jax.nn.sigmoid(x)


def _log_sigmoid(x):
    return jnp.minimum(x, 0.0) - jnp.log(1.0 + jnp.exp(-jnp.abs(x)))


_CONTRACT_LAST = (((1,), (1,)), ((), ()))
LOG2E = 1.4426950408889634


ADA_TN = 1024
ADA_ROWS = 16


def _ada_kernel(c_ref, w_ref, b_ref, o_ref):
    sc = _silu(c_ref[...]).astype(BF16)
    o_ref[...] = jnp.dot(sc, w_ref[...].astype(BF16), preferred_element_type=F32) + b_ref[...]


def _ada(c_pad, w_ada, b_ada):
    n = w_ada.shape[1]
    return pl.pallas_call(
        _ada_kernel,
        out_shape=jax.ShapeDtypeStruct((ADA_ROWS, n), F32),
        grid=(n // ADA_TN,),
        in_specs=[
            pl.BlockSpec((ADA_ROWS, D_MODEL), lambda j: (0, 0)),
            pl.BlockSpec((D_MODEL, ADA_TN), lambda j: (0, j)),
            pl.BlockSpec((1, ADA_TN), lambda j: (0, j)),
        ],
        out_specs=pl.BlockSpec((ADA_ROWS, ADA_TN), lambda j: (0, j)),
        compiler_params=_cparams(("arbitrary",)),
        name="ada_mod",
    )(c_pad, w_ada, b_ada)


LN_CHUNK = 256


def _inproj_kernel(x_ref, mod_ref, w_ref, wif_ref, p_ref, g_ref, u_ref, xbuf, xsem):
    m = pl.program_id(0)
    n = pl.program_id(1)

    def x_copy(tile):
        return pltpu.make_async_copy(x_ref.at[pl.ds(pl.multiple_of(tile * IN_TM, IN_TM), IN_TM)], xbuf, xsem)

    @pl.when((m == 0) & (n == 0))
    def _():
        x_copy(0).start()

    @pl.when((n == 1) & (m + 1 < pl.num_programs(0)))
    def _():
        x_copy(m + 1).start()

    @pl.when(n == 0)
    def _():
        x_copy(m).wait()
        shift = mod_ref[0:1, :]
        scale = 1.0 + mod_ref[1:2, :]
        wif = wif_ref[...].astype(BF16)
        w = w_ref[...].astype(BF16)
        for ci in range(IN_TM // LN_CHUNK):
            rows = slice(ci * LN_CHUNK, (ci + 1) * LN_CHUNK)
            u = (_normalize(xbuf[rows, :]) * scale + shift).astype(BF16)
            u_ref[rows, :] = u
            g_ref[rows, :] = lax.dot_general(u, wif, _CONTRACT_LAST, preferred_element_type=F32)
            p_ref[rows, :] = lax.dot_general(u, w, _CONTRACT_LAST, preferred_element_type=F32)

    @pl.when(n > 0)
    def _():
        p_ref[...] = lax.dot_general(u_ref[...], w_ref[...].astype(BF16), _CONTRACT_LAST,
                                     preferred_element_type=F32)


def _inproj(x2, mod, w_in_t, w_if_t):
    tiles_per_batch = SEQ // IN_TM
    return pl.pallas_call(
        _inproj_kernel,
        out_shape=(
            jax.ShapeDtypeStruct((TOKENS, P_WIDTH), F32),
            jax.ShapeDtypeStruct((TOKENS, LANES), F32),
            jax.ShapeDtypeStruct((TOKENS, D_MODEL), BF16),
        ),
        grid=(TOKENS // IN_TM, P_TILES_IN),
        in_specs=[
            pl.BlockSpec(memory_space=pl.ANY),
            pl.BlockSpec((None, 6, D_MODEL), lambda m, n: (m // tiles_per_batch, 0, 0)),
            pl.BlockSpec((PROJ_TN, D_MODEL), lambda m, n: (n, 0)),
            pl.BlockSpec((LANES, D_MODEL), lambda m, n: (0, 0)),
        ],
        out_specs=(
            pl.BlockSpec((IN_TM, PROJ_TN),
                         lambda m, n: (m, jnp.where(n < P_ATT_TILES, n + P_M_TILES, n - P_ATT_TILES))),
            pl.BlockSpec((IN_TM, LANES), lambda m, n: (m, 0)),
            pl.BlockSpec((IN_TM, D_MODEL), lambda m, n: (m, 0)),
        ),
        scratch_shapes=[pltpu.VMEM((IN_TM, D_MODEL), F32), pltpu.SemaphoreType.DMA],
        compiler_params=_cparams(("arbitrary", "arbitrary")),
        name="in_proj",
    )(x2, mod, w_in_t, w_if_t)


def _gateproj_kernel(u_ref, w_ref, b_ref, o_ref):
    acc = jnp.dot(u_ref[...], w_ref[...].astype(BF16), preferred_element_type=F32)
    o_ref[...] = jax.nn.sigmoid(acc + b_ref[...]).astype(BF16)


def _gateproj(u, w_gate, b_gate):
    n = w_gate.shape[1]
    return pl.pallas_call(
        _gateproj_kernel,
        out_shape=jax.ShapeDtypeStruct((TOKENS, n), BF16),
        grid=(TOKENS // PROJ_TM, n // GATE_TN),
        in_specs=[
            pl.BlockSpec((PROJ_TM, D_MODEL), lambda m, j: (m, 0)),
            pl.BlockSpec((D_MODEL, GATE_TN), lambda m, j: (0, j)),
            pl.BlockSpec((1, GATE_TN), lambda m, j: (0, j)),
        ],
        out_specs=pl.BlockSpec((PROJ_TM, GATE_TN), lambda m, j: (m, j)),
        compiler_params=_cparams(("arbitrary", "arbitrary")),
        name="gate_proj",
    )(u, w_gate, b_gate)


ROPE_ROWS = 256
ATT_LOCKSTEP = 8


def _rows(start, size, stride):
    return pl.ds(start, size) if stride == 1 else pl.ds(start, size, stride=stride)


def _attn_kernel(q0, k0, v0, q1, k1, v1, q2, k2, v2, cos_ref, sin_ref, y_ref,
                 qr_sc, kr_sc, o0, o1, o2, l0, l1, l2):
    scale = HEAD_DIM_A ** -0.5
    blk = ATT_BLOCK
    qi2 = lax.broadcasted_iota(jnp.int32, (blk, 2 * blk), 0)
    kc2 = lax.broadcasted_iota(jnp.int32, (blk, 2 * blk), 1)
    mask_prev_cur = (kc2 >= qi2) & (kc2 <= qi2 + blk)
    qi1 = lax.broadcasted_iota(jnp.int32, (blk, blk), 0)
    kc1 = lax.broadcasted_iota(jnp.int32, (blk, blk), 1)
    mask_cur = kc1 <= qi1
    half = HEAD_DIM_A // 2
    groups = ((q0, k0, v0, o0, l0), (q1, k1, v1, o1, l1), (q2, k2, v2, o2, l2))
    for (window, d), (q_ref, k_ref, v_ref, o_sc, l_sc) in zip(DIL_CONFIGS, groups):
        nb = SEQ // d // blk
        for src_ref, dst_sc in ((q_ref, qr_sc), (k_ref, kr_sc)):
            for c0 in range(0, SEQ, ROPE_ROWS):
                rs = slice(c0, c0 + ROPE_ROWS)
                xr = src_ref[rs, :]
                dst_sc[rs, :] = xr * cos_ref[rs, :] + pltpu.roll(xr, half, 1) * sin_ref[rs, :]
        blocks = [(r, j) for r in range(d) for j in range(nb)]
        for b0 in range(0, len(blocks), ATT_LOCKSTEP):
            batch = blocks[b0:b0 + ATT_LOCKSTEP]
            cur, keys, mask = [], [], []
            for r, j in batch:
                start = r + j * blk * d
                cur.append(_rows(start, blk, d))
                keys.append(cur[-1] if j == 0 else _rows(start - blk * d, 2 * blk, d))
                mask.append(mask_cur if j == 0 else mask_prev_cur)
            nbb = range(len(batch))
            qb = [qr_sc[cur[i], :].astype(BF16) for i in nbb]
            kw = [kr_sc[keys[i], :].astype(BF16) for i in nbb]
            s = [lax.dot_general(qb[i], kw[i], _CONTRACT_LAST, preferred_element_type=F32) for i in nbb]
            s = [jnp.where(mask[i], s[i], -jnp.inf) for i in nbb]
            mx = [jnp.max(s[i], axis=-1, keepdims=True) for i in nbb]
            p = [jnp.exp2((s[i] - mx[i]) * (scale * LOG2E)).astype(BF16) for i in nbb]
            vb = [v_ref[keys[i], :].astype(BF16) for i in nbb]
            pv = [jnp.dot(p[i], jnp.concatenate([vb[i], jnp.ones_like(vb[i])], axis=1),
                          preferred_element_type=F32) for i in nbb]
            for i in nbb:
                den = pv[i][:, HEAD_DIM_A:]
                o_sc[cur[i], :] = pv[i][:, :HEAD_DIM_A] / den
                l_sc[cur[i], :] = mx[i] * scale + jnp.log(den)
    for c0 in range(0, SEQ, ROPE_ROWS):
        rs = slice(c0, c0 + ROPE_ROWS)
        la = l0[rs, :]
        lb = l1[rs, :]
        lc = l2[rs, :]
        mx = jnp.maximum(jnp.maximum(la, lb), lc)
        ea = jnp.exp(la - mx)
        eb = jnp.exp(lb - mx)
        ec = jnp.exp(lc - mx)
        den = ea + eb + ec
        y_ref[rs, :] = ((ea / den) * o0[rs, :] + (eb / den) * o1[rs, :] + (ec / den) * o2[rs, :]).astype(BF16)


def _attention(p, cos_t, sin_t):
    for window, d in DIL_CONFIGS:
        assert window // d == ATT_BLOCK and SEQ % (d * ATT_BLOCK) == 0
    col0 = P_ATT_OFF // HEAD_DIM_A

    def slab(part, g):
        off = col0 + (part * A_QKV_W + g * A_GROUP_W) // HEAD_DIM_A
        return pl.BlockSpec((SEQ, HEAD_DIM_A), lambda b, h: (b, off + h))

    in_specs = [slab(part, g) for g in range(N_DIL_GROUPS) for part in range(3)]
    table = pl.BlockSpec((SEQ, HEAD_DIM_A), lambda b, h: (0, 0))
    return pl.pallas_call(
        _attn_kernel,
        out_shape=jax.ShapeDtypeStruct((TOKENS, A_GROUP_W), BF16),
        grid=(BATCH, HEADS_PER_GROUP),
        in_specs=in_specs + [table, table],
        out_specs=pl.BlockSpec((SEQ, HEAD_DIM_A), lambda b, h: (b, h)),
        scratch_shapes=[pltpu.VMEM((SEQ, HEAD_DIM_A), F32)] * 8,
        compiler_params=_cparams(("arbitrary", "arbitrary")),
        name="dil_attn",
    )(*([p] * 9), cos_t, sin_t)


CONV_HALO = SUBLANES
M_BATCH = 2


def _mlstm_kernel(q_ref, k_ref, v_ref, og_ref, gc_ref, gr_ref, bc_ref, br_ref, cw_ref, cb_ref, ng_ref,
                  y_ref, xq_sc, xk_sc, ct_sc, n_sc, m_sc):
    c = pl.program_id(1)
    L = M_CHUNK
    lo = CONV_HALO

    nbat = q_ref.shape[0]

    @pl.when(c == 0)
    def _():
        xq_sc[:, 0:lo, :] = jnp.zeros((nbat, lo, M_W), F32)
        xk_sc[:, 0:lo, :] = jnp.zeros((nbat, lo, M_W), F32)
        ct_sc[...] = jnp.zeros_like(ct_sc)
        n_sc[...] = jnp.zeros_like(n_sc)
        m_sc[...] = jnp.zeros_like(m_sc)

    for bi in range(nbat):
        xq_sc[bi, lo:lo + L, :] = q_ref[bi]
        xk_sc[bi, lo:lo + L, :] = k_ref[bi]

    def conv(x_sc, bi, w, b):
        acc = x_sc[bi, lo:lo + L, :] * w[CONV_K - 1:CONV_K, :] + b
        for j in range(CONV_K - 1):
            off = lo - (CONV_K - 1) + j
            acc = acc + x_sc[bi, off:off + L, :] * w[j:j + 1, :]
        return acc

    cw = cw_ref[...]
    cb = cb_ref[...]
    qc = [_silu(conv(xq_sc, bi, cw[:, :M_W], cb[:, :M_W])) for bi in range(nbat)]
    kc = [_silu(conv(xk_sc, bi, cw[:, M_W:], cb[:, M_W:])) * (M_HEAD_DIM ** -0.5) for bi in range(nbat)]
    for bi in range(nbat):
        xq_sc[bi, 0:lo, :] = xq_sc[bi, L:L + lo, :]
        xk_sc[bi, 0:lo, :] = xk_sc[bi, L:L + lo, :]

    row = lax.broadcasted_iota(jnp.int32, (L, L), 0)
    col = lax.broadcasted_iota(jnp.int32, (L, L), 1)
    causal = row >= col
    heads = range(nbat * M_HEADS)
    bat = [ch // M_HEADS for ch in heads]
    hd = [ch % M_HEADS for ch in heads]
    hsl = [slice(hd[ch] * M_HEAD_DIM, (hd[ch] + 1) * M_HEAD_DIM) for ch in heads]
    ct = [ct_sc[h] for h in heads]
    n_row = [n_sc[h, 0:1, :] for h in heads]
    m_prev = [m_sc[h, 0:1, 0:1] for h in heads]
    q = [qc[bat[h]][:, hsl[h]] for h in heads]
    k = [kc[bat[h]][:, hsl[h]] for h in heads]
    qb = [q[h].astype(BF16) for h in heads]
    kb = [k[h].astype(BF16) for h in heads]
    vb = [v_ref[bat[h], :, hsl[h]].astype(BF16) for h in heads]
    qk = [lax.dot_general(qb[h], kb[h], _CONTRACT_LAST, preferred_element_type=F32) for h in heads]
    qc_state = [jnp.dot(qb[h], ct[h].astype(BF16), preferred_element_type=F32) for h in heads]

    def gate_col(h, j):
        return gc_ref[bat[h], :, j:j + 1] + bc_ref[:, j:j + 1]

    def gate_row(h, j):
        return gr_ref[bat[h], j:j + 1, :] + br_ref[j:j + 1, :]

    i_col = [gate_col(h, hd[h]) for h in heads]
    lf_col = [_log_sigmoid(gate_col(h, M_HEADS + hd[h])) for h in heads]
    i_row = [gate_row(h, hd[h]) for h in heads]
    lf_row = [_log_sigmoid(gate_row(h, M_HEADS + hd[h])) for h in heads]
    b_col = [jnp.sum(jnp.where(causal, lf_row[h], 0.0), axis=1, keepdims=True) for h in heads]
    b_row = [jnp.sum(jnp.where(row <= col, lf_col[h], 0.0), axis=0, keepdims=True) for h in heads]
    log_d = [jnp.where(causal, b_col[h] - b_row[h] + i_row[h], -jnp.inf) for h in heads]
    log_inter = [b_col[h] + m_prev[h] for h in heads]
    m_t = [jnp.maximum(jnp.max(log_d[h], axis=1, keepdims=True), log_inter[h]) for h in heads]
    s = [qk[h] * jnp.exp(log_d[h] - m_t[h]) for h in heads]
    inter = [jnp.exp(log_inter[h] - m_t[h]) for h in heads]
    sv = [jnp.dot(s[h].astype(BF16), vb[h], preferred_element_type=F32) for h in heads]

    m_new = [m_t[h][L - 1:L, :] for h in heads]
    b_last = [b_col[h][L - 1:L, :] for h in heads]
    w_col = [jnp.exp(b_last[h] - b_col[h] + i_col[h] - m_new[h]) for h in heads]
    decay = [jnp.exp(b_last[h] + m_prev[h] - m_new[h]) for h in heads]
    kw = [k[h] * w_col[h] for h in heads]
    kv = [lax.dot_general(kw[h].astype(BF16), vb[h], (((0,), (0,)), ((), ())), preferred_element_type=F32)
          for h in heads]

    num = [sv[h] + inter[h] * qc_state[h] for h in heads]
    den = [jnp.sum(s[h], axis=1, keepdims=True) + inter[h] * jnp.sum(q[h] * n_row[h], axis=1, keepdims=True)
           for h in heads]
    hh = [num[h] / jnp.maximum(jnp.abs(den[h]), jnp.exp(-m_t[h])) for h in heads]
    z = [jax.nn.sigmoid(og_ref[bat[h], :, hsl[h]]) * hh[h] for h in heads]
    y = [(_normalize(z[h]) * ng_ref[:, hsl[h]]).astype(BF16) for h in heads]
    ct_new = [decay[h] * ct[h] + kv[h] for h in heads]
    n_new = [decay[h] * n_row[h] + jnp.sum(kw[h], axis=0, keepdims=True) for h in heads]
    for h in heads:
        ct_sc[h] = ct_new[h]
        n_sc[h] = jnp.broadcast_to(n_new[h], (SUBLANES, M_HEAD_DIM))
        m_sc[h] = jnp.broadcast_to(m_new[h], (SUBLANES, LANES))
        y_ref[bat[h], :, hsl[h]] = y[h]


def _mlstm(p, gcol, grow, bcol, brow, conv_w, conv_b, norm_g):
    nc = SEQ // M_CHUNK
    mb = M_BATCH
    p3 = p.reshape(BATCH, SEQ, P_WIDTH)
    slab = lambda blk: pl.BlockSpec((mb, M_CHUNK, M_W), lambda b, c: (b, c, blk))
    y = pl.pallas_call(
        _mlstm_kernel,
        out_shape=jax.ShapeDtypeStruct((BATCH, SEQ, M_W), BF16),
        grid=(BATCH // mb, nc),
        in_specs=[
            slab(P_QM_BLK), slab(P_KM_BLK), slab(P_VM_BLK), slab(P_OM_BLK),
            pl.BlockSpec((mb, M_CHUNK, 2 * M_HEADS), lambda b, c: (b, c, 0)),
            pl.BlockSpec((mb, 2 * M_HEADS, M_CHUNK), lambda b, c: (b, 0, c)),
            pl.BlockSpec((1, 2 * M_HEADS), lambda b, c: (0, 0)),
            pl.BlockSpec((2 * M_HEADS, 1), lambda b, c: (0, 0)),
            pl.BlockSpec((CONV_K, 2 * M_W), lambda b, c: (0, 0)),
            pl.BlockSpec((1, 2 * M_W), lambda b, c: (0, 0)),
            pl.BlockSpec((1, M_W), lambda b, c: (0, 0)),
        ],
        out_specs=pl.BlockSpec((mb, M_CHUNK, M_W), lambda b, c: (b, c, 0)),
        scratch_shapes=[
            pltpu.VMEM((mb, M_CHUNK + CONV_HALO, M_W), F32),
            pltpu.VMEM((mb, M_CHUNK + CONV_HALO, M_W), F32),
            pltpu.VMEM((mb * M_HEADS, M_HEAD_DIM, M_HEAD_DIM), F32),
            pltpu.VMEM((mb * M_HEADS, SUBLANES, M_HEAD_DIM), F32),
            pltpu.VMEM((mb * M_HEADS, SUBLANES, LANES), F32),
        ],
        compiler_params=_cparams(("arbitrary", "arbitrary")),
        name="mlstm",
    )(p3, p3, p3, p3, gcol, grow, bcol, brow, conv_w, conv_b, norm_g)
    return y.reshape(TOKENS, M_W)


RT_E1, RT_E2, RT_W1, RT_W2 = 0, 1, 2, 3


def _route(logits):
    lane = lax.broadcasted_iota(jnp.int32, logits.shape, 1).astype(F32)
    big = float(LANES)
    is_g = lane < N_EXPERT_GROUPS
    gl = jnp.where(is_g, logits, -jnp.inf)
    gexp = jnp.exp(gl - jnp.max(gl, axis=1, keepdims=True))
    gprob = gexp / jnp.sum(gexp, axis=1, keepdims=True)
    g_w = jnp.max(gprob, axis=1, keepdims=True)
    g_top = jnp.min(jnp.where(is_g & (gprob == g_w), lane, big), axis=1, keepdims=True)
    lo = N_EXPERT_GROUPS + EXPERTS_PER_GROUP * g_top
    in_grp = (lane >= lo) & (lane < lo + EXPERTS_PER_GROUP)
    el = jnp.where(in_grp, logits, -jnp.inf)
    eexp = jnp.exp(el - jnp.max(el, axis=1, keepdims=True))
    eprob = eexp / jnp.sum(eexp, axis=1, keepdims=True)
    v1 = jnp.max(eprob, axis=1, keepdims=True)
    i1 = jnp.min(jnp.where(in_grp & (eprob == v1), lane, big), axis=1, keepdims=True)
    rest = jnp.where(in_grp & (lane != i1), eprob, -1.0)
    v2 = jnp.max(rest, axis=1, keepdims=True)
    i2 = jnp.min(jnp.where(rest == v2, lane, big), axis=1, keepdims=True)
    tot = v1 + v2
    w1 = g_w * (v1 / tot)
    w2 = g_w * (v2 / tot)
    e1 = i1 - N_EXPERT_GROUPS
    e2 = i2 - N_EXPERT_GROUPS
    rec = jnp.where(lane == RT_E1, e1, jnp.where(lane == RT_E2, e2, jnp.where(lane == RT_W1, w1, w2)))
    return jnp.where(lane <= RT_W2, rec, 0.0)


def _merge_kernel(ya_ref, ym_ref, g_ref, x_ref, mod_ref,
                  wpa_ref, wpm_ref, wout_ref, lng_ref, lnb_ref, wr_ref, br_ref,
                  x1_ref, u2_ref, rt_ref):
    pa = jnp.dot(ya_ref[...], wpa_ref[...], preferred_element_type=F32)
    pm = jnp.dot(ym_ref[...], wpm_ref[...], preferred_element_type=F32)
    merged = g_ref[:, :D_MODEL].astype(F32) * pa + g_ref[:, D_MODEL:].astype(F32) * pm
    mix = jnp.dot(merged.astype(BF16), wout_ref[...], preferred_element_type=F32)
    z = DEEPNORM_ALPHA * x_ref[...] + mod_ref[2:3, :] * mix
    x1 = _normalize(z) * lng_ref[...] + lnb_ref[...]
    x1_ref[...] = x1
    u2 = _normalize(x1) * (1.0 + mod_ref[4:5, :]) + mod_ref[3:4, :]
    u2_ref[...] = u2
    logits = jnp.dot(u2.astype(BF16), wr_ref[...], preferred_element_type=F32) + br_ref[...]
    rt_ref[...] = _route(logits)


def _merge(ya, ym, g, x2, mod, wpa, wpm, wout, lng, lnb, wr, br):
    tm = MERGE_TM
    tiles_per_batch = SEQ // tm
    rowblk = lambda w: pl.BlockSpec((tm, w), lambda m: (m, 0))
    const = lambda shape: pl.BlockSpec(shape, lambda m: (0,) * len(shape), pipeline_mode=pl.Buffered(1))
    return pl.pallas_call(
        _merge_kernel,
        out_shape=(
            jax.ShapeDtypeStruct((TOKENS, D_MODEL), F32),
            jax.ShapeDtypeStruct((TOKENS, D_MODEL), F32),
            jax.ShapeDtypeStruct((TOKENS, LANES), F32),
        ),
        grid=(TOKENS // tm,),
        in_specs=[
            rowblk(A_GROUP_W), rowblk(M_W), rowblk(2 * D_MODEL), rowblk(D_MODEL),
            pl.BlockSpec((None, 6, D_MODEL), lambda m: (m // tiles_per_batch, 0, 0)),
            const((A_GROUP_W, D_MODEL)), const((M_W, D_MODEL)), const((D_MODEL, D_MODEL)),
            const((1, D_MODEL)), const((1, D_MODEL)),
            const((D_MODEL, LANES)), const((1, LANES)),
        ],
        out_specs=(rowblk(D_MODEL), rowblk(D_MODEL), rowblk(LANES)),
        compiler_params=_cparams(("arbitrary",)),
        name="merge_ln1_route",
    )(ya, ym, g, x2, mod, wpa, wpm, wout, lng, lnb, wr, br)


CAST_ROWS = 128


def _issue_rows(src_ref, idx_ref, base, buf, slot, sem, nrows):
    def body(blk, carry):
        for j in range(SUBLANES):
            row = idx_ref[base + blk * SUBLANES + j]
            src = src_ref.at[lax.shift_right_logical(row, 3), pl.ds(row & (SUBLANES - 1), 1)]
            pltpu.make_async_copy(src, buf.at[slot, blk, pl.ds(j, 1)], sem.at[slot]).start(priority=j % 2)
        return carry

    lax.fori_loop(0, nrows // SUBLANES, body, 0)


def _wait_rows(src_ref, buf, slot, sem, nrows):
    groups = nrows // SUBLANES
    pltpu.make_async_copy(src_ref.at[pl.ds(0, groups)], buf.at[slot, pl.ds(0, groups)], sem.at[slot]).wait()


class _TileTable:
    def __init__(self, meta_ref):
        self.ref = meta_ref

    def expert(self, t):
        return self.ref[META_TE * LANES + t]

    def next_expert(self, t):
        return self.ref[META_TN * LANES + t]

    def valid_rows(self, t):
        return self.ref[META_TV * LANES + t]

    def tiles_used(self):
        return self.ref[META_NU * LANES]


def _expert_changed(te_ref, i):
    return (i == 0) | (te_ref.expert(i) != te_ref.expert(jnp.maximum(i - 1, 0)))


def _moe_kernel(meta_ref, tok_ref, u_ref, wg_ref, wu_ref, wd_ref, o_ref,
                stg, stu, std, wgb, wub, wdb, xbuf, wsem, xsem):
    i = pl.program_id(0)
    te_ref = _TileTable(meta_ref)
    nu = te_ref.tiles_used()
    tm = o_ref.shape[0]
    half = tm // 2

    def weight_copies(e):
        return (pltpu.make_async_copy(wg_ref.at[e], stg, wsem.at[0]),
                pltpu.make_async_copy(wu_ref.at[e], stu, wsem.at[1]),
                pltpu.make_async_copy(wd_ref.at[e], std, wsem.at[2]))

    def for_tile_rows(t, fn):
        @pl.when(te_ref.valid_rows(t) <= half)
        def _():
            fn(half)

        @pl.when(te_ref.valid_rows(t) > half)
        def _():
            fn(tm)

    def issue_tile(t):
        for_tile_rows(t, lambda nrows: _issue_rows(u_ref, tok_ref, t * tm, xbuf, t % MOE_ROW_SLOTS, xsem, nrows))

    @pl.when(i == 0)
    def _():
        issue_tile(0)

        @pl.when(nu > 1)
        def _():
            issue_tile(1)

        for cp in weight_copies(te_ref.expert(0)):
            cp.start(priority=1)

    @pl.when(i < nu)
    def _():
        slot = i % MOE_ROW_SLOTS

        @pl.when(i + 2 < nu)
        def _():
            issue_tile(i + 2)

        changed = _expert_changed(te_ref, i)

        @pl.when(changed)
        def _():
            for cp in weight_copies(te_ref.expert(i)):
                cp.wait()
            def cast_rows(ci, carry):
                r = pl.multiple_of(ci * CAST_ROWS, CAST_ROWS)
                wgb[pl.ds(r, CAST_ROWS), :] = stg[pl.ds(r, CAST_ROWS), :].astype(BF16)
                return carry

            lax.fori_loop(0, D_MODEL // CAST_ROWS, cast_rows, 0)

        def convert_inline(src, dst):
            for r0 in range(0, src.shape[0], CAST_ROWS):
                dst[r0:r0 + CAST_ROWS, :] = src[r0:r0 + CAST_ROWS, :].astype(BF16)

        def compute(nrows, new_expert):
            _wait_rows(u_ref, xbuf, slot, xsem, nrows)
            x = xbuf[slot, 0:nrows // SUBLANES].reshape(nrows, D_MODEL).astype(BF16)
            a = jnp.dot(x, wgb[...], preferred_element_type=F32)
            if new_expert:
                convert_inline(stu, wub)
            b = jnp.dot(x, wub[...], preferred_element_type=F32)
            if new_expert:
                convert_inline(std, wdb)

                @pl.when(te_ref.next_expert(i) >= 0)
                def _():
                    for cp in weight_copies(te_ref.next_expert(i)):
                        cp.start(priority=1)
            h = (_silu(a) * b).astype(BF16)
            o_ref[0:nrows, :] = jnp.dot(h, wdb[...], preferred_element_type=F32)
            if nrows < tm:
                o_ref[nrows:tm, :] = jnp.zeros((tm - nrows, D_MODEL), F32)

        @pl.when(changed)
        def _():
            for_tile_rows(i, functools.partial(compute, new_expert=True))

        @pl.when(jnp.logical_not(changed))
        def _():
            for_tile_rows(i, functools.partial(compute, new_expert=False))

    @pl.when(i >= nu)
    def _():
        o_ref[...] = jnp.zeros_like(o_ref)


def _moe(meta, row_token, u2, w_eg, w_eu, w_ed):
    tm = MOE_TM
    return pl.pallas_call(
        _moe_kernel,
        out_shape=jax.ShapeDtypeStruct((MOE_ROWS, D_MODEL), F32),
        grid_spec=pltpu.PrefetchScalarGridSpec(
            num_scalar_prefetch=2,
            grid=(MOE_TILES,),
            in_specs=[pl.BlockSpec(memory_space=pl.ANY)] * 4,
            out_specs=pl.BlockSpec((tm, D_MODEL), lambda i, meta, tok: (i, 0)),
            scratch_shapes=[
                pltpu.VMEM((D_MODEL, D_FF_EXPERT), F32),
                pltpu.VMEM((D_MODEL, D_FF_EXPERT), F32),
                pltpu.VMEM((D_FF_EXPERT, D_MODEL), F32),
                pltpu.VMEM((D_MODEL, D_FF_EXPERT), BF16),
                pltpu.VMEM((D_MODEL, D_FF_EXPERT), BF16),
                pltpu.VMEM((D_FF_EXPERT, D_MODEL), BF16),
                pltpu.VMEM((MOE_ROW_SLOTS, tm // SUBLANES, SUBLANES, D_MODEL), F32),
                pltpu.SemaphoreType.DMA((3,)),
                pltpu.SemaphoreType.DMA((MOE_ROW_SLOTS,)),
            ],
        ),
        compiler_params=_cparams(("arbitrary",)),
        name="moe_experts",
    )(meta, row_token, u2.reshape(TOKENS // SUBLANES, SUBLANES, D_MODEL), w_eg, w_eu, w_ed)


def _final_kernel(pos_ref, o_ref, x1_ref, rt_ref, mod_ref, lng_ref, lnb_ref, y_ref, buf, sem):
    i = pl.program_id(0)
    tb = x1_ref.shape[0]
    slot = i % 2

    @pl.when(i == 0)
    def _():
        _issue_rows(o_ref, pos_ref, 0, buf, 0, sem, 2 * tb)

    @pl.when(i + 1 < pl.num_programs(0))
    def _():
        _issue_rows(o_ref, pos_ref, (i + 1) * (2 * tb), buf, 1 - slot, sem, 2 * tb)

    _wait_rows(o_ref, buf, slot, sem, 2 * tb)
    rows = buf[slot].reshape(2 * tb, D_MODEL)
    ffn = rt_ref[:, RT_W1:RT_W1 + 1] * rows[0:tb, :] + rt_ref[:, RT_W2:RT_W2 + 1] * rows[tb:2 * tb, :]
    z = DEEPNORM_ALPHA * x1_ref[...] + mod_ref[5:6, :] * ffn
    y_ref[...] = _normalize(z) * lng_ref[...] + lnb_ref[...]


def _final(pos_tiles, moe_out, x1, rt, mod, lng, lnb):
    tb = FINAL_TB
    tiles_per_batch = SEQ // tb
    return pl.pallas_call(
        _final_kernel,
        out_shape=jax.ShapeDtypeStruct((TOKENS, D_MODEL), F32),
        grid_spec=pltpu.PrefetchScalarGridSpec(
            num_scalar_prefetch=1,
            grid=(TOKENS // tb,),
            in_specs=[
                pl.BlockSpec(memory_space=pl.ANY),
                pl.BlockSpec((tb, D_MODEL), lambda i, pos: (i, 0)),
                pl.BlockSpec((tb, LANES), lambda i, pos: (i, 0)),
                pl.BlockSpec((None, 6, D_MODEL), lambda i, pos: (i // tiles_per_batch, 0, 0)),
                pl.BlockSpec((1, D_MODEL), lambda i, pos: (0, 0)),
                pl.BlockSpec((1, D_MODEL), lambda i, pos: (0, 0)),
            ],
            out_specs=pl.BlockSpec((tb, D_MODEL), lambda i, pos: (i, 0)),
            scratch_shapes=[pltpu.VMEM((2, 2 * tb // SUBLANES, SUBLANES, D_MODEL), F32),
                            pltpu.SemaphoreType.DMA((2,))],
        ),
        compiler_params=_cparams(("arbitrary",)),
        name="combine_ln2",
    )(pos_tiles, moe_out.reshape(MOE_ROWS // SUBLANES, SUBLANES, D_MODEL), x1, rt, mod, lng, lnb)


ROUTE_BLK = 256
ROUTE_LOCKSTEP = 4
META_TE, META_TN, META_TV, META_NU = 0, 1, 2, 3


def _lane_cumsum(x, lane):
    s = 1
    while s < LANES:
        x = x + jnp.where(lane >= s, pltpu.roll(x, s, 1), 0.0)
        s *= 2
    return x


def _route_tables_kernel(rt_ref, pos_ref, meta_ref, rank_sc):
    tm = float(MOE_TM)
    nblk = TOKENS // ROUTE_BLK
    lane_i = lax.broadcasted_iota(jnp.int32, (ROUTE_BLK, LANES), 1)
    lane_f = lane_i.astype(F32)
    earlier = (lax.broadcasted_iota(jnp.int32, (ROUTE_BLK, ROUTE_BLK), 0)
               > lax.broadcasted_iota(jnp.int32, (ROUTE_BLK, ROUTE_BLK), 1)).astype(BF16)

    def onehots(b):
        blk = rt_ref[pl.ds(pl.multiple_of(b * ROUTE_BLK, ROUTE_BLK), ROUTE_BLK), :]
        return blk[:, RT_E1:RT_E1 + 1] == lane_f, blk[:, RT_E2:RT_E2 + 1] == lane_f

    def count_pass(g, carry):
        c1, c2 = carry
        bs = [g * ROUTE_LOCKSTEP + i for i in range(ROUTE_LOCKSTEP)]
        oh = [onehots(b) for b in bs]
        f = [(jnp.where(o1, 1.0, 0.0), jnp.where(o2, 1.0, 0.0)) for o1, o2 in oh]
        p = [(jnp.dot(earlier, f1.astype(BF16), preferred_element_type=F32),
              jnp.dot(earlier, f2.astype(BF16), preferred_element_type=F32)) for f1, f2 in f]
        tot = [(jnp.sum(f1, axis=0, keepdims=True), jnp.sum(f2, axis=0, keepdims=True)) for f1, f2 in f]
        for i, b in enumerate(bs):
            r1 = jnp.sum(jnp.where(oh[i][0], p[i][0] + c1, 0.0), axis=1, keepdims=True)
            r2 = jnp.sum(jnp.where(oh[i][1], p[i][1] + c2, 0.0), axis=1, keepdims=True)
            rank_sc[pl.ds(pl.multiple_of(b * ROUTE_BLK, ROUTE_BLK), ROUTE_BLK), :] = jnp.where(
                lane_i == 0, r1, jnp.where(lane_i == 1, r2, 0.0))
            c1 = c1 + tot[i][0]
            c2 = c2 + tot[i][1]
        return c1, c2

    zero = jnp.zeros((1, LANES), F32)
    c1, c2 = lax.fori_loop(0, nblk // ROUTE_LOCKSTEP, count_pass, (zero, zero))

    lane8 = lax.broadcasted_iota(jnp.int32, (SUBLANES, LANES), 1)
    counts = jnp.broadcast_to(c1 + c2, (SUBLANES, LANES))
    padded = jnp.floor((counts + (tm - 1.0)) * (1.0 / tm)) * tm
    pend = _lane_cumsum(padded, lane8)
    pstart = pend - padded
    start1 = pstart[0:1, :]
    start2 = start1 + c1

    def place_pass(g, carry):
        bs = [g * ROUTE_LOCKSTEP + i for i in range(ROUTE_LOCKSTEP)]
        oh = [onehots(b) for b in bs]
        rows = [pl.ds(pl.multiple_of(b * ROUTE_BLK, ROUTE_BLK), ROUTE_BLK) for b in bs]
        rk = [rank_sc[r, :] for r in rows]
        d1 = [jnp.sum(jnp.where(oh[i][0], start1, 0.0), axis=1, keepdims=True) + rk[i][:, 0:1]
              for i in range(ROUTE_LOCKSTEP)]
        d2 = [jnp.sum(jnp.where(oh[i][1], start2, 0.0), axis=1, keepdims=True) + rk[i][:, 1:2]
              for i in range(ROUTE_LOCKSTEP)]
        for i in range(ROUTE_LOCKSTEP):
            pos_ref[rows[i], :] = jnp.where(lane_i == 0, d1[i], jnp.where(lane_i == 1, d2[i], 0.0)).astype(jnp.int32)
        return carry

    lax.fori_loop(0, nblk // ROUTE_LOCKSTEP, place_pass, 0)

    tile = lax.broadcasted_iota(jnp.int32, (LANES, LANES), 0).astype(F32)
    lane = lax.broadcasted_iota(jnp.int32, (LANES, LANES), 1)
    expert_lane = lane < N_EXPERTS
    pend_b = jnp.broadcast_to(pend[0:1, :], (LANES, LANES))
    pstart_b = jnp.broadcast_to(pstart[0:1, :], (LANES, LANES))
    counts_b = jnp.broadcast_to(counts[0:1, :], (LANES, LANES))
    n_used = jnp.sum(jnp.where(lane == N_EXPERTS - 1, pend_b, 0.0), axis=1, keepdims=True) * (1.0 / tm)
    te = jnp.sum(jnp.where(expert_lane & (pend_b <= tile * tm), 1.0, 0.0), axis=1, keepdims=True)
    te = jnp.minimum(te, float(N_EXPERTS - 1))
    te_last = jnp.sum(jnp.where(tile[:, 0:1] == n_used - 1.0, te, 0.0), axis=0, keepdims=True)
    te = jnp.where(tile[:, 0:1] < n_used, te, te_last)
    of_tile = lane.astype(F32) == te
    pend_te = jnp.sum(jnp.where(of_tile, pend_b, 0.0), axis=1, keepdims=True)
    last_row = jnp.sum(jnp.where(of_tile, pstart_b + counts_b, 0.0), axis=1, keepdims=True)
    tile_rows = jnp.clip(last_row - tile[:, 0:1] * tm, 0.0, tm)
    next_run = pend_te * (1.0 / tm)
    te_by_lane = jnp.broadcast_to(te, (LANES, LANES)).T
    te_at_next = jnp.sum(jnp.where(lane.astype(F32) == next_run, te_by_lane, 0.0), axis=1, keepdims=True)
    te_next = jnp.where(next_run < n_used, te_at_next, -1.0)
    cols = jnp.where(lane == META_TE, te, jnp.where(lane == META_TN, te_next,
                     jnp.where(lane == META_TV, tile_rows, jnp.where(lane == META_NU, n_used, 0.0))))
    meta_ref[...] = cols.T[0:SUBLANES, :].astype(jnp.int32)


def _routing_tables(rt):
    pos, meta = pl.pallas_call(
        _route_tables_kernel,
        out_shape=(
            jax.ShapeDtypeStruct((TOKENS, LANES), jnp.int32),
            jax.ShapeDtypeStruct((SUBLANES, LANES), jnp.int32),
        ),
        scratch_shapes=[pltpu.VMEM((TOKENS, LANES), F32)],
        compiler_params=_cparams(None),
        name="route_tables",
    )(rt)
    pos = pos[:, 0:2]
    tok = jnp.arange(2 * TOKENS, dtype=jnp.int32) // 2
    row_token = jnp.zeros((MOE_ROWS,), jnp.int32).at[pos.reshape(-1)].set(
        tok, unique_indices=True, mode="promise_in_bounds")
    return row_token, meta.reshape(-1), pos


def _rope_tables():
    inv = ROPE_THETA ** (-jnp.arange(0, HEAD_DIM_A, 2, dtype=F32) / HEAD_DIM_A)
    ang = jnp.arange(SEQ, dtype=F32)[:, None] * inv[None, :]
    cos = jnp.cos(ang)
    sin = jnp.sin(ang)
    return jnp.concatenate([cos, cos], axis=-1), jnp.concatenate([-sin, sin], axis=-1)


def kernel(x, c, w_ada, b_ada, w_in, b_mgate, conv_w, conv_b, m_norm_g, w_proj_a, w_proj_m, w_gate, b_gate,
           w_out, ln1_g, ln1_b, w_rg, b_rg, w_re, b_re, w_eg, w_eu, w_ed, ln2_g, ln2_b):
    assert x.shape == (BATCH, SEQ, D_MODEL) and w_ada.shape[0] == 1
    l = 0
    x2 = x.reshape(TOKENS, D_MODEL)

    c_pad = jnp.zeros((ADA_ROWS, D_MODEL), F32).at[:BATCH].set(c)
    mod = _ada(c_pad, w_ada[l], b_ada[l][None, :])[:BATCH].reshape(BATCH, 6, D_MODEL)

    w_in_t = jnp.swapaxes(w_in[l], 0, 1)
    w_if_t = jnp.zeros((LANES, D_MODEL), F32).at[:2 * M_HEADS].set(w_in_t[N_IN_MAIN:])
    p, gates, u = _inproj(x2, mod, w_in_t, w_if_t)
    g = _gateproj(u, w_gate[l], b_gate[l][None, :])

    cos_t, sin_t = _rope_tables()
    ya = _attention(p, cos_t, sin_t)

    gcol = gates[:, :2 * M_HEADS].reshape(BATCH, SEQ, 2 * M_HEADS)
    grow = jnp.transpose(gcol, (0, 2, 1))
    ym = _mlstm(p, gcol, grow, b_mgate[l][None, :], b_mgate[l][:, None], conv_w[l], conv_b[l][None, :],
                m_norm_g[l][None, :])

    wr = (jnp.zeros((D_MODEL, LANES), F32)
          .at[:, :N_EXPERT_GROUPS].set(w_rg[l])
          .at[:, N_EXPERT_GROUPS:N_EXPERT_GROUPS + N_EXPERTS].set(w_re[l])).astype(BF16)
    br = (jnp.zeros((1, LANES), F32)
          .at[0, :N_EXPERT_GROUPS].set(b_rg[l])
          .at[0, N_EXPERT_GROUPS:N_EXPERT_GROUPS + N_EXPERTS].set(b_re[l]))
    x1, u2, rt = _merge(ya, ym, g, x2, mod,
                        w_proj_a[l].astype(BF16), w_proj_m[l].astype(BF16), w_out[l].astype(BF16),
                        ln1_g[l][None, :], ln1_b[l][None, :], wr, br)

    row_token, meta, pos = _routing_tables(rt)

    mo = _moe(meta, row_token, u2,
              w_eg[l].reshape(N_EXPERTS, D_MODEL, D_FF_EXPERT), w_eu[l].reshape(N_EXPERTS, D_MODEL, D_FF_EXPERT),
              w_ed[l].reshape(N_EXPERTS, D_FF_EXPERT, D_MODEL))

    nt = TOKENS // FINAL_TB
    pos_tiles = jnp.transpose(pos.reshape(nt, FINAL_TB, 2), (0, 2, 1)).reshape(-1)
    y = _final(pos_tiles, mo, x1, rt, mod, ln2_g[l][None, :], ln2_b[l][None, :])
    return y.reshape(BATCH, SEQ, D_MODEL)
```
